```python
import math
import jax
import jax.numpy as jnp
from jax import lax
import numpy as np


D_MODEL = 2048
BATCH = 8
SEQ = 2048
DEPTH = 4

N_MIXERS = 2
N_SSD_LAYERS = (DEPTH + 1) // 2
N_POOL_LAYERS = DEPTH // 2
EPS = 1e-6

D_FF = 5632

SSD_EXPAND = 2
D_INNER = SSD_EXPAND * D_MODEL
SSD_HEAD_DIM = 64
SSD_HEADS = D_INNER // SSD_HEAD_DIM
SSD_GROUPS = 8
SSD_HEADS_PER_GROUP = SSD_HEADS // SSD_GROUPS
SSD_STATE = 128
SSD_CONV_W = 5
SSD_CHUNK = 128
SSD_CONV_DIM = D_INNER + 2 * SSD_GROUPS * SSD_STATE
SSD_IN_DIM = D_INNER + SSD_CONV_DIM + 2 * SSD_HEADS

POOL_WINDOWS = (2, 4, 8, 16)
N_POOL_GROUPS = 4
D_POOL = D_MODEL
POOL_GROUP_DIM = D_POOL // N_POOL_GROUPS

kernel_name = "hybrid_ssd_pool_macaron_encoder"


def rmsnorm(x, g):
    xf = x.astype(jnp.float32)
    y = xf * lax.rsqrt(jnp.mean(xf * xf, axis=-1, keepdims=True) + EPS)
    return (y * g.astype(jnp.float32)).astype(x.dtype)


def swiglu(h, w_gate, w_up, w_down):
    return (jax.nn.silu(h @ w_gate) * (h @ w_up)) @ w_down


def centred_dwconv(u, w, bias):
    pad = SSD_CONV_W // 2
    out = lax.conv_general_dilated(
        u, w[:, None, :].astype(u.dtype), window_strides=(1,), padding=[(pad, pad)],
        dimension_numbers=("NWC", "WIO", "NWC"), feature_group_count=u.shape[-1])
    return out + bias.astype(u.dtype)


def ssd_chunked(xh, dt, a, bm, cm):
    b, l, h, p = xh.shape
    q = SSD_CHUNK
    nc = l // q
    g, hg, n = SSD_GROUPS, SSD_HEADS_PER_GROUP, SSD_STATE
    xdt = (xh * dt[..., None].astype(xh.dtype)).reshape(b, nc, q, g, hg, p)
    bm = bm.reshape(b, nc, q, g, n)
    cm = cm.reshape(b, nc, q, g, n)
    a_cs = jnp.cumsum((dt * a).reshape(b, nc, q, g, hg), axis=2)
    tri = jnp.tril(jnp.ones((q, q), dtype=bool))[:, :, None, None]
    seg = a_cs[:, :, :, None] - a_cs[:, :, None, :]
    lmat = jnp.exp(jnp.where(tri, seg, -jnp.inf))
    cb = jnp.einsum("bcign,bcjgn->bcgij", cm, bm)
    y_diag = jnp.einsum("bcgij,bcijgh,bcjghp->bcighp", cb, lmat, xdt)
    decay_states = jnp.exp(a_cs[:, :, -1:] - a_cs)
    states = jnp.einsum("bcjgn,bcjgh,bcjghp->bcghpn", bm, decay_states, xdt)
    chunk_decay = jnp.exp(a_cs[:, :, -1])

    def step(hstate, inp):
        st, dec = inp
        return hstate * dec[..., None, None] + st, hstate

    h0 = jnp.zeros((b, g, hg, p, n), states.dtype)
    _, h_in = lax.scan(step, h0, (jnp.moveaxis(states, 1, 0), jnp.moveaxis(chunk_decay, 1, 0)))
    h_in = jnp.moveaxis(h_in, 0, 1)
    y_off = jnp.einsum("bcign,bcigh,bcghpn->bcighp", cm, jnp.exp(a_cs), h_in)
    return (y_diag + y_off).reshape(b, l, h, p).astype(xh.dtype)


def ssd_mixer(h, w_in, conv_w, conv_b, dt_bias, a_log, d_skip, norm_g, w_out):
    b, l, _ = h.shape
    proj = h @ w_in
    z, xbc, dt_raw = jnp.split(proj, [D_INNER, D_INNER + SSD_CONV_DIM], axis=-1)
    xbc = jax.nn.silu(centred_dwconv(xbc, conv_w, conv_b))
    xs, bm, cm = jnp.split(xbc, [D_INNER, D_INNER + SSD_GROUPS * SSD_STATE], axis=-1)
    xh = xs.reshape(b, l, SSD_HEADS, SSD_HEAD_DIM)
    bm = bm.reshape(b, l, SSD_GROUPS, SSD_STATE)
    cm = cm.reshape(b, l, SSD_GROUPS, SSD_STATE)
    dt = jax.nn.softplus(dt_raw.astype(jnp.float32).reshape(b, l, 2, SSD_HEADS)
                         + dt_bias.astype(jnp.float32))
    a = -jnp.exp(a_log.astype(jnp.float32))
    y_fwd = ssd_chunked(xh, dt[:, :, 0], a[0], bm, cm)
    y_bwd = jnp.flip(ssd_chunked(jnp.flip(xh, 1), jnp.flip(dt[:, :, 1], 1), a[1],
                                 jnp.flip(bm, 1), jnp.flip(cm, 1)), axis=1)
    y = y_fwd + y_bwd + xh * d_skip[:, None].astype(xh.dtype)
    y = y.reshape(b, l, D_INNER)
    y = rmsnorm(y * jax.nn.silu(z), norm_g)
    return y @ w_out


def pool_mixer(h, w_in, w_group, scale, w_out):
    b, l, _ = h.shape
    u = (h @ w_in).reshape(b, l, N_POOL_GROUPS, POOL_GROUP_DIM)
    uf = u.astype(jnp.float32)
    cs = jnp.pad(jnp.cumsum(uf, axis=1), ((0, 0), (1, 0), (0, 0), (0, 0)))
    t = jnp.arange(l)
    pooled = []
    for gi, w in enumerate(POOL_WINDOWS):
        lo = jnp.clip(t - w // 2, 0, l)
        hi = jnp.clip(t + w // 2, 0, l)
        cs_g = cs[:, :, gi]
        cnt = (hi - lo).astype(jnp.float32)[None, :, None]
        pooled.append((cs_g[:, hi] - cs_g[:, lo]) / cnt)
    pooled = jnp.stack(pooled, axis=2)
    mix = (pooled - uf).astype(h.dtype)
    v = jnp.einsum("blgc,gcd->blgd", mix, w_group).reshape(b, l, D_POOL)
    return (v * scale) @ w_out


def _fwd_setup_inputs(seed: int = 0) -> dict:
    key = jax.random.key(seed)
    ks = jax.random.split(key, 24)
    f32 = jnp.float32

    def nrm(k, shape, fan_in):
        return jax.random.normal(k, shape, f32) * (fan_in ** -0.5)

    def gain(k, shape):
        return 1.0 + 0.02 * jax.random.normal(k, shape, f32)

    x = jax.random.normal(ks[0], (BATCH, SEQ, D_MODEL), f32)
    ffn_norm = gain(ks[1], (DEPTH, 2, D_MODEL))
    ffn_w_gate = nrm(ks[2], (DEPTH, 2, D_MODEL, D_FF), D_MODEL)
    ffn_w_up = nrm(ks[3], (DEPTH, 2, D_MODEL, D_FF), D_MODEL)
    ffn_w_down = nrm(ks[4], (DEPTH, 2, D_FF, D_MODEL), D_FF)
    mix_norm = gain(ks[5], (DEPTH, D_MODEL))

    ssd_w_in = nrm(ks[6], (N_SSD_LAYERS, D_MODEL, SSD_IN_DIM), D_MODEL)
    ssd_conv_w = nrm(ks[7], (N_SSD_LAYERS, SSD_CONV_W, SSD_CONV_DIM), SSD_CONV_W)
    ssd_conv_b = 0.02 * jax.random.normal(ks[8], (N_SSD_LAYERS, SSD_CONV_DIM), f32)
    dt0 = jnp.exp(jax.random.uniform(ks[9], (N_SSD_LAYERS, 2, SSD_HEADS), f32,
                                     minval=math.log(1e-3), maxval=math.log(1e-1)))
    ssd_dt_bias = dt0 + jnp.log(-jnp.expm1(-dt0))
    ssd_a_log = jnp.log(jax.random.uniform(ks[10], (N_SSD_LAYERS, 2, SSD_HEADS), f32,
                                           minval=1.0, maxval=16.0))
    ssd_d = 1.0 + 0.1 * jax.random.normal(ks[11], (N_SSD_LAYERS, SSD_HEADS), f32)
    ssd_norm = gain(ks[12], (N_SSD_LAYERS, D_INNER))
    ssd_w_out = nrm(ks[13], (N_SSD_LAYERS, D_INNER, D_MODEL), D_INNER)

    pool_w_in = nrm(ks[14], (N_POOL_LAYERS, D_MODEL, D_POOL), D_MODEL)
    pool_w_group = nrm(ks[15], (N_POOL_LAYERS, N_POOL_GROUPS, POOL_GROUP_DIM, POOL_GROUP_DIM),
                       POOL_GROUP_DIM)
    pool_scale = 1.0 + 0.1 * jax.random.normal(ks[16], (N_POOL_LAYERS, D_POOL), f32)
    pool_w_out = nrm(ks[17], (N_POOL_LAYERS, D_POOL, D_MODEL), D_POOL)

    final_norm = gain(ks[18], (D_MODEL,))
    return {
        "x": x, "ffn_norm": ffn_norm, "ffn_w_gate": ffn_w_gate, "ffn_w_up": ffn_w_up,
        "ffn_w_down": ffn_w_down, "mix_norm": mix_norm,
        "ssd_w_in": ssd_w_in, "ssd_conv_w": ssd_conv_w, "ssd_conv_b": ssd_conv_b,
        "ssd_dt_bias": ssd_dt_bias, "ssd_a_log": ssd_a_log, "ssd_d": ssd_d,
        "ssd_norm": ssd_norm, "ssd_w_out": ssd_w_out,
        "pool_w_in": pool_w_in, "pool_w_group": pool_w_group, "pool_scale": pool_scale,
        "pool_w_out": pool_w_out, "final_norm": final_norm,
    }


def _fwd_reference(x, ffn_norm, ffn_w_gate, ffn_w_up, ffn_w_down, mix_norm,
              ssd_w_in, ssd_conv_w, ssd_conv_b, ssd_dt_bias, ssd_a_log, ssd_d,
              ssd_norm, ssd_w_out, pool_w_in, pool_w_group, pool_scale, pool_w_out,
              final_norm):
    for i in range(DEPTH):
        x = x + 0.5 * swiglu(rmsnorm(x, ffn_norm[i, 0]), ffn_w_gate[i, 0], ffn_w_up[i, 0],
                             ffn_w_down[i, 0])
        h = rmsnorm(x, mix_norm[i])
        j = i // N_MIXERS
        if i % N_MIXERS == 0:
            x = x + ssd_mixer(h, ssd_w_in[j], ssd_conv_w[j], ssd_conv_b[j], ssd_dt_bias[j],
                              ssd_a_log[j], ssd_d[j], ssd_norm[j], ssd_w_out[j])
        else:
            x = x + pool_mixer(h, pool_w_in[j], pool_w_group[j], pool_scale[j], pool_w_out[j])
        x = x + 0.5 * swiglu(rmsnorm(x, ffn_norm[i, 1]), ffn_w_gate[i, 1], ffn_w_up[i, 1],
                             ffn_w_down[i, 1])
    return rmsnorm(x, final_norm)


import jax as _jax
import jax.numpy as _jnp

TWIN_FORMAT = 'train_step'
FWD_PARAMS = ['x', 'ffn_norm', 'ffn_w_gate', 'ffn_w_up', 'ffn_w_down', 'mix_norm', 'ssd_w_in', 'ssd_conv_w', 'ssd_conv_b', 'ssd_dt_bias', 'ssd_a_log', 'ssd_d', 'ssd_norm', 'ssd_w_out', 'pool_w_in', 'pool_w_group', 'pool_scale', 'pool_w_out', 'final_norm']
TWIN_WEIGHTS = ['ffn_norm', 'ffn_w_gate', 'ffn_w_up', 'ffn_w_down', 'mix_norm', 'ssd_w_in', 'ssd_conv_w', 'ssd_conv_b', 'ssd_dt_bias', 'ssd_a_log', 'ssd_d', 'ssd_norm', 'ssd_w_out', 'pool_w_in', 'pool_w_group', 'pool_scale', 'pool_w_out', 'final_norm']
TWIN_DIFF_INPUT = 'x'
TWIN_INPUTS = ['x', 'ffn_norm', 'ffn_w_gate', 'ffn_w_up', 'ffn_w_down', 'mix_norm', 'ssd_w_in', 'ssd_conv_w', 'ssd_conv_b', 'ssd_dt_bias', 'ssd_a_log', 'ssd_d', 'ssd_norm', 'ssd_w_out', 'pool_w_in', 'pool_w_group', 'pool_scale', 'pool_w_out', 'final_norm', 'loss_target', 'm_ffn_norm', 'm_ffn_w_gate', 'm_ffn_w_up', 'm_ffn_w_down', 'm_mix_norm', 'm_ssd_w_in', 'm_ssd_conv_w', 'm_ssd_conv_b', 'm_ssd_dt_bias', 'm_ssd_a_log', 'm_ssd_d', 'm_ssd_norm', 'm_ssd_w_out', 'm_pool_w_in', 'm_pool_w_group', 'm_pool_scale', 'm_pool_w_out', 'm_final_norm', 'v_ffn_norm', 'v_ffn_w_gate', 'v_ffn_w_up', 'v_ffn_w_down', 'v_mix_norm', 'v_ssd_w_in', 'v_ssd_conv_w', 'v_ssd_conv_b', 'v_ssd_dt_bias', 'v_ssd_a_log', 'v_ssd_d', 'v_ssd_norm', 'v_ssd_w_out', 'v_pool_w_in', 'v_pool_w_group', 'v_pool_scale', 'v_pool_w_out', 'v_final_norm']
TWIN_OUTPUTS = ['loss', 'grad_x', 'grad_ffn_norm', 'grad_ffn_w_gate', 'grad_ffn_w_up', 'grad_ffn_w_down', 'grad_mix_norm', 'grad_ssd_w_in', 'grad_ssd_conv_w', 'grad_ssd_conv_b', 'grad_ssd_dt_bias', 'grad_ssd_a_log', 'grad_ssd_d', 'grad_ssd_norm', 'grad_ssd_w_out', 'grad_pool_w_in', 'grad_pool_w_group', 'grad_pool_scale', 'grad_pool_w_out', 'grad_final_norm', 'delta_ffn_norm', 'delta_ffn_w_gate', 'delta_ffn_w_up', 'delta_ffn_w_down', 'delta_mix_norm', 'delta_ssd_w_in', 'delta_ssd_conv_w', 'delta_ssd_conv_b', 'delta_ssd_dt_bias', 'delta_ssd_a_log', 'delta_ssd_d', 'delta_ssd_norm', 'delta_ssd_w_out', 'delta_pool_w_in', 'delta_pool_w_group', 'delta_pool_scale', 'delta_pool_w_out', 'delta_final_norm', 'new_m_ffn_norm', 'new_m_ffn_w_gate', 'new_m_ffn_w_up', 'new_m_ffn_w_down', 'new_m_mix_norm', 'new_m_ssd_w_in', 'new_m_ssd_conv_w', 'new_m_ssd_conv_b', 'new_m_ssd_dt_bias', 'new_m_ssd_a_log', 'new_m_ssd_d', 'new_m_ssd_norm', 'new_m_ssd_w_out', 'new_m_pool_w_in', 'new_m_pool_w_group', 'new_m_pool_scale', 'new_m_pool_w_out', 'new_m_final_norm', 'new_v_ffn_norm', 'new_v_ffn_w_gate', 'new_v_ffn_w_up', 'new_v_ffn_w_down', 'new_v_mix_norm', 'new_v_ssd_w_in', 'new_v_ssd_conv_w', 'new_v_ssd_conv_b', 'new_v_ssd_dt_bias', 'new_v_ssd_a_log', 'new_v_ssd_d', 'new_v_ssd_norm', 'new_v_ssd_w_out', 'new_v_pool_w_in', 'new_v_pool_w_group', 'new_v_pool_scale', 'new_v_pool_w_out', 'new_v_final_norm']
TWIN_LEAF_KINDS = {'loss': 'loss', 'grad_x': 'grad_x', 'grad_ffn_norm': 'grad_w', 'grad_ffn_w_gate': 'grad_w', 'grad_ffn_w_up': 'grad_w', 'grad_ffn_w_down': 'grad_w', 'grad_mix_norm': 'grad_w', 'grad_ssd_w_in': 'grad_w', 'grad_ssd_conv_w': 'grad_w', 'grad_ssd_conv_b': 'grad_w', 'grad_ssd_dt_bias': 'grad_w', 'grad_ssd_a_log': 'grad_w', 'grad_ssd_d': 'grad_w', 'grad_ssd_norm': 'grad_w', 'grad_ssd_w_out': 'grad_w', 'grad_pool_w_in': 'grad_w', 'grad_pool_w_group': 'grad_w', 'grad_pool_scale': 'grad_w', 'grad_pool_w_out': 'grad_w', 'grad_final_norm': 'grad_w', 'delta_ffn_norm': 'delta_w', 'delta_ffn_w_gate': 'delta_w', 'delta_ffn_w_up': 'delta_w', 'delta_ffn_w_down': 'delta_w', 'delta_mix_norm': 'delta_w', 'delta_ssd_w_in': 'delta_w', 'delta_ssd_conv_w': 'delta_w', 'delta_ssd_conv_b': 'delta_w', 'delta_ssd_dt_bias': 'delta_w', 'delta_ssd_a_log': 'delta_w', 'delta_ssd_d': 'delta_w', 'delta_ssd_norm': 'delta_w', 'delta_ssd_w_out': 'delta_w', 'delta_pool_w_in': 'delta_w', 'delta_pool_w_group': 'delta_w', 'delta_pool_scale': 'delta_w', 'delta_pool_w_out': 'delta_w', 'delta_final_norm': 'delta_w', 'new_m_ffn_norm': 'new_m', 'new_m_ffn_w_gate': 'new_m', 'new_m_ffn_w_up': 'new_m', 'new_m_ffn_w_down': 'new_m', 'new_m_mix_norm': 'new_m', 'new_m_ssd_w_in': 'new_m', 'new_m_ssd_conv_w': 'new_m', 'new_m_ssd_conv_b': 'new_m', 'new_m_ssd_dt_bias': 'new_m', 'new_m_ssd_a_log': 'new_m', 'new_m_ssd_d': 'new_m', 'new_m_ssd_norm': 'new_m', 'new_m_ssd_w_out': 'new_m', 'new_m_pool_w_in': 'new_m', 'new_m_pool_w_group': 'new_m', 'new_m_pool_scale': 'new_m', 'new_m_pool_w_out': 'new_m', 'new_m_final_norm': 'new_m', 'new_v_ffn_norm': 'new_v', 'new_v_ffn_w_gate': 'new_v', 'new_v_ffn_w_up': 'new_v', 'new_v_ffn_w_down': 'new_v', 'new_v_mix_norm': 'new_v', 'new_v_ssd_w_in': 'new_v', 'new_v_ssd_conv_w': 'new_v', 'new_v_ssd_conv_b': 'new_v', 'new_v_ssd_dt_bias': 'new_v', 'new_v_ssd_a_log': 'new_v', 'new_v_ssd_d': 'new_v', 'new_v_ssd_norm': 'new_v', 'new_v_ssd_w_out': 'new_v', 'new_v_pool_w_in': 'new_v', 'new_v_pool_w_group': 'new_v', 'new_v_pool_scale': 'new_v', 'new_v_pool_w_out': 'new_v', 'new_v_final_norm': 'new_v'}


def _forward(args):
    return _fwd_reference(*[args[k] for k in FWD_PARAMS])


def _output_shape():
    out = _jax.eval_shape(lambda: _forward(_fwd_setup_inputs(0)))
    return out.shape, out.dtype

N_MICROBATCH = 1
ADAM_LR = 0.001
ADAM_B1 = 0.9
ADAM_B2 = 0.999
ADAM_EPS = 1e-08
ADAM_WD = 0.01
ADAM_STEP = 10
PER_EXAMPLE_BATCH_AXIS = {'x': 0, 'loss_target': 0}
SHARED_INPUTS = []
_WEIGHT_DTYPES = {'ffn_norm': _jnp.float32, 'ffn_w_gate': _jnp.float32, 'ffn_w_up': _jnp.float32, 'ffn_w_down': _jnp.float32, 'mix_norm': _jnp.float32, 'ssd_w_in': _jnp.float32, 'ssd_conv_w': _jnp.float32, 'ssd_conv_b': _jnp.float32, 'ssd_dt_bias': _jnp.float32, 'ssd_a_log': _jnp.float32, 'ssd_d': _jnp.float32, 'ssd_norm': _jnp.float32, 'ssd_w_out': _jnp.float32, 'pool_w_in': _jnp.float32, 'pool_w_group': _jnp.float32, 'pool_scale': _jnp.float32, 'pool_w_out': _jnp.float32, 'final_norm': _jnp.float32}
MOMENT_SCALE = {'ffn_norm': 2.214423e-02, 'ffn_w_gate': 9.499720e-03, 'ffn_w_up': 9.196844e-03, 'ffn_w_down': 1.524971e-02, 'mix_norm': 5.426639e-02, 'ssd_w_in': 3.013177e-02, 'ssd_conv_w': 2.737263e-02, 'ssd_conv_b': 4.443824e-02, 'ssd_dt_bias': 5.655695e-02, 'ssd_a_log': 8.505094e-02, 'ssd_d': 1.271132e-01, 'ssd_norm': 3.128861e-02, 'ssd_w_out': 4.474408e-02, 'pool_w_in': 3.197666e-02, 'pool_w_group': 3.195345e-02, 'pool_scale': 3.229596e-02, 'pool_w_out': 3.198718e-02, 'final_norm': 8.011876e+00}


def _to_microbatches(a, axis):
    t = _jnp.moveaxis(a, axis, 0)
    t = t.reshape((N_MICROBATCH, t.shape[0] // N_MICROBATCH) + t.shape[1:])
    return _jnp.moveaxis(t, 1, axis + 1)


def setup_inputs(seed: int = 0) -> dict:
    inp = _fwd_setup_inputs(seed)
    key = _jax.random.fold_in(_jax.random.key(seed), 7919)
    shape, _ = _output_shape()
    out = dict(inp)
    out["loss_target"] = _jax.random.normal(_jax.random.fold_in(key, 0), shape, _jnp.float32)
    for i, name in enumerate(TWIN_WEIGHTS):
        w = inp[name].astype(_jnp.float32)
        if MOMENT_SCALE is None:
            s = _jnp.sqrt(_jnp.mean(_jnp.square(w)) + 1e-30)
        else:
            s = MOMENT_SCALE[name]
        km, kv = _jax.random.split(_jax.random.fold_in(key, i + 1))
        out[name] = w
        out["m_" + name] = s * _jax.random.normal(km, w.shape, _jnp.float32)
        out["v_" + name] = (s * s) * _jax.random.uniform(kv, w.shape, _jnp.float32, 0.5, 1.5)
    if N_MICROBATCH > 1:
        for name, axis in PER_EXAMPLE_BATCH_AXIS.items():
            out[name] = _to_microbatches(out[name], axis)
    return {'x': out['x'], 'ffn_norm': out['ffn_norm'], 'ffn_w_gate': out['ffn_w_gate'], 'ffn_w_up': out['ffn_w_up'], 'ffn_w_down': out['ffn_w_down'], 'mix_norm': out['mix_norm'], 'ssd_w_in': out['ssd_w_in'], 'ssd_conv_w': out['ssd_conv_w'], 'ssd_conv_b': out['ssd_conv_b'], 'ssd_dt_bias': out['ssd_dt_bias'], 'ssd_a_log': out['ssd_a_log'], 'ssd_d': out['ssd_d'], 'ssd_norm': out['ssd_norm'], 'ssd_w_out': out['ssd_w_out'], 'pool_w_in': out['pool_w_in'], 'pool_w_group': out['pool_w_group'], 'pool_scale': out['pool_scale'], 'pool_w_out': out['pool_w_out'], 'final_norm': out['final_norm'], 'loss_target': out['loss_target'], 'm_ffn_norm': out['m_ffn_norm'], 'm_ffn_w_gate': out['m_ffn_w_gate'], 'm_ffn_w_up': out['m_ffn_w_up'], 'm_ffn_w_down': out['m_ffn_w_down'], 'm_mix_norm': out['m_mix_norm'], 'm_ssd_w_in': out['m_ssd_w_in'], 'm_ssd_conv_w': out['m_ssd_conv_w'], 'm_ssd_conv_b': out['m_ssd_conv_b'], 'm_ssd_dt_bias': out['m_ssd_dt_bias'], 'm_ssd_a_log': out['m_ssd_a_log'], 'm_ssd_d': out['m_ssd_d'], 'm_ssd_norm': out['m_ssd_norm'], 'm_ssd_w_out': out['m_ssd_w_out'], 'm_pool_w_in': out['m_pool_w_in'], 'm_pool_w_group': out['m_pool_w_group'], 'm_pool_scale': out['m_pool_scale'], 'm_pool_w_out': out['m_pool_w_out'], 'm_final_norm': out['m_final_norm'], 'v_ffn_norm': out['v_ffn_norm'], 'v_ffn_w_gate': out['v_ffn_w_gate'], 'v_ffn_w_up': out['v_ffn_w_up'], 'v_ffn_w_down': out['v_ffn_w_down'], 'v_mix_norm': out['v_mix_norm'], 'v_ssd_w_in': out['v_ssd_w_in'], 'v_ssd_conv_w': out['v_ssd_conv_w'], 'v_ssd_conv_b': out['v_ssd_conv_b'], 'v_ssd_dt_bias': out['v_ssd_dt_bias'], 'v_ssd_a_log': out['v_ssd_a_log'], 'v_ssd_d': out['v_ssd_d'], 'v_ssd_norm': out['v_ssd_norm'], 'v_ssd_w_out': out['v_ssd_w_out'], 'v_pool_w_in': out['v_pool_w_in'], 'v_pool_w_group': out['v_pool_w_group'], 'v_pool_scale': out['v_pool_scale'], 'v_pool_w_out': out['v_pool_w_out'], 'v_final_norm': out['v_final_norm']}


def _loss(weights, diff, rest, loss_target):
    with _jax.named_scope("forward"):
        args = {**rest, TWIN_DIFF_INPUT: diff, **{k: w.astype(_WEIGHT_DTYPES[k]) for k, w in weights.items()}}
        y = _forward(args)
    with _jax.named_scope("loss_head"):
        err = _jnp.square(y.astype(_jnp.float32) - loss_target)
        return 0.5 * _jnp.sum(_jnp.mean(err, axis=-1)) if err.ndim else 0.5 * err


def _adamw(w, g, m, v):
    m = ADAM_B1 * m + (1.0 - ADAM_B1) * g
    v = ADAM_B2 * v + (1.0 - ADAM_B2) * _jnp.square(g)
    m_hat = m / (1.0 - ADAM_B1 ** ADAM_STEP)
    v_hat = v / (1.0 - ADAM_B2 ** ADAM_STEP)
    delta = -ADAM_LR * (m_hat / (_jnp.sqrt(v_hat) + ADAM_EPS) + ADAM_WD * w)
    return delta, m, v


def reference(x, ffn_norm, ffn_w_gate, ffn_w_up, ffn_w_down, mix_norm, ssd_w_in, ssd_conv_w, ssd_conv_b, ssd_dt_bias, ssd_a_log, ssd_d, ssd_norm, ssd_w_out, pool_w_in, pool_w_group, pool_scale, pool_w_out, final_norm, loss_target, m_ffn_norm, m_ffn_w_gate, m_ffn_w_up, m_ffn_w_down, m_mix_norm, m_ssd_w_in, m_ssd_conv_w, m_ssd_conv_b, m_ssd_dt_bias, m_ssd_a_log, m_ssd_d, m_ssd_norm, m_ssd_w_out, m_pool_w_in, m_pool_w_group, m_pool_scale, m_pool_w_out, m_final_norm, v_ffn_norm, v_ffn_w_gate, v_ffn_w_up, v_ffn_w_down, v_mix_norm, v_ssd_w_in, v_ssd_conv_w, v_ssd_conv_b, v_ssd_dt_bias, v_ssd_a_log, v_ssd_d, v_ssd_norm, v_ssd_w_out, v_pool_w_in, v_pool_w_group, v_pool_scale, v_pool_w_out, v_final_norm):
    given = dict(x=x, ffn_norm=ffn_norm, ffn_w_gate=ffn_w_gate, ffn_w_up=ffn_w_up, ffn_w_down=ffn_w_down, mix_norm=mix_norm, ssd_w_in=ssd_w_in, ssd_conv_w=ssd_conv_w, ssd_conv_b=ssd_conv_b, ssd_dt_bias=ssd_dt_bias, ssd_a_log=ssd_a_log, ssd_d=ssd_d, ssd_norm=ssd_norm, ssd_w_out=ssd_w_out, pool_w_in=pool_w_in, pool_w_group=pool_w_group, pool_scale=pool_scale, pool_w_out=pool_w_out, final_norm=final_norm, loss_target=loss_target, m_ffn_norm=m_ffn_norm, m_ffn_w_gate=m_ffn_w_gate, m_ffn_w_up=m_ffn_w_up, m_ffn_w_down=m_ffn_w_down, m_mix_norm=m_mix_norm, m_ssd_w_in=m_ssd_w_in, m_ssd_conv_w=m_ssd_conv_w, m_ssd_conv_b=m_ssd_conv_b, m_ssd_dt_bias=m_ssd_dt_bias, m_ssd_a_log=m_ssd_a_log, m_ssd_d=m_ssd_d, m_ssd_norm=m_ssd_norm, m_ssd_w_out=m_ssd_w_out, m_pool_w_in=m_pool_w_in, m_pool_w_group=m_pool_w_group, m_pool_scale=m_pool_scale, m_pool_w_out=m_pool_w_out, m_final_norm=m_final_norm, v_ffn_norm=v_ffn_norm, v_ffn_w_gate=v_ffn_w_gate, v_ffn_w_up=v_ffn_w_up, v_ffn_w_down=v_ffn_w_down, v_mix_norm=v_mix_norm, v_ssd_w_in=v_ssd_w_in, v_ssd_conv_w=v_ssd_conv_w, v_ssd_conv_b=v_ssd_conv_b, v_ssd_dt_bias=v_ssd_dt_bias, v_ssd_a_log=v_ssd_a_log, v_ssd_d=v_ssd_d, v_ssd_norm=v_ssd_norm, v_ssd_w_out=v_ssd_w_out, v_pool_w_in=v_pool_w_in, v_pool_w_group=v_pool_w_group, v_pool_scale=v_pool_scale, v_pool_w_out=v_pool_w_out, v_final_norm=v_final_norm)
    weights = {n: given[n] for n in TWIN_WEIGHTS}
    shared = {n: given[n] for n in SHARED_INPUTS}
    per_example = {n: given[n] for n in ['x']}
    grad_fn = _jax.value_and_grad(_loss, argnums=(0, 1))

    def one_microbatch(ex, loss_target):
        ex = dict(ex)
        diff = ex.pop(TWIN_DIFF_INPUT)
        return grad_fn(weights, diff, {**shared, **ex}, loss_target)

    if N_MICROBATCH == 1:
        loss, (grad_w, grad_x) = one_microbatch(per_example, given["loss_target"])
    else:
        def body(carry, xs):
            loss_sum, grad_sum = carry
            l_k, (gw_k, gx_k) = one_microbatch(xs[0], xs[1])
            with _jax.named_scope("update"):
                return (loss_sum + l_k, _jax.tree.map(_jnp.add, grad_sum, gw_k)), gx_k

        init = (_jnp.zeros((), _jnp.float32), _jax.tree.map(_jnp.zeros_like, weights))
        (loss, grad_w), grad_x = _jax.lax.scan(body, init, (per_example, given["loss_target"]))
    with _jax.named_scope("update"):
        delta_w, new_m, new_v = {}, {}, {}
        for n in TWIN_WEIGHTS:
            delta_w[n], new_m[n], new_v[n] = _adamw(weights[n], grad_w[n], given["m_" + n], given["v_" + n])
    return (loss, grad_x, *[grad_w[n] for n in TWIN_WEIGHTS], *[delta_w[n] for n in TWIN_WEIGHTS],
            *[new_m[n] for n in TWIN_WEIGHTS], *[new_v[n] for n in TWIN_WEIGHTS])
```

```python
import functools

import jax
import jax.numpy as jnp
from jax import lax
from jax.experimental import pallas as pl
from jax.experimental.pallas import tpu as pltpu

F32 = jnp.float32
BF16 = jnp.bfloat16
EPS = 1e-6
MESH = pl.DeviceIdType.MESH

SSD_CHUNK = 128
SSD_STATE = 128
SSD_HEAD_DIM = 64
HEADS_PER_GROUP = 8
GROUP_W = HEADS_PER_GROUP * SSD_HEAD_DIM
CONV_W = 5
POOL_WINDOWS = (2, 4, 8, 16)
N_CHIPS = 4

ADAM_LR = 0.001
ADAM_B1 = 0.9
ADAM_B2 = 0.999
ADAM_EPS = 1e-08
ADAM_WD = 0.01
ADAM_STEP = 10

VMEM_LIMIT = 56 * 1024 * 1024
LANE = 128


def _cp(*sem):
    return pltpu.CompilerParams(dimension_semantics=sem, vmem_limit_bytes=VMEM_LIMIT)


def _tile(dim, pref, unit=LANE):
    if dim <= pref:
        return dim
    t = (pref // unit) * unit
    while t >= unit:
        if dim % t == 0:
            return t
        t -= unit
    return dim


def _sigmoid(v):
    return 1.0 / (1.0 + jnp.exp(-v))


def _dot(a, b):
    return jnp.dot(a, b, preferred_element_type=F32)


def _dot_nt(a, b):
    return lax.dot_general(a, b, (((1,), (1,)), ((), ())), preferred_element_type=F32)


def _split3(v):
    h1 = v.astype(BF16)
    r1 = v - h1.astype(F32)
    h2 = r1.astype(BF16)
    h3 = (r1 - h2.astype(F32)).astype(BF16)
    return h1, h2, h3


def _dot_exact01(m01, v):
    mb = m01.astype(BF16)
    h1, h2, h3 = _split3(v)
    return _dot(mb, h1) + _dot(mb, h2) + _dot(mb, h3)


def _dot_nt_exact01(m01, v):
    mb = m01.astype(BF16)
    h1, h2, h3 = _split3(v)
    return _dot_nt(mb, h1) + _dot_nt(mb, h2) + _dot_nt(mb, h3)


def _mm(a, b, *, name, ta=False, tb=False, a_sel=(), b_sel=(), pair2=None, add=None, scale=1.0,
        out_dtype=F32, tm=1024, tn=1024, tk=2048):
    am, ak = a.shape[-2:][::-1] if ta else a.shape[-2:]
    bk, bn = b.shape[-2:][::-1] if tb else b.shape[-2:]
    assert ak == bk, (a.shape, b.shape, ta, tb)
    m_dim, n_dim, k_dim = am, bn, ak
    tm, tn, tk = _tile(m_dim, tm), _tile(n_dim, tn), _tile(k_dim, tk)
    nk = k_dim // tk
    grid = (n_dim // tn, m_dim // tm, nk)

    def a_spec(sel):
        lead = (None,) * len(sel)
        if ta:
            return pl.BlockSpec(lead + (tk, tm), lambda n, m, k: tuple(sel) + (k, m))
        return pl.BlockSpec(lead + (tm, tk), lambda n, m, k: tuple(sel) + (m, k))

    def b_spec(sel):
        lead = (None,) * len(sel)
        if tb:
            return pl.BlockSpec(lead + (tn, tk), lambda n, m, k: tuple(sel) + (n, k))
        return pl.BlockSpec(lead + (tk, tn), lambda n, m, k: tuple(sel) + (k, n))

    ins, specs = [a, b], [a_spec(a_sel), b_spec(b_sel)]
    if pair2 is not None:
        a2, b2, a2_sel, b2_sel = pair2
        ins += [a2, b2]
        specs += [a_spec(a2_sel), b_spec(b2_sel)]
    if add is not None:
        ins.append(add)
        specs.append(pl.BlockSpec((tm, tn), lambda n, m, k: (m, n)))
    dn = (((0 if ta else 1,), (1 if tb else 0,)), ((), ()))
    n_pairs = 2 if pair2 is not None else 1

    def body(*refs):
        pairs = [(refs[2 * i], refs[2 * i + 1]) for i in range(n_pairs)]
        pos = 2 * n_pairs
        add_ref = None
        if add is not None:
            add_ref = refs[pos]
            pos += 1
        o_ref = refs[pos]
        acc_ref = refs[pos + 1] if nk > 1 else None

        def prod():
            tot = None
            for ar, br in pairs:
                p = lax.dot_general(ar[...].astype(BF16), br[...].astype(BF16), dn, preferred_element_type=F32)
                tot = p if tot is None else tot + p
            return tot

        def finish(r):
            if scale != 1.0:
                r = r * scale
            if add_ref is not None:
                r = add_ref[...] + r
            o_ref[...] = r.astype(out_dtype)

        if nk == 1:
            finish(prod())
        else:
            k = pl.program_id(2)

            @pl.when(k == 0)
            def _():
                acc_ref[...] = jnp.zeros_like(acc_ref)

            acc_ref[...] += prod()

            @pl.when(k == nk - 1)
            def _():
                finish(acc_ref[...])

    return pl.pallas_call(
        body, name=name, grid=grid, in_specs=specs,
        out_specs=pl.BlockSpec((tm, tn), lambda n, m, k: (m, n)),
        out_shape=jax.ShapeDtypeStruct((m_dim, n_dim), out_dtype),
        scratch_shapes=[pltpu.VMEM((tm, tn), F32)] if nk > 1 else [],
        compiler_params=_cp("parallel", "parallel", "arbitrary"),
    )(*ins)


def _rmsnorm(x, g):
    t_dim, d = x.shape
    tr = _tile(t_dim, 256, 8)

    def body(x_ref, g_ref, o_ref):
        xv = x_ref[...]
        r = lax.rsqrt(jnp.mean(xv * xv, axis=-1, keepdims=True) + EPS)
        o_ref[...] = (xv * r * g_ref[...]).astype(BF16)

    return pl.pallas_call(
        body, name="rmsnorm", grid=(t_dim // tr,),
        in_specs=[pl.BlockSpec((tr, d), lambda i: (i, 0)), pl.BlockSpec((1, d), lambda i: (0, 0))],
        out_specs=pl.BlockSpec((tr, d), lambda i: (i, 0)),
        out_shape=jax.ShapeDtypeStruct((t_dim, d), BF16),
        compiler_params=_cp("parallel"),
    )(x, g)


def _rmsnorm_bwd(x, g, dh, dres):
    t_dim, d = x.shape
    tr = _tile(t_dim, 256, 8)

    def body(x_ref, g_ref, dh_ref, dres_ref, dx_ref, dg_ref):
        i = pl.program_id(0)
        xv = x_ref[...]
        r = lax.rsqrt(jnp.mean(xv * xv, axis=-1, keepdims=True) + EPS)
        n = xv * r
        dhv = dh_ref[...]
        dn = dhv * g_ref[...]

        @pl.when(i == 0)
        def _():
            dg_ref[...] = jnp.zeros_like(dg_ref)

        dg_ref[...] += jnp.sum(dhv * n, axis=0, keepdims=True)
        dx_ref[...] = dres_ref[...] + r * (dn - n * jnp.mean(dn * n, axis=-1, keepdims=True))

    row = pl.BlockSpec((tr, d), lambda i: (i, 0))
    vec = pl.BlockSpec((1, d), lambda i: (0, 0))
    return pl.pallas_call(
        body, name="rmsnorm_bwd", grid=(t_dim // tr,),
        in_specs=[row, vec, row, row], out_specs=[row, vec],
        out_shape=[jax.ShapeDtypeStruct((t_dim, d), F32), jax.ShapeDtypeStruct((1, d), F32)],
        compiler_params=_cp("arbitrary"),
    )(x, g, dh, dres)


def _loss_head(x, g, target):
    t_dim, d = x.shape
    tr = _tile(t_dim, 256, 8)

    def body(x_ref, g_ref, t_ref, loss_ref, dx_ref, dg_ref):
        i = pl.program_id(0)
        xv = x_ref[...]
        gv = g_ref[...]
        r = lax.rsqrt(jnp.mean(xv * xv, axis=-1, keepdims=True) + EPS)
        n = xv * r
        err = n * gv - t_ref[...]

        @pl.when(i == 0)
        def _():
            dg_ref[...] = jnp.zeros_like(dg_ref)
            loss_ref[...] = jnp.zeros_like(loss_ref)

        per_tok = jnp.mean(err * err, axis=-1, keepdims=True)
        loss_ref[...] += 0.5 * jnp.sum(per_tok, axis=0, keepdims=True)
        dy = err * (1.0 / d)
        dn = dy * gv
        dg_ref[...] += jnp.sum(dy * n, axis=0, keepdims=True)
        dx_ref[...] = r * (dn - n * jnp.mean(dn * n, axis=-1, keepdims=True))

    row = pl.BlockSpec((tr, d), lambda i: (i, 0))
    vec = pl.BlockSpec((1, d), lambda i: (0, 0))
    one = pl.BlockSpec((1, 1), lambda i: (0, 0))
    return pl.pallas_call(
        body, name="loss_head", grid=(t_dim // tr,),
        in_specs=[row, vec, row], out_specs=[one, row, vec],
        out_shape=[jax.ShapeDtypeStruct((1, 1), F32), jax.ShapeDtypeStruct((t_dim, d), F32),
                   jax.ShapeDtypeStruct((1, d), F32)],
        compiler_params=_cp("arbitrary"),
    )(x, g, target)


def _ffn_in(h, wg, wu, half):
    t_dim, d = h.shape
    f = wg.shape[-1]
    tm, tn = _tile(t_dim, 512), _tile(f, 1408)

    def body(h_ref, wg_ref, wu_ref, g_ref, u_ref, a_ref):
        hv = h_ref[...]
        gv = _dot(hv, wg_ref[...])
        uv = _dot(hv, wu_ref[...])
        g_ref[...] = gv.astype(BF16)
        u_ref[...] = uv.astype(BF16)
        a_ref[...] = (gv * _sigmoid(gv) * uv).astype(BF16)

    wspec = pl.BlockSpec((None, d, tn), lambda n, m: (half, 0, n))
    ospec = pl.BlockSpec((tm, tn), lambda n, m: (m, n))
    oshape = jax.ShapeDtypeStruct((t_dim, f), BF16)
    return pl.pallas_call(
        body, name="ffn_in", grid=(f // tn, t_dim // tm),
        in_specs=[pl.BlockSpec((tm, d), lambda n, m: (m, 0)), wspec, wspec],
        out_specs=[ospec, ospec, ospec], out_shape=[oshape, oshape, oshape],
        compiler_params=_cp("parallel", "parallel"),
    )(h, wg, wu)


def _ffn_bwd_act(dx, wd, g, u, half):
    t_dim, d = dx.shape
    f = wd.shape[-2]
    tm, tn = _tile(t_dim, 512), _tile(f, 1408)

    def body(dx_ref, wd_ref, g_ref, u_ref, dg_ref, du_ref):
        da = 0.5 * _dot_nt(dx_ref[...].astype(BF16), wd_ref[...])
        gv = g_ref[...].astype(F32)
        uv = u_ref[...].astype(F32)
        s = _sigmoid(gv)
        dg_ref[...] = (da * uv * (s * (1.0 + gv * (1.0 - s)))).astype(BF16)
        du_ref[...] = (da * gv * s).astype(BF16)

    tile = pl.BlockSpec((tm, tn), lambda n, m: (m, n))
    oshape = jax.ShapeDtypeStruct((t_dim, f), BF16)
    return pl.pallas_call(
        body, name="ffn_bwd_act", grid=(f // tn, t_dim // tm),
        in_specs=[pl.BlockSpec((tm, d), lambda n, m: (m, 0)),
                  pl.BlockSpec((None, tn, d), lambda n, m: (half, n, 0)), tile, tile],
        out_specs=[tile, tile], out_shape=[oshape, oshape],
        compiler_params=_cp("parallel", "parallel"),
    )(dx, wd, g, u)


def _ffn_fwd(x, norm_g, wg, wu, wd, half):
    h = _rmsnorm(x, norm_g)
    g, u, a = _ffn_in(h, wg, wu, half)
    x_new = _mm(a, wd, name="ffn_out", b_sel=(half,), add=x, scale=0.5, tk=1408)
    return x_new, (x, h, g, u, a)


def _ffn_bwd(dx, saved, norm_g, wg, wu, wd, half):
    x, h, g, u, a = saved
    dg, du = _ffn_bwd_act(dx, wd, g, u, half)
    d_wd = _mm(a, dx, name="ffn_dwd", ta=True, scale=0.5, out_dtype=BF16, tm=1408, tn=1024)
    d_wg = _mm(h, dg, name="ffn_dwgu", ta=True, out_dtype=BF16, tm=1024, tn=1408)
    d_wu = _mm(h, du, name="ffn_dwgu", ta=True, out_dtype=BF16, tm=1024, tn=1408)
    dh = _mm(dg, wg, name="ffn_dh", tb=True, b_sel=(half,), pair2=(du, wu, (), (half,)), tk=1408)
    dx_new, dnorm = _rmsnorm_bwd(x, norm_g, dh, dx)
    return dx_new, dnorm, d_wg, d_wu, d_wd


def _shifted(v, off, t_idx):
    if off == 0:
        return v
    t_dim = v.shape[0]
    sh = pltpu.roll(v, (-off) % t_dim, 0)
    valid = jnp.logical_and(t_idx + off >= 0, t_idx + off < t_dim)
    return jnp.where(valid, sh, 0.0)


def _conv_pre(u, w_ref, b_ref, t_idx):
    acc = jnp.zeros_like(u) + b_ref[...]
    shifted = []
    for k in range(CONV_W):
        sh = _shifted(u, k - CONV_W // 2, t_idx)
        shifted.append(sh)
        acc = acc + w_ref[k:k + 1, :] * sh
    return acc, shifted


def _conv_silu(proj, conv_w, conv_b, col0):
    t_dim = proj.shape[0]
    cd = conv_w.shape[-1]
    cb = _tile(cd, 256)
    assert col0 % cb == 0

    def body(u_ref, w_ref, b_ref, o_ref):
        t_idx = lax.broadcasted_iota(jnp.int32, (t_dim, cb), 0)
        pre, _ = _conv_pre(u_ref[...], w_ref, b_ref, t_idx)
        o_ref[...] = pre * _sigmoid(pre)

    return pl.pallas_call(
        body, name="conv_silu", grid=(cd // cb,),
        in_specs=[pl.BlockSpec((t_dim, cb), lambda j: (0, col0 // cb + j)),
                  pl.BlockSpec((CONV_W, cb), lambda j: (0, j)), pl.BlockSpec((1, cb), lambda j: (0, j))],
        out_specs=pl.BlockSpec((t_dim, cb), lambda j: (0, j)),
        out_shape=jax.ShapeDtypeStruct((t_dim, cd), F32),
        compiler_params=_cp("parallel"),
    )(proj, conv_w, conv_b)


def _conv_silu_bwd(proj, conv_w, conv_b, dact2, col0):
    t_dim = proj.shape[0]
    cd = conv_w.shape[-1]
    cb = _tile(cd, 256)

    def body(u_ref, w_ref, b_ref, da_ref, du_ref, dwb_ref):
        t_idx = lax.broadcasted_iota(jnp.int32, (t_dim, cb), 0)
        pre, shifted = _conv_pre(u_ref[...], w_ref, b_ref, t_idx)
        s = _sigmoid(pre)
        dpre = (da_ref[0] + da_ref[1]) * (s * (1.0 + pre * (1.0 - s)))
        du = jnp.zeros_like(dpre)
        for k in range(CONV_W):
            du = du + w_ref[k:k + 1, :] * _shifted(dpre, -(k - CONV_W // 2), t_idx)
            dwb_ref[k:k + 1, :] = jnp.sum(dpre * shifted[k], axis=0, keepdims=True)
        dwb_ref[CONV_W:CONV_W + 1, :] = jnp.sum(dpre, axis=0, keepdims=True)
        dwb_ref[CONV_W + 1:8, :] = jnp.zeros((8 - CONV_W - 1, cb), F32)
        du_ref[...] = du.astype(BF16)

    return pl.pallas_call(
        body, name="conv_silu_bwd", grid=(cd // cb,),
        in_specs=[pl.BlockSpec((t_dim, cb), lambda j: (0, col0 // cb + j)),
                  pl.BlockSpec((CONV_W, cb), lambda j: (0, j)), pl.BlockSpec((1, cb), lambda j: (0, j)),
                  pl.BlockSpec((2, t_dim, cb), lambda j: (0, 0, j))],
        out_specs=[pl.BlockSpec((t_dim, cb), lambda j: (0, j)), pl.BlockSpec((8, cb), lambda j: (0, j))],
        out_shape=[jax.ShapeDtypeStruct((t_dim, cd), BF16), jax.ShapeDtypeStruct((8, cd), F32)],
        compiler_params=_cp("parallel"),
    )(proj, conv_w, conv_b, dact2)


def _softplus_fwd(dt_raw, bias):
    def body(r_ref, b_ref, o_ref):
        v = r_ref[...] + b_ref[...]
        o_ref[...] = jnp.maximum(v, 0.0) + jnp.log(1.0 + jnp.exp(-jnp.abs(v)))

    return pl.pallas_call(body, name="softplus", out_shape=jax.ShapeDtypeStruct(dt_raw.shape, F32))(dt_raw, bias)


def _softplus_bwd(dt_raw, bias, ddt, da, a_log):
    def body(r_ref, b_ref, ddt_ref, da_ref, al_ref, dr_ref, db_ref, dal_ref):
        dv = ddt_ref[...] * _sigmoid(r_ref[...] + b_ref[...])
        dr_ref[...] = dv.astype(BF16)
        db_ref[...] = jnp.sum(dv, axis=0, keepdims=True)
        dal_ref[...] = -da_ref[...] * jnp.exp(al_ref[...])

    vec = jax.ShapeDtypeStruct(bias.shape, F32)
    return pl.pallas_call(
        body, name="softplus_bwd",
        out_shape=[jax.ShapeDtypeStruct(dt_raw.shape, BF16), vec, vec])(dt_raw, bias, ddt, da, a_log)


def _chunk_setup(d, dt_ref, al_ref):
    q = SSD_CHUNK
    ii = lax.broadcasted_iota(jnp.int32, (q, q), 0)
    jj = lax.broadcasted_iota(jnp.int32, (q, q), 1)
    sgn = 1 - 2 * d
    mask = (jj - ii) * sgn <= 0
    mask_t = (ii - jj) * sgn <= 0
    m01 = mask.astype(F32)
    m01_t = mask_t.astype(F32)
    dt = dt_ref[...]
    a = -jnp.exp(al_ref[...])
    dta = dt * a
    cs = _dot_exact01(m01, dta)
    tot = jnp.sum(dta, axis=0, keepdims=True)
    return mask, mask_t, m01, m01_t, dt, a, dta, cs, tot


def _ssd_specs(t_dim, di, g_cnt, chunk_of):
    q, ns = SSD_CHUNK, SSD_STATE
    x_spec = pl.BlockSpec((q, GROUP_W), lambda d, g, c: (chunk_of(d, c), g))
    b_spec = pl.BlockSpec((q, ns), lambda d, g, c: (chunk_of(d, c), di // ns + g))
    c_spec = pl.BlockSpec((q, ns), lambda d, g, c: (chunk_of(d, c), di // ns + g_cnt + g))
    dt_spec = pl.BlockSpec((None, None, q, HEADS_PER_GROUP), lambda d, g, c: (d, g, chunk_of(d, c), 0))
    al_spec = pl.BlockSpec((None, None, 1, HEADS_PER_GROUP), lambda d, g, c: (d, g, 0, 0))
    return x_spec, b_spec, c_spec, dt_spec, al_spec


def _ssd_fwd(xbc, dt4, al4):
    t_dim = xbc.shape[0]
    g_cnt = dt4.shape[1]
    di = g_cnt * GROUP_W
    q, ns, p = SSD_CHUNK, SSD_STATE, SSD_HEAD_DIM
    nc = t_dim // q

    def chunk_of(d, c):
        return c + d * (nc - 1 - 2 * c)

    def body(x_ref, b_ref, c_ref, dt_ref, al_ref, y_ref, hin_ref, h_sc, xd_sc):
        d = pl.program_id(0)
        c = pl.program_id(2)

        @pl.when(c == 0)
        def _():
            h_sc[...] = jnp.zeros_like(h_sc)

        mask, mask_t, m01, m01_t, dt, a, dta, cs, tot = _chunk_setup(d, dt_ref, al_ref)
        xv = x_ref[...]
        bb = b_ref[...].astype(BF16)
        cb16 = c_ref[...].astype(BF16)
        bt16 = b_ref[...].T.astype(BF16)
        cb = _dot(cb16, bt16)
        hin = h_sc[...]
        hin_ref[...] = hin
        ch = _dot(cb16, hin.astype(BF16))
        for h in range(HEADS_PER_GROUP):
            sl = slice(h * p, (h + 1) * p)
            cs_col = cs[:, h:h + 1]
            cs_row = jnp.sum(dta[:, h:h + 1] * m01_t, axis=0, keepdims=True)
            lmat = jnp.exp(jnp.where(mask, cs_col - cs_row, -1e30))
            xdt = xv[:, sl] * dt[:, h:h + 1]
            y_diag = _dot((cb * lmat).astype(BF16), xdt.astype(BF16))
            y_ref[:, sl] = y_diag + ch[:, sl] * jnp.exp(cs_col)
            xd_sc[:, sl] = xdt * jnp.exp(tot[:, h:h + 1] - cs_col)
        st = _dot(bt16, xd_sc[...].astype(BF16))
        for h in range(HEADS_PER_GROUP):
            sl = slice(h * p, (h + 1) * p)
            h_sc[:, sl] = hin[:, sl] * jnp.exp(tot[:, h:h + 1]) + st[:, sl]

    x_spec, b_spec, c_spec, dt_spec, al_spec = _ssd_specs(t_dim, di, g_cnt, chunk_of)
    return pl.pallas_call(
        body, name="ssd_fwd", grid=(2, g_cnt, nc),
        in_specs=[x_spec, b_spec, c_spec, dt_spec, al_spec],
        out_specs=[pl.BlockSpec((None, q, GROUP_W), lambda d, g, c: (d, chunk_of(d, c), g)),
                   pl.BlockSpec((None, None, None, ns, GROUP_W), lambda d, g, c: (d, g, chunk_of(d, c), 0, 0))],
        out_shape=[jax.ShapeDtypeStruct((2, t_dim, di), F32),
                   jax.ShapeDtypeStruct((2, g_cnt, nc, ns, GROUP_W), F32)],
        scratch_shapes=[pltpu.VMEM((ns, GROUP_W), F32), pltpu.VMEM((q, GROUP_W), F32)],
        compiler_params=_cp("parallel", "parallel", "arbitrary"),
    )(xbc, xbc, xbc, dt4, al4)


def _ssd_bwd(xbc, dt4, al4, hin_all, dy, dvec):
    t_dim = xbc.shape[0]
    g_cnt = dt4.shape[1]
    di = g_cnt * GROUP_W
    q, ns, p, hg = SSD_CHUNK, SSD_STATE, SSD_HEAD_DIM, HEADS_PER_GROUP
    nc = t_dim // q

    def chunk_of(d, c):
        return (nc - 1 - c) + d * (2 * c - nc + 1)

    def body(x_ref, b_ref, c_ref, dt_ref, al_ref, hin_ref, dy_ref, dv_ref,
             dx_ref, db_ref, dc_ref, ddt_ref, da_ref, g_sc, xd_sc, dye_sc, col_sc, zrow_sc, tot_sc):
        d = pl.program_id(0)
        c = pl.program_id(2)

        @pl.when(c == 0)
        def _():
            g_sc[...] = jnp.zeros_like(g_sc)
            da_ref[...] = jnp.zeros_like(da_ref)

        mask, mask_t, m01, m01_t, dt, a, dta, cs, tot = _chunk_setup(d, dt_ref, al_ref)
        xv = x_ref[...]
        dyv = dy_ref[...]
        bb = b_ref[...].astype(BF16)
        cb16 = c_ref[...].astype(BF16)
        bt16 = b_ref[...].T.astype(BF16)
        ct16 = c_ref[...].T.astype(BF16)
        cb = _dot(cb16, bt16)
        cbt = _dot(bb, ct16)
        hin = hin_ref[...]
        hin16 = hin.astype(BF16)
        gst = g_sc[...]
        gst16 = gst.astype(BF16)
        ch = _dot(cb16, hin16)
        wst = _dot(bb, gst16)
        skip = jnp.where(d == 0, 1.0, 0.0)
        dcb = jnp.zeros((q, q), F32)
        for h in range(hg):
            sl = slice(h * p, (h + 1) * p)
            cs_col = cs[:, h:h + 1]
            cs_row = jnp.sum(dta[:, h:h + 1] * m01_t, axis=0, keepdims=True)
            lmat = jnp.exp(jnp.where(mask, cs_col - cs_row, -1e30))
            lmat_t = jnp.exp(jnp.where(mask_t, cs_row - cs_col, -1e30))
            e_col = jnp.exp(cs_col)
            dec = jnp.exp(tot[:, h:h + 1] - cs_col)
            dt_h = dt[:, h:h + 1]
            x_h = xv[:, sl]
            xdt = x_h * dt_h
            xdt16 = xdt.astype(BF16)
            dy_h = dyv[:, sl]
            dy16 = dy_h.astype(BF16)
            dye_sc[:, sl] = dy_h * e_col
            xd_sc[:, sl] = xdt * dec
            dcs = jnp.sum(dy_h * ch[:, sl] * e_col, axis=-1, keepdims=True)
            w_h = wst[:, sl]
            dxdt = w_h * dec
            t1 = jnp.sum(w_h * xdt, axis=-1, keepdims=True) * dec
            dcs = dcs - t1
            dtot = jnp.sum(t1, axis=0, keepdims=True)
            m_f = cb * lmat
            dxdt = dxdt + _dot((cbt * lmat_t).astype(BF16), dy16)
            dm = _dot_nt(dy16, xdt16)
            z = dm * m_f
            dcb = dcb + dm * lmat
            dcs = dcs + jnp.sum(z, axis=-1, keepdims=True)
            zrow_sc[h:h + 1, :] = jnp.sum(z, axis=0, keepdims=True)
            t_h = jnp.exp(tot[:, h:h + 1])
            dtot = dtot + jnp.sum(jnp.sum(gst[:, sl] * hin[:, sl], axis=-1, keepdims=True), axis=0,
                                  keepdims=True) * t_h
            dx_ref[:, sl] = dxdt * dt_h + skip * dy_h * dv_ref[:, sl]
            col_sc[:, h:h + 1] = dcs
            col_sc[:, hg + h:hg + h + 1] = jnp.sum(dxdt * x_h, axis=-1, keepdims=True)
            tot_sc[:, h:h + 1] = dtot
        dye16 = dye_sc[...].astype(BF16)
        dcb16 = dcb.astype(BF16)
        dc_ref[...] = _dot_nt(dye16, hin16) + _dot(dcb16, bb)
        db_ref[...] = _dot_nt(xd_sc[...].astype(BF16), gst16) + _dot(dcb.T.astype(BF16), cb16)
        dhin = _dot(ct16, dye16)
        for h in range(hg):
            sl = slice(h * p, (h + 1) * p)
            g_sc[:, sl] = dhin[:, sl] + gst[:, sl] * jnp.exp(tot[:, h:h + 1])
        ddta = _dot_exact01(m01_t, col_sc[:, 0:hg]) - _dot_nt_exact01(m01_t, zrow_sc[...]) + tot_sc[...]
        ddt_ref[...] = col_sc[:, hg:2 * hg] + ddta * a
        da_ref[...] += jnp.sum(ddta * dt, axis=0, keepdims=True)

    x_spec, b_spec, c_spec, dt_spec, al_spec = _ssd_specs(t_dim, di, g_cnt, chunk_of)
    hin_spec = pl.BlockSpec((None, None, None, ns, GROUP_W), lambda d, g, c: (d, g, chunk_of(d, c), 0, 0))
    dy_spec = pl.BlockSpec((q, GROUP_W), lambda d, g, c: (chunk_of(d, c), g))
    dv_spec = pl.BlockSpec((1, GROUP_W), lambda d, g, c: (0, g))
    gn = g_cnt * ns
    return pl.pallas_call(
        body, name="ssd_bwd", grid=(2, g_cnt, nc),
        in_specs=[x_spec, b_spec, c_spec, dt_spec, al_spec, hin_spec, dy_spec, dv_spec],
        out_specs=[pl.BlockSpec((None, q, GROUP_W), lambda d, g, c: (d, chunk_of(d, c), g)),
                   pl.BlockSpec((None, q, ns), lambda d, g, c: (d, chunk_of(d, c), g)),
                   pl.BlockSpec((None, q, ns), lambda d, g, c: (d, chunk_of(d, c), g)),
                   pl.BlockSpec((None, None, q, hg), lambda d, g, c: (d, g, chunk_of(d, c), 0)),
                   pl.BlockSpec((None, None, 1, hg), lambda d, g, c: (d, g, 0, 0))],
        out_shape=[jax.ShapeDtypeStruct((2, t_dim, di), F32), jax.ShapeDtypeStruct((2, t_dim, gn), F32),
                   jax.ShapeDtypeStruct((2, t_dim, gn), F32), jax.ShapeDtypeStruct((2, g_cnt, t_dim, hg), F32),
                   jax.ShapeDtypeStruct((2, g_cnt, 1, hg), F32)],
        scratch_shapes=[pltpu.VMEM((ns, GROUP_W), F32), pltpu.VMEM((q, GROUP_W), F32),
                        pltpu.VMEM((q, GROUP_W), F32), pltpu.VMEM((q, 2 * hg), F32),
                        pltpu.VMEM((hg, q), F32), pltpu.VMEM((1, hg), F32)],
        compiler_params=_cp("parallel", "parallel", "arbitrary"),
    )(xbc, xbc, xbc, dt4, al4, hin_all, dy, dvec)


def _gate_norm(y2, xbc, proj, dvec, ng):
    t_dim, di = y2.shape[1:]
    tr = _tile(t_dim, 128, 8)

    def body(y2_ref, x_ref, z_ref, dv_ref, ng_ref, o_ref):
        y = y2_ref[0] + y2_ref[1] + x_ref[...] * dv_ref[...]
        z = z_ref[...]
        v = y * z * _sigmoid(z)
        r = lax.rsqrt(jnp.mean(v * v, axis=-1, keepdims=True) + EPS)
        o_ref[...] = (v * r * ng_ref[...]).astype(BF16)

    row = pl.BlockSpec((tr, di), lambda i: (i, 0))
    vec = pl.BlockSpec((1, di), lambda i: (0, 0))
    return pl.pallas_call(
        body, name="gate_norm", grid=(t_dim // tr,),
        in_specs=[pl.BlockSpec((2, tr, di), lambda i: (0, i, 0)), row, row, vec, vec],
        out_specs=row, out_shape=jax.ShapeDtypeStruct((t_dim, di), BF16),
        compiler_params=_cp("parallel"),
    )(y2, xbc, proj, dvec, ng)


def _gate_norm_bwd(y2, xbc, proj, dvec, ng, dyn):
    t_dim, di = y2.shape[1:]
    tr = _tile(t_dim, 128, 8)

    def body(y2_ref, x_ref, z_ref, dv_ref, ng_ref, dyn_ref, dy_ref, dz_ref, dng_ref, dd_ref):
        i = pl.program_id(0)
        xv = x_ref[...]
        y = y2_ref[0] + y2_ref[1] + xv * dv_ref[...]
        z = z_ref[...]
        s = _sigmoid(z)
        v = y * z * s
        r = lax.rsqrt(jnp.mean(v * v, axis=-1, keepdims=True) + EPS)
        n = v * r
        dynv = dyn_ref[...]
        dn = dynv * ng_ref[...]
        dv = r * (dn - n * jnp.mean(dn * n, axis=-1, keepdims=True))
        dy = dv * z * s

        @pl.when(i == 0)
        def _():
            dng_ref[...] = jnp.zeros_like(dng_ref)
            dd_ref[...] = jnp.zeros_like(dd_ref)

        dng_ref[...] += jnp.sum(dynv * n, axis=0, keepdims=True)
        dd_ref[...] += jnp.sum(dy * xv, axis=0, keepdims=True)
        dy_ref[...] = dy
        dz_ref[...] = (dv * y * (s * (1.0 + z * (1.0 - s)))).astype(BF16)

    row = pl.BlockSpec((tr, di), lambda i: (i, 0))
    vec = pl.BlockSpec((1, di), lambda i: (0, 0))
    return pl.pallas_call(
        body, name="gate_norm_bwd", grid=(t_dim // tr,),
        in_specs=[pl.BlockSpec((2, tr, di), lambda i: (0, i, 0)), row, row, vec, vec, row],
        out_specs=[row, row, vec, vec],
        out_shape=[jax.ShapeDtypeStruct((t_dim, di), F32), jax.ShapeDtypeStruct((t_dim, di), BF16),
                   jax.ShapeDtypeStruct((1, di), F32), jax.ShapeDtypeStruct((1, di), F32)],
        compiler_params=_cp("arbitrary"),
    )(y2, xbc, proj, dvec, ng, dyn)


def _dt_to_groups(dt):
    t_dim, h2 = dt.shape
    g_cnt = h2 // 2 // HEADS_PER_GROUP
    return dt.reshape(t_dim, 2, g_cnt, HEADS_PER_GROUP).transpose(1, 2, 0, 3)


def _dt_from_groups(dt4):
    _, g_cnt, t_dim, hg = dt4.shape
    return dt4.transpose(2, 0, 1, 3).reshape(t_dim, 2 * g_cnt * hg)


def _ssd_mixer_fwd(x, norm_g, w_in, conv_w, conv_b, dt_bias, a_log, d_skip, ssd_norm, w_out, j):
    heads = d_skip.shape[0]
    di = heads * SSD_HEAD_DIM
    g_cnt = heads // HEADS_PER_GROUP
    cd = conv_w.shape[-1]
    hn = _rmsnorm(x, norm_g)
    proj = _mm(hn, w_in, name="ssd_proj", b_sel=(j,), tn=1152)
    xbc = _conv_silu(proj, conv_w, conv_b, di)
    dt_raw = proj[:, di + cd:]
    dt = _softplus_fwd(dt_raw, dt_bias)
    dt4 = _dt_to_groups(dt)
    al4 = a_log.reshape(2, g_cnt, 1, HEADS_PER_GROUP)
    y2, hin = _ssd_fwd(xbc, dt4, al4)
    dvec = jnp.repeat(d_skip, SSD_HEAD_DIM).reshape(1, di)
    yn = _gate_norm(y2, xbc, proj, dvec, ssd_norm)
    x_new = _mm(yn, w_out, name="ssd_out", b_sel=(j,), add=x)
    return x_new, (x, hn, proj, xbc, dt_raw, dt4, al4, y2, hin, dvec, yn)


def _ssd_mixer_bwd(dx, saved, norm_g, w_in, conv_w, conv_b, dt_bias, a_log, ssd_norm, w_out, j):
    x, hn, proj, xbc, dt_raw, dt4, al4, y2, hin, dvec, yn = saved
    di = dvec.shape[1]
    heads = di // SSD_HEAD_DIM
    d_wout = _mm(yn, dx, name="ssd_dwout", ta=True, out_dtype=BF16)
    dyn = _mm(dx, w_out, name="ssd_dyn", tb=True, b_sel=(j,))
    dy, dz, d_ng, dd_col = _gate_norm_bwd(y2, xbc, proj, dvec, ssd_norm, dyn)
    dx2, db2, dc2, ddt4, da4 = _ssd_bwd(xbc, dt4, al4, hin, dy, dvec)
    dact2 = jnp.concatenate([dx2, db2, dc2], axis=-1)
    dxbc, dwb = _conv_silu_bwd(proj, conv_w, conv_b, dact2, di)
    ddt_raw, d_bias, d_alog = _softplus_bwd(dt_raw, dt_bias, _dt_from_groups(ddt4), da4.reshape(1, 2 * heads), a_log)
    dproj = jnp.concatenate([dz, dxbc, ddt_raw], axis=-1)
    d_win = _mm(hn, dproj, name="ssd_dwin", ta=True, out_dtype=BF16, tn=1152)
    dhn = _mm(dproj, w_in, name="ssd_dhn", tb=True, b_sel=(j,), tk=1152)
    dx_new, d_norm = _rmsnorm_bwd(x, norm_g, dhn, dx)
    small = dict(mix_norm=d_norm, conv_w=dwb[:CONV_W], conv_b=dwb[CONV_W:CONV_W + 1], dt_bias=d_bias, a_log=d_alog,
                 ssd_d=dd_col.reshape(heads, SSD_HEAD_DIM).sum(axis=1), ssd_norm=d_ng)
    return dx_new, small, d_win, d_wout


def _pool_count(t_idx, w, t_dim):
    hi = jnp.minimum(t_idx + w // 2, t_dim)
    lo = jnp.maximum(t_idx - w // 2, 0)
    return (hi - lo).astype(F32)


def _pool_mix(u, transpose):
    t_dim, d = u.shape
    gd = d // len(POOL_WINDOWS)
    cb = _tile(gd, 256)
    per = gd // cb

    def body(u_ref, o_ref):
        gi = pl.program_id(0)
        t_idx = lax.broadcasted_iota(jnp.int32, (t_dim, cb), 0)
        uv = u_ref[...]
        for widx, w in enumerate(POOL_WINDOWS):
            @pl.when(gi == widx)
            def _(w=w):
                cnt = _pool_count(t_idx, w, t_dim)
                src = uv / cnt if transpose else uv
                acc = jnp.zeros_like(uv)
                for k in range(-(w // 2), w // 2):
                    acc = acc + _shifted(src, -k if transpose else k, t_idx)
                res = acc - uv if transpose else acc / cnt - uv
                o_ref[...] = res.astype(BF16)

    spec = pl.BlockSpec((t_dim, cb), lambda gi, j: (0, gi * per + j))
    return pl.pallas_call(
        body, name="pool_mix_t" if transpose else "pool_mix", grid=(len(POOL_WINDOWS), per),
        in_specs=[spec], out_specs=spec, out_shape=jax.ShapeDtypeStruct((t_dim, d), BF16),
        compiler_params=_cp("parallel", "parallel"),
    )(u)


def _pool_group(mix, wgrp, scale, j):
    t_dim, d = mix.shape
    gd = wgrp.shape[-1]
    tm = _tile(t_dim, 512)

    def body(m_ref, w_ref, s_ref, v_ref, vs_ref):
        v = _dot(m_ref[...], w_ref[...])
        v_ref[...] = v
        vs_ref[...] = (v * s_ref[...]).astype(BF16)

    tile = pl.BlockSpec((tm, gd), lambda gi, m: (m, gi))
    return pl.pallas_call(
        body, name="pool_group", grid=(d // gd, t_dim // tm),
        in_specs=[tile, pl.BlockSpec((None, None, gd, gd), lambda gi, m: (j, gi, 0, 0)),
                  pl.BlockSpec((1, gd), lambda gi, m: (0, gi))],
        out_specs=[tile, tile],
        out_shape=[jax.ShapeDtypeStruct((t_dim, d), F32), jax.ShapeDtypeStruct((t_dim, d), BF16)],
        compiler_params=_cp("parallel", "parallel"),
    )(mix, wgrp, scale)


def _pool_group_bwd(dvs, v, mix, wgrp, scale, j):
    t_dim, d = mix.shape
    gd = wgrp.shape[-1]
    n_g = d // gd
    tm = _tile(t_dim, 512)
    nm = t_dim // tm

    def body(dvs_ref, v_ref, m_ref, w_ref, s_ref, dmix_ref, ds_ref, dw_ref, acc_ref):
        m = pl.program_id(1)
        dvsv = dvs_ref[...]

        @pl.when(m == 0)
        def _():
            ds_ref[...] = jnp.zeros_like(ds_ref)
            acc_ref[...] = jnp.zeros_like(acc_ref)

        ds_ref[...] += jnp.sum(dvsv * v_ref[...], axis=0, keepdims=True)
        dv16 = (dvsv * s_ref[...]).astype(BF16)
        dmix_ref[...] = _dot_nt(dv16, w_ref[...])
        acc_ref[...] += _dot(m_ref[...].T, dv16)

        @pl.when(m == nm - 1)
        def _():
            dw_ref[...] = acc_ref[...].astype(BF16)

    tile = pl.BlockSpec((tm, gd), lambda gi, m: (m, gi))
    vec = pl.BlockSpec((1, gd), lambda gi, m: (0, gi))
    return pl.pallas_call(
        body, name="pool_group_bwd", grid=(n_g, nm),
        in_specs=[tile, tile, tile, pl.BlockSpec((None, None, gd, gd), lambda gi, m: (j, gi, 0, 0)), vec],
        out_specs=[tile, vec, pl.BlockSpec((None, gd, gd), lambda gi, m: (gi, 0, 0))],
        out_shape=[jax.ShapeDtypeStruct((t_dim, d), F32), jax.ShapeDtypeStruct((1, d), F32),
                   jax.ShapeDtypeStruct((n_g, gd, gd), BF16)],
        scratch_shapes=[pltpu.VMEM((gd, gd), F32)],
        compiler_params=_cp("parallel", "arbitrary"),
    )(dvs, v, mix, wgrp, scale)


def _pool_mixer_fwd(x, norm_g, w_in, wgrp, scale, w_out, j):
    hn = _rmsnorm(x, norm_g)
    u = _mm(hn, w_in, name="pool_u", b_sel=(j,))
    mix = _pool_mix(u, False)
    v, vs = _pool_group(mix, wgrp, scale, j)
    x_new = _mm(vs, w_out, name="pool_out", b_sel=(j,), add=x)
    return x_new, (x, hn, mix, v, vs)


def _pool_mixer_bwd(dx, saved, norm_g, w_in, wgrp, scale, w_out, j):
    x, hn, mix, v, vs = saved
    d_wout = _mm(vs, dx, name="pool_dw", ta=True, out_dtype=BF16)
    dvs = _mm(dx, w_out, name="pool_dvs", tb=True, b_sel=(j,))
    dmix, d_scale, d_wgrp = _pool_group_bwd(dvs, v, mix, wgrp, scale, j)
    du = _pool_mix(dmix, True)
    d_win = _mm(hn, du, name="pool_dw", ta=True, out_dtype=BF16)
    dhn = _mm(du, w_in, name="pool_dhn", tb=True, b_sel=(j,))
    dx_new, d_norm = _rmsnorm_bwd(x, norm_g, dhn, dx)
    return dx_new, d_norm, d_scale, d_win, d_wgrp, d_wout


def _local_step(x, target, big, small):
    depth = len(big["ffn_gate"])
    d = x.shape[1]
    saved = []
    for i in range(depth):
        j = i // 2
        x, s0 = _ffn_fwd(x, small["ffn_norm"][i, 0][None], big["ffn_gate"][i], big["ffn_up"][i], big["ffn_down"][i], 0)
        mg = small["mix_norm"][i][None]
        if i % 2 == 0:
            heads = small["ssd_d"].shape[1]
            x, s1 = _ssd_mixer_fwd(x, mg, big["ssd_w_in"], small["ssd_conv_w"][j], small["ssd_conv_b"][j][None],
                                   small["ssd_dt_bias"][j].reshape(1, 2 * heads), small["ssd_a_log"][j].reshape(1, 2 * heads),
                                   small["ssd_d"][j], small["ssd_norm"][j][None], big["ssd_w_out"], j)
        else:
            x, s1 = _pool_mixer_fwd(x, mg, big["pool_w_in"], big["pool_w_group"], small["pool_scale"][j][None],
                                    big["pool_w_out"], j)
        x, s2 = _ffn_fwd(x, small["ffn_norm"][i, 1][None], big["ffn_gate"][i], big["ffn_up"][i], big["ffn_down"][i], 1)
        saved.append((s0, s1, s2))
    loss, dx, d_final = _loss_head(x, small["final_norm"][None], target)

    gbig = {k: {} for k in ("ffn_gate", "ffn_up", "ffn_down", "ssd_w_in", "ssd_w_out", "pool_w_in", "pool_w_group",
                            "pool_w_out")}
    gs = {k: {} for k in ("ffn_norm", "mix_norm", "ssd_conv_w", "ssd_conv_b", "ssd_dt_bias", "ssd_a_log", "ssd_d",
                          "ssd_norm", "pool_scale")}
    for i in reversed(range(depth)):
        j = i // 2
        s0, s1, s2 = saved[i]
        for half, sv in ((1, s2), (0, None)):
            if half == 0:
                sv = s0
                mg = small["mix_norm"][i][None]
                if i % 2 == 0:
                    heads = small["ssd_d"].shape[1]
                    dx, sm, d_win, d_wout = _ssd_mixer_bwd(
                        dx, s1, mg, big["ssd_w_in"], small["ssd_conv_w"][j], small["ssd_conv_b"][j][None],
                        small["ssd_dt_bias"][j].reshape(1, 2 * heads), small["ssd_a_log"][j].reshape(1, 2 * heads),
                        small["ssd_norm"][j][None], big["ssd_w_out"], j)
                    gs["mix_norm"][i] = sm["mix_norm"][0]
                    gs["ssd_conv_w"][j] = sm["conv_w"]
                    gs["ssd_conv_b"][j] = sm["conv_b"][0]
                    gs["ssd_dt_bias"][j] = sm["dt_bias"].reshape(2, heads)
                    gs["ssd_a_log"][j] = sm["a_log"].reshape(2, heads)
                    gs["ssd_d"][j] = sm["ssd_d"]
                    gs["ssd_norm"][j] = sm["ssd_norm"][0]
                    gbig["ssd_w_in"][j] = d_win
                    gbig["ssd_w_out"][j] = d_wout
                else:
                    dx, d_norm, d_scale, d_win, d_wgrp, d_wout = _pool_mixer_bwd(
                        dx, s1, mg, big["pool_w_in"], big["pool_w_group"], small["pool_scale"][j][None],
                        big["pool_w_out"], j)
                    gs["mix_norm"][i] = d_norm[0]
                    gs["pool_scale"][j] = d_scale[0]
                    gbig["pool_w_in"][j] = d_win
                    gbig["pool_w_group"][j] = d_wgrp
                    gbig["pool_w_out"][j] = d_wout
            dx, d_norm, d_wg, d_wu, d_wd = _ffn_bwd(dx, sv, small["ffn_norm"][i, half][None], big["ffn_gate"][i],
                                                   big["ffn_up"][i], big["ffn_down"][i], half)
            gs["ffn_norm"][(i, half)] = d_norm[0]
            gbig["ffn_gate"][(i, half)] = d_wg
            gbig["ffn_up"][(i, half)] = d_wu
            gbig["ffn_down"][(i, half)] = d_wd
    n_s, n_p = (depth + 1) // 2, depth // 2
    gsmall = dict(
        ffn_norm=jnp.stack([jnp.stack([gs["ffn_norm"][(i, h)] for h in range(2)]) for i in range(depth)]),
        mix_norm=jnp.stack([gs["mix_norm"][i] for i in range(depth)]),
        ssd_conv_w=jnp.stack([gs["ssd_conv_w"][j] for j in range(n_s)]),
        ssd_conv_b=jnp.stack([gs["ssd_conv_b"][j] for j in range(n_s)]),
        ssd_dt_bias=jnp.stack([gs["ssd_dt_bias"][j] for j in range(n_s)]),
        ssd_a_log=jnp.stack([gs["ssd_a_log"][j] for j in range(n_s)]),
        ssd_d=jnp.stack([gs["ssd_d"][j] for j in range(n_s)]),
        ssd_norm=jnp.stack([gs["ssd_norm"][j] for j in range(n_s)]),
        pool_scale=jnp.stack([gs["pool_scale"][j] for j in range(n_p)]),
        final_norm=d_final[0],
    )
    return loss, dx, gbig, gsmall


ANY = pl.BlockSpec(memory_space=pl.ANY)


def _mesh_pos():
    return lax.axis_index("x"), lax.axis_index("y"), lax.axis_index("c")


def _other_chips(x, y):
    return [(1 - x, y), (x, 1 - y), (1 - x, 1 - y)]


def _win(ref, windows, lead=()):
    rest = len(ref.shape) - len(lead)
    idx = tuple(lead) + tuple(pl.ds(*windows[ax]) if ax in windows else slice(None) for ax in range(rest))
    return ref.at[idx]


def _remote(src, dst, send_sems, recv_sems, k, peer):
    return pltpu.make_async_remote_copy(src_ref=src, dst_ref=dst, send_sem=send_sems.at[k], recv_sem=recv_sems.at[k],
                                        device_id=peer, device_id_type=MESH)


def _gather(name, shards, axes):
    n = len(shards)
    out_shapes = []
    for sh, (sa, _) in zip(shards, axes):
        full = list(sh.shape)
        full[sa] *= N_CHIPS
        out_shapes.append(jax.ShapeDtypeStruct(tuple(full), sh.dtype))

    def body(*refs):
        ins, outs = refs[:n], refs[n:2 * n]
        send_sems, recv_sems, loc_sems = refs[2 * n:]
        x, y, c = _mesh_pos()
        s = 2 * x + y
        chips = _other_chips(x, y)

        def full_win(t, sidx, hidx=None):
            sa, ha = axes[t]
            ssz = shards[t].shape[sa]
            w = {sa: (sidx * ssz, ssz)}
            if hidx is not None:
                hsz = shards[t].shape[ha] // 2
                w[ha] = (hidx * hsz, hsz)
            return _win(outs[t], w)

        locs, sends, fwds = [], [], []
        for t in range(n):
            lc = pltpu.make_async_copy(ins[t], full_win(t, s), loc_sems.at[t])
            lc.start()
            locs.append(lc)
            ha = axes[t][1]
            hsz = shards[t].shape[ha] // 2
            src = _win(ins[t], {ha: (c * hsz, hsz)})
            for j, (px, py) in enumerate(chips):
                cp = _remote(src, full_win(t, s, c), send_sems, recv_sems, 6 * t + j, (px, py, c))
                cp.start()
                sends.append(cp)
        for t in range(n):
            for j, (px, py) in enumerate(chips):
                w = full_win(t, 2 * px + py, c)
                _remote(w, w, send_sems, recv_sems, 6 * t + j, (px, py, c)).wait_recv()
                fw = _remote(w, w, send_sems, recv_sems, 6 * t + 3 + j, (x, y, 1 - c))
                fw.start()
                fwds.append(fw)
        for t in range(n):
            for j, (px, py) in enumerate(chips):
                w = full_win(t, 2 * px + py, 1 - c)
                _remote(w, w, send_sems, recv_sems, 6 * t + 3 + j, (x, y, 1 - c)).wait_recv()
        for cp in sends + fwds:
            cp.wait_send()
        for lc in locs:
            lc.wait()

    return pl.pallas_call(
        body, name=name, in_specs=[ANY] * n, out_specs=[ANY] * n, out_shape=out_shapes,
        scratch_shapes=[pltpu.SemaphoreType.DMA((6 * n,)), pltpu.SemaphoreType.DMA((6 * n,)),
                        pltpu.SemaphoreType.DMA((n,))],
    )(*shards)


def _exchange(name, inputs, out_shapes, plan, n_copies):
    n_in, n_out = len(inputs), len(out_shapes)

    def body(*refs):
        in_refs, out_refs = refs[:n_in], refs[n_in:n_in + n_out]
        send_sems, recv_sems = refs[n_in + n_out:]
        x, y, c = _mesh_pos()
        copies = plan(in_refs, out_refs, x, y, c)
        assert len(copies) == n_copies
        started = []
        for k, (src, dst, peer) in enumerate(copies):
            if peer is None:
                cp = pltpu.make_async_copy(src, dst, send_sems.at[k])
            else:
                cp = _remote(src, dst, send_sems, recv_sems, k, peer)
            cp.start()
            started.append(cp)
        for cp in started:
            cp.wait()

    return pl.pallas_call(
        body, name=name, in_specs=[ANY] * n_in, out_specs=[ANY] * n_out, out_shape=out_shapes,
        scratch_shapes=[pltpu.SemaphoreType.DMA((n_copies,)), pltpu.SemaphoreType.DMA((n_copies,))],
    )(*inputs)


def _halved(shape, ha):
    out = list(shape)
    out[ha] //= 2
    return tuple(out)


def _sharded(shape, sa):
    out = list(shape)
    out[sa] //= N_CHIPS
    return tuple(out)


def _rs_cores(name, grads, axes):
    n = len(grads)
    shapes = [jax.ShapeDtypeStruct(_halved(g.shape, ha), g.dtype) for g, (_, ha) in zip(grads, axes)]

    def plan(in_refs, out_refs, x, y, c):
        copies = []
        for t in range(n):
            ha = axes[t][1]
            hsz = grads[t].shape[ha] // 2
            copies.append((_win(in_refs[t], {ha: ((1 - c) * hsz, hsz)}), out_refs[n + t], (x, y, 1 - c)))
            copies.append((_win(in_refs[t], {ha: (c * hsz, hsz)}), out_refs[t], None))
        return copies

    outs = _exchange(name, grads, shapes + shapes, plan, 2 * n)
    return outs[:n], outs[n:]


def _rs_chips(name, halves, axes):
    n = len(halves)
    own_shapes = [jax.ShapeDtypeStruct(_sharded(h.shape, sa), h.dtype) for h, (sa, _) in zip(halves, axes)]
    recv_shapes = [jax.ShapeDtypeStruct((N_CHIPS - 1,) + o.shape, o.dtype) for o in own_shapes]

    def plan(in_refs, out_refs, x, y, c):
        copies = []
        for t in range(n):
            sa = axes[t][0]
            ssz = halves[t].shape[sa] // N_CHIPS
            for j, (px, py) in enumerate(_other_chips(x, y)):
                copies.append((_win(in_refs[t], {sa: ((2 * px + py) * ssz, ssz)}), out_refs[n + t].at[j], (px, py, c)))
            copies.append((_win(in_refs[t], {sa: ((2 * x + y) * ssz, ssz)}), out_refs[t], None))
        return copies

    outs = _exchange(name, halves, own_shapes + recv_shapes, plan, 4 * n)
    return outs[:n], outs[n:]


def _rs_finish(name, sums, places, out_shapes):
    n = len(sums)

    def plan(in_refs, out_refs, x, y, c):
        copies = []
        for t in range(n):
            oi, lead, ha = places[t]
            hsz = sums[t].shape[ha]
            dst = _win(out_refs[oi], {ha: (c * hsz, hsz)}, lead)
            copies.append((in_refs[t], dst, (x, y, 1 - c)))
            copies.append((in_refs[t], dst, None))
        return copies

    return _exchange(name, sums, out_shapes, plan, 2 * n)


def _allgather8(v, reduce):
    rows = v.shape[0]

    def body(v_ref, o_ref, *scratch):
        if reduce:
            buf, send_sems, recv_sems, loc_sem = scratch
        else:
            buf = o_ref
            send_sems, recv_sems, loc_sem = scratch
        x, y, c = _mesh_pos()
        me = 4 * x + 2 * y + c
        lc = pltpu.make_async_copy(v_ref, buf.at[me], loc_sem)
        lc.start()
        sends = []
        for k in range(1, 8):
            kx, ky, kc = k // 4, (k // 2) % 2, k % 2
            peer = (x ^ kx, y ^ ky, c ^ kc)
            cp = _remote(v_ref, buf.at[me], send_sems, recv_sems, k - 1, peer)
            cp.start()
            sends.append(cp)
        for k in range(1, 8):
            kx, ky, kc = k // 4, (k // 2) % 2, k % 2
            src = 4 * (x ^ kx) + 2 * (y ^ ky) + (c ^ kc)
            _remote(v_ref, buf.at[src], send_sems, recv_sems, k - 1, (x, y, c)).wait_recv()
        for cp in sends:
            cp.wait_send()
        lc.wait()
        if reduce:
            tot = buf[0]
            for i in range(1, 8):
                tot = tot + buf[i]
            o_ref[...] = tot

    vm = pl.BlockSpec(memory_space=pltpu.VMEM)
    sems = [pltpu.SemaphoreType.DMA((7,)), pltpu.SemaphoreType.DMA((7,)), pltpu.SemaphoreType.DMA]
    if reduce:
        out_shape = jax.ShapeDtypeStruct((rows, LANE), F32)
        scratch = [pltpu.VMEM((8, rows, LANE), F32)] + sems
    else:
        out_shape = jax.ShapeDtypeStruct((8, rows, LANE), F32)
        scratch = sems
    return pl.pallas_call(
        body, name="allreduce_small" if reduce else "allgather_small", in_specs=[vm], out_specs=vm,
        out_shape=out_shape, scratch_shapes=scratch,
    )(v)


def _rows2d(a):
    return a.reshape(-1, a.shape[-1])


def _add_pair(a, b):
    a2, b2 = _rows2d(a), _rows2d(b)
    rows, cols = a2.shape
    tr = _tile(rows, 256, 16)

    def body(a_ref, b_ref, o_ref):
        o_ref[...] = (a_ref[...].astype(F32) + b_ref[...].astype(F32)).astype(BF16)

    spec = pl.BlockSpec((tr, cols), lambda i: (i, 0))
    out = pl.pallas_call(body, name="add_pair", grid=(rows // tr,), in_specs=[spec, spec], out_specs=spec,
                         out_shape=jax.ShapeDtypeStruct((rows, cols), BF16), compiler_params=_cp("parallel"))(a2, b2)
    return out.reshape(a.shape)


def _add_four(own, recv):
    o2 = _rows2d(own)
    r3 = recv.reshape(N_CHIPS - 1, -1, recv.shape[-1])
    rows, cols = o2.shape
    tr = _tile(rows, 512, 16)

    def body(o_ref, r_ref, out_ref):
        out_ref[...] = ((o_ref[...].astype(F32) + r_ref[0].astype(F32)) + r_ref[1].astype(F32)) + r_ref[2].astype(F32)

    out = pl.pallas_call(
        body, name="add_four", grid=(rows // tr,),
        in_specs=[pl.BlockSpec((tr, cols), lambda i: (i, 0)), pl.BlockSpec((N_CHIPS - 1, tr, cols), lambda i: (0, i, 0))],
        out_specs=pl.BlockSpec((tr, cols), lambda i: (i, 0)),
        out_shape=jax.ShapeDtypeStruct((rows, cols), F32), compiler_params=_cp("parallel"))(o2, r3)
    return out.reshape(own.shape)


def _adamw(w, g, m, v):
    shape = w.shape
    w2, g2, m2, v2 = (_rows2d(a) if a.ndim > 1 else a.reshape(1, -1) for a in (w, g, m, v))
    rows, cols = w2.shape
    tr = _tile(rows, 256, 8)
    c1 = 1.0 - ADAM_B1 ** ADAM_STEP
    c2 = 1.0 - ADAM_B2 ** ADAM_STEP

    def body(w_ref, g_ref, m_ref, v_ref, d_ref, nm_ref, nv_ref):
        gv = g_ref[...]
        nm = ADAM_B1 * m_ref[...] + (1.0 - ADAM_B1) * gv
        nv = ADAM_B2 * v_ref[...] + (1.0 - ADAM_B2) * (gv * gv)
        nm_ref[...] = nm
        nv_ref[...] = nv
        d_ref[...] = -ADAM_LR * ((nm / c1) / (jnp.sqrt(nv / c2) + ADAM_EPS) + ADAM_WD * w_ref[...])

    spec = pl.BlockSpec((tr, cols), lambda i: (i, 0))
    osh = jax.ShapeDtypeStruct((rows, cols), F32)
    outs = pl.pallas_call(body, name="adamw", grid=(rows // tr,), in_specs=[spec] * 4, out_specs=[spec] * 3,
                          out_shape=[osh] * 3, compiler_params=_cp("parallel"))(w2, g2, m2, v2)
    return tuple(o.reshape(shape) for o in outs)


def _pack(arrs):
    flat = jnp.concatenate([a.reshape(-1) for a in arrs])
    n = flat.shape[0]
    rows = -(-n // (8 * LANE)) * 8
    return jnp.pad(flat, (0, rows * LANE - n)).reshape(rows, LANE)


def _unpack(packed, shapes):
    flat = packed.reshape(-1)
    out, pos = [], 0
    for sh in shapes:
        size = 1
        for dsz in sh:
            size *= dsz
        out.append(flat[pos:pos + size].reshape(sh))
        pos += size
    return out


BIG = ("ffn_w_gate", "ffn_w_up", "ffn_w_down", "ssd_w_in", "ssd_w_out", "pool_w_in", "pool_w_group", "pool_w_out")
WEIGHTS = ("ffn_norm", "ffn_w_gate", "ffn_w_up", "ffn_w_down", "mix_norm", "ssd_w_in", "ssd_conv_w", "ssd_conv_b",
           "ssd_dt_bias", "ssd_a_log", "ssd_d", "ssd_norm", "ssd_w_out", "pool_w_in", "pool_w_group", "pool_scale",
           "pool_w_out", "final_norm")
SMALL = tuple(k for k in WEIGHTS if k not in BIG)
SMALL_SHARDED = {"ffn_norm": 2, "ssd_conv_w": 2, "pool_scale": 1}


def _reduce_group(tag, grads, axes):
    own_a, recv_a = _rs_cores("rs_cores_" + tag, grads, axes)
    chip_sums = [_add_pair(a, b) for a, b in zip(own_a, recv_a)]
    own_b, recv_b = _rs_chips("rs_chips_" + tag, chip_sums, axes)
    return [_add_four(a, b) for a, b in zip(own_b, recv_b)]


def kernel(x, ffn_norm, ffn_w_gate, ffn_w_up, ffn_w_down, mix_norm, ssd_w_in, ssd_conv_w, ssd_conv_b, ssd_dt_bias, ssd_a_log, ssd_d, ssd_norm, ssd_w_out, pool_w_in, pool_w_group, pool_scale, pool_w_out, final_norm, loss_target, m_ffn_norm, m_ffn_w_gate, m_ffn_w_up, m_ffn_w_down, m_mix_norm, m_ssd_w_in, m_ssd_conv_w, m_ssd_conv_b, m_ssd_dt_bias, m_ssd_a_log, m_ssd_d, m_ssd_norm, m_ssd_w_out, m_pool_w_in, m_pool_w_group, m_pool_scale, m_pool_w_out, m_final_norm, v_ffn_norm, v_ffn_w_gate, v_ffn_w_up, v_ffn_w_down, v_mix_norm, v_ssd_w_in, v_ssd_conv_w, v_ssd_conv_b, v_ssd_dt_bias, v_ssd_a_log, v_ssd_d, v_ssd_norm, v_ssd_w_out, v_pool_w_in, v_pool_w_group, v_pool_scale, v_pool_w_out, v_final_norm):
    w = dict(ffn_norm=ffn_norm, ffn_w_gate=ffn_w_gate, ffn_w_up=ffn_w_up, ffn_w_down=ffn_w_down, mix_norm=mix_norm,
             ssd_w_in=ssd_w_in, ssd_conv_w=ssd_conv_w, ssd_conv_b=ssd_conv_b, ssd_dt_bias=ssd_dt_bias,
             ssd_a_log=ssd_a_log, ssd_d=ssd_d, ssd_norm=ssd_norm, ssd_w_out=ssd_w_out, pool_w_in=pool_w_in,
             pool_w_group=pool_w_group, pool_scale=pool_scale, pool_w_out=pool_w_out, final_norm=final_norm)
    mom = dict(ffn_norm=m_ffn_norm, ffn_w_gate=m_ffn_w_gate, ffn_w_up=m_ffn_w_up, ffn_w_down=m_ffn_w_down,
               mix_norm=m_mix_norm, ssd_w_in=m_ssd_w_in, ssd_conv_w=m_ssd_conv_w, ssd_conv_b=m_ssd_conv_b,
               ssd_dt_bias=m_ssd_dt_bias, ssd_a_log=m_ssd_a_log, ssd_d=m_ssd_d, ssd_norm=m_ssd_norm,
               ssd_w_out=m_ssd_w_out, pool_w_in=m_pool_w_in, pool_w_group=m_pool_w_group, pool_scale=m_pool_scale,
               pool_w_out=m_pool_w_out, final_norm=m_final_norm)
    vel = dict(ffn_norm=v_ffn_norm, ffn_w_gate=v_ffn_w_gate, ffn_w_up=v_ffn_w_up, ffn_w_down=v_ffn_w_down,
               mix_norm=v_mix_norm, ssd_w_in=v_ssd_w_in, ssd_conv_w=v_ssd_conv_w, ssd_conv_b=v_ssd_conv_b,
               ssd_dt_bias=v_ssd_dt_bias, ssd_a_log=v_ssd_a_log, ssd_d=v_ssd_d, ssd_norm=v_ssd_norm,
               ssd_w_out=v_ssd_w_out, pool_w_in=v_pool_w_in, pool_w_group=v_pool_w_group, pool_scale=v_pool_scale,
               pool_w_out=v_pool_w_out, final_norm=v_final_norm)
    depth = ffn_w_gate.shape[0]
    n_s, n_p = ssd_w_in.shape[0], pool_w_in.shape[0]
    chip = 2 * lax.axis_index("x") + lax.axis_index("y")

    wb = {k: w[k].astype(BF16) for k in BIG}
    big = dict(ffn_gate=[], ffn_up=[], ffn_down=[])
    for i in range(depth):
        g_i, u_i, d_i = _gather("gather_ffn", [wb["ffn_w_gate"][i], wb["ffn_w_up"][i], wb["ffn_w_down"][i]],
                                [(2, 1), (2, 1), (1, 2)])
        big["ffn_gate"].append(g_i)
        big["ffn_up"].append(u_i)
        big["ffn_down"].append(d_i)
    d_model = ssd_w_in.shape[1]
    w_in4, big["ssd_w_out"] = _gather("gather_ssd", [wb["ssd_w_in"][:, None], wb["ssd_w_out"]], [(1, 2), (1, 2)])
    big["ssd_w_in"] = w_in4.transpose(0, 2, 1, 3).reshape(n_s, d_model, -1)
    big["pool_w_in"], big["pool_w_group"], big["pool_w_out"] = _gather(
        "gather_pool", [wb["pool_w_in"], wb["pool_w_group"], wb["pool_w_out"]], [(1, 2), (2, 3), (1, 2)])
    sharded_names = tuple(SMALL_SHARDED)
    gathered = _allgather8(_pack([w[k] for k in sharded_names]), False)
    small = {k: w[k] for k in SMALL if k not in SMALL_SHARDED}
    per_chip = [_unpack(gathered[2 * s], [w[k].shape for k in sharded_names]) for s in range(N_CHIPS)]
    for t, k in enumerate(sharded_names):
        small[k] = jnp.concatenate([per_chip[s][t] for s in range(N_CHIPS)], axis=SMALL_SHARDED[k])

    loss, dx, gbig, gsmall = _local_step(x[0], loss_target[0], big, small)
    loss = lax.psum(loss[0, 0], ("x", "y", "c"))

    sums = {}
    for i in range(depth):
        grads = [gbig[k][(i, h)] for k in ("ffn_gate", "ffn_up", "ffn_down") for h in range(2)]
        res = _reduce_group("ffn", grads, [(1, 0)] * 4 + [(0, 1)] * 2)
        for t, k in enumerate(("ffn_w_gate", "ffn_w_up", "ffn_w_down")):
            for h in range(2):
                sums[(k, i, h)] = res[2 * t + h]
    for j in range(n_s):
        g_in = gbig["ssd_w_in"][j]
        g_in = g_in.reshape(d_model, N_CHIPS, -1).transpose(1, 0, 2)
        res = _reduce_group("ssd", [g_in, gbig["ssd_w_out"][j]], [(0, 1), (0, 1)])
        sums[("ssd_w_in", j)] = res[0][0]
        sums[("ssd_w_out", j)] = res[1]
    for j in range(n_p):
        res = _reduce_group("pool", [gbig["pool_w_in"][j], gbig["pool_w_group"][j], gbig["pool_w_out"][j]],
                            [(0, 1), (1, 2), (0, 1)])
        sums[("pool_w_in", j)], sums[("pool_w_group", j)], sums[("pool_w_out", j)] = res
    grad = {}
    for k in ("ffn_w_gate", "ffn_w_up", "ffn_w_down"):
        ha = 1 if k == "ffn_w_down" else 0
        items = [(sums[(k, i, h)], (0, (i, h), ha)) for i in range(depth) for h in range(2)]
        (grad[k],) = _rs_finish("rs_finish_ffn", [a for a, _ in items], [p for _, p in items],
                                [jax.ShapeDtypeStruct(w[k].shape, F32)])
    rest = ("ssd_w_in", "ssd_w_out", "pool_w_in", "pool_w_group", "pool_w_out")
    half_axis = dict(ssd_w_in=0, ssd_w_out=1, pool_w_in=1, pool_w_group=2, pool_w_out=1)
    items = [(sums[(k, j)], (t, (j,), half_axis[k])) for t, k in enumerate(rest) for j in range(w[k].shape[0])]
    outs = _rs_finish("rs_finish_mixers", [a for a, _ in items], [p for _, p in items],
                      [jax.ShapeDtypeStruct(w[k].shape, F32) for k in rest])
    for k, o in zip(rest, outs):
        grad[k] = o

    small_shapes = [gsmall[k].shape for k in SMALL]
    summed = _unpack(_allgather8(_pack([gsmall[k] for k in SMALL]), True), small_shapes)
    for k, g in zip(SMALL, summed):
        if k in SMALL_SHARDED:
            ax = SMALL_SHARDED[k]
            size = w[k].shape[ax]
            g = lax.dynamic_slice_in_dim(g, chip * size, size, axis=ax)
        grad[k] = g

    delta, new_m, new_v = {}, {}, {}
    for k in BIG:
        delta[k], new_m[k], new_v[k] = _adamw(w[k], grad[k], mom[k], vel[k])
    shapes = [w[k].shape for k in SMALL]
    packed = _adamw(*(_pack([src[k] for k in SMALL]) for src in (w, grad, mom, vel)))
    for dst, p in zip((delta, new_m, new_v), packed):
        for k, a in zip(SMALL, _unpack(p, shapes)):
            dst[k] = a

    return (loss, dx[None], *[grad[k] for k in WEIGHTS], *[delta[k] for k in WEIGHTS],
            *[new_m[k] for k in WEIGHTS], *[new_v[k] for k in WEIGHTS])
```

```python
import functools

import jax
import jax.numpy as jnp
from jax import lax
from jax.experimental import pallas as pl
from jax.experimental.pallas import tpu as pltpu

F32 = jnp.float32
BF16 = jnp.bfloat16
EPS = 1e-6
MESH = pl.DeviceIdType.MESH

SSD_CHUNK = 128
SSD_STATE = 128
SSD_HEAD_DIM = 64
HEADS_PER_GROUP = 8
GROUP_W = HEADS_PER_GROUP * SSD_HEAD_DIM
CONV_W = 5
POOL_WINDOWS = (2, 4, 8, 16)
N_CHIPS = 4

ADAM_LR = 0.001
ADAM_B1 = 0.9
ADAM_B2 = 0.999
ADAM_EPS = 1e-08
ADAM_WD = 0.01
ADAM_STEP = 10

VMEM_LIMIT = 56 * 1024 * 1024
LANE = 128


def _cp(*sem):
    return pltpu.CompilerParams(dimension_semantics=sem, vmem_limit_bytes=VMEM_LIMIT)


def _tile(dim, pref, unit=LANE):
    if dim <= pref:
        return dim
    t = (pref // unit) * unit
    while t >= unit:
        if dim % t == 0:
            return t
        t -= unit
    return dim


def _sigmoid(v):
    return 1.0 / (1.0 + jnp.exp(-v))


def _dot(a, b):
    return jnp.dot(a, b, preferred_element_type=F32)


def _dot_nt(a, b):
    return lax.dot_general(a, b, (((1,), (1,)), ((), ())), preferred_element_type=F32)


def _split3(v):
    h1 = v.astype(BF16)
    r1 = v - h1.astype(F32)
    h2 = r1.astype(BF16)
    h3 = (r1 - h2.astype(F32)).astype(BF16)
    return h1, h2, h3


def _dot_exact01(m01, v):
    mb = m01.astype(BF16)
    h1, h2, h3 = _split3(v)
    return _dot(mb, h1) + _dot(mb, h2) + _dot(mb, h3)


def _dot_nt_exact01(m01, v):
    mb = m01.astype(BF16)
    h1, h2, h3 = _split3(v)
    return _dot_nt(mb, h1) + _dot_nt(mb, h2) + _dot_nt(mb, h3)


def _mm(a, b, *, name, ta=False, tb=False, a_sel=(), b_sel=(), pair2=None, add=None, scale=1.0,
        out_dtype=F32, tm=1024, tn=1024, tk=2048):
    am, ak = a.shape[-2:][::-1] if ta else a.shape[-2:]
    bk, bn = b.shape[-2:][::-1] if tb else b.shape[-2:]
    assert ak == bk, (a.shape, b.shape, ta, tb)
    m_dim, n_dim, k_dim = am, bn, ak
    tm, tn, tk = _tile(m_dim, tm), _tile(n_dim, tn), _tile(k_dim, tk)
    nk = k_dim // tk
    grid = (n_dim // tn, m_dim // tm, nk)

    def a_spec(sel):
        lead = (None,) * len(sel)
        if ta:
            return pl.BlockSpec(lead + (tk, tm), lambda n, m, k: tuple(sel) + (k, m))
        return pl.BlockSpec(lead + (tm, tk), lambda n, m, k: tuple(sel) + (m, k))

    def b_spec(sel):
        lead = (None,) * len(sel)
        if tb:
            return pl.BlockSpec(lead + (tn, tk), lambda n, m, k: tuple(sel) + (n, k))
        return pl.BlockSpec(lead + (tk, tn), lambda n, m, k: tuple(sel) + (k, n))

    ins, specs = [a, b], [a_spec(a_sel), b_spec(b_sel)]
    if pair2 is not None:
        a2, b2, a2_sel, b2_sel = pair2
        ins += [a2, b2]
        specs += [a_spec(a2_sel), b_spec(b2_sel)]
    if add is not None:
        ins.append(add)
        specs.append(pl.BlockSpec((tm, tn), lambda n, m, k: (m, n)))
    dn = (((0 if ta else 1,), (1 if tb else 0,)), ((), ()))
    n_pairs = 2 if pair2 is not None else 1

    def body(*refs):
        pairs = [(refs[2 * i], refs[2 * i + 1]) for i in range(n_pairs)]
        pos = 2 * n_pairs
        add_ref = None
        if add is not None:
            add_ref = refs[pos]
            pos += 1
        o_ref = refs[pos]
        acc_ref = refs[pos + 1] if nk > 1 else None

        def prod():
            tot = None
            for ar, br in pairs:
                p = lax.dot_general(ar[...].astype(BF16), br[...].astype(BF16), dn, preferred_element_type=F32)
                tot = p if tot is None else tot + p
            return tot

        def finish(r):
            if scale != 1.0:
                r = r * scale
            if add_ref is not None:
                r = add_ref[...] + r
            o_ref[...] = r.astype(out_dtype)

        if nk == 1:
            finish(prod())
        else:
            k = pl.program_id(2)

            @pl.when(k == 0)
            def _():
                acc_ref[...] = jnp.zeros_like(acc_ref)

            acc_ref[...] += prod()

            @pl.when(k == nk - 1)
            def _():
                finish(acc_ref[...])

    return pl.pallas_call(
        body, name=name, grid=grid, in_specs=specs,
        out_specs=pl.BlockSpec((tm, tn), lambda n, m, k: (m, n)),
        out_shape=jax.ShapeDtypeStruct((m_dim, n_dim), out_dtype),
        scratch_shapes=[pltpu.VMEM((tm, tn), F32)] if nk > 1 else [],
        compiler_params=_cp("parallel", "parallel", "arbitrary"),
    )(*ins)


def _rmsnorm(x, g):
    t_dim, d = x.shape
    tr = _tile(t_dim, 256, 8)

    def body(x_ref, g_ref, o_ref):
        xv = x_ref[...]
        r = lax.rsqrt(jnp.mean(xv * xv, axis=-1, keepdims=True) + EPS)
        o_ref[...] = (xv * r * g_ref[...]).astype(BF16)

    return pl.pallas_call(
        body, name="rmsnorm", grid=(t_dim // tr,),
        in_specs=[pl.BlockSpec((tr, d), lambda i: (i, 0)), pl.BlockSpec((1, d), lambda i: (0, 0))],
        out_specs=pl.BlockSpec((tr, d), lambda i: (i, 0)),
        out_shape=jax.ShapeDtypeStruct((t_dim, d), BF16),
        compiler_params=_cp("parallel"),
    )(x, g)


def _rmsnorm_bwd(x, g, dh, dres):
    t_dim, d = x.shape
    tr = _tile(t_dim, 256, 8)

    def body(x_ref, g_ref, dh_ref, dres_ref, dx_ref, dg_ref):
        i = pl.program_id(0)
        xv = x_ref[...]
        r = lax.rsqrt(jnp.mean(xv * xv, axis=-1, keepdims=True) + EPS)
        n = xv * r
        dhv = dh_ref[...]
        dn = dhv * g_ref[...]

        @pl.when(i == 0)
        def _():
            dg_ref[...] = jnp.zeros_like(dg_ref)

        dg_ref[...] += jnp.sum(dhv * n, axis=0, keepdims=True)
        dx_ref[...] = dres_ref[...] + r * (dn - n * jnp.mean(dn * n, axis=-1, keepdims=True))

    row = pl.BlockSpec((tr, d), lambda i: (i, 0))
    vec = pl.BlockSpec((1, d), lambda i: (0, 0))
    return pl.pallas_call(
        body, name="rmsnorm_bwd", grid=(t_dim // tr,),
        in_specs=[row, vec, row, row], out_specs=[row, vec],
        out_shape=[jax.ShapeDtypeStruct((t_dim, d), F32), jax.ShapeDtypeStruct((1, d), F32)],
        compiler_params=_cp("arbitrary"),
    )(x, g, dh, dres)


def _loss_head(x, g, target):
    t_dim, d = x.shape
    tr = _tile(t_dim, 256, 8)

    def body(x_ref, g_ref, t_ref, loss_ref, dx_ref, dg_ref):
        i = pl.program_id(0)
        xv = x_ref[...]
        gv = g_ref[...]
        r = lax.rsqrt(jnp.mean(xv * xv, axis=-1, keepdims=True) + EPS)
        n = xv * r
        err = n * gv - t_ref[...]

        @pl.when(i == 0)
        def _():
            dg_ref[...] = jnp.zeros_like(dg_ref)
            loss_ref[...] = jnp.zeros_like(loss_ref)

        per_tok = jnp.mean(err * err, axis=-1, keepdims=True)
        loss_ref[...] += 0.5 * jnp.sum(per_tok, axis=0, keepdims=True)
        dy = err * (1.0 / d)
        dn = dy * gv
        dg_ref[...] += jnp.sum(dy * n, axis=0, keepdims=True)
        dx_ref[...] = r * (dn - n * jnp.mean(dn * n, axis=-1, keepdims=True))

    row = pl.BlockSpec((tr, d), lambda i: (i, 0))
    vec = pl.BlockSpec((1, d), lambda i: (0, 0))
    one = pl.BlockSpec((1, 1), lambda i: (0, 0))
    return pl.pallas_call(
        body, name="loss_head", grid=(t_dim // tr,),
        in_specs=[row, vec, row], out_specs=[one, row, vec],
        out_shape=[jax.ShapeDtypeStruct((1, 1), F32), jax.ShapeDtypeStruct((t_dim, d), F32),
                   jax.ShapeDtypeStruct((1, d), F32)],
        compiler_params=_cp("arbitrary"),
    )(x, g, target)


def _ffn_in(h, wg, wu, half):
    t_dim, d = h.shape
    f = wg.shape[-1]
    tm, tn = _tile(t_dim, 512), _tile(f, 1408)

    def body(h_ref, wg_ref, wu_ref, g_ref, u_ref, a_ref):
        hv = h_ref[...]
        gv = _dot(hv, wg_ref[...])
        uv = _dot(hv, wu_ref[...])
        g_ref[...] = gv.astype(BF16)
        u_ref[...] = uv.astype(BF16)
        a_ref[...] = (gv * _sigmoid(gv) * uv).astype(BF16)

    wspec = pl.BlockSpec((None, d, tn), lambda n, m: (half, 0, n))
    ospec = pl.BlockSpec((tm, tn), lambda n, m: (m, n))
    oshape = jax.ShapeDtypeStruct((t_dim, f), BF16)
    return pl.pallas_call(
        body, name="ffn_in", grid=(f // tn, t_dim // tm),
        in_specs=[pl.BlockSpec((tm, d), lambda n, m: (m, 0)), wspec, wspec],
        out_specs=[ospec, ospec, ospec], out_shape=[oshape, oshape, oshape],
        compiler_params=_cp("parallel", "parallel"),
    )(h, wg, wu)


def _ffn_bwd_act(dx, wd, g, u, half):
    t_dim, d = dx.shape
    f = wd.shape[-2]
    tm, tn = _tile(t_dim, 512), _tile(f, 1408)

    def body(dx_ref, wd_ref, g_ref, u_ref, dg_ref, du_ref):
        da = 0.5 * _dot_nt(dx_ref[...].astype(BF16), wd_ref[...])
        gv = g_ref[...].astype(F32)
        uv = u_ref[...].astype(F32)
        s = _sigmoid(gv)
        dg_ref[...] = (da * uv * (s * (1.0 + gv * (1.0 - s)))).astype(BF16)
        du_ref[...] = (da * gv * s).astype(BF16)

    tile = pl.BlockSpec((tm, tn), lambda n, m: (m, n))
    oshape = jax.ShapeDtypeStruct((t_dim, f), BF16)
    return pl.pallas_call(
        body, name="ffn_bwd_act", grid=(f // tn, t_dim // tm),
        in_specs=[pl.BlockSpec((tm, d), lambda n, m: (m, 0)),
                  pl.BlockSpec((None, tn, d), lambda n, m: (half, n, 0)), tile, tile],
        out_specs=[tile, tile], out_shape=[oshape, oshape],
        compiler_params=_cp("parallel", "parallel"),
    )(dx, wd, g, u)


def _ffn_fwd(x, norm_g, wg, wu, wd, half):
    h = _rmsnorm(x, norm_g)
    g, u, a = _ffn_in(h, wg, wu, half)
    x_new = _mm(a, wd, name="ffn_out", b_sel=(half,), add=x, scale=0.5, tk=1408)
    return x_new, (x, h, g, u, a)


def _ffn_bwd(dx, saved, norm_g, wg, wu, wd, half):
    x, h, g, u, a = saved
    dg, du = _ffn_bwd_act(dx, wd, g, u, half)
    d_wd = _mm(a, dx, name="ffn_dwd", ta=True, scale=0.5, out_dtype=BF16, tm=1408, tn=1024)
    d_wg = _mm(h, dg, name="ffn_dwgu", ta=True, out_dtype=BF16, tm=1024, tn=1408)
    d_wu = _mm(h, du, name="ffn_dwgu", ta=True, out_dtype=BF16, tm=1024, tn=1408)
    dh = _mm(dg, wg, name="ffn_dh", tb=True, b_sel=(half,), pair2=(du, wu, (), (half,)), tk=1408)
    dx_new, dnorm = _rmsnorm_bwd(x, norm_g, dh, dx)
    return dx_new, dnorm, d_wg, d_wu, d_wd


def _shifted(v, off, t_idx):
    if off == 0:
        return v
    t_dim = v.shape[0]
    sh = pltpu.roll(v, (-off) % t_dim, 0)
    valid = jnp.logical_and(t_idx + off >= 0, t_idx + off < t_dim)
    return jnp.where(valid, sh, 0.0)


def _conv_pre(u, w_ref, b_ref, t_idx):
    acc = jnp.zeros_like(u) + b_ref[...]
    shifted = []
    for k in range(CONV_W):
        sh = _shifted(u, k - CONV_W // 2, t_idx)
        shifted.append(sh)
        acc = acc + w_ref[k:k + 1, :] * sh
    return acc, shifted


def _conv_silu(proj, conv_w, conv_b, col0):
    t_dim = proj.shape[0]
    cd = conv_w.shape[-1]
    cb = _tile(cd, 256)
    assert col0 % cb == 0

    def body(u_ref, w_ref, b_ref, o_ref):
        t_idx = lax.broadcasted_iota(jnp.int32, (t_dim, cb), 0)
        pre, _ = _conv_pre(u_ref[...], w_ref, b_ref, t_idx)
        o_ref[...] = pre * _sigmoid(pre)

    return pl.pallas_call(
        body, name="conv_silu", grid=(cd // cb,),
        in_specs=[pl.BlockSpec((t_dim, cb), lambda j: (0, col0 // cb + j)),
                  pl.BlockSpec((CONV_W, cb), lambda j: (0, j)), pl.BlockSpec((1, cb), lambda j: (0, j))],
        out_specs=pl.BlockSpec((t_dim, cb), lambda j: (0, j)),
        out_shape=jax.ShapeDtypeStruct((t_dim, cd), F32),
        compiler_params=_cp("parallel"),
    )(proj, conv_w, conv_b)


def _conv_silu_bwd(proj, conv_w, conv_b, dact2, col0):
    t_dim = proj.shape[0]
    cd = conv_w.shape[-1]
    cb = _tile(cd, 256)

    def body(u_ref, w_ref, b_ref, da_ref, du_ref, dwb_ref):
        t_idx = lax.broadcasted_iota(jnp.int32, (t_dim, cb), 0)
        pre, shifted = _conv_pre(u_ref[...], w_ref, b_ref, t_idx)
        s = _sigmoid(pre)
        dpre = (da_ref[0] + da_ref[1]) * (s * (1.0 + pre * (1.0 - s)))
        du = jnp.zeros_like(dpre)
        for k in range(CONV_W):
            du = du + w_ref[k:k + 1, :] * _shifted(dpre, -(k - CONV_W // 2), t_idx)
            dwb_ref[k:k + 1, :] = jnp.sum(dpre * shifted[k], axis=0, keepdims=True)
        dwb_ref[CONV_W:CONV_W + 1, :] = jnp.sum(dpre, axis=0, keepdims=True)
        dwb_ref[CONV_W + 1:8, :] = jnp.zeros((8 - CONV_W - 1, cb), F32)
        du_ref[...] = du.astype(BF16)

    return pl.pallas_call(
        body, name="conv_silu_bwd", grid=(cd // cb,),
        in_specs=[pl.BlockSpec((t_dim, cb), lambda j: (0, col0 // cb + j)),
                  pl.BlockSpec((CONV_W, cb), lambda j: (0, j)), pl.BlockSpec((1, cb), lambda j: (0, j)),
                  pl.BlockSpec((2, t_dim, cb), lambda j: (0, 0, j))],
        out_specs=[pl.BlockSpec((t_dim, cb), lambda j: (0, j)), pl.BlockSpec((8, cb), lambda j: (0, j))],
        out_shape=[jax.ShapeDtypeStruct((t_dim, cd), BF16), jax.ShapeDtypeStruct((8, cd), F32)],
        compiler_params=_cp("parallel"),
    )(proj, conv_w, conv_b, dact2)


def _softplus_fwd(dt_raw, bias):
    def body(r_ref, b_ref, o_ref):
        v = r_ref[...] + b_ref[...]
        o_ref[...] = jnp.maximum(v, 0.0) + jnp.log(1.0 + jnp.exp(-jnp.abs(v)))

    return pl.pallas_call(body, name="softplus", out_shape=jax.ShapeDtypeStruct(dt_raw.shape, F32))(dt_raw, bias)


def _softplus_bwd(dt_raw, bias, ddt, da, a_log):
    def body(r_ref, b_ref, ddt_ref, da_ref, al_ref, dr_ref, db_ref, dal_ref):
        dv = ddt_ref[...] * _sigmoid(r_ref[...] + b_ref[...])
        dr_ref[...] = dv.astype(BF16)
        db_ref[...] = jnp.sum(dv, axis=0, keepdims=True)
        dal_ref[...] = -da_ref[...] * jnp.exp(al_ref[...])

    vec = jax.ShapeDtypeStruct(bias.shape, F32)
    return pl.pallas_call(
        body, name="softplus_bwd",
        out_shape=[jax.ShapeDtypeStruct(dt_raw.shape, BF16), vec, vec])(dt_raw, bias, ddt, da, a_log)


def _chunk_setup(d, dt_ref, al_ref):
    q = SSD_CHUNK
    ii = lax.broadcasted_iota(jnp.int32, (q, q), 0)
    jj = lax.broadcasted_iota(jnp.int32, (q, q), 1)
    sgn = 1 - 2 * d
    mask = (jj - ii) * sgn <= 0
    mask_t = (ii - jj) * sgn <= 0
    m01 = mask.astype(F32)
    m01_t = mask_t.astype(F32)
    dt = dt_ref[...]
    a = -jnp.exp(al_ref[...])
    dta = dt * a
    cs = _dot_exact01(m01, dta)
    tot = jnp.sum(dta, axis=0, keepdims=True)
    return mask, mask_t, m01, m01_t, dt, a, dta, cs, tot


def _ssd_specs(t_dim, di, g_cnt, chunk_of):
    q, ns = SSD_CHUNK, SSD_STATE
    x_spec = pl.BlockSpec((q, GROUP_W), lambda d, g, c: (chunk_of(d, c), g))
    b_spec = pl.BlockSpec((q, ns), lambda d, g, c: (chunk_of(d, c), di // ns + g))
    c_spec = pl.BlockSpec((q, ns), lambda d, g, c: (chunk_of(d, c), di // ns + g_cnt + g))
    dt_spec = pl.BlockSpec((None, None, q, HEADS_PER_GROUP), lambda d, g, c: (d, g, chunk_of(d, c), 0))
    al_spec = pl.BlockSpec((None, None, 1, HEADS_PER_GROUP), lambda d, g, c: (d, g, 0, 0))
    return x_spec, b_spec, c_spec, dt_spec, al_spec


def _ssd_fwd(xbc, dt4, al4):
    t_dim = xbc.shape[0]
    g_cnt = dt4.shape[1]
    di = g_cnt * GROUP_W
    q, ns, p = SSD_CHUNK, SSD_STATE, SSD_HEAD_DIM
    nc = t_dim // q

    def chunk_of(d, c):
        return c + d * (nc - 1 - 2 * c)

    def body(x_ref, b_ref, c_ref, dt_ref, al_ref, y_ref, hin_ref, h_sc, xd_sc):
        d = pl.program_id(0)
        c = pl.program_id(2)

        @pl.when(c == 0)
        def _():
            h_sc[...] = jnp.zeros_like(h_sc)

        mask, mask_t, m01, m01_t, dt, a, dta, cs, tot = _chunk_setup(d, dt_ref, al_ref)
        xv = x_ref[...]
        bb = b_ref[...].astype(BF16)
        cb16 = c_ref[...].astype(BF16)
        bt16 = b_ref[...].T.astype(BF16)
        cb = _dot(cb16, bt16)
        hin = h_sc[...]
        hin_ref[...] = hin
        ch = _dot(cb16, hin.astype(BF16))
        for h in range(HEADS_PER_GROUP):
            sl = slice(h * p, (h + 1) * p)
            cs_col = cs[:, h:h + 1]
            cs_row = jnp.sum(dta[:, h:h + 1] * m01_t, axis=0, keepdims=True)
            lmat = jnp.exp(jnp.where(mask, cs_col - cs_row, -1e30))
            xdt = xv[:, sl] * dt[:, h:h + 1]
            y_diag = _dot((cb * lmat).astype(BF16), xdt.astype(BF16))
            y_ref[:, sl] = y_diag + ch[:, sl] * jnp.exp(cs_col)
            xd_sc[:, sl] = xdt * jnp.exp(tot[:, h:h + 1] - cs_col)
        st = _dot(bt16, xd_sc[...].astype(BF16))
        for h in range(HEADS_PER_GROUP):
            sl = slice(h * p, (h + 1) * p)
            h_sc[:, sl] = hin[:, sl] * jnp.exp(tot[:, h:h + 1]) + st[:, sl]

    x_spec, b_spec, c_spec, dt_spec, al_spec = _ssd_specs(t_dim, di, g_cnt, chunk_of)
    return pl.pallas_call(
        body, name="ssd_fwd", grid=(2, g_cnt, nc),
        in_specs=[x_spec, b_spec, c_spec, dt_spec, al_spec],
        out_specs=[pl.BlockSpec((None, q, GROUP_W), lambda d, g, c: (d, chunk_of(d, c), g)),
                   pl.BlockSpec((None, None, None, ns, GROUP_W), lambda d, g, c: (d, g, chunk_of(d, c), 0, 0))],
        out_shape=[jax.ShapeDtypeStruct((2, t_dim, di), F32),
                   jax.ShapeDtypeStruct((2, g_cnt, nc, ns, GROUP_W), F32)],
        scratch_shapes=[pltpu.VMEM((ns, GROUP_W), F32), pltpu.VMEM((q, GROUP_W), F32)],
        compiler_params=_cp("parallel", "parallel", "arbitrary"),
    )(xbc, xbc, xbc, dt4, al4)


def _ssd_bwd(xbc, dt4, al4, hin_all, dy, dvec):
    t_dim = xbc.shape[0]
    g_cnt = dt4.shape[1]
    di = g_cnt * GROUP_W
    q, ns, p, hg = SSD_CHUNK, SSD_STATE, SSD_HEAD_DIM, HEADS_PER_GROUP
    nc = t_dim // q

    def chunk_of(d, c):
        return (nc - 1 - c) + d * (2 * c - nc + 1)

    def body(x_ref, b_ref, c_ref, dt_ref, al_ref, hin_ref, dy_ref, dv_ref,
             dx_ref, db_ref, dc_ref, ddt_ref, da_ref, g_sc, xd_sc, dye_sc, col_sc, zrow_sc, tot_sc):
        d = pl.program_id(0)
        c = pl.program_id(2)

        @pl.when(c == 0)
        def _():
            g_sc[...] = jnp.zeros_like(g_sc)
            da_ref[...] = jnp.zeros_like(da_ref)

        mask, mask_t, m01, m01_t, dt, a, dta, cs, tot = _chunk_setup(d, dt_ref, al_ref)
        xv = x_ref[...]
        dyv = dy_ref[...]
        bb = b_ref[...].astype(BF16)
        cb16 = c_ref[...].astype(BF16)
        bt16 = b_ref[...].T.astype(BF16)
        ct16 = c_ref[...].T.astype(BF16)
        cb = _dot(cb16, bt16)
        cbt = _dot(bb, ct16)
        hin = hin_ref[...]
        hin16 = hin.astype(BF16)
        gst = g_sc[...]
        gst16 = gst.astype(BF16)
        ch = _dot(cb16, hin16)
        wst = _dot(bb, gst16)
        skip = jnp.where(d == 0, 1.0, 0.0)
        dcb = jnp.zeros((q, q), F32)
        for h in range(hg):
            sl = slice(h * p, (h + 1) * p)
            cs_col = cs[:, h:h + 1]
            cs_row = jnp.sum(dta[:, h:h + 1] * m01_t, axis=0, keepdims=True)
            lmat = jnp.exp(jnp.where(mask, cs_col - cs_row, -1e30))
            lmat_t = jnp.exp(jnp.where(mask_t, cs_row - cs_col, -1e30))
            e_col = jnp.exp(cs_col)
            dec = jnp.exp(tot[:, h:h + 1] - cs_col)
            dt_h = dt[:, h:h + 1]
            x_h = xv[:, sl]
            xdt = x_h * dt_h
            xdt16 = xdt.astype(BF16)
            dy_h = dyv[:, sl]
            dy16 = dy_h.astype(BF16)
            dye_sc[:, sl] = dy_h * e_col
            xd_sc[:, sl] = xdt * dec
            dcs = jnp.sum(dy_h * ch[:, sl] * e_col, axis=-1, keepdims=True)
            w_h = wst[:, sl]
            dxdt = w_h * dec
            t1 = jnp.sum(w_h * xdt, axis=-1, keepdims=True) * dec
            dcs = dcs - t1
            dtot = jnp.sum(t1, axis=0, keepdims=True)
            m_f = cb * lmat
            dxdt = dxdt + _dot((cbt * lmat_t).astype(BF16), dy16)
            dm = _dot_nt(dy16, xdt16)
            z = dm * m_f
            dcb = dcb + dm * lmat
            dcs = dcs + jnp.sum(z, axis=-1, keepdims=True)
            zrow_sc[h:h + 1, :] = jnp.sum(z, axis=0, keepdims=True)
            t_h = jnp.exp(tot[:, h:h + 1])
            dtot = dtot + jnp.sum(jnp.sum(gst[:, sl] * hin[:, sl], axis=-1, keepdims=True), axis=0,
                                  keepdims=True) * t_h
            dx_ref[:, sl] = dxdt * dt_h + skip * dy_h * dv_ref[:, sl]
            col_sc[:, h:h + 1] = dcs
            col_sc[:, hg + h:hg + h + 1] = jnp.sum(dxdt * x_h, axis=-1, keepdims=True)
            tot_sc[:, h:h + 1] = dtot
        dye16 = dye_sc[...].astype(BF16)
        dcb16 = dcb.astype(BF16)
        dc_ref[...] = _dot_nt(dye16, hin16) + _dot(dcb16, bb)
        db_ref[...] = _dot_nt(xd_sc[...].astype(BF16), gst16) + _dot(dcb.T.astype(BF16), cb16)
        dhin = _dot(ct16, dye16)
        for h in range(hg):
            sl = slice(h * p, (h + 1) * p)
            g_sc[:, sl] = dhin[:, sl] + gst[:, sl] * jnp.exp(tot[:, h:h + 1])
        ddta = _dot_exact01(m01_t, col_sc[:, 0:hg]) - _dot_nt_exact01(m01_t, zrow_sc[...]) + tot_sc[...]
        ddt_ref[...] = col_sc[:, hg:2 * hg] + ddta * a
        da_ref[...] += jnp.sum(ddta * dt, axis=0, keepdims=True)

    x_spec, b_spec, c_spec, dt_spec, al_spec = _ssd_specs(t_dim, di, g_cnt, chunk_of)
    hin_spec = pl.BlockSpec((None, None, None, ns, GROUP_W), lambda d, g, c: (d, g, chunk_of(d, c), 0, 0))
    dy_spec = pl.BlockSpec((q, GROUP_W), lambda d, g, c: (chunk_of(d, c), g))
    dv_spec = pl.BlockSpec((1, GROUP_W), lambda d, g, c: (0, g))
    gn = g_cnt * ns
    return pl.pallas_call(
        body, name="ssd_bwd", grid=(2, g_cnt, nc),
        in_specs=[x_spec, b_spec, c_spec, dt_spec, al_spec, hin_spec, dy_spec, dv_spec],
        out_specs=[pl.BlockSpec((None, q, GROUP_W), lambda d, g, c: (d, chunk_of(d, c), g)),
                   pl.BlockSpec((None, q, ns), lambda d, g, c: (d, chunk_of(d, c), g)),
                   pl.BlockSpec((None, q, ns), lambda d, g, c: (d, chunk_of(d, c), g)),
                   pl.BlockSpec((None, None, q, hg), lambda d, g, c: (d, g, chunk_of(d, c), 0)),
                   pl.BlockSpec((None, None, 1, hg), lambda d, g, c: (d, g, 0, 0))],
        out_shape=[jax.ShapeDtypeStruct((2, t_dim, di), F32), jax.ShapeDtypeStruct((2, t_dim, gn), F32),
                   jax.ShapeDtypeStruct((2, t_dim, gn), F32), jax.ShapeDtypeStruct((2, g_cnt, t_dim, hg), F32),
                   jax.ShapeDtypeStruct((2, g_cnt, 1, hg), F32)],
        scratch_shapes=[pltpu.VMEM((ns, GROUP_W), F32), pltpu.VMEM((q, GROUP_W), F32),
                        pltpu.VMEM((q, GROUP_W), F32), pltpu.VMEM((q, 2 * hg), F32),
                        pltpu.VMEM((hg, q), F32), pltpu.VMEM((1, hg), F32)],
        compiler_params=_cp("parallel", "parallel", "arbitrary"),
    )(xbc, xbc, xbc, dt4, al4, hin_all, dy, dvec)


def _gate_norm(y2, xbc, proj, dvec, ng):
    t_dim, di = y2.shape[1:]
    tr = _tile(t_dim, 128, 8)

    def body(y2_ref, x_ref, z_ref, dv_ref, ng_ref, o_ref):
        y = y2_ref[0] + y2_ref[1] + x_ref[...] * dv_ref[...]
        z = z_ref[...]
        v = y * z * _sigmoid(z)
        r = lax.rsqrt(jnp.mean(v * v, axis=-1, keepdims=True) + EPS)
        o_ref[...] = (v * r * ng_ref[...]).astype(BF16)

    row = pl.BlockSpec((tr, di), lambda i: (i, 0))
    vec = pl.BlockSpec((1, di), lambda i: (0, 0))
    return pl.pallas_call(
        body, name="gate_norm", grid=(t_dim // tr,),
        in_specs=[pl.BlockSpec((2, tr, di), lambda i: (0, i, 0)), row, row, vec, vec],
        out_specs=row, out_shape=jax.ShapeDtypeStruct((t_dim, di), BF16),
        compiler_params=_cp("parallel"),
    )(y2, xbc, proj, dvec, ng)


def _gate_norm_bwd(y2, xbc, proj, dvec, ng, dyn):
    t_dim, di = y2.shape[1:]
    tr = _tile(t_dim, 128, 8)

    def body(y2_ref, x_ref, z_ref, dv_ref, ng_ref, dyn_ref, dy_ref, dz_ref, dng_ref, dd_ref):
        i = pl.program_id(0)
        xv = x_ref[...]
        y = y2_ref[0] + y2_ref[1] + xv * dv_ref[...]
        z = z_ref[...]
        s = _sigmoid(z)
        v = y * z * s
        r = lax.rsqrt(jnp.mean(v * v, axis=-1, keepdims=True) + EPS)
        n = v * r
        dynv = dyn_ref[...]
        dn = dynv * ng_ref[...]
        dv = r * (dn - n * jnp.mean(dn * n, axis=-1, keepdims=True))
        dy = dv * z * s

        @pl.when(i == 0)
        def _():
            dng_ref[...] = jnp.zeros_like(dng_ref)
            dd_ref[...] = jnp.zeros_like(dd_ref)

        dng_ref[...] += jnp.sum(dynv * n, axis=0, keepdims=True)
        dd_ref[...] += jnp.sum(dy * xv, axis=0, keepdims=True)
        dy_ref[...] = dy
        dz_ref[...] = (dv * y * (s * (1.0 + z * (1.0 - s)))).astype(BF16)

    row = pl.BlockSpec((tr, di), lambda i: (i, 0))
    vec = pl.BlockSpec((1, di), lambda i: (0, 0))
    return pl.pallas_call(
        body, name="gate_norm_bwd", grid=(t_dim // tr,),
        in_specs=[pl.BlockSpec((2, tr, di), lambda i: (0, i, 0)), row, row, vec, vec, row],
        out_specs=[row, row, vec, vec],
        out_shape=[jax.ShapeDtypeStruct((t_dim, di), F32), jax.ShapeDtypeStruct((t_dim, di), BF16),
                   jax.ShapeDtypeStruct((1, di), F32), jax.ShapeDtypeStruct((1, di), F32)],
        compiler_params=_cp("arbitrary"),
    )(y2, xbc, proj, dvec, ng, dyn)


def _dt_to_groups(dt):
    t_dim, h2 = dt.shape
    g_cnt = h2 // 2 // HEADS_PER_GROUP
    return dt.reshape(t_dim, 2, g_cnt, HEADS_PER_GROUP).transpose(1, 2, 0, 3)


def _dt_from_groups(dt4):
    _, g_cnt, t_dim, hg = dt4.shape
    return dt4.transpose(2, 0, 1, 3).reshape(t_dim, 2 * g_cnt * hg)


def _ssd_mixer_fwd(x, norm_g, w_in, conv_w, conv_b, dt_bias, a_log, d_skip, ssd_norm, w_out, j):
    heads = d_skip.shape[0]
    di = heads * SSD_HEAD_DIM
    g_cnt = heads // HEADS_PER_GROUP
    cd = conv_w.shape[-1]
    hn = _rmsnorm(x, norm_g)
    proj = _mm(hn, w_in, name="ssd_proj", b_sel=(j,), tn=1152)
    xbc = _conv_silu(proj, conv_w, conv_b, di)
    dt_raw = proj[:, di + cd:]
    dt = _softplus_fwd(dt_raw, dt_bias)
    dt4 = _dt_to_groups(dt)
    al4 = a_log.reshape(2, g_cnt, 1, HEADS_PER_GROUP)
    y2, hin = _ssd_fwd(xbc, dt4, al4)
    dvec = jnp.repeat(d_skip, SSD_HEAD_DIM).reshape(1, di)
    yn = _gate_norm(y2, xbc, proj, dvec, ssd_norm)
    x_new = _mm(yn, w_out, name="ssd_out", b_sel=(j,), add=x)
    return x_new, (x, hn, proj, xbc, dt_raw, dt4, al4, y2, hin, dvec, yn)


def _ssd_mixer_bwd(dx, saved, norm_g, w_in, conv_w, conv_b, dt_bias, a_log, ssd_norm, w_out, j):
    x, hn, proj, xbc, dt_raw, dt4, al4, y2, hin, dvec, yn = saved
    di = dvec.shape[1]
    heads = di // SSD_HEAD_DIM
    d_wout = _mm(yn, dx, name="ssd_dwout", ta=True, out_dtype=BF16)
    dyn = _mm(dx, w_out, name="ssd_dyn", tb=True, b_sel=(j,))
    dy, dz, d_ng, dd_col = _gate_norm_bwd(y2, xbc, proj, dvec, ssd_norm, dyn)
    dx2, db2, dc2, ddt4, da4 = _ssd_bwd(xbc, dt4, al4, hin, dy, dvec)
    dact2 = jnp.concatenate([dx2, db2, dc2], axis=-1)
    dxbc, dwb = _conv_silu_bwd(proj, conv_w, conv_b, dact2, di)
    ddt_raw, d_bias, d_alog = _softplus_bwd(dt_raw, dt_bias, _dt_from_groups(ddt4), da4.reshape(1, 2 * heads), a_log)
    dproj = jnp.concatenate([dz, dxbc, ddt_raw], axis=-1)
    d_win = _mm(hn, dproj, name="ssd_dwin", ta=True, out_dtype=BF16, tn=1152)
    dhn = _mm(dproj, w_in, name="ssd_dhn", tb=True, b_sel=(j,), tk=1152)
    dx_new, d_norm = _rmsnorm_bwd(x, norm_g, dhn, dx)
    small = dict(mix_norm=d_norm, conv_w=dwb[:CONV_W], conv_b=dwb[CONV_W:CONV_W + 1], dt_bias=d_bias, a_log=d_alog,
                 ssd_d=dd_col.reshape(heads, SSD_HEAD_DIM).sum(axis=1), ssd_norm=d_ng)
    return dx_new, small, d_win, d_wout


def _pool_count(t_idx, w, t_dim):
    hi = jnp.minimum(t_idx + w // 2, t_dim)
    lo = jnp.maximum(t_idx - w // 2, 0)
    return (hi - lo).astype(F32)


def _pool_mix(u, transpose):
    t_dim, d = u.shape
    gd = d // len(POOL_WINDOWS)
    cb = _tile(gd, 256)
    per = gd // cb

    def body(u_ref, o_ref):
        gi = pl.program_id(0)
        t_idx = lax.broadcasted_iota(jnp.int32, (t_dim, cb), 0)
        uv = u_ref[...]
        for widx, w in enumerate(POOL_WINDOWS):
            @pl.when(gi == widx)
            def _(w=w):
                cnt = _pool_count(t_idx, w, t_dim)
                src = uv / cnt if transpose else uv
                acc = jnp.zeros_like(uv)
                for k in range(-(w // 2), w // 2):
                    acc = acc + _shifted(src, -k if transpose else k, t_idx)
                res = acc - uv if transpose else acc / cnt - uv
                o_ref[...] = res.astype(BF16)

    spec = pl.BlockSpec((t_dim, cb), lambda gi, j: (0, gi * per + j))
    return pl.pallas_call(
        body, name="pool_mix_t" if transpose else "pool_mix", grid=(len(POOL_WINDOWS), per),
        in_specs=[spec], out_specs=spec, out_shape=jax.ShapeDtypeStruct((t_dim, d), BF16),
        compiler_params=_cp("parallel", "parallel"),
    )(u)


def _pool_group(mix, wgrp, scale, j):
    t_dim, d = mix.shape
    gd = wgrp.shape[-1]
    tm = _tile(t_dim, 512)

    def body(m_ref, w_ref, s_ref, v_ref, vs_ref):
        v = _dot(m_ref[...], w_ref[...])
        v_ref[...] = v
        vs_ref[...] = (v * s_ref[...]).astype(BF16)

    tile = pl.BlockSpec((tm, gd), lambda gi, m: (m, gi))
    return pl.pallas_call(
        body, name="pool_group", grid=(d // gd, t_dim // tm),
        in_specs=[tile, pl.BlockSpec((None, None, gd, gd), lambda gi, m: (j, gi, 0, 0)),
                  pl.BlockSpec((1, gd), lambda gi, m: (0, gi))],
        out_specs=[tile, tile],
        out_shape=[jax.ShapeDtypeStruct((t_dim, d), F32), jax.ShapeDtypeStruct((t_dim, d), BF16)],
        compiler_params=_cp("parallel", "parallel"),
    )(mix, wgrp, scale)


def _pool_group_bwd(dvs, v, mix, wgrp, scale, j):
    t_dim, d = mix.shape
    gd = wgrp.shape[-1]
    n_g = d // gd
    tm = _tile(t_dim, 512)
    nm = t_dim // tm

    def body(dvs_ref, v_ref, m_ref, w_ref, s_ref, dmix_ref, ds_ref, dw_ref, acc_ref):
        m = pl.program_id(1)
        dvsv = dvs_ref[...]

        @pl.when(m == 0)
        def _():
            ds_ref[...] = jnp.zeros_like(ds_ref)
            acc_ref[...] = jnp.zeros_like(acc_ref)

        ds_ref[...] += jnp.sum(dvsv * v_ref[...], axis=0, keepdims=True)
        dv16 = (dvsv * s_ref[...]).astype(BF16)
        dmix_ref[...] = _dot_nt(dv16, w_ref[...])
        acc_ref[...] += _dot(m_ref[...].T, dv16)

        @pl.when(m == nm - 1)
        def _():
            dw_ref[...] = acc_ref[...].astype(BF16)

    tile = pl.BlockSpec((tm, gd), lambda gi, m: (m, gi))
    vec = pl.BlockSpec((1, gd), lambda gi, m: (0, gi))
    return pl.pallas_call(
        body, name="pool_group_bwd", grid=(n_g, nm),
        in_specs=[tile, tile, tile, pl.BlockSpec((None, None, gd, gd), lambda gi, m: (j, gi, 0, 0)), vec],
        out_specs=[tile, vec, pl.BlockSpec((None, gd, gd), lambda gi, m: (gi, 0, 0))],
        out_shape=[jax.ShapeDtypeStruct((t_dim, d), F32), jax.ShapeDtypeStruct((1, d), F32),
                   jax.ShapeDtypeStruct((n_g, gd, gd), BF16)],
        scratch_shapes=[pltpu.VMEM((gd, gd), F32)],
        compiler_params=_cp("parallel", "arbitrary"),
    )(dvs, v, mix, wgrp, scale)


def _pool_mixer_fwd(x, norm_g, w_in, wgrp, scale, w_out, j):
    hn = _rmsnorm(x, norm_g)
    u = _mm(hn, w_in, name="pool_u", b_sel=(j,))
    mix = _pool_mix(u, False)
    v, vs = _pool_group(mix, wgrp, scale, j)
    x_new = _mm(vs, w_out, name="pool_out", b_sel=(j,), add=x)
    return x_new, (x, hn, mix, v, vs)


def _pool_mixer_bwd(dx, saved, norm_g, w_in, wgrp, scale, w_out, j):
    x, hn, mix, v, vs = saved
    d_wout = _mm(vs, dx, name="pool_dw", ta=True, out_dtype=BF16)
    dvs = _mm(dx, w_out, name="pool_dvs", tb=True, b_sel=(j,))
    dmix, d_scale, d_wgrp = _pool_group_bwd(dvs, v, mix, wgrp, scale, j)
    du = _pool_mix(dmix, True)
    d_win = _mm(hn, du, name="pool_dw", ta=True, out_dtype=BF16)
    dhn = _mm(du, w_in, name="pool_dhn", tb=True, b_sel=(j,))
    dx_new, d_norm = _rmsnorm_bwd(x, norm_g, dhn, dx)
    return dx_new, d_norm, d_scale, d_win, d_wgrp, d_wout


def _local_step(x, target, big, small):
    depth = len(big["ffn_gate"])
    d = x.shape[1]
    saved = []
    for i in range(depth):
        j = i // 2
        x, s0 = _ffn_fwd(x, small["ffn_norm"][i, 0][None], big["ffn_gate"][i], big["ffn_up"][i], big["ffn_down"][i], 0)
        mg = small["mix_norm"][i][None]
        if i % 2 == 0:
            heads = small["ssd_d"].shape[1]
            x, s1 = _ssd_mixer_fwd(x, mg, big["ssd_w_in"], small["ssd_conv_w"][j], small["ssd_conv_b"][j][None],
                                   small["ssd_dt_bias"][j].reshape(1, 2 * heads), small["ssd_a_log"][j].reshape(1, 2 * heads),
                                   small["ssd_d"][j], small["ssd_norm"][j][None], big["ssd_w_out"], j)
        else:
            x, s1 = _pool_mixer_fwd(x, mg, big["pool_w_in"], big["pool_w_group"], small["pool_scale"][j][None],
                                    big["pool_w_out"], j)
        x, s2 = _ffn_fwd(x, small["ffn_norm"][i, 1][None], big["ffn_gate"][i], big["ffn_up"][i], big["ffn_down"][i], 1)
        saved.append((s0, s1, s2))
    loss, dx, d_final = _loss_head(x, small["final_norm"][None], target)

    gbig = {k: {} for k in ("ffn_gate", "ffn_up", "ffn_down", "ssd_w_in", "ssd_w_out", "pool_w_in", "pool_w_group",
                            "pool_w_out")}
    gs = {k: {} for k in ("ffn_norm", "mix_norm", "ssd_conv_w", "ssd_conv_b", "ssd_dt_bias", "ssd_a_log", "ssd_d",
                          "ssd_norm", "pool_scale")}
    for i in reversed(range(depth)):
        j = i // 2
        s0, s1, s2 = saved[i]
        for half, sv in ((1, s2), (0, None)):
            if half == 0:
                sv = s0
                mg = small["mix_norm"][i][None]
                if i % 2 == 0:
                    heads = small["ssd_d"].shape[1]
                    dx, sm, d_win, d_wout = _ssd_mixer_bwd(
                        dx, s1, mg, big["ssd_w_in"], small["ssd_conv_w"][j], small["ssd_conv_b"][j][None],
                        small["ssd_dt_bias"][j].reshape(1, 2 * heads), small["ssd_a_log"][j].reshape(1, 2 * heads),
                        small["ssd_norm"][j][None], big["ssd_w_out"], j)
                    gs["mix_norm"][i] = sm["mix_norm"][0]
                    gs["ssd_conv_w"][j] = sm["conv_w"]
                    gs["ssd_conv_b"][j] = sm["conv_b"][0]
                    gs["ssd_dt_bias"][j] = sm["dt_bias"].reshape(2, heads)
                    gs["ssd_a_log"][j] = sm["a_log"].reshape(2, heads)
                    gs["ssd_d"][j] = sm["ssd_d"]
                    gs["ssd_norm"][j] = sm["ssd_norm"][0]
                    gbig["ssd_w_in"][j] = d_win
                    gbig["ssd_w_out"][j] = d_wout
                else:
                    dx, d_norm, d_scale, d_win, d_wgrp, d_wout = _pool_mixer_bwd(
                        dx, s1, mg, big["pool_w_in"], big["pool_w_group"], small["pool_scale"][j][None],
                        big["pool_w_out"], j)
                    gs["mix_norm"][i] = d_norm[0]
                    gs["pool_scale"][j] = d_scale[0]
                    gbig["pool_w_in"][j] = d_win
                    gbig["pool_w_group"][j] = d_wgrp
                    gbig["pool_w_out"][j] = d_wout
            dx, d_norm, d_wg, d_wu, d_wd = _ffn_bwd(dx, sv, small["ffn_norm"][i, half][None], big["ffn_gate"][i],
                                                   big["ffn_up"][i], big["ffn_down"][i], half)
            gs["ffn_norm"][(i, half)] = d_norm[0]
            gbig["ffn_gate"][(i, half)] = d_wg
            gbig["ffn_up"][(i, half)] = d_wu
            gbig["ffn_down"][(i, half)] = d_wd
    n_s, n_p = (depth + 1) // 2, depth // 2
    gsmall = dict(
        ffn_norm=jnp.stack([jnp.stack([gs["ffn_norm"][(i, h)] for h in range(2)]) for i in range(depth)]),
        mix_norm=jnp.stack([gs["mix_norm"][i] for i in range(depth)]),
        ssd_conv_w=jnp.stack([gs["ssd_conv_w"][j] for j in range(n_s)]),
        ssd_conv_b=jnp.stack([gs["ssd_conv_b"][j] for j in range(n_s)]),
        ssd_dt_bias=jnp.stack([gs["ssd_dt_bias"][j] for j in range(n_s)]),
        ssd_a_log=jnp.stack([gs["ssd_a_log"][j] for j in range(n_s)]),
        ssd_d=jnp.stack([gs["ssd_d"][j] for j in range(n_s)]),
        ssd_norm=jnp.stack([gs["ssd_norm"][j] for j in range(n_s)]),
        pool_scale=jnp.stack([gs["pool_scale"][j] for j in range(n_p)]),
        final_norm=d_final[0],
    )
    return loss, dx, gbig, gsmall


ANY = pl.BlockSpec(memory_space=pl.ANY)


def _mesh_pos():
    return lax.axis_index("x"), lax.axis_index("y"), lax.axis_index("c")


def _other_chips(x, y):
    return [(1 - x, y), (x, 1 - y), (1 - x, 1 - y)]


def _win(ref, windows, lead=()):
    rest = len(ref.shape) - len(lead)
    idx = tuple(lead) + tuple(pl.ds(*windows[ax]) if ax in windows else slice(None) for ax in range(rest))
    return ref.at[idx]


def _remote(src, dst, send_sems, recv_sems, k, peer):
    return pltpu.make_async_remote_copy(src_ref=src, dst_ref=dst, send_sem=send_sems.at[k], recv_sem=recv_sems.at[k],
                                        device_id=peer, device_id_type=MESH)


def _cast_place(w3, chip1, mode, l0, nl):
    _, r, c = w3.shape
    tr = _tile(r, 256, 16)
    nr = r // tr
    if mode == "cols":
        out_shape, blk = (nl, r, N_CHIPS * c), (None, tr, c)
        omap = lambda l, i, s: (l, i, s[0])
    elif mode == "rows":
        out_shape, blk = (nl, N_CHIPS * r, c), (None, tr, c)
        omap = lambda l, i, s: (l, s[0] * nr + i, 0)
    else:
        out_shape, blk = (nl, N_CHIPS, r, c), (None, None, tr, c)
        omap = lambda l, i, s: (l, s[0], i, 0)

    def body(s_ref, w_ref, o_ref):
        o_ref[...] = w_ref[...].astype(BF16)

    return pl.pallas_call(
        body, name="cast_place_" + mode, out_shape=jax.ShapeDtypeStruct(out_shape, BF16),
        grid_spec=pltpu.PrefetchScalarGridSpec(
            num_scalar_prefetch=1, grid=(nl, nr),
            in_specs=[pl.BlockSpec((None, tr, c), lambda l, i, s: (l0 + l, i, 0))],
            out_specs=pl.BlockSpec(blk, omap)),
        compiler_params=_cp("parallel", "parallel"),
    )(chip1, w3)


def _gather(name, fulls, axes):
    n = len(fulls)

    def body(*refs):
        outs = refs[n:2 * n]
        send_sems, recv_sems = refs[2 * n:]
        x, y, c = _mesh_pos()
        s = 2 * x + y
        chips = _other_chips(x, y)

        def full_win(t, sidx, hidx=None):
            sa, ha = axes[t]
            ssz = fulls[t].shape[sa] // N_CHIPS
            w = {sa: (sidx * ssz, ssz)}
            if hidx is not None:
                hsz = fulls[t].shape[ha] // 2
                w[ha] = (hidx * hsz, hsz)
            return _win(outs[t], w)

        sends, fwds = [], []
        for t in range(n):
            for j, (px, py) in enumerate(chips):
                cp = _remote(full_win(t, s, c), full_win(t, s, c), send_sems, recv_sems, 6 * t + j, (px, py, c))
                cp.start()
                sends.append(cp)
        for t in range(n):
            for j, (px, py) in enumerate(chips):
                w = full_win(t, 2 * px + py, c)
                _remote(w, w, send_sems, recv_sems, 6 * t + j, (px, py, c)).wait_recv()
                fw = _remote(w, w, send_sems, recv_sems, 6 * t + 3 + j, (x, y, 1 - c))
                fw.start()
                fwds.append(fw)
        for t in range(n):
            for j, (px, py) in enumerate(chips):
                w = full_win(t, 2 * px + py, 1 - c)
                _remote(w, w, send_sems, recv_sems, 6 * t + 3 + j, (x, y, 1 - c)).wait_recv()
        for cp in sends + fwds:
            cp.wait_send()

    return pl.pallas_call(
        body, name=name, in_specs=[ANY] * n, out_specs=[ANY] * n,
        out_shape=[jax.ShapeDtypeStruct(f.shape, f.dtype) for f in fulls],
        input_output_aliases={t: t for t in range(n)},
        scratch_shapes=[pltpu.SemaphoreType.DMA((6 * n,)), pltpu.SemaphoreType.DMA((6 * n,))],
    )(*fulls)


def _exchange(name, inputs, out_shapes, plan, n_copies):
    n_in, n_out = len(inputs), len(out_shapes)

    def body(*refs):
        in_refs, out_refs = refs[:n_in], refs[n_in:n_in + n_out]
        send_sems, recv_sems = refs[n_in + n_out:]
        x, y, c = _mesh_pos()
        copies = plan(in_refs, out_refs, x, y, c)
        assert len(copies) == n_copies
        started = []
        for k, (src, dst, peer) in enumerate(copies):
            cp = _remote(src, dst, send_sems, recv_sems, k, peer)
            cp.start()
            started.append(cp)
        for cp in started:
            cp.wait()

    return pl.pallas_call(
        body, name=name, in_specs=[ANY] * n_in, out_specs=[ANY] * n_out, out_shape=out_shapes,
        scratch_shapes=[pltpu.SemaphoreType.DMA((n_copies,)), pltpu.SemaphoreType.DMA((n_copies,))],
    )(*inputs)


def _halved(shape, ha):
    out = list(shape)
    out[ha] //= 2
    return tuple(out)


def _sharded(shape, sa):
    out = list(shape)
    out[sa] //= N_CHIPS
    return tuple(out)


def _rs_cores(name, grads, axes):
    n = len(grads)
    shapes = [jax.ShapeDtypeStruct(_halved(g.shape, ha), g.dtype) for g, (_, ha) in zip(grads, axes)]

    def plan(in_refs, out_refs, x, y, c):
        copies = []
        for t in range(n):
            ha = axes[t][1]
            hsz = grads[t].shape[ha] // 2
            copies.append((_win(in_refs[t], {ha: ((1 - c) * hsz, hsz)}), out_refs[t], (x, y, 1 - c)))
        return copies

    return _exchange(name, grads, shapes, plan, n)


def _rs_chips(name, halves, axes):
    n = len(halves)
    recv_shapes = [jax.ShapeDtypeStruct((N_CHIPS - 1,) + _sharded(h.shape, sa), h.dtype)
                   for h, (sa, _) in zip(halves, axes)]

    def plan(in_refs, out_refs, x, y, c):
        copies = []
        for t in range(n):
            sa = axes[t][0]
            ssz = halves[t].shape[sa] // N_CHIPS
            for j, (px, py) in enumerate(_other_chips(x, y)):
                copies.append((_win(in_refs[t], {sa: ((2 * px + py) * ssz, ssz)}), out_refs[t].at[j], (px, py, c)))
        return copies

    return _exchange(name, halves, recv_shapes, plan, 3 * n)


def _rs_finish(name, groups):
    flat = [(gi, t) for gi, grp in enumerate(groups) for t in range(len(grp))]
    shapes = [jax.ShapeDtypeStruct((len(grp),) + grp[0].shape, grp[0].dtype) for grp in groups]

    def plan(in_refs, out_refs, x, y, c):
        return [(in_refs[k], out_refs[gi].at[t], (x, y, 1 - c)) for k, (gi, t) in enumerate(flat)]

    return _exchange(name, [a for grp in groups for a in grp], shapes, plan, len(flat))


def _allgather8(v, reduce):
    rows = v.shape[0]

    def body(v_ref, o_ref, *scratch):
        if reduce:
            buf, send_sems, recv_sems, loc_sem = scratch
        else:
            buf = o_ref
            send_sems, recv_sems, loc_sem = scratch
        x, y, c = _mesh_pos()
        me = 4 * x + 2 * y + c
        lc = pltpu.make_async_copy(v_ref, buf.at[me], loc_sem)
        lc.start()
        sends = []
        for k in range(1, 8):
            kx, ky, kc = k // 4, (k // 2) % 2, k % 2
            peer = (x ^ kx, y ^ ky, c ^ kc)
            cp = _remote(v_ref, buf.at[me], send_sems, recv_sems, k - 1, peer)
            cp.start()
            sends.append(cp)
        for k in range(1, 8):
            kx, ky, kc = k // 4, (k // 2) % 2, k % 2
            src = 4 * (x ^ kx) + 2 * (y ^ ky) + (c ^ kc)
            _remote(v_ref, buf.at[src], send_sems, recv_sems, k - 1, (x, y, c)).wait_recv()
        for cp in sends:
            cp.wait_send()
        lc.wait()
        if reduce:
            tot = buf[0]
            for i in range(1, 8):
                tot = tot + buf[i]
            o_ref[...] = tot

    vm = pl.BlockSpec(memory_space=pltpu.VMEM)
    sems = [pltpu.SemaphoreType.DMA((7,)), pltpu.SemaphoreType.DMA((7,)), pltpu.SemaphoreType.DMA]
    if reduce:
        out_shape = jax.ShapeDtypeStruct((rows, LANE), F32)
        scratch = [pltpu.VMEM((8, rows, LANE), F32)] + sems
    else:
        out_shape = jax.ShapeDtypeStruct((8, rows, LANE), F32)
        scratch = sems
    return pl.pallas_call(
        body, name="allreduce_small" if reduce else "allgather_small", in_specs=[vm], out_specs=vm,
        out_shape=out_shape, scratch_shapes=scratch,
    )(v)


def _rows2d(a):
    return a.reshape(-1, a.shape[-1])


def _view3(a):
    return a.reshape((-1,) + a.shape[-2:])


def _add_pair(g, recv, ha, core1):
    g3, r3 = _view3(g), _view3(recv)
    rows_half = ha + 3 - g.ndim == 1
    n_l, r, c = r3.shape
    tr = _tile(r, 256, 16)
    nr = r // tr
    gmap = (lambda l, i, s: (l, s[0] * nr + i, 0)) if rows_half else (lambda l, i, s: (l, i, s[0]))

    def body(s_ref, g_ref, r_ref, o_ref):
        o_ref[...] = (g_ref[...].astype(F32) + r_ref[...].astype(F32)).astype(BF16)

    spec = pl.BlockSpec((None, tr, c), lambda l, i, s: (l, i, 0))
    out = pl.pallas_call(
        body, name="add_pair", out_shape=jax.ShapeDtypeStruct(r3.shape, BF16),
        grid_spec=pltpu.PrefetchScalarGridSpec(num_scalar_prefetch=1, grid=(n_l, nr),
                                               in_specs=[pl.BlockSpec((None, tr, c), gmap), spec], out_specs=spec),
        compiler_params=_cp("parallel", "parallel"))(core1, g3, r3)
    return out.reshape(recv.shape)


def _add_four(cs, recv, sa, chip1):
    c3 = _view3(cs)
    s3 = sa + 3 - cs.ndim
    lo, ro, co = (dim // N_CHIPS if ax == s3 else dim for ax, dim in enumerate(c3.shape))
    r4 = recv.reshape(N_CHIPS - 1, lo, ro, co)
    tr = _tile(ro, 256, 16)
    nr = ro // tr
    if s3 == 0:
        cmap = lambda l, i, s: (s[0] * lo + l, i, 0)
    elif s3 == 1:
        cmap = lambda l, i, s: (l, s[0] * nr + i, 0)
    else:
        cmap = lambda l, i, s: (l, i, s[0])

    def body(s_ref, c_ref, r_ref, out_ref):
        out_ref[...] = ((c_ref[...].astype(F32) + r_ref[0].astype(F32)) + r_ref[1].astype(F32)) + r_ref[2].astype(F32)

    out = pl.pallas_call(
        body, name="add_four", out_shape=jax.ShapeDtypeStruct((lo, ro, co), F32),
        grid_spec=pltpu.PrefetchScalarGridSpec(
            num_scalar_prefetch=1, grid=(lo, nr),
            in_specs=[pl.BlockSpec((None, tr, co), cmap),
                      pl.BlockSpec((N_CHIPS - 1, None, tr, co), lambda l, i, s: (0, l, i, 0))],
            out_specs=pl.BlockSpec((None, tr, co), lambda l, i, s: (l, i, 0))),
        compiler_params=_cp("parallel", "parallel"))(chip1, c3, r4)
    return out.reshape(recv.shape[1:])


def _adamw_halves(w, own, recv, m, v, rows_half, core1):
    w3, m3, v3, o3, r3 = (_view3(a) for a in (w, m, v, own, recv))
    n_l, _, c = w3.shape
    _, rh, ch = o3.shape
    tr = _tile(rh, 128, 8)
    nr = rh // tr
    c1 = 1.0 - ADAM_B1 ** ADAM_STEP
    c2 = 1.0 - ADAM_B2 ** ADAM_STEP

    def body(s_ref, w_ref, o_ref, r_ref, m_ref, v_ref, g_ref, d_ref, nm_ref, nv_ref):
        gv = jnp.where(pl.program_id(1) == s_ref[0], o_ref[...], r_ref[...])
        nm = ADAM_B1 * m_ref[...] + (1.0 - ADAM_B1) * gv
        nv = ADAM_B2 * v_ref[...] + (1.0 - ADAM_B2) * (gv * gv)
        g_ref[...] = gv
        nm_ref[...] = nm
        nv_ref[...] = nv
        d_ref[...] = -ADAM_LR * ((nm / c1) / (jnp.sqrt(nv / c2) + ADAM_EPS) + ADAM_WD * w_ref[...])

    wmap = (lambda l, h, i, s: (l, h * nr + i, 0)) if rows_half else (lambda l, h, i, s: (l, i, h))
    wspec = pl.BlockSpec((None, tr, ch), wmap)
    gspec = pl.BlockSpec((None, tr, ch), lambda l, h, i, s: (l, i, 0))
    osh = jax.ShapeDtypeStruct(w3.shape, F32)
    outs = pl.pallas_call(
        body, name="adamw_halves", out_shape=[osh] * 4,
        grid_spec=pltpu.PrefetchScalarGridSpec(num_scalar_prefetch=1, grid=(n_l, 2, nr),
                                               in_specs=[wspec, gspec, gspec, wspec, wspec], out_specs=[wspec] * 4),
        compiler_params=_cp("parallel", "parallel", "parallel"))(core1, w3, o3, r3, m3, v3)
    return tuple(o.reshape(w.shape) for o in outs)


def _adamw(w, g, m, v):
    shape = w.shape
    w2, g2, m2, v2 = (_rows2d(a) if a.ndim > 1 else a.reshape(1, -1) for a in (w, g, m, v))
    rows, cols = w2.shape
    tr = _tile(rows, 256, 8)
    c1 = 1.0 - ADAM_B1 ** ADAM_STEP
    c2 = 1.0 - ADAM_B2 ** ADAM_STEP

    def body(w_ref, g_ref, m_ref, v_ref, d_ref, nm_ref, nv_ref):
        gv = g_ref[...]
        nm = ADAM_B1 * m_ref[...] + (1.0 - ADAM_B1) * gv
        nv = ADAM_B2 * v_ref[...] + (1.0 - ADAM_B2) * (gv * gv)
        nm_ref[...] = nm
        nv_ref[...] = nv
        d_ref[...] = -ADAM_LR * ((nm / c1) / (jnp.sqrt(nv / c2) + ADAM_EPS) + ADAM_WD * w_ref[...])

    spec = pl.BlockSpec((tr, cols), lambda i: (i, 0))
    osh = jax.ShapeDtypeStruct((rows, cols), F32)
    outs = pl.pallas_call(body, name="adamw", grid=(rows // tr,), in_specs=[spec] * 4, out_specs=[spec] * 3,
                          out_shape=[osh] * 3, compiler_params=_cp("parallel"))(w2, g2, m2, v2)
    return tuple(o.reshape(shape) for o in outs)


def _pack(arrs):
    flat = jnp.concatenate([a.reshape(-1) for a in arrs])
    n = flat.shape[0]
    rows = -(-n // (8 * LANE)) * 8
    return jnp.pad(flat, (0, rows * LANE - n)).reshape(rows, LANE)


def _unpack(packed, shapes):
    flat = packed.reshape(-1)
    out, pos = [], 0
    for sh in shapes:
        size = 1
        for dsz in sh:
            size *= dsz
        out.append(flat[pos:pos + size].reshape(sh))
        pos += size
    return out


BIG = ("ffn_w_gate", "ffn_w_up", "ffn_w_down", "ssd_w_in", "ssd_w_out", "pool_w_in", "pool_w_group", "pool_w_out")
WEIGHTS = ("ffn_norm", "ffn_w_gate", "ffn_w_up", "ffn_w_down", "mix_norm", "ssd_w_in", "ssd_conv_w", "ssd_conv_b",
           "ssd_dt_bias", "ssd_a_log", "ssd_d", "ssd_norm", "ssd_w_out", "pool_w_in", "pool_w_group", "pool_scale",
           "pool_w_out", "final_norm")
SMALL = tuple(k for k in WEIGHTS if k not in BIG)
SMALL_SHARDED = {"ffn_norm": 2, "ssd_conv_w": 2, "pool_scale": 1}


def _reduce_group(tag, grads, axes, core1, chip1):
    recv_a = _rs_cores("rs_cores_" + tag, grads, axes)
    chip_sums = [_add_pair(g, r, ha, core1) for g, r, (_, ha) in zip(grads, recv_a, axes)]
    recv_b = _rs_chips("rs_chips_" + tag, chip_sums, axes)
    return [_add_four(cs, r, sa, chip1) for cs, r, (sa, _) in zip(chip_sums, recv_b, axes)]


def kernel(x, ffn_norm, ffn_w_gate, ffn_w_up, ffn_w_down, mix_norm, ssd_w_in, ssd_conv_w, ssd_conv_b, ssd_dt_bias, ssd_a_log, ssd_d, ssd_norm, ssd_w_out, pool_w_in, pool_w_group, pool_scale, pool_w_out, final_norm, loss_target, m_ffn_norm, m_ffn_w_gate, m_ffn_w_up, m_ffn_w_down, m_mix_norm, m_ssd_w_in, m_ssd_conv_w, m_ssd_conv_b, m_ssd_dt_bias, m_ssd_a_log, m_ssd_d, m_ssd_norm, m_ssd_w_out, m_pool_w_in, m_pool_w_group, m_pool_scale, m_pool_w_out, m_final_norm, v_ffn_norm, v_ffn_w_gate, v_ffn_w_up, v_ffn_w_down, v_mix_norm, v_ssd_w_in, v_ssd_conv_w, v_ssd_conv_b, v_ssd_dt_bias, v_ssd_a_log, v_ssd_d, v_ssd_norm, v_ssd_w_out, v_pool_w_in, v_pool_w_group, v_pool_scale, v_pool_w_out, v_final_norm):
    w = dict(ffn_norm=ffn_norm, ffn_w_gate=ffn_w_gate, ffn_w_up=ffn_w_up, ffn_w_down=ffn_w_down, mix_norm=mix_norm,
             ssd_w_in=ssd_w_in, ssd_conv_w=ssd_conv_w, ssd_conv_b=ssd_conv_b, ssd_dt_bias=ssd_dt_bias,
             ssd_a_log=ssd_a_log, ssd_d=ssd_d, ssd_norm=ssd_norm, ssd_w_out=ssd_w_out, pool_w_in=pool_w_in,
             pool_w_group=pool_w_group, pool_scale=pool_scale, pool_w_out=pool_w_out, final_norm=final_norm)
    mom = dict(ffn_norm=m_ffn_norm, ffn_w_gate=m_ffn_w_gate, ffn_w_up=m_ffn_w_up, ffn_w_down=m_ffn_w_down,
               mix_norm=m_mix_norm, ssd_w_in=m_ssd_w_in, ssd_conv_w=m_ssd_conv_w, ssd_conv_b=m_ssd_conv_b,
               ssd_dt_bias=m_ssd_dt_bias, ssd_a_log=m_ssd_a_log, ssd_d=m_ssd_d, ssd_norm=m_ssd_norm,
               ssd_w_out=m_ssd_w_out, pool_w_in=m_pool_w_in, pool_w_group=m_pool_w_group, pool_scale=m_pool_scale,
               pool_w_out=m_pool_w_out, final_norm=m_final_norm)
    vel = dict(ffn_norm=v_ffn_norm, ffn_w_gate=v_ffn_w_gate, ffn_w_up=v_ffn_w_up, ffn_w_down=v_ffn_w_down,
               mix_norm=v_mix_norm, ssd_w_in=v_ssd_w_in, ssd_conv_w=v_ssd_conv_w, ssd_conv_b=v_ssd_conv_b,
               ssd_dt_bias=v_ssd_dt_bias, ssd_a_log=v_ssd_a_log, ssd_d=v_ssd_d, ssd_norm=v_ssd_norm,
               ssd_w_out=v_ssd_w_out, pool_w_in=v_pool_w_in, pool_w_group=v_pool_w_group, pool_scale=v_pool_scale,
               pool_w_out=v_pool_w_out, final_norm=v_final_norm)
    depth = ffn_w_gate.shape[0]
    n_s, n_p = ssd_w_in.shape[0], pool_w_in.shape[0]
    chip = 2 * lax.axis_index("x") + lax.axis_index("y")

    chip1 = jnp.reshape(chip, (1,)).astype(jnp.int32)
    core1 = jnp.reshape(lax.axis_index("c"), (1,)).astype(jnp.int32)

    big = dict(ffn_gate=[], ffn_up=[], ffn_down=[])
    gate3, up3, down3 = _view3(ffn_w_gate), _view3(ffn_w_up), _view3(ffn_w_down)
    for i in range(depth):
        fulls = [_cast_place(gate3, chip1, "cols", 2 * i, 2), _cast_place(up3, chip1, "cols", 2 * i, 2),
                 _cast_place(down3, chip1, "rows", 2 * i, 2)]
        g_i, u_i, d_i = _gather("gather_ffn", fulls, [(2, 1), (2, 1), (1, 2)])
        big["ffn_gate"].append(g_i)
        big["ffn_up"].append(u_i)
        big["ffn_down"].append(d_i)
    d_model = ssd_w_in.shape[1]
    w_in4, big["ssd_w_out"] = _gather(
        "gather_ssd", [_cast_place(ssd_w_in, chip1, "slot", 0, n_s), _cast_place(ssd_w_out, chip1, "rows", 0, n_s)],
        [(1, 2), (1, 2)])
    big["ssd_w_in"] = w_in4.transpose(0, 2, 1, 3).reshape(n_s, d_model, -1)
    grp_full = _cast_place(_view3(pool_w_group), chip1, "rows", 0, n_p * pool_w_group.shape[1])
    grp_full = grp_full.reshape(pool_w_group.shape[:2] + grp_full.shape[1:])
    big["pool_w_in"], big["pool_w_group"], big["pool_w_out"] = _gather(
        "gather_pool", [_cast_place(pool_w_in, chip1, "rows", 0, n_p), grp_full,
                        _cast_place(pool_w_out, chip1, "rows", 0, n_p)], [(1, 2), (2, 3), (1, 2)])
    sharded_names = tuple(SMALL_SHARDED)
    gathered = _allgather8(_pack([w[k] for k in sharded_names]), False)
    small = {k: w[k] for k in SMALL if k not in SMALL_SHARDED}
    per_chip = [_unpack(gathered[2 * s], [w[k].shape for k in sharded_names]) for s in range(N_CHIPS)]
    for t, k in enumerate(sharded_names):
        small[k] = jnp.concatenate([per_chip[s][t] for s in range(N_CHIPS)], axis=SMALL_SHARDED[k])

    loss, dx, gbig, gsmall = _local_step(x[0], loss_target[0], big, small)
    loss = lax.psum(loss[0, 0], ("x", "y", "c"))

    sums = {k: [] for k in BIG}
    ffn_names = ("ffn_w_gate", "ffn_w_up", "ffn_w_down")
    for i in range(depth):
        grads = [gbig[k][(i, h)] for k in ("ffn_gate", "ffn_up", "ffn_down") for h in range(2)]
        res = _reduce_group("ffn", grads, [(1, 0)] * 4 + [(0, 1)] * 2, core1, chip1)
        for t, k in enumerate(ffn_names):
            sums[k] += res[2 * t:2 * t + 2]
    for j in range(n_s):
        g_in = gbig["ssd_w_in"][j]
        g_in = g_in.reshape(d_model, N_CHIPS, -1).transpose(1, 0, 2)
        res = _reduce_group("ssd", [g_in, gbig["ssd_w_out"][j]], [(0, 1), (0, 1)], core1, chip1)
        sums["ssd_w_in"].append(res[0])
        sums["ssd_w_out"].append(res[1])
    for j in range(n_p):
        res = _reduce_group("pool", [gbig["pool_w_in"][j], gbig["pool_w_group"][j], gbig["pool_w_out"][j]],
                            [(0, 1), (1, 2), (0, 1)], core1, chip1)
        for k, r in zip(("pool_w_in", "pool_w_group", "pool_w_out"), res):
            sums[k].append(r)
    mixer_names = tuple(k for k in BIG if k not in ffn_names)
    from_sibling = dict(zip(ffn_names, _rs_finish("rs_finish_ffn", [sums[k] for k in ffn_names])))
    from_sibling.update(zip(mixer_names, _rs_finish("rs_finish_mixers", [sums[k] for k in mixer_names])))

    grad, delta, new_m, new_v = {}, {}, {}, {}
    rows_half = dict(ffn_w_gate=True, ffn_w_up=True, ffn_w_down=False, ssd_w_in=True, ssd_w_out=False,
                     pool_w_in=False, pool_w_group=False, pool_w_out=False)
    for k in BIG:
        grad[k], delta[k], new_m[k], new_v[k] = _adamw_halves(w[k], jnp.stack(sums[k]), from_sibling[k], mom[k],
                                                              vel[k], rows_half[k], core1)

    small_shapes = [gsmall[k].shape for k in SMALL]
    summed = _unpack(_allgather8(_pack([gsmall[k] for k in SMALL]), True), small_shapes)
    for k, g in zip(SMALL, summed):
        if k in SMALL_SHARDED:
            ax = SMALL_SHARDED[k]
            size = w[k].shape[ax]
            g = lax.dynamic_slice_in_dim(g, chip * size, size, axis=ax)
        grad[k] = g

    shapes = [w[k].shape for k in SMALL]
    packed = _adamw(*(_pack([src[k] for k in SMALL]) for src in (w, grad, mom, vel)))
    for dst, p in zip((delta, new_m, new_v), packed):
        for k, a in zip(SMALL, _unpack(p, shapes)):
            dst[k] = a

    return (loss, dx[None], *[grad[k] for k in WEIGHTS], *[delta[k] for k in WEIGHTS],
            *[new_m[k] for k in WEIGHTS], *[new_v[k] for k in WEIGHTS])
```

```python
import functools

import jax
import jax.numpy as jnp
from jax import lax
from jax.experimental import pallas as pl
from jax.experimental.pallas import tpu as pltpu

F32 = jnp.float32
BF16 = jnp.bfloat16
EPS = 1e-6
MESH = pl.DeviceIdType.MESH

SSD_CHUNK = 128
SSD_STATE = 128
SSD_HEAD_DIM = 64
HEADS_PER_GROUP = 8
GROUP_W = HEADS_PER_GROUP * SSD_HEAD_DIM
CONV_W = 5
POOL_WINDOWS = (2, 4, 8, 16)
N_CHIPS = 4

ADAM_LR = 0.001
ADAM_B1 = 0.9
ADAM_B2 = 0.999
ADAM_EPS = 1e-08
ADAM_WD = 0.01
ADAM_STEP = 10

VMEM_LIMIT = 56 * 1024 * 1024
LANE = 128


def _cp(*sem):
    return pltpu.CompilerParams(dimension_semantics=sem, vmem_limit_bytes=VMEM_LIMIT)


def _tile(dim, pref, unit=LANE):
    if dim <= pref:
        return dim
    t = (pref // unit) * unit
    while t >= unit:
        if dim % t == 0:
            return t
        t -= unit
    return dim


def _sigmoid(v):
    return 1.0 / (1.0 + jnp.exp(-v))


def _dot(a, b):
    return jnp.dot(a, b, preferred_element_type=F32)


def _dot_nt(a, b):
    return lax.dot_general(a, b, (((1,), (1,)), ((), ())), preferred_element_type=F32)


def _split3(v):
    h1 = v.astype(BF16)
    r1 = v - h1.astype(F32)
    h2 = r1.astype(BF16)
    h3 = (r1 - h2.astype(F32)).astype(BF16)
    return h1, h2, h3


def _dot_exact01(m01, v):
    mb = m01.astype(BF16)
    h1, h2, h3 = _split3(v)
    return _dot(mb, h1) + _dot(mb, h2) + _dot(mb, h3)


def _dot_nt_exact01(m01, v):
    mb = m01.astype(BF16)
    h1, h2, h3 = _split3(v)
    return _dot_nt(mb, h1) + _dot_nt(mb, h2) + _dot_nt(mb, h3)


def _mm(a, b, *, name, ta=False, tb=False, a_sel=(), b_sel=(), pair2=None, add=None, scale=1.0,
        out_dtype=F32, tm=1024, tn=1024, tk=2048):
    am, ak = a.shape[-2:][::-1] if ta else a.shape[-2:]
    bk, bn = b.shape[-2:][::-1] if tb else b.shape[-2:]
    assert ak == bk, (a.shape, b.shape, ta, tb)
    m_dim, n_dim, k_dim = am, bn, ak
    tm, tn, tk = _tile(m_dim, tm), _tile(n_dim, tn), _tile(k_dim, tk)
    nk = k_dim // tk
    grid = (n_dim // tn, m_dim // tm, nk)

    def a_spec(sel):
        lead = (None,) * len(sel)
        if ta:
            return pl.BlockSpec(lead + (tk, tm), lambda n, m, k: tuple(sel) + (k, m))
        return pl.BlockSpec(lead + (tm, tk), lambda n, m, k: tuple(sel) + (m, k))

    def b_spec(sel):
        lead = (None,) * len(sel)
        if tb:
            return pl.BlockSpec(lead + (tn, tk), lambda n, m, k: tuple(sel) + (n, k))
        return pl.BlockSpec(lead + (tk, tn), lambda n, m, k: tuple(sel) + (k, n))

    ins, specs = [a, b], [a_spec(a_sel), b_spec(b_sel)]
    if pair2 is not None:
        a2, b2, a2_sel, b2_sel = pair2
        ins += [a2, b2]
        specs += [a_spec(a2_sel), b_spec(b2_sel)]
    if add is not None:
        ins.append(add)
        specs.append(pl.BlockSpec((tm, tn), lambda n, m, k: (m, n)))
    dn = (((0 if ta else 1,), (1 if tb else 0,)), ((), ()))
    n_pairs = 2 if pair2 is not None else 1

    def body(*refs):
        pairs = [(refs[2 * i], refs[2 * i + 1]) for i in range(n_pairs)]
        pos = 2 * n_pairs
        add_ref = None
        if add is not None:
            add_ref = refs[pos]
            pos += 1
        o_ref = refs[pos]
        acc_ref = refs[pos + 1] if nk > 1 else None

        def prod():
            tot = None
            for ar, br in pairs:
                p = lax.dot_general(ar[...].astype(BF16), br[...].astype(BF16), dn, preferred_element_type=F32)
                tot = p if tot is None else tot + p
            return tot

        def finish(r):
            if scale != 1.0:
                r = r * scale
            if add_ref is not None:
                r = add_ref[...] + r
            o_ref[...] = r.astype(out_dtype)

        if nk == 1:
            finish(prod())
        else:
            k = pl.program_id(2)

            @pl.when(k == 0)
            def _():
                acc_ref[...] = jnp.zeros_like(acc_ref)

            acc_ref[...] += prod()

            @pl.when(k == nk - 1)
            def _():
                finish(acc_ref[...])

    return pl.pallas_call(
        body, name=name, grid=grid, in_specs=specs,
        out_specs=pl.BlockSpec((tm, tn), lambda n, m, k: (m, n)),
        out_shape=jax.ShapeDtypeStruct((m_dim, n_dim), out_dtype),
        scratch_shapes=[pltpu.VMEM((tm, tn), F32)] if nk > 1 else [],
        compiler_params=_cp("parallel", "parallel", "arbitrary"),
    )(*ins)


def _rmsnorm(x, g):
    t_dim, d = x.shape
    tr = _tile(t_dim, 256, 8)

    def body(x_ref, g_ref, o_ref):
        xv = x_ref[...]
        r = lax.rsqrt(jnp.mean(xv * xv, axis=-1, keepdims=True) + EPS)
        o_ref[...] = (xv * r * g_ref[...]).astype(BF16)

    return pl.pallas_call(
        body, name="rmsnorm", grid=(t_dim // tr,),
        in_specs=[pl.BlockSpec((tr, d), lambda i: (i, 0)), pl.BlockSpec((1, d), lambda i: (0, 0))],
        out_specs=pl.BlockSpec((tr, d), lambda i: (i, 0)),
        out_shape=jax.ShapeDtypeStruct((t_dim, d), BF16),
        compiler_params=_cp("parallel"),
    )(x, g)


def _rmsnorm_bwd(x, g, dh, dres):
    t_dim, d = x.shape
    tr = _tile(t_dim, 256, 8)

    def body(x_ref, g_ref, dh_ref, dres_ref, dx_ref, dg_ref):
        i = pl.program_id(0)
        xv = x_ref[...]
        r = lax.rsqrt(jnp.mean(xv * xv, axis=-1, keepdims=True) + EPS)
        n = xv * r
        dhv = dh_ref[...]
        dn = dhv * g_ref[...]

        @pl.when(i == 0)
        def _():
            dg_ref[...] = jnp.zeros_like(dg_ref)

        dg_ref[...] += jnp.sum(dhv * n, axis=0, keepdims=True)
        dx_ref[...] = dres_ref[...] + r * (dn - n * jnp.mean(dn * n, axis=-1, keepdims=True))

    row = pl.BlockSpec((tr, d), lambda i: (i, 0))
    vec = pl.BlockSpec((1, d), lambda i: (0, 0))
    return pl.pallas_call(
        body, name="rmsnorm_bwd", grid=(t_dim // tr,),
        in_specs=[row, vec, row, row], out_specs=[row, vec],
        out_shape=[jax.ShapeDtypeStruct((t_dim, d), F32), jax.ShapeDtypeStruct((1, d), F32)],
        compiler_params=_cp("arbitrary"),
    )(x, g, dh, dres)


def _loss_head(x, g, target):
    t_dim, d = x.shape
    tr = _tile(t_dim, 256, 8)

    def body(x_ref, g_ref, t_ref, loss_ref, dx_ref, dg_ref):
        i = pl.program_id(0)
        xv = x_ref[...]
        gv = g_ref[...]
        r = lax.rsqrt(jnp.mean(xv * xv, axis=-1, keepdims=True) + EPS)
        n = xv * r
        err = n * gv - t_ref[...]

        @pl.when(i == 0)
        def _():
            dg_ref[...] = jnp.zeros_like(dg_ref)
            loss_ref[...] = jnp.zeros_like(loss_ref)

        per_tok = jnp.mean(err * err, axis=-1, keepdims=True)
        loss_ref[...] += 0.5 * jnp.sum(per_tok, axis=0, keepdims=True)
        dy = err * (1.0 / d)
        dn = dy * gv
        dg_ref[...] += jnp.sum(dy * n, axis=0, keepdims=True)
        dx_ref[...] = r * (dn - n * jnp.mean(dn * n, axis=-1, keepdims=True))

    row = pl.BlockSpec((tr, d), lambda i: (i, 0))
    vec = pl.BlockSpec((1, d), lambda i: (0, 0))
    one = pl.BlockSpec((1, 1), lambda i: (0, 0))
    return pl.pallas_call(
        body, name="loss_head", grid=(t_dim // tr,),
        in_specs=[row, vec, row], out_specs=[one, row, vec],
        out_shape=[jax.ShapeDtypeStruct((1, 1), F32), jax.ShapeDtypeStruct((t_dim, d), F32),
                   jax.ShapeDtypeStruct((1, d), F32)],
        compiler_params=_cp("arbitrary"),
    )(x, g, target)


def _ffn_in(h, wg, wu, half):
    t_dim, d = h.shape
    f = wg.shape[-1]
    tm, tn = _tile(t_dim, 512), _tile(f, 1408)

    def body(h_ref, wg_ref, wu_ref, g_ref, u_ref, a_ref):
        hv = h_ref[...]
        gv = _dot(hv, wg_ref[...])
        uv = _dot(hv, wu_ref[...])
        g_ref[...] = gv.astype(BF16)
        u_ref[...] = uv.astype(BF16)
        a_ref[...] = (gv * _sigmoid(gv) * uv).astype(BF16)

    wspec = pl.BlockSpec((None, d, tn), lambda n, m: (half, 0, n))
    ospec = pl.BlockSpec((tm, tn), lambda n, m: (m, n))
    oshape = jax.ShapeDtypeStruct((t_dim, f), BF16)
    return pl.pallas_call(
        body, name="ffn_in", grid=(f // tn, t_dim // tm),
        in_specs=[pl.BlockSpec((tm, d), lambda n, m: (m, 0)), wspec, wspec],
        out_specs=[ospec, ospec, ospec], out_shape=[oshape, oshape, oshape],
        compiler_params=_cp("parallel", "parallel"),
    )(h, wg, wu)


def _ffn_bwd_act(dx, wd, g, u, half):
    t_dim, d = dx.shape
    f = wd.shape[-2]
    tm, tn = _tile(t_dim, 512), _tile(f, 1408)

    def body(dx_ref, wd_ref, g_ref, u_ref, dg_ref, du_ref):
        da = 0.5 * _dot_nt(dx_ref[...].astype(BF16), wd_ref[...])
        gv = g_ref[...].astype(F32)
        uv = u_ref[...].astype(F32)
        s = _sigmoid(gv)
        dg_ref[...] = (da * uv * (s * (1.0 + gv * (1.0 - s)))).astype(BF16)
        du_ref[...] = (da * gv * s).astype(BF16)

    tile = pl.BlockSpec((tm, tn), lambda n, m: (m, n))
    oshape = jax.ShapeDtypeStruct((t_dim, f), BF16)
    return pl.pallas_call(
        body, name="ffn_bwd_act", grid=(f // tn, t_dim // tm),
        in_specs=[pl.BlockSpec((tm, d), lambda n, m: (m, 0)),
                  pl.BlockSpec((None, tn, d), lambda n, m: (half, n, 0)), tile, tile],
        out_specs=[tile, tile], out_shape=[oshape, oshape],
        compiler_params=_cp("parallel", "parallel"),
    )(dx, wd, g, u)


def _ffn_fwd(x, norm_g, wg, wu, wd, half):
    h = _rmsnorm(x, norm_g)
    g, u, a = _ffn_in(h, wg, wu, half)
    x_new = _mm(a, wd, name="ffn_out", b_sel=(half,), add=x, scale=0.5, tk=1408)
    return x_new, (x, h, g, u, a)


def _ffn_bwd(dx, saved, norm_g, wg, wu, wd, half):
    x, h, g, u, a = saved
    dg, du = _ffn_bwd_act(dx, wd, g, u, half)
    d_wd = _mm(a, dx, name="ffn_dwd", ta=True, scale=0.5, out_dtype=BF16, tm=1408, tn=1024)
    d_wg = _mm(h, dg, name="ffn_dwgu", ta=True, out_dtype=BF16, tm=1024, tn=1408)
    d_wu = _mm(h, du, name="ffn_dwgu", ta=True, out_dtype=BF16, tm=1024, tn=1408)
    dh = _mm(dg, wg, name="ffn_dh", tb=True, b_sel=(half,), pair2=(du, wu, (), (half,)), tk=1408)
    dx_new, dnorm = _rmsnorm_bwd(x, norm_g, dh, dx)
    return dx_new, dnorm, d_wg, d_wu, d_wd


def _shifted(v, off, t_idx):
    if off == 0:
        return v
    t_dim = v.shape[0]
    sh = pltpu.roll(v, (-off) % t_dim, 0)
    valid = jnp.logical_and(t_idx + off >= 0, t_idx + off < t_dim)
    return jnp.where(valid, sh, 0.0)


def _conv_pre(u, w_ref, b_ref, t_idx):
    acc = jnp.zeros_like(u) + b_ref[...]
    shifted = []
    for k in range(CONV_W):
        sh = _shifted(u, k - CONV_W // 2, t_idx)
        shifted.append(sh)
        acc = acc + w_ref[k:k + 1, :] * sh
    return acc, shifted


def _conv_silu(proj, conv_w, conv_b, col0):
    t_dim = proj.shape[0]
    cd = conv_w.shape[-1]
    cb = _tile(cd, 256)
    assert col0 % cb == 0

    def body(u_ref, w_ref, b_ref, o_ref):
        t_idx = lax.broadcasted_iota(jnp.int32, (t_dim, cb), 0)
        pre, _ = _conv_pre(u_ref[...], w_ref, b_ref, t_idx)
        o_ref[...] = pre * _sigmoid(pre)

    return pl.pallas_call(
        body, name="conv_silu", grid=(cd // cb,),
        in_specs=[pl.BlockSpec((t_dim, cb), lambda j: (0, col0 // cb + j)),
                  pl.BlockSpec((CONV_W, cb), lambda j: (0, j)), pl.BlockSpec((1, cb), lambda j: (0, j))],
        out_specs=pl.BlockSpec((t_dim, cb), lambda j: (0, j)),
        out_shape=jax.ShapeDtypeStruct((t_dim, cd), F32),
        compiler_params=_cp("parallel"),
    )(proj, conv_w, conv_b)


def _conv_silu_bwd(proj, conv_w, conv_b, dact2, col0):
    t_dim = proj.shape[0]
    cd = conv_w.shape[-1]
    cb = _tile(cd, 256)

    def body(u_ref, w_ref, b_ref, da_ref, du_ref, dwb_ref):
        t_idx = lax.broadcasted_iota(jnp.int32, (t_dim, cb), 0)
        pre, shifted = _conv_pre(u_ref[...], w_ref, b_ref, t_idx)
        s = _sigmoid(pre)
        dpre = (da_ref[0] + da_ref[1]) * (s * (1.0 + pre * (1.0 - s)))
        du = jnp.zeros_like(dpre)
        for k in range(CONV_W):
            du = du + w_ref[k:k + 1, :] * _shifted(dpre, -(k - CONV_W // 2), t_idx)
            dwb_ref[k:k + 1, :] = jnp.sum(dpre * shifted[k], axis=0, keepdims=True)
        dwb_ref[CONV_W:CONV_W + 1, :] = jnp.sum(dpre, axis=0, keepdims=True)
        dwb_ref[CONV_W + 1:8, :] = jnp.zeros((8 - CONV_W - 1, cb), F32)
        du_ref[...] = du.astype(BF16)

    return pl.pallas_call(
        body, name="conv_silu_bwd", grid=(cd // cb,),
        in_specs=[pl.BlockSpec((t_dim, cb), lambda j: (0, col0 // cb + j)),
                  pl.BlockSpec((CONV_W, cb), lambda j: (0, j)), pl.BlockSpec((1, cb), lambda j: (0, j)),
                  pl.BlockSpec((2, t_dim, cb), lambda j: (0, 0, j))],
        out_specs=[pl.BlockSpec((t_dim, cb), lambda j: (0, j)), pl.BlockSpec((8, cb), lambda j: (0, j))],
        out_shape=[jax.ShapeDtypeStruct((t_dim, cd), BF16), jax.ShapeDtypeStruct((8, cd), F32)],
        compiler_params=_cp("parallel"),
    )(proj, conv_w, conv_b, dact2)


def _softplus_fwd(dt_raw, bias):
    def body(r_ref, b_ref, o_ref):
        v = r_ref[...] + b_ref[...]
        o_ref[...] = jnp.maximum(v, 0.0) + jnp.log(1.0 + jnp.exp(-jnp.abs(v)))

    return pl.pallas_call(body, name="softplus", out_shape=jax.ShapeDtypeStruct(dt_raw.shape, F32))(dt_raw, bias)


def _softplus_bwd(dt_raw, bias, ddt, da, a_log):
    def body(r_ref, b_ref, ddt_ref, da_ref, al_ref, dr_ref, db_ref, dal_ref):
        dv = ddt_ref[...] * _sigmoid(r_ref[...] + b_ref[...])
        dr_ref[...] = dv.astype(BF16)
        db_ref[...] = jnp.sum(dv, axis=0, keepdims=True)
        dal_ref[...] = -da_ref[...] * jnp.exp(al_ref[...])

    vec = jax.ShapeDtypeStruct(bias.shape, F32)
    return pl.pallas_call(
        body, name="softplus_bwd",
        out_shape=[jax.ShapeDtypeStruct(dt_raw.shape, BF16), vec, vec])(dt_raw, bias, ddt, da, a_log)


def _chunk_setup(d, dt_ref, al_ref):
    q = SSD_CHUNK
    ii = lax.broadcasted_iota(jnp.int32, (q, q), 0)
    jj = lax.broadcasted_iota(jnp.int32, (q, q), 1)
    sgn = 1 - 2 * d
    mask = (jj - ii) * sgn <= 0
    mask_t = (ii - jj) * sgn <= 0
    m01 = mask.astype(F32)
    m01_t = mask_t.astype(F32)
    dt = dt_ref[...]
    a = -jnp.exp(al_ref[...])
    dta = dt * a
    cs = _dot_exact01(m01, dta)
    tot = jnp.sum(dta, axis=0, keepdims=True)
    return mask, mask_t, m01, m01_t, dt, a, dta, cs, tot


def _ssd_specs(t_dim, di, g_cnt, chunk_of):
    q, ns = SSD_CHUNK, SSD_STATE
    x_spec = pl.BlockSpec((q, GROUP_W), lambda d, g, c: (chunk_of(d, c), g))
    b_spec = pl.BlockSpec((q, ns), lambda d, g, c: (chunk_of(d, c), di // ns + g))
    c_spec = pl.BlockSpec((q, ns), lambda d, g, c: (chunk_of(d, c), di // ns + g_cnt + g))
    dt_spec = pl.BlockSpec((None, None, q, HEADS_PER_GROUP), lambda d, g, c: (d, g, chunk_of(d, c), 0))
    al_spec = pl.BlockSpec((None, None, 1, HEADS_PER_GROUP), lambda d, g, c: (d, g, 0, 0))
    return x_spec, b_spec, c_spec, dt_spec, al_spec


def _ssd_fwd(xbc, dt4, al4):
    t_dim = xbc.shape[0]
    g_cnt = dt4.shape[1]
    di = g_cnt * GROUP_W
    q, ns, p = SSD_CHUNK, SSD_STATE, SSD_HEAD_DIM
    nc = t_dim // q

    def chunk_of(d, c):
        return c + d * (nc - 1 - 2 * c)

    def body(x_ref, b_ref, c_ref, dt_ref, al_ref, y_ref, hin_ref, h_sc, xd_sc):
        d = pl.program_id(0)
        c = pl.program_id(2)

        @pl.when(c == 0)
        def _():
            h_sc[...] = jnp.zeros_like(h_sc)

        mask, mask_t, m01, m01_t, dt, a, dta, cs, tot = _chunk_setup(d, dt_ref, al_ref)
        xv = x_ref[...]
        bb = b_ref[...].astype(BF16)
        cb16 = c_ref[...].astype(BF16)
        bt16 = b_ref[...].T.astype(BF16)
        cb = _dot(cb16, bt16)
        hin = h_sc[...]
        hin_ref[...] = hin
        ch = _dot(cb16, hin.astype(BF16))
        for h in range(HEADS_PER_GROUP):
            sl = slice(h * p, (h + 1) * p)
            cs_col = cs[:, h:h + 1]
            cs_row = jnp.sum(dta[:, h:h + 1] * m01_t, axis=0, keepdims=True)
            lmat = jnp.exp(jnp.where(mask, cs_col - cs_row, -1e30))
            xdt = xv[:, sl] * dt[:, h:h + 1]
            y_diag = _dot((cb * lmat).astype(BF16), xdt.astype(BF16))
            y_ref[:, sl] = y_diag + ch[:, sl] * jnp.exp(cs_col)
            xd_sc[:, sl] = xdt * jnp.exp(tot[:, h:h + 1] - cs_col)
        st = _dot(bt16, xd_sc[...].astype(BF16))
        for h in range(HEADS_PER_GROUP):
            sl = slice(h * p, (h + 1) * p)
            h_sc[:, sl] = hin[:, sl] * jnp.exp(tot[:, h:h + 1]) + st[:, sl]

    x_spec, b_spec, c_spec, dt_spec, al_spec = _ssd_specs(t_dim, di, g_cnt, chunk_of)
    return pl.pallas_call(
        body, name="ssd_fwd", grid=(2, g_cnt, nc),
        in_specs=[x_spec, b_spec, c_spec, dt_spec, al_spec],
        out_specs=[pl.BlockSpec((None, q, GROUP_W), lambda d, g, c: (d, chunk_of(d, c), g)),
                   pl.BlockSpec((None, None, None, ns, GROUP_W), lambda d, g, c: (d, g, chunk_of(d, c), 0, 0))],
        out_shape=[jax.ShapeDtypeStruct((2, t_dim, di), F32),
                   jax.ShapeDtypeStruct((2, g_cnt, nc, ns, GROUP_W), F32)],
        scratch_shapes=[pltpu.VMEM((ns, GROUP_W), F32), pltpu.VMEM((q, GROUP_W), F32)],
        compiler_params=_cp("parallel", "parallel", "arbitrary"),
    )(xbc, xbc, xbc, dt4, al4)


def _ssd_bwd(xbc, dt4, al4, hin_all, dy, dvec):
    t_dim = xbc.shape[0]
    g_cnt = dt4.shape[1]
    di = g_cnt * GROUP_W
    q, ns, p, hg = SSD_CHUNK, SSD_STATE, SSD_HEAD_DIM, HEADS_PER_GROUP
    nc = t_dim // q

    def chunk_of(d, c):
        return (nc - 1 - c) + d * (2 * c - nc + 1)

    def body(x_ref, b_ref, c_ref, dt_ref, al_ref, hin_ref, dy_ref, dv_ref,
             dx_ref, db_ref, dc_ref, ddt_ref, da_ref, g_sc, xd_sc, dye_sc, col_sc, zrow_sc, tot_sc):
        d = pl.program_id(0)
        c = pl.program_id(2)

        @pl.when(c == 0)
        def _():
            g_sc[...] = jnp.zeros_like(g_sc)
            da_ref[...] = jnp.zeros_like(da_ref)

        mask, mask_t, m01, m01_t, dt, a, dta, cs, tot = _chunk_setup(d, dt_ref, al_ref)
        xv = x_ref[...]
        dyv = dy_ref[...]
        bb = b_ref[...].astype(BF16)
        cb16 = c_ref[...].astype(BF16)
        bt16 = b_ref[...].T.astype(BF16)
        ct16 = c_ref[...].T.astype(BF16)
        cb = _dot(cb16, bt16)
        cbt = _dot(bb, ct16)
        hin = hin_ref[...]
        hin16 = hin.astype(BF16)
        gst = g_sc[...]
        gst16 = gst.astype(BF16)
        ch = _dot(cb16, hin16)
        wst = _dot(bb, gst16)
        skip = jnp.where(d == 0, 1.0, 0.0)
        dcb = jnp.zeros((q, q), F32)
        for h in range(hg):
            sl = slice(h * p, (h + 1) * p)
            cs_col = cs[:, h:h + 1]
            cs_row = jnp.sum(dta[:, h:h + 1] * m01_t, axis=0, keepdims=True)
            lmat = jnp.exp(jnp.where(mask, cs_col - cs_row, -1e30))
            lmat_t = jnp.exp(jnp.where(mask_t, cs_row - cs_col, -1e30))
            e_col = jnp.exp(cs_col)
            dec = jnp.exp(tot[:, h:h + 1] - cs_col)
            dt_h = dt[:, h:h + 1]
            x_h = xv[:, sl]
            xdt = x_h * dt_h
            xdt16 = xdt.astype(BF16)
            dy_h = dyv[:, sl]
            dy16 = dy_h.astype(BF16)
            dye_sc[:, sl] = dy_h * e_col
            xd_sc[:, sl] = xdt * dec
            dcs = jnp.sum(dy_h * ch[:, sl] * e_col, axis=-1, keepdims=True)
            w_h = wst[:, sl]
            dxdt = w_h * dec
            t1 = jnp.sum(w_h * xdt, axis=-1, keepdims=True) * dec
            dcs = dcs - t1
            dtot = jnp.sum(t1, axis=0, keepdims=True)
            m_f = cb * lmat
            dxdt = dxdt + _dot((cbt * lmat_t).astype(BF16), dy16)
            dm = _dot_nt(dy16, xdt16)
            z = dm * m_f
            dcb = dcb + dm * lmat
            dcs = dcs + jnp.sum(z, axis=-1, keepdims=True)
            zrow_sc[h:h + 1, :] = jnp.sum(z, axis=0, keepdims=True)
            t_h = jnp.exp(tot[:, h:h + 1])
            dtot = dtot + jnp.sum(jnp.sum(gst[:, sl] * hin[:, sl], axis=-1, keepdims=True), axis=0,
                                  keepdims=True) * t_h
            dx_ref[:, sl] = dxdt * dt_h + skip * dy_h * dv_ref[:, sl]
            col_sc[:, h:h + 1] = dcs
            col_sc[:, hg + h:hg + h + 1] = jnp.sum(dxdt * x_h, axis=-1, keepdims=True)
            tot_sc[:, h:h + 1] = dtot
        dye16 = dye_sc[...].astype(BF16)
        dcb16 = dcb.astype(BF16)
        dc_ref[...] = _dot_nt(dye16, hin16) + _dot(dcb16, bb)
        db_ref[...] = _dot_nt(xd_sc[...].astype(BF16), gst16) + _dot(dcb.T.astype(BF16), cb16)
        dhin = _dot(ct16, dye16)
        for h in range(hg):
            sl = slice(h * p, (h + 1) * p)
            g_sc[:, sl] = dhin[:, sl] + gst[:, sl] * jnp.exp(tot[:, h:h + 1])
        ddta = _dot_exact01(m01_t, col_sc[:, 0:hg]) - _dot_nt_exact01(m01_t, zrow_sc[...]) + tot_sc[...]
        ddt_ref[...] = col_sc[:, hg:2 * hg] + ddta * a
        da_ref[...] += jnp.sum(ddta * dt, axis=0, keepdims=True)

    x_spec, b_spec, c_spec, dt_spec, al_spec = _ssd_specs(t_dim, di, g_cnt, chunk_of)
    hin_spec = pl.BlockSpec((None, None, None, ns, GROUP_W), lambda d, g, c: (d, g, chunk_of(d, c), 0, 0))
    dy_spec = pl.BlockSpec((q, GROUP_W), lambda d, g, c: (chunk_of(d, c), g))
    dv_spec = pl.BlockSpec((1, GROUP_W), lambda d, g, c: (0, g))
    gn = g_cnt * ns
    return pl.pallas_call(
        body, name="ssd_bwd", grid=(2, g_cnt, nc),
        in_specs=[x_spec, b_spec, c_spec, dt_spec, al_spec, hin_spec, dy_spec, dv_spec],
        out_specs=[pl.BlockSpec((None, q, GROUP_W), lambda d, g, c: (d, chunk_of(d, c), g)),
                   pl.BlockSpec((None, q, ns), lambda d, g, c: (d, chunk_of(d, c), g)),
                   pl.BlockSpec((None, q, ns), lambda d, g, c: (d, chunk_of(d, c), g)),
                   pl.BlockSpec((None, None, q, hg), lambda d, g, c: (d, g, chunk_of(d, c), 0)),
                   pl.BlockSpec((None, None, 1, hg), lambda d, g, c: (d, g, 0, 0))],
        out_shape=[jax.ShapeDtypeStruct((2, t_dim, di), F32), jax.ShapeDtypeStruct((2, t_dim, gn), F32),
                   jax.ShapeDtypeStruct((2, t_dim, gn), F32), jax.ShapeDtypeStruct((2, g_cnt, t_dim, hg), F32),
                   jax.ShapeDtypeStruct((2, g_cnt, 1, hg), F32)],
        scratch_shapes=[pltpu.VMEM((ns, GROUP_W), F32), pltpu.VMEM((q, GROUP_W), F32),
                        pltpu.VMEM((q, GROUP_W), F32), pltpu.VMEM((q, 2 * hg), F32),
                        pltpu.VMEM((hg, q), F32), pltpu.VMEM((1, hg), F32)],
        compiler_params=_cp("parallel", "parallel", "arbitrary"),
    )(xbc, xbc, xbc, dt4, al4, hin_all, dy, dvec)


def _gate_norm(y2, xbc, proj, dvec, ng):
    t_dim, di = y2.shape[1:]
    tr = _tile(t_dim, 128, 8)

    def body(y2_ref, x_ref, z_ref, dv_ref, ng_ref, o_ref):
        y = y2_ref[0] + y2_ref[1] + x_ref[...] * dv_ref[...]
        z = z_ref[...]
        v = y * z * _sigmoid(z)
        r = lax.rsqrt(jnp.mean(v * v, axis=-1, keepdims=True) + EPS)
        o_ref[...] = (v * r * ng_ref[...]).astype(BF16)

    row = pl.BlockSpec((tr, di), lambda i: (i, 0))
    vec = pl.BlockSpec((1, di), lambda i: (0, 0))
    return pl.pallas_call(
        body, name="gate_norm", grid=(t_dim // tr,),
        in_specs=[pl.BlockSpec((2, tr, di), lambda i: (0, i, 0)), row, row, vec, vec],
        out_specs=row, out_shape=jax.ShapeDtypeStruct((t_dim, di), BF16),
        compiler_params=_cp("parallel"),
    )(y2, xbc, proj, dvec, ng)


def _gate_norm_bwd(y2, xbc, proj, dvec, ng, dyn):
    t_dim, di = y2.shape[1:]
    tr = _tile(t_dim, 128, 8)

    def body(y2_ref, x_ref, z_ref, dv_ref, ng_ref, dyn_ref, dy_ref, dz_ref, dng_ref, dd_ref):
        i = pl.program_id(0)
        xv = x_ref[...]
        y = y2_ref[0] + y2_ref[1] + xv * dv_ref[...]
        z = z_ref[...]
        s = _sigmoid(z)
        v = y * z * s
        r = lax.rsqrt(jnp.mean(v * v, axis=-1, keepdims=True) + EPS)
        n = v * r
        dynv = dyn_ref[...]
        dn = dynv * ng_ref[...]
        dv = r * (dn - n * jnp.mean(dn * n, axis=-1, keepdims=True))
        dy = dv * z * s

        @pl.when(i == 0)
        def _():
            dng_ref[...] = jnp.zeros_like(dng_ref)
            dd_ref[...] = jnp.zeros_like(dd_ref)

        dng_ref[...] += jnp.sum(dynv * n, axis=0, keepdims=True)
        dd_ref[...] += jnp.sum(dy * xv, axis=0, keepdims=True)
        dy_ref[...] = dy
        dz_ref[...] = (dv * y * (s * (1.0 + z * (1.0 - s)))).astype(BF16)

    row = pl.BlockSpec((tr, di), lambda i: (i, 0))
    vec = pl.BlockSpec((1, di), lambda i: (0, 0))
    return pl.pallas_call(
        body, name="gate_norm_bwd", grid=(t_dim // tr,),
        in_specs=[pl.BlockSpec((2, tr, di), lambda i: (0, i, 0)), row, row, vec, vec, row],
        out_specs=[row, row, vec, vec],
        out_shape=[jax.ShapeDtypeStruct((t_dim, di), F32), jax.ShapeDtypeStruct((t_dim, di), BF16),
                   jax.ShapeDtypeStruct((1, di), F32), jax.ShapeDtypeStruct((1, di), F32)],
        compiler_params=_cp("arbitrary"),
    )(y2, xbc, proj, dvec, ng, dyn)


def _dt_to_groups(dt):
    t_dim, h2 = dt.shape
    g_cnt = h2 // 2 // HEADS_PER_GROUP
    return dt.reshape(t_dim, 2, g_cnt, HEADS_PER_GROUP).transpose(1, 2, 0, 3)


def _dt_from_groups(dt4):
    _, g_cnt, t_dim, hg = dt4.shape
    return dt4.transpose(2, 0, 1, 3).reshape(t_dim, 2 * g_cnt * hg)


def _ssd_mixer_fwd(x, norm_g, w_in, conv_w, conv_b, dt_bias, a_log, d_skip, ssd_norm, w_out, j):
    heads = d_skip.shape[0]
    di = heads * SSD_HEAD_DIM
    g_cnt = heads // HEADS_PER_GROUP
    cd = conv_w.shape[-1]
    hn = _rmsnorm(x, norm_g)
    proj = _mm(hn, w_in, name="ssd_proj", b_sel=(j,), tn=1152)
    xbc = _conv_silu(proj, conv_w, conv_b, di)
    dt_raw = proj[:, di + cd:]
    dt = _softplus_fwd(dt_raw, dt_bias)
    dt4 = _dt_to_groups(dt)
    al4 = a_log.reshape(2, g_cnt, 1, HEADS_PER_GROUP)
    y2, hin = _ssd_fwd(xbc, dt4, al4)
    dvec = jnp.repeat(d_skip, SSD_HEAD_DIM).reshape(1, di)
    yn = _gate_norm(y2, xbc, proj, dvec, ssd_norm)
    x_new = _mm(yn, w_out, name="ssd_out", b_sel=(j,), add=x)
    return x_new, (x, hn, proj, xbc, dt_raw, dt4, al4, y2, hin, dvec, yn)


def _ssd_mixer_bwd(dx, saved, norm_g, w_in, conv_w, conv_b, dt_bias, a_log, ssd_norm, w_out, j):
    x, hn, proj, xbc, dt_raw, dt4, al4, y2, hin, dvec, yn = saved
    di = dvec.shape[1]
    heads = di // SSD_HEAD_DIM
    d_wout = _mm(yn, dx, name="ssd_dwout", ta=True, out_dtype=BF16)
    dyn = _mm(dx, w_out, name="ssd_dyn", tb=True, b_sel=(j,))
    dy, dz, d_ng, dd_col = _gate_norm_bwd(y2, xbc, proj, dvec, ssd_norm, dyn)
    dx2, db2, dc2, ddt4, da4 = _ssd_bwd(xbc, dt4, al4, hin, dy, dvec)
    dact2 = jnp.concatenate([dx2, db2, dc2], axis=-1)
    dxbc, dwb = _conv_silu_bwd(proj, conv_w, conv_b, dact2, di)
    ddt_raw, d_bias, d_alog = _softplus_bwd(dt_raw, dt_bias, _dt_from_groups(ddt4), da4.reshape(1, 2 * heads), a_log)
    dproj = jnp.concatenate([dz, dxbc, ddt_raw], axis=-1)
    d_win = _mm(hn, dproj, name="ssd_dwin", ta=True, out_dtype=BF16, tn=1152)
    dhn = _mm(dproj, w_in, name="ssd_dhn", tb=True, b_sel=(j,), tk=1152)
    dx_new, d_norm = _rmsnorm_bwd(x, norm_g, dhn, dx)
    small = dict(mix_norm=d_norm, conv_w=dwb[:CONV_W], conv_b=dwb[CONV_W:CONV_W + 1], dt_bias=d_bias, a_log=d_alog,
                 ssd_d=dd_col.reshape(heads, SSD_HEAD_DIM).sum(axis=1), ssd_norm=d_ng)
    return dx_new, small, d_win, d_wout


def _pool_count(t_idx, w, t_dim):
    hi = jnp.minimum(t_idx + w // 2, t_dim)
    lo = jnp.maximum(t_idx - w // 2, 0)
    return (hi - lo).astype(F32)


def _pool_mix(u, transpose):
    t_dim, d = u.shape
    gd = d // len(POOL_WINDOWS)
    cb = _tile(gd, 256)
    per = gd // cb

    def body(u_ref, o_ref):
        gi = pl.program_id(0)
        t_idx = lax.broadcasted_iota(jnp.int32, (t_dim, cb), 0)
        uv = u_ref[...]
        for widx, w in enumerate(POOL_WINDOWS):
            @pl.when(gi == widx)
            def _(w=w):
                cnt = _pool_count(t_idx, w, t_dim)
                src = uv / cnt if transpose else uv
                acc = jnp.zeros_like(uv)
                for k in range(-(w // 2), w // 2):
                    acc = acc + _shifted(src, -k if transpose else k, t_idx)
                res = acc - uv if transpose else acc / cnt - uv
                o_ref[...] = res.astype(BF16)

    spec = pl.BlockSpec((t_dim, cb), lambda gi, j: (0, gi * per + j))
    return pl.pallas_call(
        body, name="pool_mix_t" if transpose else "pool_mix", grid=(len(POOL_WINDOWS), per),
        in_specs=[spec], out_specs=spec, out_shape=jax.ShapeDtypeStruct((t_dim, d), BF16),
        compiler_params=_cp("parallel", "parallel"),
    )(u)


def _pool_group(mix, wgrp, scale, j):
    t_dim, d = mix.shape
    gd = wgrp.shape[-1]
    tm = _tile(t_dim, 512)

    def body(m_ref, w_ref, s_ref, v_ref, vs_ref):
        v = _dot(m_ref[...], w_ref[...])
        v_ref[...] = v
        vs_ref[...] = (v * s_ref[...]).astype(BF16)

    tile = pl.BlockSpec((tm, gd), lambda gi, m: (m, gi))
    return pl.pallas_call(
        body, name="pool_group", grid=(d // gd, t_dim // tm),
        in_specs=[tile, pl.BlockSpec((None, None, gd, gd), lambda gi, m: (j, gi, 0, 0)),
                  pl.BlockSpec((1, gd), lambda gi, m: (0, gi))],
        out_specs=[tile, tile],
        out_shape=[jax.ShapeDtypeStruct((t_dim, d), F32), jax.ShapeDtypeStruct((t_dim, d), BF16)],
        compiler_params=_cp("parallel", "parallel"),
    )(mix, wgrp, scale)


def _pool_group_bwd(dvs, v, mix, wgrp, scale, j):
    t_dim, d = mix.shape
    gd = wgrp.shape[-1]
    n_g = d // gd
    tm = _tile(t_dim, 512)
    nm = t_dim // tm

    def body(dvs_ref, v_ref, m_ref, w_ref, s_ref, dmix_ref, ds_ref, dw_ref, acc_ref):
        m = pl.program_id(1)
        dvsv = dvs_ref[...]

        @pl.when(m == 0)
        def _():
            ds_ref[...] = jnp.zeros_like(ds_ref)
            acc_ref[...] = jnp.zeros_like(acc_ref)

        ds_ref[...] += jnp.sum(dvsv * v_ref[...], axis=0, keepdims=True)
        dv16 = (dvsv * s_ref[...]).astype(BF16)
        dmix_ref[...] = _dot_nt(dv16, w_ref[...])
        acc_ref[...] += _dot(m_ref[...].T, dv16)

        @pl.when(m == nm - 1)
        def _():
            dw_ref[...] = acc_ref[...].astype(BF16)

    tile = pl.BlockSpec((tm, gd), lambda gi, m: (m, gi))
    vec = pl.BlockSpec((1, gd), lambda gi, m: (0, gi))
    return pl.pallas_call(
        body, name="pool_group_bwd", grid=(n_g, nm),
        in_specs=[tile, tile, tile, pl.BlockSpec((None, None, gd, gd), lambda gi, m: (j, gi, 0, 0)), vec],
        out_specs=[tile, vec, pl.BlockSpec((None, gd, gd), lambda gi, m: (gi, 0, 0))],
        out_shape=[jax.ShapeDtypeStruct((t_dim, d), F32), jax.ShapeDtypeStruct((1, d), F32),
                   jax.ShapeDtypeStruct((n_g, gd, gd), BF16)],
        scratch_shapes=[pltpu.VMEM((gd, gd), F32)],
        compiler_params=_cp("parallel", "arbitrary"),
    )(dvs, v, mix, wgrp, scale)


def _pool_mixer_fwd(x, norm_g, w_in, wgrp, scale, w_out, j):
    hn = _rmsnorm(x, norm_g)
    u = _mm(hn, w_in, name="pool_u", b_sel=(j,))
    mix = _pool_mix(u, False)
    v, vs = _pool_group(mix, wgrp, scale, j)
    x_new = _mm(vs, w_out, name="pool_out", b_sel=(j,), add=x)
    return x_new, (x, hn, mix, v, vs)


def _pool_mixer_bwd(dx, saved, norm_g, w_in, wgrp, scale, w_out, j):
    x, hn, mix, v, vs = saved
    d_wout = _mm(vs, dx, name="pool_dw", ta=True, out_dtype=BF16)
    dvs = _mm(dx, w_out, name="pool_dvs", tb=True, b_sel=(j,))
    dmix, d_scale, d_wgrp = _pool_group_bwd(dvs, v, mix, wgrp, scale, j)
    du = _pool_mix(dmix, True)
    d_win = _mm(hn, du, name="pool_dw", ta=True, out_dtype=BF16)
    dhn = _mm(du, w_in, name="pool_dhn", tb=True, b_sel=(j,))
    dx_new, d_norm = _rmsnorm_bwd(x, norm_g, dhn, dx)
    return dx_new, d_norm, d_scale, d_win, d_wgrp, d_wout


def _local_step(x, target, depth, fetch, emit, on_loss, small, tok):
    saved, wts = [], {}

    def weights(group, after):
        if group not in wts:
            wts[group] = fetch(group, after)
        return wts[group]

    for i in range(depth):
        j = i // 2
        x, s0 = _ffn_fwd(x, small["ffn_norm"][i, 0][None] + tok, *weights(("ffn", i, 0), x), 0)
        mg = small["mix_norm"][i][None]
        if i % 2 == 0:
            heads = small["ssd_d"].shape[1]
            w_in, w_out = weights(("ssd", j), x)
            x, s1 = _ssd_mixer_fwd(x, mg, w_in, small["ssd_conv_w"][j], small["ssd_conv_b"][j][None],
                                   small["ssd_dt_bias"][j].reshape(1, 2 * heads), small["ssd_a_log"][j].reshape(1, 2 * heads),
                                   small["ssd_d"][j], small["ssd_norm"][j][None], w_out, 0)
        else:
            p_in, p_grp, p_out = weights(("pool", j), x)
            x, s1 = _pool_mixer_fwd(x, mg, p_in, p_grp, small["pool_scale"][j][None], p_out, 0)
        x, s2 = _ffn_fwd(x, small["ffn_norm"][i, 1][None], *weights(("ffn", i, 1), x), 0)
        saved.append((s0, s1, s2))
    loss, dx, d_final = _loss_head(x, small["final_norm"][None], target)

    gs = {k: {} for k in ("ffn_norm", "mix_norm", "ssd_conv_w", "ssd_conv_b", "ssd_dt_bias", "ssd_a_log", "ssd_d",
                          "ssd_norm", "pool_scale")}
    tok = on_loss(loss)
    for i in reversed(range(depth)):
        j = i // 2
        s0, s1, s2 = saved[i]
        for half, sv in ((1, s2), (0, None)):
            if half == 0:
                sv = s0
                mg = small["mix_norm"][i][None] + tok
                if i % 2 == 0:
                    heads = small["ssd_d"].shape[1]
                    w_in, w_out = wts[("ssd", j)]
                    dx, sm, d_win, d_wout = _ssd_mixer_bwd(
                        dx, s1, mg, w_in, small["ssd_conv_w"][j], small["ssd_conv_b"][j][None],
                        small["ssd_dt_bias"][j].reshape(1, 2 * heads), small["ssd_a_log"][j].reshape(1, 2 * heads),
                        small["ssd_norm"][j][None], w_out, 0)
                    gs["mix_norm"][i] = sm["mix_norm"][0]
                    gs["ssd_conv_w"][j] = sm["conv_w"]
                    gs["ssd_conv_b"][j] = sm["conv_b"][0]
                    gs["ssd_dt_bias"][j] = sm["dt_bias"].reshape(2, heads)
                    gs["ssd_a_log"][j] = sm["a_log"].reshape(2, heads)
                    gs["ssd_d"][j] = sm["ssd_d"]
                    gs["ssd_norm"][j] = sm["ssd_norm"][0]
                    tok = tok + emit(("ssd", j), [d_win, d_wout])
                else:
                    p_in, p_grp, p_out = wts[("pool", j)]
                    dx, d_norm, d_scale, d_win, d_wgrp, d_wout = _pool_mixer_bwd(
                        dx, s1, mg, p_in, p_grp, small["pool_scale"][j][None], p_out, 0)
                    gs["mix_norm"][i] = d_norm[0]
                    gs["pool_scale"][j] = d_scale[0]
                    tok = tok + emit(("pool", j), [d_win, d_wgrp, d_wout])
            dx, d_norm, d_wg, d_wu, d_wd = _ffn_bwd(dx, sv, small["ffn_norm"][i, half][None] + tok,
                                                   *wts[("ffn", i, half)], 0)
            gs["ffn_norm"][(i, half)] = d_norm[0]
            tok = tok + emit(("ffn", i, half), [d_wg, d_wu, d_wd])
    n_s, n_p = (depth + 1) // 2, depth // 2
    gsmall = dict(
        ffn_norm=jnp.stack([jnp.stack([gs["ffn_norm"][(i, h)] for h in range(2)]) for i in range(depth)]),
        mix_norm=jnp.stack([gs["mix_norm"][i] for i in range(depth)]),
        ssd_conv_w=jnp.stack([gs["ssd_conv_w"][j] for j in range(n_s)]),
        ssd_conv_b=jnp.stack([gs["ssd_conv_b"][j] for j in range(n_s)]),
        ssd_dt_bias=jnp.stack([gs["ssd_dt_bias"][j] for j in range(n_s)]),
        ssd_a_log=jnp.stack([gs["ssd_a_log"][j] for j in range(n_s)]),
        ssd_d=jnp.stack([gs["ssd_d"][j] for j in range(n_s)]),
        ssd_norm=jnp.stack([gs["ssd_norm"][j] for j in range(n_s)]),
        pool_scale=jnp.stack([gs["pool_scale"][j] for j in range(n_p)]),
        final_norm=d_final[0],
    )
    return loss, dx, gsmall


ANY = pl.BlockSpec(memory_space=pl.ANY)


def _mesh_pos():
    return lax.axis_index("x"), lax.axis_index("y"), lax.axis_index("c")


def _other_chips(x, y):
    return [(1 - x, y), (x, 1 - y), (1 - x, 1 - y)]


def _win(ref, windows, lead=()):
    rest = len(ref.shape) - len(lead)
    idx = tuple(lead) + tuple(pl.ds(*windows[ax]) if ax in windows else slice(None) for ax in range(rest))
    return ref.at[idx]


def _remote(src, dst, send_sems, recv_sems, k, peer):
    return pltpu.make_async_remote_copy(src_ref=src, dst_ref=dst, send_sem=send_sems.at[k], recv_sem=recv_sems.at[k],
                                        device_id=peer, device_id_type=MESH)


def _cast_place(w3, chip1, mode, l0, nl, out_dtype=BF16):
    _, r, c = w3.shape
    tr = _tile(r, 256, 16)
    nr = r // tr
    if mode == "cols":
        out_shape, blk = (nl, r, N_CHIPS * c), (None, tr, c)
        omap = lambda l, i, s: (l, i, s[0])
    elif mode == "rows":
        out_shape, blk = (nl, N_CHIPS * r, c), (None, tr, c)
        omap = lambda l, i, s: (l, s[0] * nr + i, 0)
    else:
        out_shape, blk = (nl, N_CHIPS, r, c), (None, None, tr, c)
        omap = lambda l, i, s: (l, s[0], i, 0)

    def body(s_ref, w_ref, o_ref):
        o_ref[...] = w_ref[...].astype(out_dtype)

    return pl.pallas_call(
        body, name="cast_place_" + mode, out_shape=jax.ShapeDtypeStruct(out_shape, out_dtype),
        grid_spec=pltpu.PrefetchScalarGridSpec(
            num_scalar_prefetch=1, grid=(nl, nr),
            in_specs=[pl.BlockSpec((None, tr, c), lambda l, i, s: (l0 + l, i, 0))],
            out_specs=pl.BlockSpec(blk, omap)),
        compiler_params=_cp("parallel", "parallel"),
    )(chip1, w3)


def _gather_begin(name, fulls, axes):
    n = len(fulls)
    shapes = [f.shape for f in fulls]

    def full_win(refs, t, sidx, hidx):
        sa, ha = axes[t]
        ssz, hsz = shapes[t][sa] // N_CHIPS, shapes[t][ha] // 2
        return _win(refs[t], {sa: (sidx * ssz, ssz), ha: (hidx * hsz, hsz)})

    def chips_plan(refs, x, y, c):
        mine = 2 * x + y
        return [(full_win(refs, t, mine, c), full_win(refs, t, mine, c), (px, py, c))
                for t in range(n) for px, py in _other_chips(x, y)]

    def sibling_plan(in_refs, out_refs, x, y, c):
        wins = [full_win(out_refs, t, 2 * px + py, c) for t in range(n) for px, py in _other_chips(x, y)]
        return [(w, w, (x, y, 1 - c)) for w in wins]

    sems, thru, token = _split_start(name + "_start", fulls, chips_plan, 3 * n)
    return (name, sems, thru, chips_plan, sibling_plan, 3 * n), token


def _gather_finish(pending, after):
    name, sems, thru, chips_plan, sibling_plan, n_copies = pending
    landed = _split_wait(name + "_wait", sems, thru, chips_plan, n_copies, after)
    return _exchange(name + "_sibling", landed, [jax.ShapeDtypeStruct(f.shape, f.dtype) for f in landed],
                     sibling_plan, n_copies, inplace=True)


def _exchange(name, inputs, out_shapes, plan, n_copies, inplace=False):
    n_in, n_out = len(inputs), len(out_shapes)

    def body(*refs):
        in_refs, out_refs = refs[:n_in], refs[n_in:n_in + n_out]
        send_sems, recv_sems = refs[n_in + n_out:]
        x, y, c = _mesh_pos()
        copies = plan(in_refs, out_refs, x, y, c)
        assert len(copies) == n_copies
        started = []
        for k, (src, dst, peer) in enumerate(copies):
            cp = _remote(src, dst, send_sems, recv_sems, k, peer)
            cp.start()
            started.append(cp)
        for cp in started:
            cp.wait()

    return pl.pallas_call(
        body, name=name, in_specs=[ANY] * n_in, out_specs=[ANY] * n_out, out_shape=out_shapes,
        input_output_aliases={t: t for t in range(n_in)} if inplace else {},
        scratch_shapes=[pltpu.SemaphoreType.DMA((n_copies,)), pltpu.SemaphoreType.DMA((n_copies,))],
    )(*inputs)


HBM = pl.BlockSpec(memory_space=pltpu.HBM)
SEM = pl.BlockSpec(memory_space=pltpu.SEMAPHORE)
DATAFLOW = pltpu.SideEffectType.DATAFLOW_SIDE_EFFECTING


def _split_start(name, bufs, plan, n_copies):
    n = len(bufs)

    def body(*refs):
        ins = refs[:n]
        send_sems, recv_sems = refs[n], refs[n + 1]
        token = refs[2 * n + 2]
        x, y, c = _mesh_pos()
        copies = plan(ins, x, y, c)
        assert len(copies) == n_copies
        for k, (src, dst, peer) in enumerate(copies):
            _remote(src, dst, send_sems, recv_sems, k, peer).start()
        token[...] = jnp.zeros_like(token)

    outs = pl.pallas_call(
        body, name=name,
        out_shape=(pltpu.SemaphoreType.DMA((n_copies,)), pltpu.SemaphoreType.DMA((n_copies,)),
                   *[pltpu.HBM(b.shape, b.dtype) for b in bufs], jax.ShapeDtypeStruct((8, LANE), F32)),
        in_specs=[HBM] * n, out_specs=(SEM, SEM, *[HBM] * n, pl.BlockSpec(memory_space=pltpu.VMEM)),
        input_output_aliases={t: 2 + t for t in range(n)},
        compiler_params=pltpu.CompilerParams(has_side_effects=DATAFLOW),
    )(*[pltpu.with_memory_space_constraint(b, pltpu.HBM) for b in bufs])
    return (outs[0], outs[1]), list(outs[2:2 + n]), outs[2 + n][0, 0]


def _split_wait(name, sems, bufs, plan, n_copies, after):
    n = len(bufs)

    def body(*refs):
        ins = refs[:n]
        send_sems, recv_sems = refs[n], refs[n + 1]
        x, y, c = _mesh_pos()
        copies = plan(ins, x, y, c)
        assert len(copies) == n_copies
        for k, (src, dst, peer) in enumerate(copies):
            cp = _remote(src, dst, send_sems, recv_sems, k, peer)
            cp.wait_send()
            cp.wait_recv()

    outs = pl.pallas_call(
        body, name=name, out_shape=tuple(pltpu.HBM(b.shape, b.dtype) for b in bufs),
        in_specs=[HBM] * n + [SEM, SEM, ANY], out_specs=tuple([HBM] * n),
        input_output_aliases={t: t for t in range(n)},
        compiler_params=pltpu.CompilerParams(has_side_effects=DATAFLOW),
    )(*bufs, sems[0], sems[1], after)
    return list(outs)


def _halved(shape, ha):
    out = list(shape)
    out[ha] //= 2
    return tuple(out)


def _sharded(shape, sa):
    out = list(shape)
    out[sa] //= N_CHIPS
    return tuple(out)


def _rs_cores(name, grads, axes):
    n = len(grads)
    shapes = [jax.ShapeDtypeStruct(_halved(g.shape, ha), g.dtype) for g, (_, ha) in zip(grads, axes)]

    def plan(in_refs, out_refs, x, y, c):
        copies = []
        for t in range(n):
            ha = axes[t][1]
            hsz = grads[t].shape[ha] // 2
            copies.append((_win(in_refs[t], {ha: ((1 - c) * hsz, hsz)}), out_refs[t], (x, y, 1 - c)))
        return copies

    return _exchange(name, grads, shapes, plan, n)


def _rs_chips_begin(name, halves, axes):
    n = len(halves)
    shapes = [h.shape for h in halves]
    landing = [lax.empty((N_CHIPS - 1,) + _sharded(h.shape, sa), h.dtype) for h, (sa, _) in zip(halves, axes)]

    def plan(refs, x, y, c):
        copies = []
        for t in range(n):
            sa = axes[t][0]
            ssz = shapes[t][sa] // N_CHIPS
            for j, (px, py) in enumerate(_other_chips(x, y)):
                copies.append((_win(refs[t], {sa: ((2 * px + py) * ssz, ssz)}), refs[n + t].at[j], (px, py, c)))
        return copies

    sems, thru, token = _split_start(name + "_start", list(halves) + landing, plan, 3 * n)
    return (name, sems, thru, plan, 3 * n), token


def _rs_chips_finish(pending, after):
    name, sems, thru, plan, n_copies = pending
    done = _split_wait(name + "_wait", sems, thru, plan, n_copies, after)
    n = len(done) // 2
    return done[:n], done[n:]


def _rs_finish(name, groups):
    flat = [(gi, t) for gi, grp in enumerate(groups) for t in range(len(grp))]
    shapes = [jax.ShapeDtypeStruct((len(grp),) + grp[0].shape, grp[0].dtype) for grp in groups]

    def plan(in_refs, out_refs, x, y, c):
        return [(in_refs[k], out_refs[gi].at[t], (x, y, 1 - c)) for k, (gi, t) in enumerate(flat)]

    return _exchange(name, [a for grp in groups for a in grp], shapes, plan, len(flat))


def _allgather8(v, reduce):
    rows = v.shape[0]

    def body(v_ref, o_ref, *scratch):
        if reduce:
            buf, send_sems, recv_sems, loc_sem = scratch
        else:
            buf = o_ref
            send_sems, recv_sems, loc_sem = scratch
        x, y, c = _mesh_pos()
        me = 4 * x + 2 * y + c
        lc = pltpu.make_async_copy(v_ref, buf.at[me], loc_sem)
        lc.start()
        sends = []
        for k in range(1, 8):
            kx, ky, kc = k // 4, (k // 2) % 2, k % 2
            peer = (x ^ kx, y ^ ky, c ^ kc)
            cp = _remote(v_ref, buf.at[me], send_sems, recv_sems, k - 1, peer)
            cp.start()
            sends.append(cp)
        for k in range(1, 8):
            kx, ky, kc = k // 4, (k // 2) % 2, k % 2
            src = 4 * (x ^ kx) + 2 * (y ^ ky) + (c ^ kc)
            _remote(v_ref, buf.at[src], send_sems, recv_sems, k - 1, (x, y, c)).wait_recv()
        for cp in sends:
            cp.wait_send()
        lc.wait()
        if reduce:
            tot = buf[0]
            for i in range(1, 8):
                tot = tot + buf[i]
            o_ref[...] = tot

    vm = pl.BlockSpec(memory_space=pltpu.VMEM)
    sems = [pltpu.SemaphoreType.DMA((7,)), pltpu.SemaphoreType.DMA((7,)), pltpu.SemaphoreType.DMA]
    if reduce:
        out_shape = jax.ShapeDtypeStruct((rows, LANE), F32)
        scratch = [pltpu.VMEM((8, rows, LANE), F32)] + sems
    else:
        out_shape = jax.ShapeDtypeStruct((8, rows, LANE), F32)
        scratch = sems
    return pl.pallas_call(
        body, name="allreduce_small" if reduce else "allgather_small", in_specs=[vm], out_specs=vm,
        out_shape=out_shape, scratch_shapes=scratch,
    )(v)


def _rows2d(a):
    return a.reshape(-1, a.shape[-1])


def _view3(a):
    return a.reshape((-1,) + a.shape[-2:])


def _add_pair(g, recv, ha, core1):
    g3, r3 = _view3(g), _view3(recv)
    rows_half = ha + 3 - g.ndim == 1
    n_l, r, c = r3.shape
    tr = _tile(r, 256, 16)
    nr = r // tr
    gmap = (lambda l, i, s: (l, s[0] * nr + i, 0)) if rows_half else (lambda l, i, s: (l, i, s[0]))

    def body(s_ref, g_ref, r_ref, o_ref):
        o_ref[...] = (g_ref[...].astype(F32) + r_ref[...].astype(F32)).astype(BF16)

    spec = pl.BlockSpec((None, tr, c), lambda l, i, s: (l, i, 0))
    out = pl.pallas_call(
        body, name="add_pair", out_shape=jax.ShapeDtypeStruct(r3.shape, BF16),
        grid_spec=pltpu.PrefetchScalarGridSpec(num_scalar_prefetch=1, grid=(n_l, nr),
                                               in_specs=[pl.BlockSpec((None, tr, c), gmap), spec], out_specs=spec),
        compiler_params=_cp("parallel", "parallel"))(core1, g3, r3)
    return out.reshape(recv.shape)


def _add_four(cs, recv, sa, chip1):
    c3 = _view3(cs)
    s3 = sa + 3 - cs.ndim
    lo, ro, co = (dim // N_CHIPS if ax == s3 else dim for ax, dim in enumerate(c3.shape))
    r4 = recv.reshape(N_CHIPS - 1, lo, ro, co)
    tr = _tile(ro, 256, 16)
    nr = ro // tr
    if s3 == 0:
        cmap = lambda l, i, s: (s[0] * lo + l, i, 0)
    elif s3 == 1:
        cmap = lambda l, i, s: (l, s[0] * nr + i, 0)
    else:
        cmap = lambda l, i, s: (l, i, s[0])

    def body(s_ref, c_ref, r_ref, out_ref):
        out_ref[...] = ((c_ref[...].astype(F32) + r_ref[0].astype(F32)) + r_ref[1].astype(F32)) + r_ref[2].astype(F32)

    out = pl.pallas_call(
        body, name="add_four", out_shape=jax.ShapeDtypeStruct((lo, ro, co), F32),
        grid_spec=pltpu.PrefetchScalarGridSpec(
            num_scalar_prefetch=1, grid=(lo, nr),
            in_specs=[pl.BlockSpec((None, tr, co), cmap),
                      pl.BlockSpec((N_CHIPS - 1, None, tr, co), lambda l, i, s: (0, l, i, 0))],
            out_specs=pl.BlockSpec((None, tr, co), lambda l, i, s: (l, i, 0))),
        compiler_params=_cp("parallel", "parallel"))(chip1, c3, r4)
    return out.reshape(recv.shape[1:])


def _adamw_halves(w, own, recv, m, v, rows_half, core1):
    w3, m3, v3, o3, r3 = (_view3(a) for a in (w, m, v, own, recv))
    n_l, _, c = w3.shape
    _, rh, ch = o3.shape
    tr = _tile(rh, 128, 8)
    nr = rh // tr
    c1 = 1.0 - ADAM_B1 ** ADAM_STEP
    c2 = 1.0 - ADAM_B2 ** ADAM_STEP

    def body(s_ref, w_ref, o_ref, r_ref, m_ref, v_ref, g_ref, d_ref, nm_ref, nv_ref):
        gv = jnp.where(pl.program_id(1) == s_ref[0], o_ref[...], r_ref[...])
        nm = ADAM_B1 * m_ref[...] + (1.0 - ADAM_B1) * gv
        nv = ADAM_B2 * v_ref[...] + (1.0 - ADAM_B2) * (gv * gv)
        g_ref[...] = gv
        nm_ref[...] = nm
        nv_ref[...] = nv
        d_ref[...] = -ADAM_LR * ((nm / c1) / (jnp.sqrt(nv / c2) + ADAM_EPS) + ADAM_WD * w_ref[...])

    wmap = (lambda l, h, i, s: (l, h * nr + i, 0)) if rows_half else (lambda l, h, i, s: (l, i, h))
    wspec = pl.BlockSpec((None, tr, ch), wmap)
    gspec = pl.BlockSpec((None, tr, ch), lambda l, h, i, s: (l, i, 0))
    osh = jax.ShapeDtypeStruct(w3.shape, F32)
    outs = pl.pallas_call(
        body, name="adamw_halves", out_shape=[osh] * 4,
        grid_spec=pltpu.PrefetchScalarGridSpec(num_scalar_prefetch=1, grid=(n_l, 2, nr),
                                               in_specs=[wspec, gspec, gspec, wspec, wspec], out_specs=[wspec] * 4),
        compiler_params=_cp("parallel", "parallel", "parallel"))(core1, w3, o3, r3, m3, v3)
    return tuple(o.reshape(w.shape) for o in outs)


def _adamw(w, g, m, v):
    shape = w.shape
    w2, g2, m2, v2 = (_rows2d(a) if a.ndim > 1 else a.reshape(1, -1) for a in (w, g, m, v))
    rows, cols = w2.shape
    tr = _tile(rows, 256, 8)
    c1 = 1.0 - ADAM_B1 ** ADAM_STEP
    c2 = 1.0 - ADAM_B2 ** ADAM_STEP

    def body(w_ref, g_ref, m_ref, v_ref, d_ref, nm_ref, nv_ref):
        gv = g_ref[...]
        nm = ADAM_B1 * m_ref[...] + (1.0 - ADAM_B1) * gv
        nv = ADAM_B2 * v_ref[...] + (1.0 - ADAM_B2) * (gv * gv)
        nm_ref[...] = nm
        nv_ref[...] = nv
        d_ref[...] = -ADAM_LR * ((nm / c1) / (jnp.sqrt(nv / c2) + ADAM_EPS) + ADAM_WD * w_ref[...])

    spec = pl.BlockSpec((tr, cols), lambda i: (i, 0))
    osh = jax.ShapeDtypeStruct((rows, cols), F32)
    outs = pl.pallas_call(body, name="adamw", grid=(rows // tr,), in_specs=[spec] * 4, out_specs=[spec] * 3,
                          out_shape=[osh] * 3, compiler_params=_cp("parallel"))(w2, g2, m2, v2)
    return tuple(o.reshape(shape) for o in outs)


def _pack(arrs):
    flat = jnp.concatenate([a.reshape(-1) for a in arrs])
    n = flat.shape[0]
    rows = -(-n // (8 * LANE)) * 8
    return jnp.pad(flat, (0, rows * LANE - n)).reshape(rows, LANE)


def _unpack(packed, shapes):
    flat = packed.reshape(-1)
    out, pos = [], 0
    for sh in shapes:
        size = 1
        for dsz in sh:
            size *= dsz
        out.append(flat[pos:pos + size].reshape(sh))
        pos += size
    return out


BIG = ("ffn_w_gate", "ffn_w_up", "ffn_w_down", "ssd_w_in", "ssd_w_out", "pool_w_in", "pool_w_group", "pool_w_out")
WEIGHTS = ("ffn_norm", "ffn_w_gate", "ffn_w_up", "ffn_w_down", "mix_norm", "ssd_w_in", "ssd_conv_w", "ssd_conv_b",
           "ssd_dt_bias", "ssd_a_log", "ssd_d", "ssd_norm", "ssd_w_out", "pool_w_in", "pool_w_group", "pool_scale",
           "pool_w_out", "final_norm")
SMALL = tuple(k for k in WEIGHTS if k not in BIG)
SMALL_SHARDED = {"ffn_norm": 2, "ssd_conv_w": 2, "pool_scale": 1}


def kernel(x, ffn_norm, ffn_w_gate, ffn_w_up, ffn_w_down, mix_norm, ssd_w_in, ssd_conv_w, ssd_conv_b, ssd_dt_bias, ssd_a_log, ssd_d, ssd_norm, ssd_w_out, pool_w_in, pool_w_group, pool_scale, pool_w_out, final_norm, loss_target, m_ffn_norm, m_ffn_w_gate, m_ffn_w_up, m_ffn_w_down, m_mix_norm, m_ssd_w_in, m_ssd_conv_w, m_ssd_conv_b, m_ssd_dt_bias, m_ssd_a_log, m_ssd_d, m_ssd_norm, m_ssd_w_out, m_pool_w_in, m_pool_w_group, m_pool_scale, m_pool_w_out, m_final_norm, v_ffn_norm, v_ffn_w_gate, v_ffn_w_up, v_ffn_w_down, v_mix_norm, v_ssd_w_in, v_ssd_conv_w, v_ssd_conv_b, v_ssd_dt_bias, v_ssd_a_log, v_ssd_d, v_ssd_norm, v_ssd_w_out, v_pool_w_in, v_pool_w_group, v_pool_scale, v_pool_w_out, v_final_norm):
    w = dict(ffn_norm=ffn_norm, ffn_w_gate=ffn_w_gate, ffn_w_up=ffn_w_up, ffn_w_down=ffn_w_down, mix_norm=mix_norm,
             ssd_w_in=ssd_w_in, ssd_conv_w=ssd_conv_w, ssd_conv_b=ssd_conv_b, ssd_dt_bias=ssd_dt_bias,
             ssd_a_log=ssd_a_log, ssd_d=ssd_d, ssd_norm=ssd_norm, ssd_w_out=ssd_w_out, pool_w_in=pool_w_in,
             pool_w_group=pool_w_group, pool_scale=pool_scale, pool_w_out=pool_w_out, final_norm=final_norm)
    mom = dict(ffn_norm=m_ffn_norm, ffn_w_gate=m_ffn_w_gate, ffn_w_up=m_ffn_w_up, ffn_w_down=m_ffn_w_down,
               mix_norm=m_mix_norm, ssd_w_in=m_ssd_w_in, ssd_conv_w=m_ssd_conv_w, ssd_conv_b=m_ssd_conv_b,
               ssd_dt_bias=m_ssd_dt_bias, ssd_a_log=m_ssd_a_log, ssd_d=m_ssd_d, ssd_norm=m_ssd_norm,
               ssd_w_out=m_ssd_w_out, pool_w_in=m_pool_w_in, pool_w_group=m_pool_w_group, pool_scale=m_pool_scale,
               pool_w_out=m_pool_w_out, final_norm=m_final_norm)
    vel = dict(ffn_norm=v_ffn_norm, ffn_w_gate=v_ffn_w_gate, ffn_w_up=v_ffn_w_up, ffn_w_down=v_ffn_w_down,
               mix_norm=v_mix_norm, ssd_w_in=v_ssd_w_in, ssd_conv_w=v_ssd_conv_w, ssd_conv_b=v_ssd_conv_b,
               ssd_dt_bias=v_ssd_dt_bias, ssd_a_log=v_ssd_a_log, ssd_d=v_ssd_d, ssd_norm=v_ssd_norm,
               ssd_w_out=v_ssd_w_out, pool_w_in=v_pool_w_in, pool_w_group=v_pool_w_group, pool_scale=v_pool_scale,
               pool_w_out=v_pool_w_out, final_norm=v_final_norm)
    depth = ffn_w_gate.shape[0]
    n_s, n_p = ssd_w_in.shape[0], pool_w_in.shape[0]
    chip = 2 * lax.axis_index("x") + lax.axis_index("y")

    chip1 = jnp.reshape(chip, (1,)).astype(jnp.int32)
    core1 = jnp.reshape(lax.axis_index("c"), (1,)).astype(jnp.int32)

    gate3, up3, down3 = _view3(ffn_w_gate), _view3(ffn_w_up), _view3(ffn_w_down)
    d_model = ssd_w_in.shape[1]
    pending = {}
    sharded_names = tuple(SMALL_SHARDED)
    tok = jnp.zeros((), F32)

    def begin(group, name, fulls, axes):
        nonlocal tok
        pending[group], t = _gather_begin(name, fulls, axes)
        tok = tok + t

    for i in range(depth):
        for h in range(2):
            fulls = [_cast_place(gate3, chip1, "cols", 2 * i + h, 1), _cast_place(up3, chip1, "cols", 2 * i + h, 1),
                     _cast_place(down3, chip1, "rows", 2 * i + h, 1)]
            if (i, h) == (0, 0):
                packed_small = _pack([w[k] for k in sharded_names])
                fulls.append(_cast_place(packed_small[None], chip1, "slot", 0, 1, F32))
                begin(("ffn", i, h), "gather_first", fulls, [(2, 1), (2, 1), (1, 2), (1, 2)])
            else:
                begin(("ffn", i, h), "gather_ffn", fulls, [(2, 1), (2, 1), (1, 2)])
            j = i // 2
            if h == 0 and i % 2 == 0:
                begin(("ssd", j), "gather_ssd", [_cast_place(ssd_w_in, chip1, "slot", j, 1),
                                                 _cast_place(ssd_w_out, chip1, "rows", j, 1)], [(1, 2), (1, 2)])
            if h == 0 and i % 2 == 1:
                n_g = pool_w_group.shape[1]
                grp_full = _cast_place(_view3(pool_w_group), chip1, "rows", j * n_g, n_g)
                begin(("pool", j), "gather_pool", [_cast_place(pool_w_in, chip1, "rows", j, 1), grp_full[None],
                                                   _cast_place(pool_w_out, chip1, "rows", j, 1)],
                      [(1, 2), (2, 3), (1, 2)])

    fetched = {}

    def fetch(group, after):
        if group not in fetched:
            got = _gather_finish(pending[group], after)
            if group[0] == "ssd":
                got = [got[0].transpose(0, 2, 1, 3).reshape(1, d_model, -1), got[1]]
            fetched[group] = got[:3]
            if group == ("ffn", 0, 0):
                fetched["small"] = got[3][0]
        return fetched[group]

    fetch(("ffn", 0, 0), x)
    small = {k: w[k] for k in SMALL if k not in SMALL_SHARDED}
    per_chip = [_unpack(fetched["small"][s], [w[k].shape for k in sharded_names]) for s in range(N_CHIPS)]
    for t, k in enumerate(sharded_names):
        small[k] = jnp.concatenate([per_chip[s][t] for s in range(N_CHIPS)], axis=SMALL_SHARDED[k])

    ffn_names = ("ffn_w_gate", "ffn_w_up", "ffn_w_down")
    group_axes = dict(ffn=[(1, 0), (1, 0), (0, 1)], ssd=[(0, 1), (0, 1)], pool=[(0, 1), (1, 2), (0, 1)])
    group_names = dict(ffn=ffn_names, ssd=("ssd_w_in", "ssd_w_out"), pool=("pool_w_in", "pool_w_group", "pool_w_out"))
    travelling = {}

    def emit(group, grads):
        kind = group[0]
        axes = group_axes[kind]
        if kind == "ssd":
            grads = [grads[0].reshape(d_model, N_CHIPS, -1).transpose(1, 0, 2), grads[1]]
        recv_a = _rs_cores("rs_cores_" + kind, grads, axes)
        chip_sums = [_add_pair(g, r, ha, core1) for g, r, (_, ha) in zip(grads, recv_a, axes)]
        travelling[group], t = _rs_chips_begin("rs_chips_" + kind, chip_sums, axes)
        return t

    total = {}

    def on_loss(part):
        total["loss"] = lax.psum(part[0, 0], ("x", "y", "c"))
        return jnp.minimum(total["loss"], 0.0)

    _, dx, gsmall = _local_step(x[0], loss_target[0], depth, fetch, emit, on_loss, small, tok)
    loss = total["loss"]

    sums = {k: {} for k in BIG}
    for group, pend in travelling.items():
        kind = group[0]
        chip_sums, recv_b = _rs_chips_finish(pend, dx)
        for k, cs, r, (sa, _) in zip(group_names[kind], chip_sums, recv_b, group_axes[kind]):
            sums[k][group[1:]] = _add_four(cs, r, sa, chip1)
    sums = {k: [v[idx] for idx in sorted(v)] for k, v in sums.items()}
    mixer_names = tuple(k for k in BIG if k not in ffn_names)
    from_sibling = dict(zip(ffn_names, _rs_finish("rs_finish_ffn", [sums[k] for k in ffn_names])))
    from_sibling.update(zip(mixer_names, _rs_finish("rs_finish_mixers", [sums[k] for k in mixer_names])))

    grad, delta, new_m, new_v = {}, {}, {}, {}
    rows_half = dict(ffn_w_gate=True, ffn_w_up=True, ffn_w_down=False, ssd_w_in=True, ssd_w_out=False,
                     pool_w_in=False, pool_w_group=False, pool_w_out=False)
    for k in BIG:
        grad[k], delta[k], new_m[k], new_v[k] = _adamw_halves(w[k], jnp.stack(sums[k]), from_sibling[k], mom[k],
                                                              vel[k], rows_half[k], core1)

    small_shapes = [gsmall[k].shape for k in SMALL]
    summed = _unpack(_allgather8(_pack([gsmall[k] for k in SMALL]), True), small_shapes)
    for k, g in zip(SMALL, summed):
        if k in SMALL_SHARDED:
            ax = SMALL_SHARDED[k]
            size = w[k].shape[ax]
            g = lax.dynamic_slice_in_dim(g, chip * size, size, axis=ax)
        grad[k] = g

    shapes = [w[k].shape for k in SMALL]
    packed = _adamw(*(_pack([src[k] for k in SMALL]) for src in (w, grad, mom, vel)))
    for dst, p in zip((delta, new_m, new_v), packed):
        for k, a in zip(SMALL, _unpack(p, shapes)):
            dst[k] = a

    return (loss, dx[None], *[grad[k] for k in WEIGHTS], *[delta[k] for k in WEIGHTS],
            *[new_m[k] for k in WEIGHTS], *[new_v[k] for k in WEIGHTS])
```

```python
import functools

import jax
import jax.numpy as jnp
from jax import lax
from jax.experimental import pallas as pl
from jax.experimental.pallas import tpu as pltpu

F32 = jnp.float32
BF16 = jnp.bfloat16
EPS = 1e-6
MESH = pl.DeviceIdType.MESH

SSD_CHUNK = 128
SSD_STATE = 128
SSD_HEAD_DIM = 64
HEADS_PER_GROUP = 8
GROUP_W = HEADS_PER_GROUP * SSD_HEAD_DIM
CONV_W = 5
POOL_WINDOWS = (2, 4, 8, 16)
N_CHIPS = 4

ADAM_LR = 0.001
ADAM_B1 = 0.9
ADAM_B2 = 0.999
ADAM_EPS = 1e-08
ADAM_WD = 0.01
ADAM_STEP = 10

VMEM_LIMIT = 56 * 1024 * 1024
LANE = 128


def _cp(*sem):
    return pltpu.CompilerParams(dimension_semantics=sem, vmem_limit_bytes=VMEM_LIMIT)


def _tile(dim, pref, unit=LANE):
    if dim <= pref:
        return dim
    t = (pref // unit) * unit
    while t >= unit:
        if dim % t == 0:
            return t
        t -= unit
    return dim


def _sigmoid(v):
    return 1.0 / (1.0 + jnp.exp(-v))


def _dot(a, b):
    return jnp.dot(a, b, preferred_element_type=F32)


def _dot_nt(a, b):
    return lax.dot_general(a, b, (((1,), (1,)), ((), ())), preferred_element_type=F32)


def _split3(v):
    h1 = v.astype(BF16)
    r1 = v - h1.astype(F32)
    h2 = r1.astype(BF16)
    h3 = (r1 - h2.astype(F32)).astype(BF16)
    return h1, h2, h3


def _dot_exact01(m01, v):
    mb = m01.astype(BF16)
    h1, h2, h3 = _split3(v)
    return _dot(mb, h1) + _dot(mb, h2) + _dot(mb, h3)


def _dot_nt_exact01(m01, v):
    mb = m01.astype(BF16)
    h1, h2, h3 = _split3(v)
    return _dot_nt(mb, h1) + _dot_nt(mb, h2) + _dot_nt(mb, h3)


def _mm(a, b, *, name, ta=False, tb=False, a_sel=(), b_sel=(), pair2=None, add=None, scale=1.0,
        out_dtype=F32, tm=1024, tn=1024, tk=2048):
    am, ak = a.shape[-2:][::-1] if ta else a.shape[-2:]
    bk, bn = b.shape[-2:][::-1] if tb else b.shape[-2:]
    assert ak == bk, (a.shape, b.shape, ta, tb)
    m_dim, n_dim, k_dim = am, bn, ak
    tm, tn, tk = _tile(m_dim, tm), _tile(n_dim, tn), _tile(k_dim, tk)
    nk = k_dim // tk
    grid = (n_dim // tn, m_dim // tm, nk)

    def a_spec(sel):
        lead = (None,) * len(sel)
        if ta:
            return pl.BlockSpec(lead + (tk, tm), lambda n, m, k: tuple(sel) + (k, m))
        return pl.BlockSpec(lead + (tm, tk), lambda n, m, k: tuple(sel) + (m, k))

    def b_spec(sel):
        lead = (None,) * len(sel)
        if tb:
            return pl.BlockSpec(lead + (tn, tk), lambda n, m, k: tuple(sel) + (n, k))
        return pl.BlockSpec(lead + (tk, tn), lambda n, m, k: tuple(sel) + (k, n))

    ins, specs = [a, b], [a_spec(a_sel), b_spec(b_sel)]
    if pair2 is not None:
        a2, b2, a2_sel, b2_sel = pair2
        ins += [a2, b2]
        specs += [a_spec(a2_sel), b_spec(b2_sel)]
    if add is not None:
        ins.append(add)
        specs.append(pl.BlockSpec((tm, tn), lambda n, m, k: (m, n)))
    dn = (((0 if ta else 1,), (1 if tb else 0,)), ((), ()))
    n_pairs = 2 if pair2 is not None else 1

    def body(*refs):
        pairs = [(refs[2 * i], refs[2 * i + 1]) for i in range(n_pairs)]
        pos = 2 * n_pairs
        add_ref = None
        if add is not None:
            add_ref = refs[pos]
            pos += 1
        o_ref = refs[pos]
        acc_ref = refs[pos + 1] if nk > 1 else None

        def prod():
            tot = None
            for ar, br in pairs:
                p = lax.dot_general(ar[...].astype(BF16), br[...].astype(BF16), dn, preferred_element_type=F32)
                tot = p if tot is None else tot + p
            return tot

        def finish(r):
            if scale != 1.0:
                r = r * scale
            if add_ref is not None:
                r = add_ref[...] + r
            o_ref[...] = r.astype(out_dtype)

        if nk == 1:
            finish(prod())
        else:
            k = pl.program_id(2)

            @pl.when(k == 0)
            def _():
                acc_ref[...] = jnp.zeros_like(acc_ref)

            acc_ref[...] += prod()

            @pl.when(k == nk - 1)
            def _():
                finish(acc_ref[...])

    return pl.pallas_call(
        body, name=name, grid=grid, in_specs=specs,
        out_specs=pl.BlockSpec((tm, tn), lambda n, m, k: (m, n)),
        out_shape=jax.ShapeDtypeStruct((m_dim, n_dim), out_dtype),
        scratch_shapes=[pltpu.VMEM((tm, tn), F32)] if nk > 1 else [],
        compiler_params=_cp("parallel", "parallel", "arbitrary"),
    )(*ins)


def _rmsnorm(x, g):
    t_dim, d = x.shape
    tr = _tile(t_dim, 256, 8)

    def body(x_ref, g_ref, o_ref):
        xv = x_ref[...]
        r = lax.rsqrt(jnp.mean(xv * xv, axis=-1, keepdims=True) + EPS)
        o_ref[...] = (xv * r * g_ref[...]).astype(BF16)

    return pl.pallas_call(
        body, name="rmsnorm", grid=(t_dim // tr,),
        in_specs=[pl.BlockSpec((tr, d), lambda i: (i, 0)), pl.BlockSpec((1, d), lambda i: (0, 0))],
        out_specs=pl.BlockSpec((tr, d), lambda i: (i, 0)),
        out_shape=jax.ShapeDtypeStruct((t_dim, d), BF16),
        compiler_params=_cp("parallel"),
    )(x, g)


def _rmsnorm_bwd(x, g, dh, dres):
    t_dim, d = x.shape
    tr = _tile(t_dim, 256, 8)

    def body(x_ref, g_ref, dh_ref, dres_ref, dx_ref, dg_ref):
        i = pl.program_id(0)
        xv = x_ref[...]
        r = lax.rsqrt(jnp.mean(xv * xv, axis=-1, keepdims=True) + EPS)
        n = xv * r
        dhv = dh_ref[...]
        dn = dhv * g_ref[...]

        @pl.when(i == 0)
        def _():
            dg_ref[...] = jnp.zeros_like(dg_ref)

        dg_ref[...] += jnp.sum(dhv * n, axis=0, keepdims=True)
        dx_ref[...] = dres_ref[...] + r * (dn - n * jnp.mean(dn * n, axis=-1, keepdims=True))

    row = pl.BlockSpec((tr, d), lambda i: (i, 0))
    vec = pl.BlockSpec((1, d), lambda i: (0, 0))
    return pl.pallas_call(
        body, name="rmsnorm_bwd", grid=(t_dim // tr,),
        in_specs=[row, vec, row, row], out_specs=[row, vec],
        out_shape=[jax.ShapeDtypeStruct((t_dim, d), F32), jax.ShapeDtypeStruct((1, d), F32)],
        compiler_params=_cp("arbitrary"),
    )(x, g, dh, dres)


def _loss_head(x, g, target):
    t_dim, d = x.shape
    tr = _tile(t_dim, 256, 8)

    def body(x_ref, g_ref, t_ref, loss_ref, dx_ref, dg_ref):
        i = pl.program_id(0)
        xv = x_ref[...]
        gv = g_ref[...]
        r = lax.rsqrt(jnp.mean(xv * xv, axis=-1, keepdims=True) + EPS)
        n = xv * r
        err = n * gv - t_ref[...]

        @pl.when(i == 0)
        def _():
            dg_ref[...] = jnp.zeros_like(dg_ref)
            loss_ref[...] = jnp.zeros_like(loss_ref)

        per_tok = jnp.mean(err * err, axis=-1, keepdims=True)
        loss_ref[...] += 0.5 * jnp.sum(per_tok, axis=0, keepdims=True)
        dy = err * (1.0 / d)
        dn = dy * gv
        dg_ref[...] += jnp.sum(dy * n, axis=0, keepdims=True)
        dx_ref[...] = r * (dn - n * jnp.mean(dn * n, axis=-1, keepdims=True))

    row = pl.BlockSpec((tr, d), lambda i: (i, 0))
    vec = pl.BlockSpec((1, d), lambda i: (0, 0))
    one = pl.BlockSpec((1, 1), lambda i: (0, 0))
    return pl.pallas_call(
        body, name="loss_head", grid=(t_dim // tr,),
        in_specs=[row, vec, row], out_specs=[one, row, vec],
        out_shape=[jax.ShapeDtypeStruct((1, 1), F32), jax.ShapeDtypeStruct((t_dim, d), F32),
                   jax.ShapeDtypeStruct((1, d), F32)],
        compiler_params=_cp("arbitrary"),
    )(x, g, target)


def _ffn_in(h, wg, wu, half):
    t_dim, d = h.shape
    f = wg.shape[-1]
    tm, tn = _tile(t_dim, 512), _tile(f, 1408)

    def body(h_ref, wg_ref, wu_ref, g_ref, u_ref, a_ref):
        hv = h_ref[...]
        gv = _dot(hv, wg_ref[...])
        uv = _dot(hv, wu_ref[...])
        g_ref[...] = gv.astype(BF16)
        u_ref[...] = uv.astype(BF16)
        a_ref[...] = (gv * _sigmoid(gv) * uv).astype(BF16)

    wspec = pl.BlockSpec((None, d, tn), lambda n, m: (half, 0, n))
    ospec = pl.BlockSpec((tm, tn), lambda n, m: (m, n))
    oshape = jax.ShapeDtypeStruct((t_dim, f), BF16)
    return pl.pallas_call(
        body, name="ffn_in", grid=(f // tn, t_dim // tm),
        in_specs=[pl.BlockSpec((tm, d), lambda n, m: (m, 0)), wspec, wspec],
        out_specs=[ospec, ospec, ospec], out_shape=[oshape, oshape, oshape],
        compiler_params=_cp("parallel", "parallel"),
    )(h, wg, wu)


def _ffn_bwd_act(dx, wd, g, u, half):
    t_dim, d = dx.shape
    f = wd.shape[-2]
    tm, tn = _tile(t_dim, 512), _tile(f, 1408)

    def body(dx_ref, wd_ref, g_ref, u_ref, dg_ref, du_ref):
        da = 0.5 * _dot_nt(dx_ref[...].astype(BF16), wd_ref[...])
        gv = g_ref[...].astype(F32)
        uv = u_ref[...].astype(F32)
        s = _sigmoid(gv)
        dg_ref[...] = (da * uv * (s * (1.0 + gv * (1.0 - s)))).astype(BF16)
        du_ref[...] = (da * gv * s).astype(BF16)

    tile = pl.BlockSpec((tm, tn), lambda n, m: (m, n))
    oshape = jax.ShapeDtypeStruct((t_dim, f), BF16)
    return pl.pallas_call(
        body, name="ffn_bwd_act", grid=(f // tn, t_dim // tm),
        in_specs=[pl.BlockSpec((tm, d), lambda n, m: (m, 0)),
                  pl.BlockSpec((None, tn, d), lambda n, m: (half, n, 0)), tile, tile],
        out_specs=[tile, tile], out_shape=[oshape, oshape],
        compiler_params=_cp("parallel", "parallel"),
    )(dx, wd, g, u)


def _ffn_fwd(x, norm_g, wg, wu, wd, half):
    h = _rmsnorm(x, norm_g)
    g, u, a = _ffn_in(h, wg, wu, half)
    x_new = _mm(a, wd, name="ffn_out", b_sel=(half,), add=x, scale=0.5, tk=1408)
    return x_new, (x, h, g, u, a)


def _ffn_bwd(dx, saved, norm_g, wg, wu, wd, half):
    x, h, g, u, a = saved
    dg, du = _ffn_bwd_act(dx, wd, g, u, half)
    d_wd = _mm(a, dx, name="ffn_dwd", ta=True, scale=0.5, out_dtype=BF16, tm=1408, tn=1024)
    d_wg = _mm(h, dg, name="ffn_dwgu", ta=True, out_dtype=BF16, tm=1024, tn=1408)
    d_wu = _mm(h, du, name="ffn_dwgu", ta=True, out_dtype=BF16, tm=1024, tn=1408)
    dh = _mm(dg, wg, name="ffn_dh", tb=True, b_sel=(half,), pair2=(du, wu, (), (half,)), tk=1408)
    dx_new, dnorm = _rmsnorm_bwd(x, norm_g, dh, dx)
    return dx_new, dnorm, d_wg, d_wu, d_wd


def _shifted(v, off, t_idx):
    if off == 0:
        return v
    t_dim = v.shape[0]
    sh = pltpu.roll(v, (-off) % t_dim, 0)
    valid = jnp.logical_and(t_idx + off >= 0, t_idx + off < t_dim)
    return jnp.where(valid, sh, 0.0)


def _conv_pre(u, w_ref, b_ref, t_idx):
    acc = jnp.zeros_like(u) + b_ref[...]
    shifted = []
    for k in range(CONV_W):
        sh = _shifted(u, k - CONV_W // 2, t_idx)
        shifted.append(sh)
        acc = acc + w_ref[k:k + 1, :] * sh
    return acc, shifted


def _conv_silu(proj, conv_w, conv_b, col0):
    t_dim = proj.shape[0]
    cd = conv_w.shape[-1]
    cb = _tile(cd, 256)
    assert col0 % cb == 0

    def body(u_ref, w_ref, b_ref, o_ref):
        t_idx = lax.broadcasted_iota(jnp.int32, (t_dim, cb), 0)
        pre, _ = _conv_pre(u_ref[...], w_ref, b_ref, t_idx)
        o_ref[...] = pre * _sigmoid(pre)

    return pl.pallas_call(
        body, name="conv_silu", grid=(cd // cb,),
        in_specs=[pl.BlockSpec((t_dim, cb), lambda j: (0, col0 // cb + j)),
                  pl.BlockSpec((CONV_W, cb), lambda j: (0, j)), pl.BlockSpec((1, cb), lambda j: (0, j))],
        out_specs=pl.BlockSpec((t_dim, cb), lambda j: (0, j)),
        out_shape=jax.ShapeDtypeStruct((t_dim, cd), F32),
        compiler_params=_cp("parallel"),
    )(proj, conv_w, conv_b)


def _conv_silu_bwd(proj, conv_w, conv_b, dact2, col0):
    t_dim = proj.shape[0]
    cd = conv_w.shape[-1]
    cb = _tile(cd, 256)

    def body(u_ref, w_ref, b_ref, da_ref, du_ref, dwb_ref):
        t_idx = lax.broadcasted_iota(jnp.int32, (t_dim, cb), 0)
        pre, shifted = _conv_pre(u_ref[...], w_ref, b_ref, t_idx)
        s = _sigmoid(pre)
        dpre = (da_ref[0] + da_ref[1]) * (s * (1.0 + pre * (1.0 - s)))
        du = jnp.zeros_like(dpre)
        for k in range(CONV_W):
            du = du + w_ref[k:k + 1, :] * _shifted(dpre, -(k - CONV_W // 2), t_idx)
            dwb_ref[k:k + 1, :] = jnp.sum(dpre * shifted[k], axis=0, keepdims=True)
        dwb_ref[CONV_W:CONV_W + 1, :] = jnp.sum(dpre, axis=0, keepdims=True)
        dwb_ref[CONV_W + 1:8, :] = jnp.zeros((8 - CONV_W - 1, cb), F32)
        du_ref[...] = du.astype(BF16)

    return pl.pallas_call(
        body, name="conv_silu_bwd", grid=(cd // cb,),
        in_specs=[pl.BlockSpec((t_dim, cb), lambda j: (0, col0 // cb + j)),
                  pl.BlockSpec((CONV_W, cb), lambda j: (0, j)), pl.BlockSpec((1, cb), lambda j: (0, j)),
                  pl.BlockSpec((2, t_dim, cb), lambda j: (0, 0, j))],
        out_specs=[pl.BlockSpec((t_dim, cb), lambda j: (0, j)), pl.BlockSpec((8, cb), lambda j: (0, j))],
        out_shape=[jax.ShapeDtypeStruct((t_dim, cd), BF16), jax.ShapeDtypeStruct((8, cd), F32)],
        compiler_params=_cp("parallel"),
    )(proj, conv_w, conv_b, dact2)


def _softplus_fwd(dt_raw, bias):
    def body(r_ref, b_ref, o_ref):
        v = r_ref[...] + b_ref[...]
        o_ref[...] = jnp.maximum(v, 0.0) + jnp.log(1.0 + jnp.exp(-jnp.abs(v)))

    return pl.pallas_call(body, name="softplus", out_shape=jax.ShapeDtypeStruct(dt_raw.shape, F32))(dt_raw, bias)


def _softplus_bwd(dt_raw, bias, ddt, da, a_log):
    def body(r_ref, b_ref, ddt_ref, da_ref, al_ref, dr_ref, db_ref, dal_ref):
        dv = ddt_ref[...] * _sigmoid(r_ref[...] + b_ref[...])
        dr_ref[...] = dv.astype(BF16)
        db_ref[...] = jnp.sum(dv, axis=0, keepdims=True)
        dal_ref[...] = -da_ref[...] * jnp.exp(al_ref[...])

    vec = jax.ShapeDtypeStruct(bias.shape, F32)
    return pl.pallas_call(
        body, name="softplus_bwd",
        out_shape=[jax.ShapeDtypeStruct(dt_raw.shape, BF16), vec, vec])(dt_raw, bias, ddt, da, a_log)


def _chunk_setup(d, dt_ref, al_ref):
    q = SSD_CHUNK
    ii = lax.broadcasted_iota(jnp.int32, (q, q), 0)
    jj = lax.broadcasted_iota(jnp.int32, (q, q), 1)
    sgn = 1 - 2 * d
    mask = (jj - ii) * sgn <= 0
    mask_t = (ii - jj) * sgn <= 0
    m01 = mask.astype(F32)
    m01_t = mask_t.astype(F32)
    dt = dt_ref[...]
    a = -jnp.exp(al_ref[...])
    dta = dt * a
    cs = _dot_exact01(m01, dta)
    tot = jnp.sum(dta, axis=0, keepdims=True)
    return mask, mask_t, m01, m01_t, dt, a, dta, cs, tot


def _head_lanes():
    hh = lax.broadcasted_iota(jnp.int32, (HEADS_PER_GROUP, GROUP_W), 0)
    ll = lax.broadcasted_iota(jnp.int32, (HEADS_PER_GROUP, GROUP_W), 1)
    return jnp.logical_and(ll >= hh * SSD_HEAD_DIM, ll < (hh + 1) * SSD_HEAD_DIM).astype(BF16)


def _expand(v, e16):
    h1, h2, h3 = _split3(v)
    return _dot(h1, e16) + _dot(h2, e16) + _dot(h3, e16)


def _expand_row(v, e16):
    return _expand(jnp.broadcast_to(v, (8, HEADS_PER_GROUP)), e16)[0:1]


def _per_head(v, e16):
    h1, h2, h3 = _split3(v)
    return _dot_nt(h1, e16) + _dot_nt(h2, e16) + _dot_nt(h3, e16)


def _rows_of(cs):
    n = cs.shape[1]
    eye = lax.broadcasted_iota(jnp.int32, (n, n), 0) == lax.broadcasted_iota(jnp.int32, (n, n), 1)
    return _dot_nt_exact01(eye.astype(F32), cs)


def _ssd_specs(t_dim, di, g_cnt, chunk_of):
    q, ns = SSD_CHUNK, SSD_STATE
    x_spec = pl.BlockSpec((q, GROUP_W), lambda d, g, c: (chunk_of(d, c), g))
    b_spec = pl.BlockSpec((q, ns), lambda d, g, c: (chunk_of(d, c), di // ns + g))
    c_spec = pl.BlockSpec((q, ns), lambda d, g, c: (chunk_of(d, c), di // ns + g_cnt + g))
    dt_spec = pl.BlockSpec((None, None, q, HEADS_PER_GROUP), lambda d, g, c: (d, g, chunk_of(d, c), 0))
    al_spec = pl.BlockSpec((None, None, 1, HEADS_PER_GROUP), lambda d, g, c: (d, g, 0, 0))
    return x_spec, b_spec, c_spec, dt_spec, al_spec


def _ssd_fwd(xbc, dt4, al4):
    t_dim = xbc.shape[0]
    g_cnt = dt4.shape[1]
    di = g_cnt * GROUP_W
    q, ns, p = SSD_CHUNK, SSD_STATE, SSD_HEAD_DIM
    nc = t_dim // q

    def chunk_of(d, c):
        return c + d * (nc - 1 - 2 * c)

    def body(x_ref, b_ref, c_ref, dt_ref, al_ref, y_ref, hin_ref, h_sc):
        d = pl.program_id(0)
        c = pl.program_id(2)

        @pl.when(c == 0)
        def _():
            h_sc[...] = jnp.zeros_like(h_sc)

        mask, mask_t, m01, m01_t, dt, a, dta, cs, tot = _chunk_setup(d, dt_ref, al_ref)
        e16 = _head_lanes()
        cs_rows = _rows_of(cs)
        cb16 = c_ref[...].astype(BF16)
        bt16 = b_ref[...].T.astype(BF16)
        cb = _dot(cb16, bt16)
        hin = h_sc[...]
        hin_ref[...] = hin
        xdt = x_ref[...] * _expand(dt, e16)
        y_off = _dot(cb16, hin.astype(BF16)) * _expand(jnp.exp(cs), e16)
        low = lax.broadcasted_iota(jnp.int32, (q, 2 * p), 1) < p
        for pair in range(HEADS_PER_GROUP // 2):
            ps = slice(2 * p * pair, 2 * p * (pair + 1))
            blk = xdt[:, ps]
            acc = y_off[:, ps]
            for hh in range(2):
                h = 2 * pair + hh
                lmat = jnp.exp(jnp.where(mask, cs[:, h:h + 1] - cs_rows[h:h + 1, :], -1e30))
                xm = jnp.where(low, blk, 0.0) if hh == 0 else jnp.where(low, 0.0, blk)
                acc = acc + _dot((cb * lmat).astype(BF16), xm.astype(BF16))
            y_ref[:, ps] = acc
        xd = xdt * _expand(jnp.exp(tot - cs), e16)
        st = _dot(bt16, xd.astype(BF16))
        h_sc[...] = hin * _expand_row(jnp.exp(tot), e16) + st

    x_spec, b_spec, c_spec, dt_spec, al_spec = _ssd_specs(t_dim, di, g_cnt, chunk_of)
    return pl.pallas_call(
        body, name="ssd_fwd", grid=(2, g_cnt, nc),
        in_specs=[x_spec, b_spec, c_spec, dt_spec, al_spec],
        out_specs=[pl.BlockSpec((None, q, GROUP_W), lambda d, g, c: (d, chunk_of(d, c), g)),
                   pl.BlockSpec((None, None, None, ns, GROUP_W), lambda d, g, c: (d, g, chunk_of(d, c), 0, 0))],
        out_shape=[jax.ShapeDtypeStruct((2, t_dim, di), F32),
                   jax.ShapeDtypeStruct((2, g_cnt, nc, ns, GROUP_W), F32)],
        scratch_shapes=[pltpu.VMEM((ns, GROUP_W), F32)],
        compiler_params=_cp("parallel", "parallel", "arbitrary"),
    )(xbc, xbc, xbc, dt4, al4)


def _ssd_bwd(xbc, dt4, al4, hin_all, dy, dvec):
    t_dim = xbc.shape[0]
    g_cnt = dt4.shape[1]
    di = g_cnt * GROUP_W
    q, ns, p, hg = SSD_CHUNK, SSD_STATE, SSD_HEAD_DIM, HEADS_PER_GROUP
    nc = t_dim // q

    def chunk_of(d, c):
        return (nc - 1 - c) + d * (2 * c - nc + 1)

    def body(x_ref, b_ref, c_ref, dt_ref, al_ref, hin_ref, dy_ref, dv_ref,
             dx_ref, db_ref, dc_ref, ddt_ref, da_ref, g_sc, dxdt_sc, zrow_sc):
        d = pl.program_id(0)
        c = pl.program_id(2)

        @pl.when(c == 0)
        def _():
            g_sc[...] = jnp.zeros_like(g_sc)
            da_ref[...] = jnp.zeros_like(da_ref)

        mask, mask_t, m01, m01_t, dt, a, dta, cs, tot = _chunk_setup(d, dt_ref, al_ref)
        xv = x_ref[...]
        dyv = dy_ref[...]
        bb = b_ref[...].astype(BF16)
        cb16 = c_ref[...].astype(BF16)
        bt16 = b_ref[...].T.astype(BF16)
        ct16 = c_ref[...].T.astype(BF16)
        cb = _dot(cb16, bt16)
        cbt = _dot(bb, ct16)
        hin = hin_ref[...]
        hin16 = hin.astype(BF16)
        gst = g_sc[...]
        gst16 = gst.astype(BF16)
        ch = _dot(cb16, hin16)
        wst = _dot(bb, gst16)
        skip = jnp.where(d == 0, 1.0, 0.0)
        e16 = _head_lanes()
        cs_rows = _rows_of(cs)
        dt_x = _expand(dt, e16)
        e_x = _expand(jnp.exp(cs), e16)
        dec = jnp.exp(tot - cs)
        etot = jnp.exp(tot)
        xdt = xv * dt_x
        dye = dyv * e_x
        t1 = _per_head(wst * xdt, e16) * dec
        dcs = _per_head(dye * ch, e16) - t1
        dtot = jnp.sum(t1, axis=0, keepdims=True)
        dec_x = _expand(dec, e16)
        dxdt_state = wst * dec_x
        dcb = jnp.zeros((q, q), F32)
        low = lax.broadcasted_iota(jnp.int32, (q, 2 * p), 1) < p
        col8 = lax.broadcasted_iota(jnp.int32, (q, hg), 1)
        for pair in range(hg // 2):
            ps = slice(2 * p * pair, 2 * p * (pair + 1))
            xblk, dyblk = xdt[:, ps], dyv[:, ps]
            acc = dxdt_state[:, ps]
            for hh in range(2):
                h = 2 * pair + hh
                seg = cs[:, h:h + 1] - cs_rows[h:h + 1, :]
                lmat = jnp.exp(jnp.where(mask, seg, -1e30))
                lmat_t = jnp.exp(jnp.where(mask_t, -seg, -1e30))
                if hh == 0:
                    xm, dym = jnp.where(low, xblk, 0.0).astype(BF16), jnp.where(low, dyblk, 0.0).astype(BF16)
                else:
                    xm, dym = jnp.where(low, 0.0, xblk).astype(BF16), jnp.where(low, 0.0, dyblk).astype(BF16)
                acc = acc + _dot((cbt * lmat_t).astype(BF16), dym)
                dm = _dot_nt(dym, xm)
                z = dm * (cb * lmat)
                dcb = dcb + dm * lmat
                dcs = dcs + jnp.where(col8 == h, jnp.sum(z, axis=-1, keepdims=True), 0.0)
                zrow_sc[h:h + 1, :] = jnp.sum(z, axis=0, keepdims=True)
            dxdt_sc[:, ps] = acc
        dxdt = dxdt_sc[...]
        dx_ref[...] = dxdt * dt_x + skip * dyv * dv_ref[...]
        dye16 = dye.astype(BF16)
        dcb16 = dcb.astype(BF16)
        dc_ref[...] = _dot_nt(dye16, hin16) + _dot(dcb16, bb)
        db_ref[...] = _dot_nt((xdt * dec_x).astype(BF16), gst16) + _dot(dcb.T.astype(BF16), cb16)
        g_sc[...] = _dot(ct16, dye16) + gst * _expand_row(etot, e16)
        carried = jnp.broadcast_to(jnp.sum(gst * hin, axis=0, keepdims=True), (8, GROUP_W))
        dtot = dtot + _per_head(carried, e16)[0:1] * etot
        ddta = _dot_exact01(m01_t, dcs) - _dot_nt_exact01(m01_t, zrow_sc[...]) + dtot
        ddt_ref[...] = _per_head(dxdt * xv, e16) + ddta * a
        da_ref[...] += jnp.sum(ddta * dt, axis=0, keepdims=True)

    x_spec, b_spec, c_spec, dt_spec, al_spec = _ssd_specs(t_dim, di, g_cnt, chunk_of)
    hin_spec = pl.BlockSpec((None, None, None, ns, GROUP_W), lambda d, g, c: (d, g, chunk_of(d, c), 0, 0))
    dy_spec = pl.BlockSpec((q, GROUP_W), lambda d, g, c: (chunk_of(d, c), g))
    dv_spec = pl.BlockSpec((1, GROUP_W), lambda d, g, c: (0, g))
    gn = g_cnt * ns
    return pl.pallas_call(
        body, name="ssd_bwd", grid=(2, g_cnt, nc),
        in_specs=[x_spec, b_spec, c_spec, dt_spec, al_spec, hin_spec, dy_spec, dv_spec],
        out_specs=[pl.BlockSpec((None, q, GROUP_W), lambda d, g, c: (d, chunk_of(d, c), g)),
                   pl.BlockSpec((None, q, ns), lambda d, g, c: (d, chunk_of(d, c), g)),
                   pl.BlockSpec((None, q, ns), lambda d, g, c: (d, chunk_of(d, c), g)),
                   pl.BlockSpec((None, None, q, hg), lambda d, g, c: (d, g, chunk_of(d, c), 0)),
                   pl.BlockSpec((None, None, 1, hg), lambda d, g, c: (d, g, 0, 0))],
        out_shape=[jax.ShapeDtypeStruct((2, t_dim, di), F32), jax.ShapeDtypeStruct((2, t_dim, gn), F32),
                   jax.ShapeDtypeStruct((2, t_dim, gn), F32), jax.ShapeDtypeStruct((2, g_cnt, t_dim, hg), F32),
                   jax.ShapeDtypeStruct((2, g_cnt, 1, hg), F32)],
        scratch_shapes=[pltpu.VMEM((ns, GROUP_W), F32), pltpu.VMEM((q, GROUP_W), F32), pltpu.VMEM((hg, q), F32)],
        compiler_params=_cp("parallel", "parallel", "arbitrary"),
    )(xbc, xbc, xbc, dt4, al4, hin_all, dy, dvec)


def _gate_norm(y2, xbc, proj, dvec, ng):
    t_dim, di = y2.shape[1:]
    tr = _tile(t_dim, 128, 8)

    def body(y2_ref, x_ref, z_ref, dv_ref, ng_ref, o_ref):
        y = y2_ref[0] + y2_ref[1] + x_ref[...] * dv_ref[...]
        z = z_ref[...]
        v = y * z * _sigmoid(z)
        r = lax.rsqrt(jnp.mean(v * v, axis=-1, keepdims=True) + EPS)
        o_ref[...] = (v * r * ng_ref[...]).astype(BF16)

    row = pl.BlockSpec((tr, di), lambda i: (i, 0))
    vec = pl.BlockSpec((1, di), lambda i: (0, 0))
    return pl.pallas_call(
        body, name="gate_norm", grid=(t_dim // tr,),
        in_specs=[pl.BlockSpec((2, tr, di), lambda i: (0, i, 0)), row, row, vec, vec],
        out_specs=row, out_shape=jax.ShapeDtypeStruct((t_dim, di), BF16),
        compiler_params=_cp("parallel"),
    )(y2, xbc, proj, dvec, ng)


def _gate_norm_bwd(y2, xbc, proj, dvec, ng, dyn):
    t_dim, di = y2.shape[1:]
    tr = _tile(t_dim, 128, 8)

    def body(y2_ref, x_ref, z_ref, dv_ref, ng_ref, dyn_ref, dy_ref, dz_ref, dng_ref, dd_ref):
        i = pl.program_id(0)
        xv = x_ref[...]
        y = y2_ref[0] + y2_ref[1] + xv * dv_ref[...]
        z = z_ref[...]
        s = _sigmoid(z)
        v = y * z * s
        r = lax.rsqrt(jnp.mean(v * v, axis=-1, keepdims=True) + EPS)
        n = v * r
        dynv = dyn_ref[...]
        dn = dynv * ng_ref[...]
        dv = r * (dn - n * jnp.mean(dn * n, axis=-1, keepdims=True))
        dy = dv * z * s

        @pl.when(i == 0)
        def _():
            dng_ref[...] = jnp.zeros_like(dng_ref)
            dd_ref[...] = jnp.zeros_like(dd_ref)

        dng_ref[...] += jnp.sum(dynv * n, axis=0, keepdims=True)
        dd_ref[...] += jnp.sum(dy * xv, axis=0, keepdims=True)
        dy_ref[...] = dy
        dz_ref[...] = (dv * y * (s * (1.0 + z * (1.0 - s)))).astype(BF16)

    row = pl.BlockSpec((tr, di), lambda i: (i, 0))
    vec = pl.BlockSpec((1, di), lambda i: (0, 0))
    return pl.pallas_call(
        body, name="gate_norm_bwd", grid=(t_dim // tr,),
        in_specs=[pl.BlockSpec((2, tr, di), lambda i: (0, i, 0)), row, row, vec, vec, row],
        out_specs=[row, row, vec, vec],
        out_shape=[jax.ShapeDtypeStruct((t_dim, di), F32), jax.ShapeDtypeStruct((t_dim, di), BF16),
                   jax.ShapeDtypeStruct((1, di), F32), jax.ShapeDtypeStruct((1, di), F32)],
        compiler_params=_cp("arbitrary"),
    )(y2, xbc, proj, dvec, ng, dyn)


def _dt_to_groups(dt):
    t_dim, h2 = dt.shape
    g_cnt = h2 // 2 // HEADS_PER_GROUP
    return dt.reshape(t_dim, 2, g_cnt, HEADS_PER_GROUP).transpose(1, 2, 0, 3)


def _dt_from_groups(dt4):
    _, g_cnt, t_dim, hg = dt4.shape
    return dt4.transpose(2, 0, 1, 3).reshape(t_dim, 2 * g_cnt * hg)


def _ssd_mixer_fwd(x, norm_g, w_in, conv_w, conv_b, dt_bias, a_log, d_skip, ssd_norm, w_out, j):
    heads = d_skip.shape[0]
    di = heads * SSD_HEAD_DIM
    g_cnt = heads // HEADS_PER_GROUP
    cd = conv_w.shape[-1]
    hn = _rmsnorm(x, norm_g)
    proj = _mm(hn, w_in, name="ssd_proj", b_sel=(j,), tn=1152)
    xbc = _conv_silu(proj, conv_w, conv_b, di)
    dt_raw = proj[:, di + cd:]
    dt = _softplus_fwd(dt_raw, dt_bias)
    dt4 = _dt_to_groups(dt)
    al4 = a_log.reshape(2, g_cnt, 1, HEADS_PER_GROUP)
    y2, hin = _ssd_fwd(xbc, dt4, al4)
    dvec = jnp.repeat(d_skip, SSD_HEAD_DIM).reshape(1, di)
    yn = _gate_norm(y2, xbc, proj, dvec, ssd_norm)
    x_new = _mm(yn, w_out, name="ssd_out", b_sel=(j,), add=x)
    return x_new, (x, hn, proj, xbc, dt_raw, dt4, al4, y2, hin, dvec, yn)


def _ssd_mixer_bwd(dx, saved, norm_g, w_in, conv_w, conv_b, dt_bias, a_log, ssd_norm, w_out, j):
    x, hn, proj, xbc, dt_raw, dt4, al4, y2, hin, dvec, yn = saved
    di = dvec.shape[1]
    heads = di // SSD_HEAD_DIM
    d_wout = _mm(yn, dx, name="ssd_dwout", ta=True, out_dtype=BF16)
    dyn = _mm(dx, w_out, name="ssd_dyn", tb=True, b_sel=(j,))
    dy, dz, d_ng, dd_col = _gate_norm_bwd(y2, xbc, proj, dvec, ssd_norm, dyn)
    dx2, db2, dc2, ddt4, da4 = _ssd_bwd(xbc, dt4, al4, hin, dy, dvec)
    dact2 = jnp.concatenate([dx2, db2, dc2], axis=-1)
    dxbc, dwb = _conv_silu_bwd(proj, conv_w, conv_b, dact2, di)
    ddt_raw, d_bias, d_alog = _softplus_bwd(dt_raw, dt_bias, _dt_from_groups(ddt4), da4.reshape(1, 2 * heads), a_log)
    dproj = jnp.concatenate([dz, dxbc, ddt_raw], axis=-1)
    d_win = _mm(hn, dproj, name="ssd_dwin", ta=True, out_dtype=BF16, tn=1152)
    dhn = _mm(dproj, w_in, name="ssd_dhn", tb=True, b_sel=(j,), tk=1152)
    dx_new, d_norm = _rmsnorm_bwd(x, norm_g, dhn, dx)
    small = dict(mix_norm=d_norm, conv_w=dwb[:CONV_W], conv_b=dwb[CONV_W:CONV_W + 1], dt_bias=d_bias, a_log=d_alog,
                 ssd_d=dd_col.reshape(heads, SSD_HEAD_DIM).sum(axis=1), ssd_norm=d_ng)
    return dx_new, small, d_win, d_wout


def _pool_count(t_idx, w, t_dim):
    hi = jnp.minimum(t_idx + w // 2, t_dim)
    lo = jnp.maximum(t_idx - w // 2, 0)
    return (hi - lo).astype(F32)


def _pool_mix(u, transpose):
    t_dim, d = u.shape
    gd = d // len(POOL_WINDOWS)
    cb = _tile(gd, 256)
    per = gd // cb

    def body(u_ref, o_ref):
        gi = pl.program_id(0)
        t_idx = lax.broadcasted_iota(jnp.int32, (t_dim, cb), 0)
        uv = u_ref[...]
        for widx, w in enumerate(POOL_WINDOWS):
            @pl.when(gi == widx)
            def _(w=w):
                cnt = _pool_count(t_idx, w, t_dim)
                src = uv / cnt if transpose else uv
                acc = jnp.zeros_like(uv)
                for k in range(-(w // 2), w // 2):
                    acc = acc + _shifted(src, -k if transpose else k, t_idx)
                res = acc - uv if transpose else acc / cnt - uv
                o_ref[...] = res.astype(BF16)

    spec = pl.BlockSpec((t_dim, cb), lambda gi, j: (0, gi * per + j))
    return pl.pallas_call(
        body, name="pool_mix_t" if transpose else "pool_mix", grid=(len(POOL_WINDOWS), per),
        in_specs=[spec], out_specs=spec, out_shape=jax.ShapeDtypeStruct((t_dim, d), BF16),
        compiler_params=_cp("parallel", "parallel"),
    )(u)


def _pool_group(mix, wgrp, scale, j):
    t_dim, d = mix.shape
    gd = wgrp.shape[-1]
    tm = _tile(t_dim, 512)

    def body(m_ref, w_ref, s_ref, v_ref, vs_ref):
        v = _dot(m_ref[...], w_ref[...])
        v_ref[...] = v
        vs_ref[...] = (v * s_ref[...]).astype(BF16)

    tile = pl.BlockSpec((tm, gd), lambda gi, m: (m, gi))
    return pl.pallas_call(
        body, name="pool_group", grid=(d // gd, t_dim // tm),
        in_specs=[tile, pl.BlockSpec((None, None, gd, gd), lambda gi, m: (j, gi, 0, 0)),
                  pl.BlockSpec((1, gd), lambda gi, m: (0, gi))],
        out_specs=[tile, tile],
        out_shape=[jax.ShapeDtypeStruct((t_dim, d), F32), jax.ShapeDtypeStruct((t_dim, d), BF16)],
        compiler_params=_cp("parallel", "parallel"),
    )(mix, wgrp, scale)


def _pool_group_bwd(dvs, v, mix, wgrp, scale, j):
    t_dim, d = mix.shape
    gd = wgrp.shape[-1]
    n_g = d // gd
    tm = _tile(t_dim, 512)
    nm = t_dim // tm

    def body(dvs_ref, v_ref, m_ref, w_ref, s_ref, dmix_ref, ds_ref, dw_ref, acc_ref):
        m = pl.program_id(1)
        dvsv = dvs_ref[...]

        @pl.when(m == 0)
        def _():
            ds_ref[...] = jnp.zeros_like(ds_ref)
            acc_ref[...] = jnp.zeros_like(acc_ref)

        ds_ref[...] += jnp.sum(dvsv * v_ref[...], axis=0, keepdims=True)
        dv16 = (dvsv * s_ref[...]).astype(BF16)
        dmix_ref[...] = _dot_nt(dv16, w_ref[...])
        acc_ref[...] += _dot(m_ref[...].T, dv16)

        @pl.when(m == nm - 1)
        def _():
            dw_ref[...] = acc_ref[...].astype(BF16)

    tile = pl.BlockSpec((tm, gd), lambda gi, m: (m, gi))
    vec = pl.BlockSpec((1, gd), lambda gi, m: (0, gi))
    return pl.pallas_call(
        body, name="pool_group_bwd", grid=(n_g, nm),
        in_specs=[tile, tile, tile, pl.BlockSpec((None, None, gd, gd), lambda gi, m: (j, gi, 0, 0)), vec],
        out_specs=[tile, vec, pl.BlockSpec((None, gd, gd), lambda gi, m: (gi, 0, 0))],
        out_shape=[jax.ShapeDtypeStruct((t_dim, d), F32), jax.ShapeDtypeStruct((1, d), F32),
                   jax.ShapeDtypeStruct((n_g, gd, gd), BF16)],
        scratch_shapes=[pltpu.VMEM((gd, gd), F32)],
        compiler_params=_cp("parallel", "arbitrary"),
    )(dvs, v, mix, wgrp, scale)


def _pool_mixer_fwd(x, norm_g, w_in, wgrp, scale, w_out, j):
    hn = _rmsnorm(x, norm_g)
    u = _mm(hn, w_in, name="pool_u", b_sel=(j,))
    mix = _pool_mix(u, False)
    v, vs = _pool_group(mix, wgrp, scale, j)
    x_new = _mm(vs, w_out, name="pool_out", b_sel=(j,), add=x)
    return x_new, (x, hn, mix, v, vs)


def _pool_mixer_bwd(dx, saved, norm_g, w_in, wgrp, scale, w_out, j):
    x, hn, mix, v, vs = saved
    d_wout = _mm(vs, dx, name="pool_dw", ta=True, out_dtype=BF16)
    dvs = _mm(dx, w_out, name="pool_dvs", tb=True, b_sel=(j,))
    dmix, d_scale, d_wgrp = _pool_group_bwd(dvs, v, mix, wgrp, scale, j)
    du = _pool_mix(dmix, True)
    d_win = _mm(hn, du, name="pool_dw", ta=True, out_dtype=BF16)
    dhn = _mm(du, w_in, name="pool_dhn", tb=True, b_sel=(j,))
    dx_new, d_norm = _rmsnorm_bwd(x, norm_g, dhn, dx)
    return dx_new, d_norm, d_scale, d_win, d_wgrp, d_wout


def _local_step(x, target, depth, fetch, emit, on_loss, small, tok):
    saved, wts = [], {}

    def weights(group, after):
        if group not in wts:
            wts[group] = fetch(group, after)
        return wts[group]

    for i in range(depth):
        j = i // 2
        x, s0 = _ffn_fwd(x, small["ffn_norm"][i, 0][None] + tok, *weights(("ffn", i, 0), x), 0)
        mg = small["mix_norm"][i][None]
        if i % 2 == 0:
            heads = small["ssd_d"].shape[1]
            w_in, w_out = weights(("ssd", j), x)
            x, s1 = _ssd_mixer_fwd(x, mg, w_in, small["ssd_conv_w"][j], small["ssd_conv_b"][j][None],
                                   small["ssd_dt_bias"][j].reshape(1, 2 * heads), small["ssd_a_log"][j].reshape(1, 2 * heads),
                                   small["ssd_d"][j], small["ssd_norm"][j][None], w_out, 0)
        else:
            p_in, p_grp, p_out = weights(("pool", j), x)
            x, s1 = _pool_mixer_fwd(x, mg, p_in, p_grp, small["pool_scale"][j][None], p_out, 0)
        x, s2 = _ffn_fwd(x, small["ffn_norm"][i, 1][None], *weights(("ffn", i, 1), x), 0)
        saved.append((s0, s1, s2))
    loss, dx, d_final = _loss_head(x, small["final_norm"][None], target)

    gs = {k: {} for k in ("ffn_norm", "mix_norm", "ssd_conv_w", "ssd_conv_b", "ssd_dt_bias", "ssd_a_log", "ssd_d",
                          "ssd_norm", "pool_scale")}
    tok = on_loss(loss)
    for i in reversed(range(depth)):
        j = i // 2
        s0, s1, s2 = saved[i]
        for half, sv in ((1, s2), (0, None)):
            if half == 0:
                sv = s0
                mg = small["mix_norm"][i][None] + tok
                if i % 2 == 0:
                    heads = small["ssd_d"].shape[1]
                    w_in, w_out = wts[("ssd", j)]
                    dx, sm, d_win, d_wout = _ssd_mixer_bwd(
                        dx, s1, mg, w_in, small["ssd_conv_w"][j], small["ssd_conv_b"][j][None],
                        small["ssd_dt_bias"][j].reshape(1, 2 * heads), small["ssd_a_log"][j].reshape(1, 2 * heads),
                        small["ssd_norm"][j][None], w_out, 0)
                    gs["mix_norm"][i] = sm["mix_norm"][0]
                    gs["ssd_conv_w"][j] = sm["conv_w"]
                    gs["ssd_conv_b"][j] = sm["conv_b"][0]
                    gs["ssd_dt_bias"][j] = sm["dt_bias"].reshape(2, heads)
                    gs["ssd_a_log"][j] = sm["a_log"].reshape(2, heads)
                    gs["ssd_d"][j] = sm["ssd_d"]
                    gs["ssd_norm"][j] = sm["ssd_norm"][0]
                    tok = tok + emit(("ssd", j), [d_win, d_wout])
                else:
                    p_in, p_grp, p_out = wts[("pool", j)]
                    dx, d_norm, d_scale, d_win, d_wgrp, d_wout = _pool_mixer_bwd(
                        dx, s1, mg, p_in, p_grp, small["pool_scale"][j][None], p_out, 0)
                    gs["mix_norm"][i] = d_norm[0]
                    gs["pool_scale"][j] = d_scale[0]
                    tok = tok + emit(("pool", j), [d_win, d_wgrp, d_wout])
            dx, d_norm, d_wg, d_wu, d_wd = _ffn_bwd(dx, sv, small["ffn_norm"][i, half][None] + tok,
                                                   *wts[("ffn", i, half)], 0)
            gs["ffn_norm"][(i, half)] = d_norm[0]
            tok = tok + emit(("ffn", i, half), [d_wg, d_wu, d_wd])
    n_s, n_p = (depth + 1) // 2, depth // 2
    gsmall = dict(
        ffn_norm=jnp.stack([jnp.stack([gs["ffn_norm"][(i, h)] for h in range(2)]) for i in range(depth)]),
        mix_norm=jnp.stack([gs["mix_norm"][i] for i in range(depth)]),
        ssd_conv_w=jnp.stack([gs["ssd_conv_w"][j] for j in range(n_s)]),
        ssd_conv_b=jnp.stack([gs["ssd_conv_b"][j] for j in range(n_s)]),
        ssd_dt_bias=jnp.stack([gs["ssd_dt_bias"][j] for j in range(n_s)]),
        ssd_a_log=jnp.stack([gs["ssd_a_log"][j] for j in range(n_s)]),
        ssd_d=jnp.stack([gs["ssd_d"][j] for j in range(n_s)]),
        ssd_norm=jnp.stack([gs["ssd_norm"][j] for j in range(n_s)]),
        pool_scale=jnp.stack([gs["pool_scale"][j] for j in range(n_p)]),
        final_norm=d_final[0],
    )
    return loss, dx, gsmall


ANY = pl.BlockSpec(memory_space=pl.ANY)


def _mesh_pos():
    return lax.axis_index("x"), lax.axis_index("y"), lax.axis_index("c")


def _other_chips(x, y):
    return [(1 - x, y), (x, 1 - y), (1 - x, 1 - y)]


def _win(ref, windows, lead=()):
    rest = len(ref.shape) - len(lead)
    idx = tuple(lead) + tuple(pl.ds(*windows[ax]) if ax in windows else slice(None) for ax in range(rest))
    return ref.at[idx]


def _remote(src, dst, send_sems, recv_sems, k, peer):
    return pltpu.make_async_remote_copy(src_ref=src, dst_ref=dst, send_sem=send_sems.at[k], recv_sem=recv_sems.at[k],
                                        device_id=peer, device_id_type=MESH)


def _cast_place(w3, chip1, mode, l0, nl, out_dtype=BF16):
    _, r, c = w3.shape
    tr = _tile(r, 256, 16)
    nr = r // tr
    if mode == "cols":
        out_shape, blk = (nl, r, N_CHIPS * c), (None, tr, c)
        omap = lambda l, i, s: (l, i, s[0])
    elif mode == "rows":
        out_shape, blk = (nl, N_CHIPS * r, c), (None, tr, c)
        omap = lambda l, i, s: (l, s[0] * nr + i, 0)
    else:
        out_shape, blk = (nl, N_CHIPS, r, c), (None, None, tr, c)
        omap = lambda l, i, s: (l, s[0], i, 0)

    def body(s_ref, w_ref, o_ref):
        o_ref[...] = w_ref[...].astype(out_dtype)

    return pl.pallas_call(
        body, name="cast_place_" + mode, out_shape=jax.ShapeDtypeStruct(out_shape, out_dtype),
        grid_spec=pltpu.PrefetchScalarGridSpec(
            num_scalar_prefetch=1, grid=(nl, nr),
            in_specs=[pl.BlockSpec((None, tr, c), lambda l, i, s: (l0 + l, i, 0))],
            out_specs=pl.BlockSpec(blk, omap)),
        compiler_params=_cp("parallel", "parallel"),
    )(chip1, w3)


def _gather_begin(name, fulls, axes):
    n = len(fulls)
    shapes = [f.shape for f in fulls]

    def full_win(refs, t, sidx, hidx):
        sa, ha = axes[t]
        ssz, hsz = shapes[t][sa] // N_CHIPS, shapes[t][ha] // 2
        return _win(refs[t], {sa: (sidx * ssz, ssz), ha: (hidx * hsz, hsz)})

    def chips_plan(refs, x, y, c):
        mine = 2 * x + y
        return [(full_win(refs, t, mine, c), full_win(refs, t, mine, c), (px, py, c))
                for t in range(n) for px, py in _other_chips(x, y)]

    def sibling_plan(in_refs, out_refs, x, y, c):
        wins = [full_win(out_refs, t, 2 * px + py, c) for t in range(n) for px, py in _other_chips(x, y)]
        return [(w, w, (x, y, 1 - c)) for w in wins]

    sems, thru, token = _split_start(name + "_start", fulls, chips_plan, 3 * n)
    return (name, sems, thru, chips_plan, sibling_plan, 3 * n), token


def _gather_finish(pending, after):
    name, sems, thru, chips_plan, sibling_plan, n_copies = pending
    landed = _split_wait(name + "_wait", sems, thru, chips_plan, n_copies, after)
    return _exchange(name + "_sibling", landed, [jax.ShapeDtypeStruct(f.shape, f.dtype) for f in landed],
                     sibling_plan, n_copies, inplace=True)


def _exchange(name, inputs, out_shapes, plan, n_copies, inplace=False):
    n_in, n_out = len(inputs), len(out_shapes)

    def body(*refs):
        in_refs, out_refs = refs[:n_in], refs[n_in:n_in + n_out]
        send_sems, recv_sems = refs[n_in + n_out:]
        x, y, c = _mesh_pos()
        copies = plan(in_refs, out_refs, x, y, c)
        assert len(copies) == n_copies
        started = []
        for k, (src, dst, peer) in enumerate(copies):
            cp = _remote(src, dst, send_sems, recv_sems, k, peer)
            cp.start()
            started.append(cp)
        for cp in started:
            cp.wait()

    return pl.pallas_call(
        body, name=name, in_specs=[ANY] * n_in, out_specs=[ANY] * n_out, out_shape=out_shapes,
        input_output_aliases={t: t for t in range(n_in)} if inplace else {},
        scratch_shapes=[pltpu.SemaphoreType.DMA((n_copies,)), pltpu.SemaphoreType.DMA((n_copies,))],
    )(*inputs)


HBM = pl.BlockSpec(memory_space=pltpu.HBM)
SEM = pl.BlockSpec(memory_space=pltpu.SEMAPHORE)
DATAFLOW = pltpu.SideEffectType.DATAFLOW_SIDE_EFFECTING


def _split_start(name, bufs, plan, n_copies):
    n = len(bufs)

    def body(*refs):
        ins = refs[:n]
        send_sems, recv_sems = refs[n], refs[n + 1]
        token = refs[2 * n + 2]
        x, y, c = _mesh_pos()
        copies = plan(ins, x, y, c)
        assert len(copies) == n_copies
        for k, (src, dst, peer) in enumerate(copies):
            _remote(src, dst, send_sems, recv_sems, k, peer).start()
        token[...] = jnp.zeros_like(token)

    outs = pl.pallas_call(
        body, name=name,
        out_shape=(pltpu.SemaphoreType.DMA((n_copies,)), pltpu.SemaphoreType.DMA((n_copies,)),
                   *[pltpu.HBM(b.shape, b.dtype) for b in bufs], jax.ShapeDtypeStruct((8, LANE), F32)),
        in_specs=[HBM] * n, out_specs=(SEM, SEM, *[HBM] * n, pl.BlockSpec(memory_space=pltpu.VMEM)),
        input_output_aliases={t: 2 + t for t in range(n)},
        compiler_params=pltpu.CompilerParams(has_side_effects=DATAFLOW),
    )(*[pltpu.with_memory_space_constraint(b, pltpu.HBM) for b in bufs])
    return (outs[0], outs[1]), list(outs[2:2 + n]), outs[2 + n][0, 0]


def _split_wait(name, sems, bufs, plan, n_copies, after):
    n = len(bufs)

    def body(*refs):
        ins = refs[:n]
        send_sems, recv_sems = refs[n], refs[n + 1]
        x, y, c = _mesh_pos()
        copies = plan(ins, x, y, c)
        assert len(copies) == n_copies
        for k, (src, dst, peer) in enumerate(copies):
            cp = _remote(src, dst, send_sems, recv_sems, k, peer)
            cp.wait_send()
            cp.wait_recv()

    outs = pl.pallas_call(
        body, name=name, out_shape=tuple(pltpu.HBM(b.shape, b.dtype) for b in bufs),
        in_specs=[HBM] * n + [SEM, SEM, ANY], out_specs=tuple([HBM] * n),
        input_output_aliases={t: t for t in range(n)},
        compiler_params=pltpu.CompilerParams(has_side_effects=DATAFLOW),
    )(*bufs, sems[0], sems[1], after)
    return list(outs)


def _halved(shape, ha):
    out = list(shape)
    out[ha] //= 2
    return tuple(out)


def _sharded(shape, sa):
    out = list(shape)
    out[sa] //= N_CHIPS
    return tuple(out)


def _rs_cores_begin(name, grads, axes):
    n = len(grads)
    shapes = [g.shape for g in grads]
    landing = [lax.empty(_halved(g.shape, ha), g.dtype) for g, (_, ha) in zip(grads, axes)]

    def plan(refs, x, y, c):
        copies = []
        for t in range(n):
            ha = axes[t][1]
            hsz = shapes[t][ha] // 2
            copies.append((_win(refs[t], {ha: ((1 - c) * hsz, hsz)}), refs[n + t], (x, y, 1 - c)))
        return copies

    sems, thru, token = _split_start(name + "_start", list(grads) + landing, plan, n)
    return (name, sems, thru, plan, n), token


def _rs_cores_finish(pending, after):
    name, sems, thru, plan, n = pending
    done = _split_wait(name + "_wait", sems, thru, plan, n, after)
    return done[:n], done[n:]


def _small_begin(vec):
    landing = lax.empty((8,) + vec.shape, vec.dtype)

    def plan(refs, x, y, c):
        me = 4 * x + 2 * y + c
        copies = []
        for k in range(1, 8):
            kx, ky, kc = k // 4, (k // 2) % 2, k % 2
            copies.append((refs[0], refs[1].at[me], (x ^ kx, y ^ ky, c ^ kc)))
        return copies

    sems, thru, token = _split_start("small_start", [vec, landing], plan, 7)
    return (sems, thru, plan), token


def _small_finish(pending, after, me1):
    sems, thru, plan = pending
    vec, landing = _split_wait("small_wait", sems, thru, plan, 7, after)

    def body(s_ref, v_ref, l_ref, o_ref):
        tot = jnp.where(s_ref[0] == 0, v_ref[...], l_ref[0])
        for i in range(1, 8):
            tot = tot + jnp.where(s_ref[0] == i, v_ref[...], l_ref[i])
        o_ref[...] = tot

    rows = vec.shape[0]
    return pl.pallas_call(
        body, name="small_sum", out_shape=jax.ShapeDtypeStruct(vec.shape, F32),
        grid_spec=pltpu.PrefetchScalarGridSpec(
            num_scalar_prefetch=1, grid=(1,),
            in_specs=[pl.BlockSpec((rows, LANE), lambda i, s: (0, 0)), pl.BlockSpec((8, rows, LANE), lambda i, s: (0, 0, 0))],
            out_specs=pl.BlockSpec((rows, LANE), lambda i, s: (0, 0))),
    )(me1, vec, landing)


def _rs_chips_begin(name, halves, axes):
    n = len(halves)
    shapes = [h.shape for h in halves]
    landing = [lax.empty((N_CHIPS - 1,) + _sharded(h.shape, sa), h.dtype) for h, (sa, _) in zip(halves, axes)]

    def plan(refs, x, y, c):
        copies = []
        for t in range(n):
            sa = axes[t][0]
            ssz = shapes[t][sa] // N_CHIPS
            for j, (px, py) in enumerate(_other_chips(x, y)):
                copies.append((_win(refs[t], {sa: ((2 * px + py) * ssz, ssz)}), refs[n + t].at[j], (px, py, c)))
        return copies

    sems, thru, token = _split_start(name + "_start", list(halves) + landing, plan, 3 * n)
    return (name, sems, thru, plan, 3 * n), token


def _rs_chips_finish(pending, after):
    name, sems, thru, plan, n_copies = pending
    done = _split_wait(name + "_wait", sems, thru, plan, n_copies, after)
    n = len(done) // 2
    return done[:n], done[n:]


def _rs_finish(name, groups):
    flat = [(gi, t) for gi, grp in enumerate(groups) for t in range(len(grp))]
    shapes = [jax.ShapeDtypeStruct((len(grp),) + grp[0].shape, grp[0].dtype) for grp in groups]

    def plan(in_refs, out_refs, x, y, c):
        return [(in_refs[k], out_refs[gi].at[t], (x, y, 1 - c)) for k, (gi, t) in enumerate(flat)]

    return _exchange(name, [a for grp in groups for a in grp], shapes, plan, len(flat))


def _allgather8(v, reduce):
    rows = v.shape[0]

    def body(v_ref, o_ref, *scratch):
        if reduce:
            buf, send_sems, recv_sems, loc_sem = scratch
        else:
            buf = o_ref
            send_sems, recv_sems, loc_sem = scratch
        x, y, c = _mesh_pos()
        me = 4 * x + 2 * y + c
        lc = pltpu.make_async_copy(v_ref, buf.at[me], loc_sem)
        lc.start()
        sends = []
        for k in range(1, 8):
            kx, ky, kc = k // 4, (k // 2) % 2, k % 2
            peer = (x ^ kx, y ^ ky, c ^ kc)
            cp = _remote(v_ref, buf.at[me], send_sems, recv_sems, k - 1, peer)
            cp.start()
            sends.append(cp)
        for k in range(1, 8):
            kx, ky, kc = k // 4, (k // 2) % 2, k % 2
            src = 4 * (x ^ kx) + 2 * (y ^ ky) + (c ^ kc)
            _remote(v_ref, buf.at[src], send_sems, recv_sems, k - 1, (x, y, c)).wait_recv()
        for cp in sends:
            cp.wait_send()
        lc.wait()
        if reduce:
            tot = buf[0]
            for i in range(1, 8):
                tot = tot + buf[i]
            o_ref[...] = tot

    vm = pl.BlockSpec(memory_space=pltpu.VMEM)
    sems = [pltpu.SemaphoreType.DMA((7,)), pltpu.SemaphoreType.DMA((7,)), pltpu.SemaphoreType.DMA]
    if reduce:
        out_shape = jax.ShapeDtypeStruct((rows, LANE), F32)
        scratch = [pltpu.VMEM((8, rows, LANE), F32)] + sems
    else:
        out_shape = jax.ShapeDtypeStruct((8, rows, LANE), F32)
        scratch = sems
    return pl.pallas_call(
        body, name="allreduce_small" if reduce else "allgather_small", in_specs=[vm], out_specs=vm,
        out_shape=out_shape, scratch_shapes=scratch,
    )(v)


def _rows2d(a):
    return a.reshape(-1, a.shape[-1])


def _view3(a):
    return a.reshape((-1,) + a.shape[-2:])


def _add_pair(g, recv, ha, core1):
    g3, r3 = _view3(g), _view3(recv)
    rows_half = ha + 3 - g.ndim == 1
    n_l, r, c = r3.shape
    tr = _tile(r, 256, 16)
    nr = r // tr
    gmap = (lambda l, i, s: (l, s[0] * nr + i, 0)) if rows_half else (lambda l, i, s: (l, i, s[0]))

    def body(s_ref, g_ref, r_ref, o_ref):
        o_ref[...] = (g_ref[...].astype(F32) + r_ref[...].astype(F32)).astype(BF16)

    spec = pl.BlockSpec((None, tr, c), lambda l, i, s: (l, i, 0))
    out = pl.pallas_call(
        body, name="add_pair", out_shape=jax.ShapeDtypeStruct(r3.shape, BF16),
        grid_spec=pltpu.PrefetchScalarGridSpec(num_scalar_prefetch=1, grid=(n_l, nr),
                                               in_specs=[pl.BlockSpec((None, tr, c), gmap), spec], out_specs=spec),
        compiler_params=_cp("parallel", "parallel"))(core1, g3, r3)
    return out.reshape(recv.shape)


def _add_four(cs, recv, sa, chip1):
    c3 = _view3(cs)
    s3 = sa + 3 - cs.ndim
    lo, ro, co = (dim // N_CHIPS if ax == s3 else dim for ax, dim in enumerate(c3.shape))
    r4 = recv.reshape(N_CHIPS - 1, lo, ro, co)
    tr = _tile(ro, 256, 16)
    nr = ro // tr
    if s3 == 0:
        cmap = lambda l, i, s: (s[0] * lo + l, i, 0)
    elif s3 == 1:
        cmap = lambda l, i, s: (l, s[0] * nr + i, 0)
    else:
        cmap = lambda l, i, s: (l, i, s[0])

    def body(s_ref, c_ref, r_ref, out_ref):
        out_ref[...] = ((c_ref[...].astype(F32) + r_ref[0].astype(F32)) + r_ref[1].astype(F32)) + r_ref[2].astype(F32)

    out = pl.pallas_call(
        body, name="add_four", out_shape=jax.ShapeDtypeStruct((lo, ro, co), F32),
        grid_spec=pltpu.PrefetchScalarGridSpec(
            num_scalar_prefetch=1, grid=(lo, nr),
            in_specs=[pl.BlockSpec((None, tr, co), cmap),
                      pl.BlockSpec((N_CHIPS - 1, None, tr, co), lambda l, i, s: (0, l, i, 0))],
            out_specs=pl.BlockSpec((None, tr, co), lambda l, i, s: (l, i, 0))),
        compiler_params=_cp("parallel", "parallel"))(chip1, c3, r4)
    return out.reshape(recv.shape[1:])


def _adamw_halves(w, own, recv, m, v, rows_half, core1, l0=0, prev=None):
    w3, m3, v3, o3, r3 = (_view3(a) for a in (w, m, v, own, recv))
    _, _, c = w3.shape
    n_l, rh, ch = o3.shape
    tr = _tile(rh, 128, 8)
    nr = rh // tr
    c1 = 1.0 - ADAM_B1 ** ADAM_STEP
    c2 = 1.0 - ADAM_B2 ** ADAM_STEP

    n_prev = 0 if prev is None else 4

    def body(s_ref, w_ref, o_ref, r_ref, m_ref, v_ref, *rest):
        g_ref, d_ref, nm_ref, nv_ref = rest[n_prev:]
        gv = jnp.where(pl.program_id(1) == s_ref[0], o_ref[...], r_ref[...])
        nm = ADAM_B1 * m_ref[...] + (1.0 - ADAM_B1) * gv
        nv = ADAM_B2 * v_ref[...] + (1.0 - ADAM_B2) * (gv * gv)
        g_ref[...] = gv
        nm_ref[...] = nm
        nv_ref[...] = nv
        d_ref[...] = -ADAM_LR * ((nm / c1) / (jnp.sqrt(nv / c2) + ADAM_EPS) + ADAM_WD * w_ref[...])

    wmap = (lambda l, h, i, s: (l0 + l, h * nr + i, 0)) if rows_half else (lambda l, h, i, s: (l0 + l, i, h))
    wspec = pl.BlockSpec((None, tr, ch), wmap)
    gspec = pl.BlockSpec((None, tr, ch), lambda l, h, i, s: (l, i, 0))
    osh = jax.ShapeDtypeStruct(w3.shape, F32)
    before = [] if prev is None else [_view3(p) for p in prev]
    outs = pl.pallas_call(
        body, name="adamw_halves", out_shape=[osh] * 4,
        grid_spec=pltpu.PrefetchScalarGridSpec(
            num_scalar_prefetch=1, grid=(n_l, 2, nr),
            in_specs=[wspec, gspec, gspec, wspec, wspec] + [ANY] * n_prev, out_specs=[wspec] * 4),
        input_output_aliases={6 + k: k for k in range(n_prev)},
        compiler_params=_cp("parallel", "parallel", "parallel"))(core1, w3, o3, r3, m3, v3, *before)
    return tuple(o.reshape(w.shape) for o in outs)


def _adamw(w, g, m, v):
    shape = w.shape
    w2, g2, m2, v2 = (_rows2d(a) if a.ndim > 1 else a.reshape(1, -1) for a in (w, g, m, v))
    rows, cols = w2.shape
    tr = _tile(rows, 256, 8)
    c1 = 1.0 - ADAM_B1 ** ADAM_STEP
    c2 = 1.0 - ADAM_B2 ** ADAM_STEP

    def body(w_ref, g_ref, m_ref, v_ref, d_ref, nm_ref, nv_ref):
        gv = g_ref[...]
        nm = ADAM_B1 * m_ref[...] + (1.0 - ADAM_B1) * gv
        nv = ADAM_B2 * v_ref[...] + (1.0 - ADAM_B2) * (gv * gv)
        nm_ref[...] = nm
        nv_ref[...] = nv
        d_ref[...] = -ADAM_LR * ((nm / c1) / (jnp.sqrt(nv / c2) + ADAM_EPS) + ADAM_WD * w_ref[...])

    spec = pl.BlockSpec((tr, cols), lambda i: (i, 0))
    osh = jax.ShapeDtypeStruct((rows, cols), F32)
    outs = pl.pallas_call(body, name="adamw", grid=(rows // tr,), in_specs=[spec] * 4, out_specs=[spec] * 3,
                          out_shape=[osh] * 3, compiler_params=_cp("parallel"))(w2, g2, m2, v2)
    return tuple(o.reshape(shape) for o in outs)


def _pack(arrs):
    flat = jnp.concatenate([a.reshape(-1) for a in arrs])
    n = flat.shape[0]
    rows = -(-n // (8 * LANE)) * 8
    return jnp.pad(flat, (0, rows * LANE - n)).reshape(rows, LANE)


def _unpack(packed, shapes):
    flat = packed.reshape(-1)
    out, pos = [], 0
    for sh in shapes:
        size = 1
        for dsz in sh:
            size *= dsz
        out.append(flat[pos:pos + size].reshape(sh))
        pos += size
    return out


BIG = ("ffn_w_gate", "ffn_w_up", "ffn_w_down", "ssd_w_in", "ssd_w_out", "pool_w_in", "pool_w_group", "pool_w_out")
WEIGHTS = ("ffn_norm", "ffn_w_gate", "ffn_w_up", "ffn_w_down", "mix_norm", "ssd_w_in", "ssd_conv_w", "ssd_conv_b",
           "ssd_dt_bias", "ssd_a_log", "ssd_d", "ssd_norm", "ssd_w_out", "pool_w_in", "pool_w_group", "pool_scale",
           "pool_w_out", "final_norm")
SMALL = tuple(k for k in WEIGHTS if k not in BIG)
SMALL_SHARDED = {"ffn_norm": 2, "ssd_conv_w": 2, "pool_scale": 1}


def kernel(x, ffn_norm, ffn_w_gate, ffn_w_up, ffn_w_down, mix_norm, ssd_w_in, ssd_conv_w, ssd_conv_b, ssd_dt_bias, ssd_a_log, ssd_d, ssd_norm, ssd_w_out, pool_w_in, pool_w_group, pool_scale, pool_w_out, final_norm, loss_target, m_ffn_norm, m_ffn_w_gate, m_ffn_w_up, m_ffn_w_down, m_mix_norm, m_ssd_w_in, m_ssd_conv_w, m_ssd_conv_b, m_ssd_dt_bias, m_ssd_a_log, m_ssd_d, m_ssd_norm, m_ssd_w_out, m_pool_w_in, m_pool_w_group, m_pool_scale, m_pool_w_out, m_final_norm, v_ffn_norm, v_ffn_w_gate, v_ffn_w_up, v_ffn_w_down, v_mix_norm, v_ssd_w_in, v_ssd_conv_w, v_ssd_conv_b, v_ssd_dt_bias, v_ssd_a_log, v_ssd_d, v_ssd_norm, v_ssd_w_out, v_pool_w_in, v_pool_w_group, v_pool_scale, v_pool_w_out, v_final_norm):
    w = dict(ffn_norm=ffn_norm, ffn_w_gate=ffn_w_gate, ffn_w_up=ffn_w_up, ffn_w_down=ffn_w_down, mix_norm=mix_norm,
             ssd_w_in=ssd_w_in, ssd_conv_w=ssd_conv_w, ssd_conv_b=ssd_conv_b, ssd_dt_bias=ssd_dt_bias,
             ssd_a_log=ssd_a_log, ssd_d=ssd_d, ssd_norm=ssd_norm, ssd_w_out=ssd_w_out, pool_w_in=pool_w_in,
             pool_w_group=pool_w_group, pool_scale=pool_scale, pool_w_out=pool_w_out, final_norm=final_norm)
    mom = dict(ffn_norm=m_ffn_norm, ffn_w_gate=m_ffn_w_gate, ffn_w_up=m_ffn_w_up, ffn_w_down=m_ffn_w_down,
               mix_norm=m_mix_norm, ssd_w_in=m_ssd_w_in, ssd_conv_w=m_ssd_conv_w, ssd_conv_b=m_ssd_conv_b,
               ssd_dt_bias=m_ssd_dt_bias, ssd_a_log=m_ssd_a_log, ssd_d=m_ssd_d, ssd_norm=m_ssd_norm,
               ssd_w_out=m_ssd_w_out, pool_w_in=m_pool_w_in, pool_w_group=m_pool_w_group, pool_scale=m_pool_scale,
               pool_w_out=m_pool_w_out, final_norm=m_final_norm)
    vel = dict(ffn_norm=v_ffn_norm, ffn_w_gate=v_ffn_w_gate, ffn_w_up=v_ffn_w_up, ffn_w_down=v_ffn_w_down,
               mix_norm=v_mix_norm, ssd_w_in=v_ssd_w_in, ssd_conv_w=v_ssd_conv_w, ssd_conv_b=v_ssd_conv_b,
               ssd_dt_bias=v_ssd_dt_bias, ssd_a_log=v_ssd_a_log, ssd_d=v_ssd_d, ssd_norm=v_ssd_norm,
               ssd_w_out=v_ssd_w_out, pool_w_in=v_pool_w_in, pool_w_group=v_pool_w_group, pool_scale=v_pool_scale,
               pool_w_out=v_pool_w_out, final_norm=v_final_norm)
    depth = ffn_w_gate.shape[0]
    n_s, n_p = ssd_w_in.shape[0], pool_w_in.shape[0]
    chip = 2 * lax.axis_index("x") + lax.axis_index("y")

    chip1 = jnp.reshape(chip, (1,)).astype(jnp.int32)
    core1 = jnp.reshape(lax.axis_index("c"), (1,)).astype(jnp.int32)

    gate3, up3, down3 = _view3(ffn_w_gate), _view3(ffn_w_up), _view3(ffn_w_down)
    d_model = ssd_w_in.shape[1]
    pending = {}
    sharded_names = tuple(SMALL_SHARDED)
    tok = jnp.zeros((), F32)

    def begin(group, name, fulls, axes):
        nonlocal tok
        pending[group], t = _gather_begin(name, fulls, axes)
        tok = tok + t

    for i in range(depth):
        for h in range(2):
            fulls = [_cast_place(gate3, chip1, "cols", 2 * i + h, 1), _cast_place(up3, chip1, "cols", 2 * i + h, 1),
                     _cast_place(down3, chip1, "rows", 2 * i + h, 1)]
            if (i, h) == (0, 0):
                packed_small = _pack([w[k] for k in sharded_names])
                fulls.append(_cast_place(packed_small[None], chip1, "slot", 0, 1, F32))
                begin(("ffn", i, h), "gather_first", fulls, [(2, 1), (2, 1), (1, 2), (1, 2)])
            else:
                begin(("ffn", i, h), "gather_ffn", fulls, [(2, 1), (2, 1), (1, 2)])
            j = i // 2
            if h == 0 and i % 2 == 0:
                begin(("ssd", j), "gather_ssd", [_cast_place(ssd_w_in, chip1, "slot", j, 1),
                                                 _cast_place(ssd_w_out, chip1, "rows", j, 1)], [(1, 2), (1, 2)])
            if h == 0 and i % 2 == 1:
                n_g = pool_w_group.shape[1]
                grp_full = _cast_place(_view3(pool_w_group), chip1, "rows", j * n_g, n_g)
                begin(("pool", j), "gather_pool", [_cast_place(pool_w_in, chip1, "rows", j, 1), grp_full[None],
                                                   _cast_place(pool_w_out, chip1, "rows", j, 1)],
                      [(1, 2), (2, 3), (1, 2)])

    fetched = {}

    def fetch(group, after):
        if group not in fetched:
            got = _gather_finish(pending[group], after)
            if group[0] == "ssd":
                got = [got[0].transpose(0, 2, 1, 3).reshape(1, d_model, -1), got[1]]
            fetched[group] = got[:3]
            if group == ("ffn", 0, 0):
                fetched["small"] = got[3][0]
        return fetched[group]

    fetch(("ffn", 0, 0), x)
    small = {k: w[k] for k in SMALL if k not in SMALL_SHARDED}
    per_chip = [_unpack(fetched["small"][s], [w[k].shape for k in sharded_names]) for s in range(N_CHIPS)]
    for t, k in enumerate(sharded_names):
        small[k] = jnp.concatenate([per_chip[s][t] for s in range(N_CHIPS)], axis=SMALL_SHARDED[k])

    ffn_names = ("ffn_w_gate", "ffn_w_up", "ffn_w_down")
    group_axes = dict(ffn=[(1, 0), (1, 0), (0, 1)], ssd=[(0, 1), (0, 1)], pool=[(0, 1), (1, 2), (0, 1)])
    group_names = dict(ffn=ffn_names, ssd=("ssd_w_in", "ssd_w_out"), pool=("pool_w_in", "pool_w_group", "pool_w_out"))
    travelling = {}
    between_cores = []

    def advance(after):
        group, pend = between_cores.pop()
        axes = group_axes[group[0]]
        grads, recv_a = _rs_cores_finish(pend, after)
        chip_sums = [_add_pair(g, r, ha, core1) for g, r, (_, ha) in zip(grads, recv_a, axes)]
        travelling[group], t = _rs_chips_begin("rs_chips_" + group[0], chip_sums, axes)
        return t

    def emit(group, grads):
        kind = group[0]
        if kind == "ssd":
            grads = [grads[0].reshape(d_model, N_CHIPS, -1).transpose(1, 0, 2), grads[1]]
        t = advance(grads[0]) if between_cores else jnp.zeros((), F32)
        pend, t2 = _rs_cores_begin("rs_cores_" + kind, grads, group_axes[kind])
        between_cores.append((group, pend))
        return t + t2

    total = {}

    def on_loss(part):
        total["loss"] = lax.psum(part[0, 0], ("x", "y", "c"))
        return jnp.minimum(total["loss"], 0.0)

    _, dx, gsmall = _local_step(x[0], loss_target[0], depth, fetch, emit, on_loss, small, tok)
    loss = total["loss"]

    t_last = advance(dx)
    small_pending, t_small = _small_begin(_pack([gsmall[k] for k in SMALL]))
    started = jnp.reshape(t_last + t_small, (1, 1))

    rows_half = dict(ffn_w_gate=True, ffn_w_up=True, ffn_w_down=False, ssd_w_in=True, ssd_w_out=False,
                     pool_w_in=False, pool_w_group=False, pool_w_out=False)
    results = {}

    def finish(groups, tag, after):
        sums = {k: {} for k in BIG}
        for group in groups:
            kind = group[0]
            chip_sums, recv_b = _rs_chips_finish(travelling[group], after)
            for k, cs, r, (sa, _) in zip(group_names[kind], chip_sums, recv_b, group_axes[kind]):
                sums[k][group[1:]] = _add_four(cs, r, sa, chip1)
        names = [k for k in BIG if sums[k]]
        own = {k: [sums[k][idx] for idx in sorted(sums[k])] for k in names}
        from_sibling = dict(zip(names, _rs_finish("rs_finish_" + tag, [own[k] for k in names])))
        for k in names:
            first = min(sums[k])
            l0 = first[0] * 2 + first[1] if k in ffn_names else first[0]
            results[k] = _adamw_halves(w[k], jnp.stack(own[k]), from_sibling[k], mom[k], vel[k], rows_half[k],
                                       core1, l0, results.get(k))
        return results[names[-1]][3]

    late = [g for g in travelling if g[1] == 0 and g[0] != "pool"]
    early = [g for g in travelling if g not in late]
    done_early = finish(early, "early", started) if early else started
    done_late = finish(late, "late", done_early)
    grad, delta, new_m, new_v = ({k: results[k][t] for k in BIG} for t in range(4))

    small_shapes = [gsmall[k].shape for k in SMALL]
    me1 = 2 * chip1 + core1
    summed = _unpack(_small_finish(small_pending, done_late, me1), small_shapes)
    for k, g in zip(SMALL, summed):
        if k in SMALL_SHARDED:
            ax = SMALL_SHARDED[k]
            size = w[k].shape[ax]
            g = lax.dynamic_slice_in_dim(g, chip * size, size, axis=ax)
        grad[k] = g

    shapes = [w[k].shape for k in SMALL]
    packed = _adamw(*(_pack([src[k] for k in SMALL]) for src in (w, grad, mom, vel)))
    for dst, p in zip((delta, new_m, new_v), packed):
        for k, a in zip(SMALL, _unpack(p, shapes)):
            dst[k] = a

    return (loss, dx[None], *[grad[k] for k in WEIGHTS], *[delta[k] for k in WEIGHTS],
            *[new_m[k] for k in WEIGHTS], *[new_v[k] for k in WEIGHTS])
```

```python
import functools

import jax
import jax.numpy as jnp
from jax import lax
from jax.experimental import pallas as pl
from jax.experimental.pallas import tpu as pltpu

F32 = jnp.float32
BF16 = jnp.bfloat16
EPS = 1e-6
MESH = pl.DeviceIdType.MESH

SSD_CHUNK = 128
SSD_STATE = 128
SSD_HEAD_DIM = 64
HEADS_PER_GROUP = 8
GROUP_W = HEADS_PER_GROUP * SSD_HEAD_DIM
CONV_W = 5
POOL_WINDOWS = (2, 4, 8, 16)
N_CHIPS = 4

ADAM_LR = 0.001
ADAM_B1 = 0.9
ADAM_B2 = 0.999
ADAM_EPS = 1e-08
ADAM_WD = 0.01
ADAM_STEP = 10

VMEM_LIMIT = 56 * 1024 * 1024
LANE = 128


def _cp(*sem):
    return pltpu.CompilerParams(dimension_semantics=sem, vmem_limit_bytes=VMEM_LIMIT)


def _tile(dim, pref, unit=LANE):
    if dim <= pref:
        return dim
    t = (pref // unit) * unit
    while t >= unit:
        if dim % t == 0:
            return t
        t -= unit
    return dim


def _sigmoid(v):
    return 1.0 / (1.0 + jnp.exp(-v))


def _dot(a, b):
    return jnp.dot(a, b, preferred_element_type=F32)


def _dot_nt(a, b):
    return lax.dot_general(a, b, (((1,), (1,)), ((), ())), preferred_element_type=F32)


def _split3(v):
    h1 = v.astype(BF16)
    r1 = v - h1.astype(F32)
    h2 = r1.astype(BF16)
    h3 = (r1 - h2.astype(F32)).astype(BF16)
    return h1, h2, h3


def _dot_exact01(m01, v):
    mb = m01.astype(BF16)
    h1, h2, h3 = _split3(v)
    return _dot(mb, h1) + _dot(mb, h2) + _dot(mb, h3)


def _dot_nt_exact01(m01, v):
    mb = m01.astype(BF16)
    h1, h2, h3 = _split3(v)
    return _dot_nt(mb, h1) + _dot_nt(mb, h2) + _dot_nt(mb, h3)


def _mm(a, b, *, name, ta=False, tb=False, a_sel=(), b_sel=(), pair2=None, add=None, scale=1.0,
        out_dtype=F32, tm=1024, tn=1024, tk=2048):
    am, ak = a.shape[-2:][::-1] if ta else a.shape[-2:]
    bk, bn = b.shape[-2:][::-1] if tb else b.shape[-2:]
    assert ak == bk, (a.shape, b.shape, ta, tb)
    m_dim, n_dim, k_dim = am, bn, ak
    tm, tn, tk = _tile(m_dim, tm), _tile(n_dim, tn), _tile(k_dim, tk)
    nk = k_dim // tk
    grid = (n_dim // tn, m_dim // tm, nk)

    def a_spec(sel):
        lead = (None,) * len(sel)
        if ta:
            return pl.BlockSpec(lead + (tk, tm), lambda n, m, k: tuple(sel) + (k, m))
        return pl.BlockSpec(lead + (tm, tk), lambda n, m, k: tuple(sel) + (m, k))

    def b_spec(sel):
        lead = (None,) * len(sel)
        if tb:
            return pl.BlockSpec(lead + (tn, tk), lambda n, m, k: tuple(sel) + (n, k))
        return pl.BlockSpec(lead + (tk, tn), lambda n, m, k: tuple(sel) + (k, n))

    ins, specs = [a, b], [a_spec(a_sel), b_spec(b_sel)]
    if pair2 is not None:
        a2, b2, a2_sel, b2_sel = pair2
        ins += [a2, b2]
        specs += [a_spec(a2_sel), b_spec(b2_sel)]
    if add is not None:
        ins.append(add)
        specs.append(pl.BlockSpec((tm, tn), lambda n, m, k: (m, n)))
    dn = (((0 if ta else 1,), (1 if tb else 0,)), ((), ()))
    n_pairs = 2 if pair2 is not None else 1

    def body(*refs):
        pairs = [(refs[2 * i], refs[2 * i + 1]) for i in range(n_pairs)]
        pos = 2 * n_pairs
        add_ref = None
        if add is not None:
            add_ref = refs[pos]
            pos += 1
        o_ref = refs[pos]
        acc_ref = refs[pos + 1] if nk > 1 else None

        def prod():
            tot = None
            for ar, br in pairs:
                p = lax.dot_general(ar[...].astype(BF16), br[...].astype(BF16), dn, preferred_element_type=F32)
                tot = p if tot is None else tot + p
            return tot

        def finish(r):
            if scale != 1.0:
                r = r * scale
            if add_ref is not None:
                r = add_ref[...] + r
            o_ref[...] = r.astype(out_dtype)

        if nk == 1:
            finish(prod())
        else:
            k = pl.program_id(2)

            @pl.when(k == 0)
            def _():
                acc_ref[...] = jnp.zeros_like(acc_ref)

            acc_ref[...] += prod()

            @pl.when(k == nk - 1)
            def _():
                finish(acc_ref[...])

    return pl.pallas_call(
        body, name=name, grid=grid, in_specs=specs,
        out_specs=pl.BlockSpec((tm, tn), lambda n, m, k: (m, n)),
        out_shape=jax.ShapeDtypeStruct((m_dim, n_dim), out_dtype),
        scratch_shapes=[pltpu.VMEM((tm, tn), F32)] if nk > 1 else [],
        compiler_params=_cp("parallel", "parallel", "arbitrary"),
    )(*ins)


def _rmsnorm(x, g):
    t_dim, d = x.shape
    tr = _tile(t_dim, 256, 8)

    def body(x_ref, g_ref, o_ref):
        xv = x_ref[...]
        r = lax.rsqrt(jnp.mean(xv * xv, axis=-1, keepdims=True) + EPS)
        o_ref[...] = (xv * r * g_ref[...]).astype(BF16)

    return pl.pallas_call(
        body, name="rmsnorm", grid=(t_dim // tr,),
        in_specs=[pl.BlockSpec((tr, d), lambda i: (i, 0)), pl.BlockSpec((1, d), lambda i: (0, 0))],
        out_specs=pl.BlockSpec((tr, d), lambda i: (i, 0)),
        out_shape=jax.ShapeDtypeStruct((t_dim, d), BF16),
        compiler_params=_cp("parallel"),
    )(x, g)


def _rmsnorm_bwd(x, g, dh, dres):
    t_dim, d = x.shape
    tr = _tile(t_dim, 256, 8)

    def body(x_ref, g_ref, dh_ref, dres_ref, dx_ref, dg_ref):
        i = pl.program_id(0)
        xv = x_ref[...]
        r = lax.rsqrt(jnp.mean(xv * xv, axis=-1, keepdims=True) + EPS)
        n = xv * r
        dhv = dh_ref[...]
        dn = dhv * g_ref[...]

        @pl.when(i == 0)
        def _():
            dg_ref[...] = jnp.zeros_like(dg_ref)

        dg_ref[...] += jnp.sum(dhv * n, axis=0, keepdims=True)
        dx_ref[...] = dres_ref[...] + r * (dn - n * jnp.mean(dn * n, axis=-1, keepdims=True))

    row = pl.BlockSpec((tr, d), lambda i: (i, 0))
    vec = pl.BlockSpec((1, d), lambda i: (0, 0))
    return pl.pallas_call(
        body, name="rmsnorm_bwd", grid=(t_dim // tr,),
        in_specs=[row, vec, row, row], out_specs=[row, vec],
        out_shape=[jax.ShapeDtypeStruct((t_dim, d), F32), jax.ShapeDtypeStruct((1, d), F32)],
        compiler_params=_cp("arbitrary"),
    )(x, g, dh, dres)


def _loss_head(x, g, target):
    t_dim, d = x.shape
    tr = _tile(t_dim, 256, 8)

    def body(x_ref, g_ref, t_ref, loss_ref, dx_ref, dg_ref):
        i = pl.program_id(0)
        xv = x_ref[...]
        gv = g_ref[...]
        r = lax.rsqrt(jnp.mean(xv * xv, axis=-1, keepdims=True) + EPS)
        n = xv * r
        err = n * gv - t_ref[...]

        @pl.when(i == 0)
        def _():
            dg_ref[...] = jnp.zeros_like(dg_ref)
            loss_ref[...] = jnp.zeros_like(loss_ref)

        per_tok = jnp.mean(err * err, axis=-1, keepdims=True)
        loss_ref[...] += 0.5 * jnp.sum(per_tok, axis=0, keepdims=True)
        dy = err * (1.0 / d)
        dn = dy * gv
        dg_ref[...] += jnp.sum(dy * n, axis=0, keepdims=True)
        dx_ref[...] = r * (dn - n * jnp.mean(dn * n, axis=-1, keepdims=True))

    row = pl.BlockSpec((tr, d), lambda i: (i, 0))
    vec = pl.BlockSpec((1, d), lambda i: (0, 0))
    one = pl.BlockSpec((1, 1), lambda i: (0, 0))
    return pl.pallas_call(
        body, name="loss_head", grid=(t_dim // tr,),
        in_specs=[row, vec, row], out_specs=[one, row, vec],
        out_shape=[jax.ShapeDtypeStruct((1, 1), F32), jax.ShapeDtypeStruct((t_dim, d), F32),
                   jax.ShapeDtypeStruct((1, d), F32)],
        compiler_params=_cp("arbitrary"),
    )(x, g, target)


def _ffn_in(h, wg, wu, half):
    t_dim, d = h.shape
    f = wg.shape[-1]
    tm, tn = _tile(t_dim, 512), _tile(f, 1408)

    def body(h_ref, wg_ref, wu_ref, g_ref, u_ref, a_ref):
        hv = h_ref[...]
        gv = _dot(hv, wg_ref[...])
        uv = _dot(hv, wu_ref[...])
        g_ref[...] = gv.astype(BF16)
        u_ref[...] = uv.astype(BF16)
        a_ref[...] = (gv * _sigmoid(gv) * uv).astype(BF16)

    wspec = pl.BlockSpec((None, d, tn), lambda n, m: (half, 0, n))
    ospec = pl.BlockSpec((tm, tn), lambda n, m: (m, n))
    oshape = jax.ShapeDtypeStruct((t_dim, f), BF16)
    return pl.pallas_call(
        body, name="ffn_in", grid=(f // tn, t_dim // tm),
        in_specs=[pl.BlockSpec((tm, d), lambda n, m: (m, 0)), wspec, wspec],
        out_specs=[ospec, ospec, ospec], out_shape=[oshape, oshape, oshape],
        compiler_params=_cp("parallel", "parallel"),
    )(h, wg, wu)


def _ffn_bwd_act(dx, wd, g, u, half):
    t_dim, d = dx.shape
    f = wd.shape[-2]
    tm, tn = _tile(t_dim, 512), _tile(f, 1408)

    def body(dx_ref, wd_ref, g_ref, u_ref, dg_ref, du_ref):
        da = 0.5 * _dot_nt(dx_ref[...].astype(BF16), wd_ref[...])
        gv = g_ref[...].astype(F32)
        uv = u_ref[...].astype(F32)
        s = _sigmoid(gv)
        dg_ref[...] = (da * uv * (s * (1.0 + gv * (1.0 - s)))).astype(BF16)
        du_ref[...] = (da * gv * s).astype(BF16)

    tile = pl.BlockSpec((tm, tn), lambda n, m: (m, n))
    oshape = jax.ShapeDtypeStruct((t_dim, f), BF16)
    return pl.pallas_call(
        body, name="ffn_bwd_act", grid=(f // tn, t_dim // tm),
        in_specs=[pl.BlockSpec((tm, d), lambda n, m: (m, 0)),
                  pl.BlockSpec((None, tn, d), lambda n, m: (half, n, 0)), tile, tile],
        out_specs=[tile, tile], out_shape=[oshape, oshape],
        compiler_params=_cp("parallel", "parallel"),
    )(dx, wd, g, u)


def _ffn_fwd(x, norm_g, wg, wu, wd, half):
    h = _rmsnorm(x, norm_g)
    g, u, a = _ffn_in(h, wg, wu, half)
    x_new = _mm(a, wd, name="ffn_out", b_sel=(half,), add=x, scale=0.5, tk=1408)
    return x_new, (x, h, g, u, a)


def _ffn_bwd(dx, saved, norm_g, wg, wu, wd, half):
    x, h, g, u, a = saved
    dg, du = _ffn_bwd_act(dx, wd, g, u, half)
    d_wd = _mm(a, dx, name="ffn_dwd", ta=True, scale=0.5, out_dtype=BF16, tm=1408, tn=1024)
    d_wg = _mm(h, dg, name="ffn_dwgu", ta=True, out_dtype=BF16, tm=1024, tn=1408)
    d_wu = _mm(h, du, name="ffn_dwgu", ta=True, out_dtype=BF16, tm=1024, tn=1408)
    dh = _mm(dg, wg, name="ffn_dh", tb=True, b_sel=(half,), pair2=(du, wu, (), (half,)), tk=1408)
    dx_new, dnorm = _rmsnorm_bwd(x, norm_g, dh, dx)
    return dx_new, dnorm, d_wg, d_wu, d_wd


def _shifted(v, off, t_idx):
    if off == 0:
        return v
    t_dim = v.shape[0]
    sh = pltpu.roll(v, (-off) % t_dim, 0)
    valid = jnp.logical_and(t_idx + off >= 0, t_idx + off < t_dim)
    return jnp.where(valid, sh, 0.0)


def _conv_pre(u, w_ref, b_ref, t_idx):
    acc = jnp.zeros_like(u) + b_ref[...]
    shifted = []
    for k in range(CONV_W):
        sh = _shifted(u, k - CONV_W // 2, t_idx)
        shifted.append(sh)
        acc = acc + w_ref[k:k + 1, :] * sh
    return acc, shifted


def _conv_silu(proj, conv_w, conv_b, col0):
    t_dim = proj.shape[0]
    cd = conv_w.shape[-1]
    cb = _tile(cd, 256)
    assert col0 % cb == 0

    def body(u_ref, w_ref, b_ref, o_ref):
        t_idx = lax.broadcasted_iota(jnp.int32, (t_dim, cb), 0)
        pre, _ = _conv_pre(u_ref[...], w_ref, b_ref, t_idx)
        o_ref[...] = pre * _sigmoid(pre)

    return pl.pallas_call(
        body, name="conv_silu", grid=(cd // cb,),
        in_specs=[pl.BlockSpec((t_dim, cb), lambda j: (0, col0 // cb + j)),
                  pl.BlockSpec((CONV_W, cb), lambda j: (0, j)), pl.BlockSpec((1, cb), lambda j: (0, j))],
        out_specs=pl.BlockSpec((t_dim, cb), lambda j: (0, j)),
        out_shape=jax.ShapeDtypeStruct((t_dim, cd), F32),
        compiler_params=_cp("parallel"),
    )(proj, conv_w, conv_b)


def _conv_silu_bwd(proj, conv_w, conv_b, dact2, col0):
    t_dim = proj.shape[0]
    cd = conv_w.shape[-1]
    cb = _tile(cd, 256)

    def body(u_ref, w_ref, b_ref, da_ref, du_ref, dwb_ref):
        t_idx = lax.broadcasted_iota(jnp.int32, (t_dim, cb), 0)
        pre, shifted = _conv_pre(u_ref[...], w_ref, b_ref, t_idx)
        s = _sigmoid(pre)
        dpre = (da_ref[0] + da_ref[1]) * (s * (1.0 + pre * (1.0 - s)))
        du = jnp.zeros_like(dpre)
        for k in range(CONV_W):
            du = du + w_ref[k:k + 1, :] * _shifted(dpre, -(k - CONV_W // 2), t_idx)
            dwb_ref[k:k + 1, :] = jnp.sum(dpre * shifted[k], axis=0, keepdims=True)
        dwb_ref[CONV_W:CONV_W + 1, :] = jnp.sum(dpre, axis=0, keepdims=True)
        dwb_ref[CONV_W + 1:8, :] = jnp.zeros((8 - CONV_W - 1, cb), F32)
        du_ref[...] = du.astype(BF16)

    return pl.pallas_call(
        body, name="conv_silu_bwd", grid=(cd // cb,),
        in_specs=[pl.BlockSpec((t_dim, cb), lambda j: (0, col0 // cb + j)),
                  pl.BlockSpec((CONV_W, cb), lambda j: (0, j)), pl.BlockSpec((1, cb), lambda j: (0, j)),
                  pl.BlockSpec((2, t_dim, cb), lambda j: (0, 0, j))],
        out_specs=[pl.BlockSpec((t_dim, cb), lambda j: (0, j)), pl.BlockSpec((8, cb), lambda j: (0, j))],
        out_shape=[jax.ShapeDtypeStruct((t_dim, cd), BF16), jax.ShapeDtypeStruct((8, cd), F32)],
        compiler_params=_cp("parallel"),
    )(proj, conv_w, conv_b, dact2)


def _softplus_fwd(dt_raw, bias):
    def body(r_ref, b_ref, o_ref):
        v = r_ref[...] + b_ref[...]
        o_ref[...] = jnp.maximum(v, 0.0) + jnp.log(1.0 + jnp.exp(-jnp.abs(v)))

    return pl.pallas_call(body, name="softplus", out_shape=jax.ShapeDtypeStruct(dt_raw.shape, F32))(dt_raw, bias)


def _softplus_bwd(dt_raw, bias, ddt, da, a_log):
    def body(r_ref, b_ref, ddt_ref, da_ref, al_ref, dr_ref, db_ref, dal_ref):
        dv = ddt_ref[...] * _sigmoid(r_ref[...] + b_ref[...])
        dr_ref[...] = dv.astype(BF16)
        db_ref[...] = jnp.sum(dv, axis=0, keepdims=True)
        dal_ref[...] = -da_ref[...] * jnp.exp(al_ref[...])

    vec = jax.ShapeDtypeStruct(bias.shape, F32)
    return pl.pallas_call(
        body, name="softplus_bwd",
        out_shape=[jax.ShapeDtypeStruct(dt_raw.shape, BF16), vec, vec])(dt_raw, bias, ddt, da, a_log)


def _chunk_setup(d, dt_ref, al_ref):
    q = SSD_CHUNK
    ii = lax.broadcasted_iota(jnp.int32, (q, q), 0)
    jj = lax.broadcasted_iota(jnp.int32, (q, q), 1)
    sgn = 1 - 2 * d
    mask = (jj - ii) * sgn <= 0
    mask_t = (ii - jj) * sgn <= 0
    m01 = mask.astype(F32)
    m01_t = mask_t.astype(F32)
    dt = dt_ref[...]
    a = -jnp.exp(al_ref[...])
    dta = dt * a
    cs = _dot_exact01(m01, dta)
    tot = jnp.sum(dta, axis=0, keepdims=True)
    return mask, mask_t, m01, m01_t, dt, a, dta, cs, tot


def _head_lanes():
    hh = lax.broadcasted_iota(jnp.int32, (HEADS_PER_GROUP, GROUP_W), 0)
    ll = lax.broadcasted_iota(jnp.int32, (HEADS_PER_GROUP, GROUP_W), 1)
    return jnp.logical_and(ll >= hh * SSD_HEAD_DIM, ll < (hh + 1) * SSD_HEAD_DIM).astype(BF16)


def _expand(v, e16):
    h1, h2, h3 = _split3(v)
    return _dot(h1, e16) + _dot(h2, e16) + _dot(h3, e16)


def _expand_row(v, e16):
    return _expand(jnp.broadcast_to(v, (8, HEADS_PER_GROUP)), e16)[0:1]


def _per_head(v, e16):
    h1, h2, h3 = _split3(v)
    return _dot_nt(h1, e16) + _dot_nt(h2, e16) + _dot_nt(h3, e16)


def _rows_of(cs):
    n = cs.shape[1]
    eye = lax.broadcasted_iota(jnp.int32, (n, n), 0) == lax.broadcasted_iota(jnp.int32, (n, n), 1)
    return _dot_nt_exact01(eye.astype(F32), cs)


def _ssd_specs(t_dim, di, g_cnt, chunk_of):
    q, ns = SSD_CHUNK, SSD_STATE
    x_spec = pl.BlockSpec((q, GROUP_W), lambda d, g, c: (chunk_of(d, c), g))
    b_spec = pl.BlockSpec((q, ns), lambda d, g, c: (chunk_of(d, c), di // ns + g))
    c_spec = pl.BlockSpec((q, ns), lambda d, g, c: (chunk_of(d, c), di // ns + g_cnt + g))
    dt_spec = pl.BlockSpec((None, None, q, HEADS_PER_GROUP), lambda d, g, c: (d, g, chunk_of(d, c), 0))
    al_spec = pl.BlockSpec((None, None, 1, HEADS_PER_GROUP), lambda d, g, c: (d, g, 0, 0))
    return x_spec, b_spec, c_spec, dt_spec, al_spec


def _ssd_fwd(xbc, dt4, al4):
    t_dim = xbc.shape[0]
    g_cnt = dt4.shape[1]
    di = g_cnt * GROUP_W
    q, ns, p = SSD_CHUNK, SSD_STATE, SSD_HEAD_DIM
    nc = t_dim // q

    def chunk_of(d, c):
        return c + d * (nc - 1 - 2 * c)

    def body(x_ref, b_ref, c_ref, dt_ref, al_ref, y_ref, hin_ref, h_sc):
        d = pl.program_id(0)
        c = pl.program_id(2)

        @pl.when(c == 0)
        def _():
            h_sc[...] = jnp.zeros_like(h_sc)

        mask, mask_t, m01, m01_t, dt, a, dta, cs, tot = _chunk_setup(d, dt_ref, al_ref)
        e16 = _head_lanes()
        cs_rows = _rows_of(cs)
        cb16 = c_ref[...].astype(BF16)
        bt16 = b_ref[...].T.astype(BF16)
        cb = _dot(cb16, bt16)
        hin = h_sc[...]
        hin_ref[...] = hin
        xdt = x_ref[...] * _expand(dt, e16)
        y_off = _dot(cb16, hin.astype(BF16)) * _expand(jnp.exp(cs), e16)
        low = lax.broadcasted_iota(jnp.int32, (q, 2 * p), 1) < p
        for pair in range(HEADS_PER_GROUP // 2):
            ps = slice(2 * p * pair, 2 * p * (pair + 1))
            blk = xdt[:, ps]
            acc = y_off[:, ps]
            for hh in range(2):
                h = 2 * pair + hh
                lmat = jnp.exp(jnp.where(mask, cs[:, h:h + 1] - cs_rows[h:h + 1, :], -1e30))
                xm = jnp.where(low, blk, 0.0) if hh == 0 else jnp.where(low, 0.0, blk)
                acc = acc + _dot((cb * lmat).astype(BF16), xm.astype(BF16))
            y_ref[:, ps] = acc
        xd = xdt * _expand(jnp.exp(tot - cs), e16)
        st = _dot(bt16, xd.astype(BF16))
        h_sc[...] = hin * _expand_row(jnp.exp(tot), e16) + st

    x_spec, b_spec, c_spec, dt_spec, al_spec = _ssd_specs(t_dim, di, g_cnt, chunk_of)
    return pl.pallas_call(
        body, name="ssd_fwd", grid=(2, g_cnt, nc),
        in_specs=[x_spec, b_spec, c_spec, dt_spec, al_spec],
        out_specs=[pl.BlockSpec((None, q, GROUP_W), lambda d, g, c: (d, chunk_of(d, c), g)),
                   pl.BlockSpec((None, None, None, ns, GROUP_W), lambda d, g, c: (d, g, chunk_of(d, c), 0, 0))],
        out_shape=[jax.ShapeDtypeStruct((2, t_dim, di), F32),
                   jax.ShapeDtypeStruct((2, g_cnt, nc, ns, GROUP_W), F32)],
        scratch_shapes=[pltpu.VMEM((ns, GROUP_W), F32)],
        compiler_params=_cp("parallel", "parallel", "arbitrary"),
    )(xbc, xbc, xbc, dt4, al4)


def _ssd_bwd(xbc, dt4, al4, hin_all, dy, dvec):
    t_dim = xbc.shape[0]
    g_cnt = dt4.shape[1]
    di = g_cnt * GROUP_W
    q, ns, p, hg = SSD_CHUNK, SSD_STATE, SSD_HEAD_DIM, HEADS_PER_GROUP
    nc = t_dim // q

    def chunk_of(d, c):
        return (nc - 1 - c) + d * (2 * c - nc + 1)

    def body(x_ref, b_ref, c_ref, dt_ref, al_ref, hin_ref, dy_ref, dv_ref,
             dx_ref, db_ref, dc_ref, ddt_ref, da_ref, g_sc, dxdt_sc, zrow_sc):
        d = pl.program_id(0)
        c = pl.program_id(2)

        @pl.when(c == 0)
        def _():
            g_sc[...] = jnp.zeros_like(g_sc)
            da_ref[...] = jnp.zeros_like(da_ref)

        mask, mask_t, m01, m01_t, dt, a, dta, cs, tot = _chunk_setup(d, dt_ref, al_ref)
        xv = x_ref[...]
        dyv = dy_ref[...]
        bb = b_ref[...].astype(BF16)
        cb16 = c_ref[...].astype(BF16)
        bt16 = b_ref[...].T.astype(BF16)
        ct16 = c_ref[...].T.astype(BF16)
        cb = _dot(cb16, bt16)
        cbt = _dot(bb, ct16)
        hin = hin_ref[...]
        hin16 = hin.astype(BF16)
        gst = g_sc[...]
        gst16 = gst.astype(BF16)
        ch = _dot(cb16, hin16)
        wst = _dot(bb, gst16)
        skip = jnp.where(d == 0, 1.0, 0.0)
        e16 = _head_lanes()
        cs_rows = _rows_of(cs)
        dt_x = _expand(dt, e16)
        e_x = _expand(jnp.exp(cs), e16)
        dec = jnp.exp(tot - cs)
        etot = jnp.exp(tot)
        xdt = xv * dt_x
        dye = dyv * e_x
        t1 = _per_head(wst * xdt, e16) * dec
        dcs = _per_head(dye * ch, e16) - t1
        dtot = jnp.sum(t1, axis=0, keepdims=True)
        dec_x = _expand(dec, e16)
        dxdt_state = wst * dec_x
        dcb = jnp.zeros((q, q), F32)
        low = lax.broadcasted_iota(jnp.int32, (q, 2 * p), 1) < p
        col8 = lax.broadcasted_iota(jnp.int32, (q, hg), 1)
        for pair in range(hg // 2):
            ps = slice(2 * p * pair, 2 * p * (pair + 1))
            xblk, dyblk = xdt[:, ps], dyv[:, ps]
            acc = dxdt_state[:, ps]
            for hh in range(2):
                h = 2 * pair + hh
                seg = cs[:, h:h + 1] - cs_rows[h:h + 1, :]
                lmat = jnp.exp(jnp.where(mask, seg, -1e30))
                lmat_t = jnp.exp(jnp.where(mask_t, -seg, -1e30))
                if hh == 0:
                    xm, dym = jnp.where(low, xblk, 0.0).astype(BF16), jnp.where(low, dyblk, 0.0).astype(BF16)
                else:
                    xm, dym = jnp.where(low, 0.0, xblk).astype(BF16), jnp.where(low, 0.0, dyblk).astype(BF16)
                acc = acc + _dot((cbt * lmat_t).astype(BF16), dym)
                dm = _dot_nt(dym, xm)
                z = dm * (cb * lmat)
                dcb = dcb + dm * lmat
                dcs = dcs + jnp.where(col8 == h, jnp.sum(z, axis=-1, keepdims=True), 0.0)
                zrow_sc[h:h + 1, :] = jnp.sum(z, axis=0, keepdims=True)
            dxdt_sc[:, ps] = acc
        dxdt = dxdt_sc[...]
        dx_ref[...] = dxdt * dt_x + skip * dyv * dv_ref[...]
        dye16 = dye.astype(BF16)
        dcb16 = dcb.astype(BF16)
        dc_ref[...] = _dot_nt(dye16, hin16) + _dot(dcb16, bb)
        db_ref[...] = _dot_nt((xdt * dec_x).astype(BF16), gst16) + _dot(dcb.T.astype(BF16), cb16)
        g_sc[...] = _dot(ct16, dye16) + gst * _expand_row(etot, e16)
        carried = jnp.broadcast_to(jnp.sum(gst * hin, axis=0, keepdims=True), (8, GROUP_W))
        dtot = dtot + _per_head(carried, e16)[0:1] * etot
        ddta = _dot_exact01(m01_t, dcs) - _dot_nt_exact01(m01_t, zrow_sc[...]) + dtot
        ddt_ref[...] = _per_head(dxdt * xv, e16) + ddta * a
        da_ref[...] += jnp.sum(ddta * dt, axis=0, keepdims=True)

    x_spec, b_spec, c_spec, dt_spec, al_spec = _ssd_specs(t_dim, di, g_cnt, chunk_of)
    hin_spec = pl.BlockSpec((None, None, None, ns, GROUP_W), lambda d, g, c: (d, g, chunk_of(d, c), 0, 0))
    dy_spec = pl.BlockSpec((q, GROUP_W), lambda d, g, c: (chunk_of(d, c), g))
    dv_spec = pl.BlockSpec((1, GROUP_W), lambda d, g, c: (0, g))
    gn = g_cnt * ns
    return pl.pallas_call(
        body, name="ssd_bwd", grid=(2, g_cnt, nc),
        in_specs=[x_spec, b_spec, c_spec, dt_spec, al_spec, hin_spec, dy_spec, dv_spec],
        out_specs=[pl.BlockSpec((None, q, GROUP_W), lambda d, g, c: (d, chunk_of(d, c), g)),
                   pl.BlockSpec((None, q, ns), lambda d, g, c: (d, chunk_of(d, c), g)),
                   pl.BlockSpec((None, q, ns), lambda d, g, c: (d, chunk_of(d, c), g)),
                   pl.BlockSpec((None, None, q, hg), lambda d, g, c: (d, g, chunk_of(d, c), 0)),
                   pl.BlockSpec((None, None, 1, hg), lambda d, g, c: (d, g, 0, 0))],
        out_shape=[jax.ShapeDtypeStruct((2, t_dim, di), F32), jax.ShapeDtypeStruct((2, t_dim, gn), F32),
                   jax.ShapeDtypeStruct((2, t_dim, gn), F32), jax.ShapeDtypeStruct((2, g_cnt, t_dim, hg), F32),
                   jax.ShapeDtypeStruct((2, g_cnt, 1, hg), F32)],
        scratch_shapes=[pltpu.VMEM((ns, GROUP_W), F32), pltpu.VMEM((q, GROUP_W), F32), pltpu.VMEM((hg, q), F32)],
        compiler_params=_cp("parallel", "parallel", "arbitrary"),
    )(xbc, xbc, xbc, dt4, al4, hin_all, dy, dvec)


def _gate_norm(y2, xbc, proj, dvec, ng):
    t_dim, di = y2.shape[1:]
    tr = _tile(t_dim, 128, 8)

    def body(y2_ref, x_ref, z_ref, dv_ref, ng_ref, o_ref):
        y = y2_ref[0] + y2_ref[1] + x_ref[...] * dv_ref[...]
        z = z_ref[...]
        v = y * z * _sigmoid(z)
        r = lax.rsqrt(jnp.mean(v * v, axis=-1, keepdims=True) + EPS)
        o_ref[...] = (v * r * ng_ref[...]).astype(BF16)

    row = pl.BlockSpec((tr, di), lambda i: (i, 0))
    vec = pl.BlockSpec((1, di), lambda i: (0, 0))
    return pl.pallas_call(
        body, name="gate_norm", grid=(t_dim // tr,),
        in_specs=[pl.BlockSpec((2, tr, di), lambda i: (0, i, 0)), row, row, vec, vec],
        out_specs=row, out_shape=jax.ShapeDtypeStruct((t_dim, di), BF16),
        compiler_params=_cp("parallel"),
    )(y2, xbc, proj, dvec, ng)


def _gate_norm_bwd(y2, xbc, proj, dvec, ng, dyn):
    t_dim, di = y2.shape[1:]
    tr = _tile(t_dim, 128, 8)

    def body(y2_ref, x_ref, z_ref, dv_ref, ng_ref, dyn_ref, dy_ref, dz_ref, dng_ref, dd_ref):
        i = pl.program_id(0)
        xv = x_ref[...]
        y = y2_ref[0] + y2_ref[1] + xv * dv_ref[...]
        z = z_ref[...]
        s = _sigmoid(z)
        v = y * z * s
        r = lax.rsqrt(jnp.mean(v * v, axis=-1, keepdims=True) + EPS)
        n = v * r
        dynv = dyn_ref[...]
        dn = dynv * ng_ref[...]
        dv = r * (dn - n * jnp.mean(dn * n, axis=-1, keepdims=True))
        dy = dv * z * s

        @pl.when(i == 0)
        def _():
            dng_ref[...] = jnp.zeros_like(dng_ref)
            dd_ref[...] = jnp.zeros_like(dd_ref)

        dng_ref[...] += jnp.sum(dynv * n, axis=0, keepdims=True)
        dd_ref[...] += jnp.sum(dy * xv, axis=0, keepdims=True)
        dy_ref[...] = dy
        dz_ref[...] = (dv * y * (s * (1.0 + z * (1.0 - s)))).astype(BF16)

    row = pl.BlockSpec((tr, di), lambda i: (i, 0))
    vec = pl.BlockSpec((1, di), lambda i: (0, 0))
    return pl.pallas_call(
        body, name="gate_norm_bwd", grid=(t_dim // tr,),
        in_specs=[pl.BlockSpec((2, tr, di), lambda i: (0, i, 0)), row, row, vec, vec, row],
        out_specs=[row, row, vec, vec],
        out_shape=[jax.ShapeDtypeStruct((t_dim, di), F32), jax.ShapeDtypeStruct((t_dim, di), BF16),
                   jax.ShapeDtypeStruct((1, di), F32), jax.ShapeDtypeStruct((1, di), F32)],
        compiler_params=_cp("arbitrary"),
    )(y2, xbc, proj, dvec, ng, dyn)


def _dt_to_groups(dt):
    t_dim, h2 = dt.shape
    g_cnt = h2 // 2 // HEADS_PER_GROUP
    return dt.reshape(t_dim, 2, g_cnt, HEADS_PER_GROUP).transpose(1, 2, 0, 3)


def _dt_from_groups(dt4):
    _, g_cnt, t_dim, hg = dt4.shape
    return dt4.transpose(2, 0, 1, 3).reshape(t_dim, 2 * g_cnt * hg)


def _ssd_mixer_fwd(x, norm_g, w_in, conv_w, conv_b, dt_bias, a_log, d_skip, ssd_norm, w_out, j):
    heads = d_skip.shape[0]
    di = heads * SSD_HEAD_DIM
    g_cnt = heads // HEADS_PER_GROUP
    cd = conv_w.shape[-1]
    hn = _rmsnorm(x, norm_g)
    proj = _mm(hn, w_in, name="ssd_proj", b_sel=(j,), tn=1152)
    xbc = _conv_silu(proj, conv_w, conv_b, di)
    dt_raw = proj[:, di + cd:]
    dt = _softplus_fwd(dt_raw, dt_bias)
    dt4 = _dt_to_groups(dt)
    al4 = a_log.reshape(2, g_cnt, 1, HEADS_PER_GROUP)
    y2, hin = _ssd_fwd(xbc, dt4, al4)
    dvec = jnp.repeat(d_skip, SSD_HEAD_DIM).reshape(1, di)
    yn = _gate_norm(y2, xbc, proj, dvec, ssd_norm)
    x_new = _mm(yn, w_out, name="ssd_out", b_sel=(j,), add=x)
    return x_new, (x, hn, proj, xbc, dt_raw, dt4, al4, y2, hin, dvec, yn)


def _ssd_mixer_bwd(dx, saved, norm_g, w_in, conv_w, conv_b, dt_bias, a_log, ssd_norm, w_out, j):
    x, hn, proj, xbc, dt_raw, dt4, al4, y2, hin, dvec, yn = saved
    di = dvec.shape[1]
    heads = di // SSD_HEAD_DIM
    d_wout = _mm(yn, dx, name="ssd_dwout", ta=True, out_dtype=BF16)
    dyn = _mm(dx, w_out, name="ssd_dyn", tb=True, b_sel=(j,))
    dy, dz, d_ng, dd_col = _gate_norm_bwd(y2, xbc, proj, dvec, ssd_norm, dyn)
    dx2, db2, dc2, ddt4, da4 = _ssd_bwd(xbc, dt4, al4, hin, dy, dvec)
    dact2 = jnp.concatenate([dx2, db2, dc2], axis=-1)
    dxbc, dwb = _conv_silu_bwd(proj, conv_w, conv_b, dact2, di)
    ddt_raw, d_bias, d_alog = _softplus_bwd(dt_raw, dt_bias, _dt_from_groups(ddt4), da4.reshape(1, 2 * heads), a_log)
    dproj = jnp.concatenate([dz, dxbc, ddt_raw], axis=-1)
    d_win = _mm(hn, dproj, name="ssd_dwin", ta=True, out_dtype=BF16, tn=1152)
    dhn = _mm(dproj, w_in, name="ssd_dhn", tb=True, b_sel=(j,), tk=1152)
    dx_new, d_norm = _rmsnorm_bwd(x, norm_g, dhn, dx)
    small = dict(mix_norm=d_norm, conv_w=dwb[:CONV_W], conv_b=dwb[CONV_W:CONV_W + 1], dt_bias=d_bias, a_log=d_alog,
                 ssd_d=dd_col.reshape(heads, SSD_HEAD_DIM).sum(axis=1), ssd_norm=d_ng)
    return dx_new, small, d_win, d_wout


def _pool_count(t_idx, w, t_dim):
    hi = jnp.minimum(t_idx + w // 2, t_dim)
    lo = jnp.maximum(t_idx - w // 2, 0)
    return (hi - lo).astype(F32)


def _pool_mix(u, transpose):
    t_dim, d = u.shape
    gd = d // len(POOL_WINDOWS)
    cb = _tile(gd, 256)
    per = gd // cb

    def body(u_ref, o_ref):
        gi = pl.program_id(0)
        t_idx = lax.broadcasted_iota(jnp.int32, (t_dim, cb), 0)
        uv = u_ref[...]
        for widx, w in enumerate(POOL_WINDOWS):
            @pl.when(gi == widx)
            def _(w=w):
                cnt = _pool_count(t_idx, w, t_dim)
                src = uv / cnt if transpose else uv
                acc = jnp.zeros_like(uv)
                for k in range(-(w // 2), w // 2):
                    acc = acc + _shifted(src, -k if transpose else k, t_idx)
                res = acc - uv if transpose else acc / cnt - uv
                o_ref[...] = res.astype(BF16)

    spec = pl.BlockSpec((t_dim, cb), lambda gi, j: (0, gi * per + j))
    return pl.pallas_call(
        body, name="pool_mix_t" if transpose else "pool_mix", grid=(len(POOL_WINDOWS), per),
        in_specs=[spec], out_specs=spec, out_shape=jax.ShapeDtypeStruct((t_dim, d), BF16),
        compiler_params=_cp("parallel", "parallel"),
    )(u)


def _pool_group(mix, wgrp, scale, j):
    t_dim, d = mix.shape
    gd = wgrp.shape[-1]
    tm = _tile(t_dim, 512)

    def body(m_ref, w_ref, s_ref, v_ref, vs_ref):
        v = _dot(m_ref[...], w_ref[...])
        v_ref[...] = v
        vs_ref[...] = (v * s_ref[...]).astype(BF16)

    tile = pl.BlockSpec((tm, gd), lambda gi, m: (m, gi))
    return pl.pallas_call(
        body, name="pool_group", grid=(d // gd, t_dim // tm),
        in_specs=[tile, pl.BlockSpec((None, None, gd, gd), lambda gi, m: (j, gi, 0, 0)),
                  pl.BlockSpec((1, gd), lambda gi, m: (0, gi))],
        out_specs=[tile, tile],
        out_shape=[jax.ShapeDtypeStruct((t_dim, d), F32), jax.ShapeDtypeStruct((t_dim, d), BF16)],
        compiler_params=_cp("parallel", "parallel"),
    )(mix, wgrp, scale)


def _pool_group_bwd(dvs, v, mix, wgrp, scale, j):
    t_dim, d = mix.shape
    gd = wgrp.shape[-1]
    n_g = d // gd
    tm = _tile(t_dim, 512)
    nm = t_dim // tm

    def body(dvs_ref, v_ref, m_ref, w_ref, s_ref, dmix_ref, ds_ref, dw_ref, acc_ref):
        m = pl.program_id(1)
        dvsv = dvs_ref[...]

        @pl.when(m == 0)
        def _():
            ds_ref[...] = jnp.zeros_like(ds_ref)
            acc_ref[...] = jnp.zeros_like(acc_ref)

        ds_ref[...] += jnp.sum(dvsv * v_ref[...], axis=0, keepdims=True)
        dv16 = (dvsv * s_ref[...]).astype(BF16)
        dmix_ref[...] = _dot_nt(dv16, w_ref[...])
        acc_ref[...] += _dot(m_ref[...].T, dv16)

        @pl.when(m == nm - 1)
        def _():
            dw_ref[...] = acc_ref[...].astype(BF16)

    tile = pl.BlockSpec((tm, gd), lambda gi, m: (m, gi))
    vec = pl.BlockSpec((1, gd), lambda gi, m: (0, gi))
    return pl.pallas_call(
        body, name="pool_group_bwd", grid=(n_g, nm),
        in_specs=[tile, tile, tile, pl.BlockSpec((None, None, gd, gd), lambda gi, m: (j, gi, 0, 0)), vec],
        out_specs=[tile, vec, pl.BlockSpec((None, gd, gd), lambda gi, m: (gi, 0, 0))],
        out_shape=[jax.ShapeDtypeStruct((t_dim, d), F32), jax.ShapeDtypeStruct((1, d), F32),
                   jax.ShapeDtypeStruct((n_g, gd, gd), BF16)],
        scratch_shapes=[pltpu.VMEM((gd, gd), F32)],
        compiler_params=_cp("parallel", "arbitrary"),
    )(dvs, v, mix, wgrp, scale)


def _pool_mixer_fwd(x, norm_g, w_in, wgrp, scale, w_out, j):
    hn = _rmsnorm(x, norm_g)
    u = _mm(hn, w_in, name="pool_u", b_sel=(j,))
    mix = _pool_mix(u, False)
    v, vs = _pool_group(mix, wgrp, scale, j)
    x_new = _mm(vs, w_out, name="pool_out", b_sel=(j,), add=x)
    return x_new, (x, hn, mix, v, vs)


def _pool_mixer_bwd(dx, saved, norm_g, w_in, wgrp, scale, w_out, j):
    x, hn, mix, v, vs = saved
    d_wout = _mm(vs, dx, name="pool_dw", ta=True, out_dtype=BF16)
    dvs = _mm(dx, w_out, name="pool_dvs", tb=True, b_sel=(j,))
    dmix, d_scale, d_wgrp = _pool_group_bwd(dvs, v, mix, wgrp, scale, j)
    du = _pool_mix(dmix, True)
    d_win = _mm(hn, du, name="pool_dw", ta=True, out_dtype=BF16)
    dhn = _mm(du, w_in, name="pool_dhn", tb=True, b_sel=(j,))
    dx_new, d_norm = _rmsnorm_bwd(x, norm_g, dhn, dx)
    return dx_new, d_norm, d_scale, d_win, d_wgrp, d_wout


def _local_step(x, target, depth, fetch, emit, on_loss, small, tok):
    saved, wts = [], {}

    def weights(group, after):
        if group not in wts:
            wts[group] = fetch(group, after)
        return wts[group]

    for i in range(depth):
        j = i // 2
        x, s0 = _ffn_fwd(x, small["ffn_norm"][i, 0][None] + tok, *weights(("ffn", i, 0), x), 0)
        mg = small["mix_norm"][i][None]
        if i % 2 == 0:
            heads = small["ssd_d"].shape[1]
            w_in, w_out = weights(("ssd", j), x)
            x, s1 = _ssd_mixer_fwd(x, mg, w_in, small["ssd_conv_w"][j], small["ssd_conv_b"][j][None],
                                   small["ssd_dt_bias"][j].reshape(1, 2 * heads), small["ssd_a_log"][j].reshape(1, 2 * heads),
                                   small["ssd_d"][j], small["ssd_norm"][j][None], w_out, 0)
        else:
            p_in, p_grp, p_out = weights(("pool", j), x)
            x, s1 = _pool_mixer_fwd(x, mg, p_in, p_grp, small["pool_scale"][j][None], p_out, 0)
        x, s2 = _ffn_fwd(x, small["ffn_norm"][i, 1][None], *weights(("ffn", i, 1), x), 0)
        saved.append((s0, s1, s2))
    loss, dx, d_final = _loss_head(x, small["final_norm"][None], target)

    gs = {k: {} for k in ("ffn_norm", "mix_norm", "ssd_conv_w", "ssd_conv_b", "ssd_dt_bias", "ssd_a_log", "ssd_d",
                          "ssd_norm", "pool_scale")}
    tok = on_loss(loss)
    for i in reversed(range(depth)):
        j = i // 2
        s0, s1, s2 = saved[i]
        for half, sv in ((1, s2), (0, None)):
            if half == 0:
                sv = s0
                mg = small["mix_norm"][i][None] + tok
                if i % 2 == 0:
                    heads = small["ssd_d"].shape[1]
                    w_in, w_out = wts[("ssd", j)]
                    dx, sm, d_win, d_wout = _ssd_mixer_bwd(
                        dx, s1, mg, w_in, small["ssd_conv_w"][j], small["ssd_conv_b"][j][None],
                        small["ssd_dt_bias"][j].reshape(1, 2 * heads), small["ssd_a_log"][j].reshape(1, 2 * heads),
                        small["ssd_norm"][j][None], w_out, 0)
                    gs["mix_norm"][i] = sm["mix_norm"][0]
                    gs["ssd_conv_w"][j] = sm["conv_w"]
                    gs["ssd_conv_b"][j] = sm["conv_b"][0]
                    gs["ssd_dt_bias"][j] = sm["dt_bias"].reshape(2, heads)
                    gs["ssd_a_log"][j] = sm["a_log"].reshape(2, heads)
                    gs["ssd_d"][j] = sm["ssd_d"]
                    gs["ssd_norm"][j] = sm["ssd_norm"][0]
                    tok = tok + emit(("ssd", j), [d_win, d_wout])
                else:
                    p_in, p_grp, p_out = wts[("pool", j)]
                    dx, d_norm, d_scale, d_win, d_wgrp, d_wout = _pool_mixer_bwd(
                        dx, s1, mg, p_in, p_grp, small["pool_scale"][j][None], p_out, 0)
                    gs["mix_norm"][i] = d_norm[0]
                    gs["pool_scale"][j] = d_scale[0]
                    tok = tok + emit(("pool", j), [d_win, d_wgrp, d_wout])
            dx, d_norm, d_wg, d_wu, d_wd = _ffn_bwd(dx, sv, small["ffn_norm"][i, half][None] + tok,
                                                   *wts[("ffn", i, half)], 0)
            gs["ffn_norm"][(i, half)] = d_norm[0]
            tok = tok + emit(("ffn", i, half), [d_wg, d_wu, d_wd])
    n_s, n_p = (depth + 1) // 2, depth // 2
    gsmall = dict(
        ffn_norm=jnp.stack([jnp.stack([gs["ffn_norm"][(i, h)] for h in range(2)]) for i in range(depth)]),
        mix_norm=jnp.stack([gs["mix_norm"][i] for i in range(depth)]),
        ssd_conv_w=jnp.stack([gs["ssd_conv_w"][j] for j in range(n_s)]),
        ssd_conv_b=jnp.stack([gs["ssd_conv_b"][j] for j in range(n_s)]),
        ssd_dt_bias=jnp.stack([gs["ssd_dt_bias"][j] for j in range(n_s)]),
        ssd_a_log=jnp.stack([gs["ssd_a_log"][j] for j in range(n_s)]),
        ssd_d=jnp.stack([gs["ssd_d"][j] for j in range(n_s)]),
        ssd_norm=jnp.stack([gs["ssd_norm"][j] for j in range(n_s)]),
        pool_scale=jnp.stack([gs["pool_scale"][j] for j in range(n_p)]),
        final_norm=d_final[0],
    )
    return loss, dx, gsmall, tok


ANY = pl.BlockSpec(memory_space=pl.ANY)


def _mesh_pos():
    return lax.axis_index("x"), lax.axis_index("y"), lax.axis_index("c")


def _other_chips(x, y):
    return [(1 - x, y), (x, 1 - y), (1 - x, 1 - y)]


def _win(ref, windows, lead=()):
    rest = len(ref.shape) - len(lead)
    idx = tuple(lead) + tuple(pl.ds(*windows[ax]) if ax in windows else slice(None) for ax in range(rest))
    return ref.at[idx]


def _remote(src, dst, send_sems, recv_sems, k, peer):
    return pltpu.make_async_remote_copy(src_ref=src, dst_ref=dst, send_sem=send_sems.at[k], recv_sem=recv_sems.at[k],
                                        device_id=peer, device_id_type=MESH)


def _cast_place(w3, chip1, mode, l0, nl, out_dtype=BF16):
    _, r, c = w3.shape
    tr = _tile(r, 256, 16)
    nr = r // tr
    if mode == "cols":
        out_shape, blk = (nl, r, N_CHIPS * c), (None, tr, c)
        omap = lambda l, i, s: (l, i, s[0])
    elif mode == "rows":
        out_shape, blk = (nl, N_CHIPS * r, c), (None, tr, c)
        omap = lambda l, i, s: (l, s[0] * nr + i, 0)
    else:
        out_shape, blk = (nl, N_CHIPS, r, c), (None, None, tr, c)
        omap = lambda l, i, s: (l, s[0], i, 0)

    def body(s_ref, w_ref, o_ref):
        o_ref[...] = w_ref[...].astype(out_dtype)

    return pl.pallas_call(
        body, name="cast_place_" + mode, out_shape=jax.ShapeDtypeStruct(out_shape, out_dtype),
        grid_spec=pltpu.PrefetchScalarGridSpec(
            num_scalar_prefetch=1, grid=(nl, nr),
            in_specs=[pl.BlockSpec((None, tr, c), lambda l, i, s: (l0 + l, i, 0))],
            out_specs=pl.BlockSpec(blk, omap)),
        compiler_params=_cp("parallel", "parallel"),
    )(chip1, w3)


def _gather_begin(name, fulls, axes):
    n = len(fulls)
    shapes = [f.shape for f in fulls]

    def full_win(refs, t, sidx, hidx):
        sa, ha = axes[t]
        ssz, hsz = shapes[t][sa] // N_CHIPS, shapes[t][ha] // 2
        return _win(refs[t], {sa: (sidx * ssz, ssz), ha: (hidx * hsz, hsz)})

    def chips_plan(refs, x, y, c):
        mine = 2 * x + y
        return [(full_win(refs, t, mine, c), full_win(refs, t, mine, c), (px, py, c))
                for t in range(n) for px, py in _other_chips(x, y)]

    def sibling_plan(in_refs, out_refs, x, y, c):
        wins = [full_win(out_refs, t, 2 * px + py, c) for t in range(n) for px, py in _other_chips(x, y)]
        return [(w, w, (x, y, 1 - c)) for w in wins]

    sems, thru, token = _split_start(name + "_start", fulls, chips_plan, 3 * n)
    return (name, sems, thru, chips_plan, sibling_plan, 3 * n), token


def _gather_finish(pending, after):
    name, sems, thru, chips_plan, sibling_plan, n_copies = pending
    landed = _split_wait(name + "_wait", sems, thru, chips_plan, n_copies, after)
    return _exchange(name + "_sibling", landed, [jax.ShapeDtypeStruct(f.shape, f.dtype) for f in landed],
                     sibling_plan, n_copies, inplace=True)


def _exchange(name, inputs, out_shapes, plan, n_copies, inplace=False):
    n_in, n_out = len(inputs), len(out_shapes)

    def body(*refs):
        in_refs, out_refs = refs[:n_in], refs[n_in:n_in + n_out]
        send_sems, recv_sems = refs[n_in + n_out:]
        x, y, c = _mesh_pos()
        copies = plan(in_refs, out_refs, x, y, c)
        assert len(copies) == n_copies
        started = []
        for k, (src, dst, peer) in enumerate(copies):
            cp = _remote(src, dst, send_sems, recv_sems, k, peer)
            cp.start()
            started.append(cp)
        for cp in started:
            cp.wait()

    return pl.pallas_call(
        body, name=name, in_specs=[ANY] * n_in, out_specs=[ANY] * n_out, out_shape=out_shapes,
        input_output_aliases={t: t for t in range(n_in)} if inplace else {},
        scratch_shapes=[pltpu.SemaphoreType.DMA((n_copies,)), pltpu.SemaphoreType.DMA((n_copies,))],
    )(*inputs)


HBM = pl.BlockSpec(memory_space=pltpu.HBM)
SEM = pl.BlockSpec(memory_space=pltpu.SEMAPHORE)
DATAFLOW = pltpu.SideEffectType.DATAFLOW_SIDE_EFFECTING


def _split_start(name, bufs, plan, n_copies):
    n = len(bufs)

    def body(*refs):
        ins = refs[:n]
        send_sems, recv_sems = refs[n], refs[n + 1]
        token = refs[2 * n + 2]
        x, y, c = _mesh_pos()
        copies = plan(ins, x, y, c)
        assert len(copies) == n_copies
        for k, (src, dst, peer) in enumerate(copies):
            _remote(src, dst, send_sems, recv_sems, k, peer).start()
        token[...] = jnp.zeros_like(token)

    outs = pl.pallas_call(
        body, name=name,
        out_shape=(pltpu.SemaphoreType.DMA((n_copies,)), pltpu.SemaphoreType.DMA((n_copies,)),
                   *[pltpu.HBM(b.shape, b.dtype) for b in bufs], jax.ShapeDtypeStruct((8, LANE), F32)),
        in_specs=[HBM] * n, out_specs=(SEM, SEM, *[HBM] * n, pl.BlockSpec(memory_space=pltpu.VMEM)),
        input_output_aliases={t: 2 + t for t in range(n)},
        compiler_params=pltpu.CompilerParams(has_side_effects=DATAFLOW),
    )(*[pltpu.with_memory_space_constraint(b, pltpu.HBM) for b in bufs])
    return (outs[0], outs[1]), list(outs[2:2 + n]), outs[2 + n][0, 0]


def _split_wait(name, sems, bufs, plan, n_copies, after):
    n = len(bufs)

    def body(*refs):
        ins = refs[:n]
        send_sems, recv_sems = refs[n], refs[n + 1]
        x, y, c = _mesh_pos()
        copies = plan(ins, x, y, c)
        assert len(copies) == n_copies
        for k, (src, dst, peer) in enumerate(copies):
            cp = _remote(src, dst, send_sems, recv_sems, k, peer)
            cp.wait_send()
            cp.wait_recv()

    outs = pl.pallas_call(
        body, name=name, out_shape=tuple(pltpu.HBM(b.shape, b.dtype) for b in bufs),
        in_specs=[HBM] * n + [SEM, SEM, ANY], out_specs=tuple([HBM] * n),
        input_output_aliases={t: t for t in range(n)},
        compiler_params=pltpu.CompilerParams(has_side_effects=DATAFLOW),
    )(*bufs, sems[0], sems[1], after)
    return list(outs)


def _halved(shape, ha):
    out = list(shape)
    out[ha] //= 2
    return tuple(out)


def _sharded(shape, sa):
    out = list(shape)
    out[sa] //= N_CHIPS
    return tuple(out)


def _rs_cores_begin(name, grads, axes):
    n = len(grads)
    shapes = [g.shape for g in grads]
    landing = [lax.empty(_halved(g.shape, ha), g.dtype) for g, (_, ha) in zip(grads, axes)]

    def plan(refs, x, y, c):
        copies = []
        for t in range(n):
            ha = axes[t][1]
            hsz = shapes[t][ha] // 2
            copies.append((_win(refs[t], {ha: ((1 - c) * hsz, hsz)}), refs[n + t], (x, y, 1 - c)))
        return copies

    sems, thru, token = _split_start(name + "_start", list(grads) + landing, plan, n)
    return (name, sems, thru, plan, n), token


def _rs_cores_finish(pending, after):
    name, sems, thru, plan, n = pending
    done = _split_wait(name + "_wait", sems, thru, plan, n, after)
    return done[:n], done[n:]


def _small_begin(vec):
    landing = lax.empty((8,) + vec.shape, vec.dtype)

    def plan(refs, x, y, c):
        me = 4 * x + 2 * y + c
        copies = []
        for k in range(1, 8):
            kx, ky, kc = k // 4, (k // 2) % 2, k % 2
            copies.append((refs[0], refs[1].at[me], (x ^ kx, y ^ ky, c ^ kc)))
        return copies

    sems, thru, token = _split_start("small_start", [vec, landing], plan, 7)
    return (sems, thru, plan), token


def _small_finish(pending, after, me1):
    sems, thru, plan = pending
    vec, landing = _split_wait("small_wait", sems, thru, plan, 7, after)

    def body(s_ref, v_ref, l_ref, o_ref):
        tot = jnp.where(s_ref[0] == 0, v_ref[...], l_ref[0])
        for i in range(1, 8):
            tot = tot + jnp.where(s_ref[0] == i, v_ref[...], l_ref[i])
        o_ref[...] = tot

    rows = vec.shape[0]
    return pl.pallas_call(
        body, name="small_sum", out_shape=jax.ShapeDtypeStruct(vec.shape, F32),
        grid_spec=pltpu.PrefetchScalarGridSpec(
            num_scalar_prefetch=1, grid=(1,),
            in_specs=[pl.BlockSpec((rows, LANE), lambda i, s: (0, 0)), pl.BlockSpec((8, rows, LANE), lambda i, s: (0, 0, 0))],
            out_specs=pl.BlockSpec((rows, LANE), lambda i, s: (0, 0))),
    )(me1, vec, landing)


def _rs_chips_begin(name, halves, axes):
    n = len(halves)
    shapes = [h.shape for h in halves]
    landing = [lax.empty((N_CHIPS - 1,) + _sharded(h.shape, sa), h.dtype) for h, (sa, _) in zip(halves, axes)]

    def plan(refs, x, y, c):
        copies = []
        for t in range(n):
            sa = axes[t][0]
            ssz = shapes[t][sa] // N_CHIPS
            for j, (px, py) in enumerate(_other_chips(x, y)):
                copies.append((_win(refs[t], {sa: ((2 * px + py) * ssz, ssz)}), refs[n + t].at[j], (px, py, c)))
        return copies

    sems, thru, token = _split_start(name + "_start", list(halves) + landing, plan, 3 * n)
    return (name, sems, thru, plan, 3 * n), token


def _rs_chips_finish(pending, after):
    name, sems, thru, plan, n_copies = pending
    done = _split_wait(name + "_wait", sems, thru, plan, n_copies, after)
    n = len(done) // 2
    return done[:n], done[n:]


def _rs_finish(name, groups):
    flat = [(gi, t) for gi, grp in enumerate(groups) for t in range(len(grp))]
    shapes = [jax.ShapeDtypeStruct((len(grp),) + grp[0].shape, grp[0].dtype) for grp in groups]

    def plan(in_refs, out_refs, x, y, c):
        return [(in_refs[k], out_refs[gi].at[t], (x, y, 1 - c)) for k, (gi, t) in enumerate(flat)]

    return _exchange(name, [a for grp in groups for a in grp], shapes, plan, len(flat))


def _rows2d(a):
    return a.reshape(-1, a.shape[-1])


def _view3(a):
    return a.reshape((-1,) + a.shape[-2:])


def _add_pair(g, recv, ha, core1):
    g3, r3 = _view3(g), _view3(recv)
    rows_half = ha + 3 - g.ndim == 1
    n_l, r, c = r3.shape
    tr = _tile(r, 256, 16)
    nr = r // tr
    gmap = (lambda l, i, s: (l, s[0] * nr + i, 0)) if rows_half else (lambda l, i, s: (l, i, s[0]))

    def body(s_ref, g_ref, r_ref, o_ref):
        o_ref[...] = (g_ref[...].astype(F32) + r_ref[...].astype(F32)).astype(BF16)

    spec = pl.BlockSpec((None, tr, c), lambda l, i, s: (l, i, 0))
    out = pl.pallas_call(
        body, name="add_pair", out_shape=jax.ShapeDtypeStruct(r3.shape, BF16),
        grid_spec=pltpu.PrefetchScalarGridSpec(num_scalar_prefetch=1, grid=(n_l, nr),
                                               in_specs=[pl.BlockSpec((None, tr, c), gmap), spec], out_specs=spec),
        compiler_params=_cp("parallel", "parallel"))(core1, g3, r3)
    return out.reshape(recv.shape)


def _add_four(cs, recv, sa, chip1):
    c3 = _view3(cs)
    s3 = sa + 3 - cs.ndim
    lo, ro, co = (dim // N_CHIPS if ax == s3 else dim for ax, dim in enumerate(c3.shape))
    r4 = recv.reshape(N_CHIPS - 1, lo, ro, co)
    tr = _tile(ro, 256, 16)
    nr = ro // tr
    if s3 == 0:
        cmap = lambda l, i, s: (s[0] * lo + l, i, 0)
    elif s3 == 1:
        cmap = lambda l, i, s: (l, s[0] * nr + i, 0)
    else:
        cmap = lambda l, i, s: (l, i, s[0])

    def body(s_ref, c_ref, r_ref, out_ref):
        out_ref[...] = ((c_ref[...].astype(F32) + r_ref[0].astype(F32)) + r_ref[1].astype(F32)) + r_ref[2].astype(F32)

    out = pl.pallas_call(
        body, name="add_four", out_shape=jax.ShapeDtypeStruct((lo, ro, co), F32),
        grid_spec=pltpu.PrefetchScalarGridSpec(
            num_scalar_prefetch=1, grid=(lo, nr),
            in_specs=[pl.BlockSpec((None, tr, co), cmap),
                      pl.BlockSpec((N_CHIPS - 1, None, tr, co), lambda l, i, s: (0, l, i, 0))],
            out_specs=pl.BlockSpec((None, tr, co), lambda l, i, s: (l, i, 0))),
        compiler_params=_cp("parallel", "parallel"))(chip1, c3, r4)
    return out.reshape(recv.shape[1:])


def _adamw_halves(w, own, recv, m, v, rows_half, core1, l0=0, prev=None):
    w3, m3, v3, o3, r3 = (_view3(a) for a in (w, m, v, own, recv))
    _, _, c = w3.shape
    n_l, rh, ch = o3.shape
    tr = _tile(rh, 128, 8)
    nr = rh // tr
    c1 = 1.0 - ADAM_B1 ** ADAM_STEP
    c2 = 1.0 - ADAM_B2 ** ADAM_STEP

    n_prev = 0 if prev is None else 4

    def body(s_ref, w_ref, o_ref, r_ref, m_ref, v_ref, *rest):
        g_ref, d_ref, nm_ref, nv_ref = rest[n_prev:]
        gv = jnp.where(pl.program_id(1) == s_ref[0], o_ref[...], r_ref[...])
        nm = ADAM_B1 * m_ref[...] + (1.0 - ADAM_B1) * gv
        nv = ADAM_B2 * v_ref[...] + (1.0 - ADAM_B2) * (gv * gv)
        g_ref[...] = gv
        nm_ref[...] = nm
        nv_ref[...] = nv
        d_ref[...] = -ADAM_LR * ((nm / c1) / (jnp.sqrt(nv / c2) + ADAM_EPS) + ADAM_WD * w_ref[...])

    wmap = (lambda l, h, i, s: (l0 + l, h * nr + i, 0)) if rows_half else (lambda l, h, i, s: (l0 + l, i, h))
    wspec = pl.BlockSpec((None, tr, ch), wmap)
    ospec = pl.BlockSpec((None, tr, ch), lambda l, h, i, s: (jnp.where(h == s[0], l, 0), jnp.where(h == s[0], i, 0), 0))
    rspec = pl.BlockSpec((None, tr, ch), lambda l, h, i, s: (jnp.where(h == s[0], 0, l), jnp.where(h == s[0], 0, i), 0))
    osh = jax.ShapeDtypeStruct(w3.shape, F32)
    before = [] if prev is None else [_view3(p) for p in prev]
    outs = pl.pallas_call(
        body, name="adamw_halves", out_shape=[osh] * 4,
        grid_spec=pltpu.PrefetchScalarGridSpec(
            num_scalar_prefetch=1, grid=(n_l, 2, nr),
            in_specs=[wspec, ospec, rspec, wspec, wspec] + [ANY] * n_prev, out_specs=[wspec] * 4),
        input_output_aliases={6 + k: k for k in range(n_prev)},
        compiler_params=_cp("parallel", "parallel", "parallel"))(core1, w3, o3, r3, m3, v3, *before)
    return tuple(o.reshape(w.shape) for o in outs)


def _adamw(w, g, m, v):
    shape = w.shape
    w2, g2, m2, v2 = (_rows2d(a) if a.ndim > 1 else a.reshape(1, -1) for a in (w, g, m, v))
    rows, cols = w2.shape
    tr = _tile(rows, 256, 8)
    c1 = 1.0 - ADAM_B1 ** ADAM_STEP
    c2 = 1.0 - ADAM_B2 ** ADAM_STEP

    def body(w_ref, g_ref, m_ref, v_ref, d_ref, nm_ref, nv_ref):
        gv = g_ref[...]
        nm = ADAM_B1 * m_ref[...] + (1.0 - ADAM_B1) * gv
        nv = ADAM_B2 * v_ref[...] + (1.0 - ADAM_B2) * (gv * gv)
        nm_ref[...] = nm
        nv_ref[...] = nv
        d_ref[...] = -ADAM_LR * ((nm / c1) / (jnp.sqrt(nv / c2) + ADAM_EPS) + ADAM_WD * w_ref[...])

    spec = pl.BlockSpec((tr, cols), lambda i: (i, 0))
    osh = jax.ShapeDtypeStruct((rows, cols), F32)
    outs = pl.pallas_call(body, name="adamw", grid=(rows // tr,), in_specs=[spec] * 4, out_specs=[spec] * 3,
                          out_shape=[osh] * 3, compiler_params=_cp("parallel"))(w2, g2, m2, v2)
    return tuple(o.reshape(shape) for o in outs)


def _pack(arrs):
    flat = jnp.concatenate([a.reshape(-1) for a in arrs])
    n = flat.shape[0]
    rows = -(-n // (8 * LANE)) * 8
    return jnp.pad(flat, (0, rows * LANE - n)).reshape(rows, LANE)


def _unpack(packed, shapes):
    flat = packed.reshape(-1)
    out, pos = [], 0
    for sh in shapes:
        size = 1
        for dsz in sh:
            size *= dsz
        out.append(flat[pos:pos + size].reshape(sh))
        pos += size
    return out


BIG = ("ffn_w_gate", "ffn_w_up", "ffn_w_down", "ssd_w_in", "ssd_w_out", "pool_w_in", "pool_w_group", "pool_w_out")
WEIGHTS = ("ffn_norm", "ffn_w_gate", "ffn_w_up", "ffn_w_down", "mix_norm", "ssd_w_in", "ssd_conv_w", "ssd_conv_b",
           "ssd_dt_bias", "ssd_a_log", "ssd_d", "ssd_norm", "ssd_w_out", "pool_w_in", "pool_w_group", "pool_scale",
           "pool_w_out", "final_norm")
SMALL = tuple(k for k in WEIGHTS if k not in BIG)
SMALL_SHARDED = {"ffn_norm": 2, "ssd_conv_w": 2, "pool_scale": 1}


def kernel(x, ffn_norm, ffn_w_gate, ffn_w_up, ffn_w_down, mix_norm, ssd_w_in, ssd_conv_w, ssd_conv_b, ssd_dt_bias, ssd_a_log, ssd_d, ssd_norm, ssd_w_out, pool_w_in, pool_w_group, pool_scale, pool_w_out, final_norm, loss_target, m_ffn_norm, m_ffn_w_gate, m_ffn_w_up, m_ffn_w_down, m_mix_norm, m_ssd_w_in, m_ssd_conv_w, m_ssd_conv_b, m_ssd_dt_bias, m_ssd_a_log, m_ssd_d, m_ssd_norm, m_ssd_w_out, m_pool_w_in, m_pool_w_group, m_pool_scale, m_pool_w_out, m_final_norm, v_ffn_norm, v_ffn_w_gate, v_ffn_w_up, v_ffn_w_down, v_mix_norm, v_ssd_w_in, v_ssd_conv_w, v_ssd_conv_b, v_ssd_dt_bias, v_ssd_a_log, v_ssd_d, v_ssd_norm, v_ssd_w_out, v_pool_w_in, v_pool_w_group, v_pool_scale, v_pool_w_out, v_final_norm):
    w = dict(ffn_norm=ffn_norm, ffn_w_gate=ffn_w_gate, ffn_w_up=ffn_w_up, ffn_w_down=ffn_w_down, mix_norm=mix_norm,
             ssd_w_in=ssd_w_in, ssd_conv_w=ssd_conv_w, ssd_conv_b=ssd_conv_b, ssd_dt_bias=ssd_dt_bias,
             ssd_a_log=ssd_a_log, ssd_d=ssd_d, ssd_norm=ssd_norm, ssd_w_out=ssd_w_out, pool_w_in=pool_w_in,
             pool_w_group=pool_w_group, pool_scale=pool_scale, pool_w_out=pool_w_out, final_norm=final_norm)
    mom = dict(ffn_norm=m_ffn_norm, ffn_w_gate=m_ffn_w_gate, ffn_w_up=m_ffn_w_up, ffn_w_down=m_ffn_w_down,
               mix_norm=m_mix_norm, ssd_w_in=m_ssd_w_in, ssd_conv_w=m_ssd_conv_w, ssd_conv_b=m_ssd_conv_b,
               ssd_dt_bias=m_ssd_dt_bias, ssd_a_log=m_ssd_a_log, ssd_d=m_ssd_d, ssd_norm=m_ssd_norm,
               ssd_w_out=m_ssd_w_out, pool_w_in=m_pool_w_in, pool_w_group=m_pool_w_group, pool_scale=m_pool_scale,
               pool_w_out=m_pool_w_out, final_norm=m_final_norm)
    vel = dict(ffn_norm=v_ffn_norm, ffn_w_gate=v_ffn_w_gate, ffn_w_up=v_ffn_w_up, ffn_w_down=v_ffn_w_down,
               mix_norm=v_mix_norm, ssd_w_in=v_ssd_w_in, ssd_conv_w=v_ssd_conv_w, ssd_conv_b=v_ssd_conv_b,
               ssd_dt_bias=v_ssd_dt_bias, ssd_a_log=v_ssd_a_log, ssd_d=v_ssd_d, ssd_norm=v_ssd_norm,
               ssd_w_out=v_ssd_w_out, pool_w_in=v_pool_w_in, pool_w_group=v_pool_w_group, pool_scale=v_pool_scale,
               pool_w_out=v_pool_w_out, final_norm=v_final_norm)
    depth = ffn_w_gate.shape[0]
    n_s, n_p = ssd_w_in.shape[0], pool_w_in.shape[0]
    chip = 2 * lax.axis_index("x") + lax.axis_index("y")

    chip1 = jnp.reshape(chip, (1,)).astype(jnp.int32)
    core1 = jnp.reshape(lax.axis_index("c"), (1,)).astype(jnp.int32)

    gate3, up3, down3 = _view3(ffn_w_gate), _view3(ffn_w_up), _view3(ffn_w_down)
    d_model = ssd_w_in.shape[1]
    pending = {}
    sharded_names = tuple(SMALL_SHARDED)
    tok = jnp.zeros((), F32)

    def begin(group, name, fulls, axes):
        nonlocal tok
        pending[group], t = _gather_begin(name, fulls, axes)
        tok = tok + t

    for i in range(depth):
        for h in range(2):
            fulls = [_cast_place(gate3, chip1, "cols", 2 * i + h, 1), _cast_place(up3, chip1, "cols", 2 * i + h, 1),
                     _cast_place(down3, chip1, "rows", 2 * i + h, 1)]
            if (i, h) == (0, 0):
                packed_small = _pack([w[k] for k in sharded_names])
                fulls.append(_cast_place(packed_small[None], chip1, "slot", 0, 1, F32))
                begin(("ffn", i, h), "gather_first", fulls, [(2, 1), (2, 1), (1, 2), (1, 2)])
            else:
                begin(("ffn", i, h), "gather_ffn", fulls, [(2, 1), (2, 1), (1, 2)])
            j = i // 2
            if h == 0 and i % 2 == 0:
                begin(("ssd", j), "gather_ssd", [_cast_place(ssd_w_in, chip1, "slot", j, 1),
                                                 _cast_place(ssd_w_out, chip1, "rows", j, 1)], [(1, 2), (1, 2)])
            if h == 0 and i % 2 == 1:
                n_g = pool_w_group.shape[1]
                grp_full = _cast_place(_view3(pool_w_group), chip1, "rows", j * n_g, n_g)
                begin(("pool", j), "gather_pool", [_cast_place(pool_w_in, chip1, "rows", j, 1), grp_full[None],
                                                   _cast_place(pool_w_out, chip1, "rows", j, 1)],
                      [(1, 2), (2, 3), (1, 2)])

    fetched = {}

    def fetch(group, after):
        if group not in fetched:
            got = _gather_finish(pending[group], after)
            if group[0] == "ssd":
                got = [got[0].transpose(0, 2, 1, 3).reshape(1, d_model, -1), got[1]]
            fetched[group] = got[:3]
            if group == ("ffn", 0, 0):
                fetched["small"] = got[3][0]
        return fetched[group]

    fetch(("ffn", 0, 0), x)
    small = {k: w[k] for k in SMALL if k not in SMALL_SHARDED}
    per_chip = [_unpack(fetched["small"][s], [w[k].shape for k in sharded_names]) for s in range(N_CHIPS)]
    for t, k in enumerate(sharded_names):
        small[k] = jnp.concatenate([per_chip[s][t] for s in range(N_CHIPS)], axis=SMALL_SHARDED[k])

    ffn_names = ("ffn_w_gate", "ffn_w_up", "ffn_w_down")
    group_axes = dict(ffn=[(1, 0), (1, 0), (0, 1)], ssd=[(0, 1), (0, 1)], pool=[(0, 1), (1, 2), (0, 1)])
    group_names = dict(ffn=ffn_names, ssd=("ssd_w_in", "ssd_w_out"), pool=("pool_w_in", "pool_w_group", "pool_w_out"))
    travelling = {}
    between_cores = []

    def advance(after):
        group, pend = between_cores.pop()
        axes = group_axes[group[0]]
        grads, recv_a = _rs_cores_finish(pend, after)
        chip_sums = [_add_pair(g, r, ha, core1) for g, r, (_, ha) in zip(grads, recv_a, axes)]
        travelling[group], t = _rs_chips_begin("rs_chips_" + group[0], chip_sums, axes)
        return t

    def emit(group, grads):
        kind = group[0]
        if kind == "ssd":
            grads = [grads[0].reshape(d_model, N_CHIPS, -1).transpose(1, 0, 2), grads[1]]
        t = advance(grads[0]) if between_cores else jnp.zeros((), F32)
        pend, t2 = _rs_cores_begin("rs_cores_" + kind, grads, group_axes[kind])
        between_cores.append((group, pend))
        return t + t2

    total = {}

    def on_loss(part):
        total["loss"] = lax.psum(part[0, 0], ("x", "y", "c"))
        return jnp.minimum(total["loss"], 0.0)

    _, dx, gsmall, t_emits = _local_step(x[0], loss_target[0], depth, fetch, emit, on_loss, small, tok)
    loss = total["loss"]

    t_last = advance(dx)
    small_pending, t_small = _small_begin(_pack([gsmall[k] for k in SMALL]))
    started = jnp.reshape(t_emits + t_last + t_small, (1, 1))

    rows_half = dict(ffn_w_gate=True, ffn_w_up=True, ffn_w_down=False, ssd_w_in=True, ssd_w_out=False,
                     pool_w_in=False, pool_w_group=False, pool_w_out=False)
    results = {}

    def finish(groups, tag, after):
        sums = {k: {} for k in BIG}
        for group in groups:
            kind = group[0]
            chip_sums, recv_b = _rs_chips_finish(travelling[group], after)
            for k, cs, r, (sa, _) in zip(group_names[kind], chip_sums, recv_b, group_axes[kind]):
                sums[k][group[1:]] = _add_four(cs, r, sa, chip1)
        names = [k for k in BIG if sums[k]]
        own = {k: [sums[k][idx] for idx in sorted(sums[k])] for k in names}
        from_sibling = dict(zip(names, _rs_finish("rs_finish_" + tag, [own[k] for k in names])))
        for k in names:
            first = min(sums[k])
            l0 = first[0] * 2 + first[1] if k in ffn_names else first[0]
            results[k] = _adamw_halves(w[k], jnp.stack(own[k]), from_sibling[k], mom[k], vel[k], rows_half[k],
                                       core1, l0, results.get(k))
        return jnp.stack([results[k][3][(0,) * results[k][3].ndim] for k in names]).reshape(1, -1)

    late = [g for g in travelling if g[1] == 0 and g[0] != "pool"]
    early = [g for g in travelling if g not in late]
    done_early = finish(early, "early", started) if early else started
    done_late = finish(late, "late", done_early)
    grad, delta, new_m, new_v = ({k: results[k][t] for k in BIG} for t in range(4))

    small_shapes = [gsmall[k].shape for k in SMALL]
    me1 = 2 * chip1 + core1
    summed = _unpack(_small_finish(small_pending, done_late, me1), small_shapes)
    for k, g in zip(SMALL, summed):
        if k in SMALL_SHARDED:
            ax = SMALL_SHARDED[k]
            size = w[k].shape[ax]
            g = lax.dynamic_slice_in_dim(g, chip * size, size, axis=ax)
        grad[k] = g

    shapes = [w[k].shape for k in SMALL]
    packed = _adamw(*(_pack([src[k] for k in SMALL]) for src in (w, grad, mom, vel)))
    for dst, p in zip((delta, new_m, new_v), packed):
        for k, a in zip(SMALL, _unpack(p, shapes)):
            dst[k] = a

    return (loss, dx[None], *[grad[k] for k in WEIGHTS], *[delta[k] for k in WEIGHTS],
            *[new_m[k] for k in WEIGHTS], *[new_v[k] for k in WEIGHTS])
```

```python
import functools

import jax
import jax.numpy as jnp
from jax import lax
from jax.experimental import pallas as pl
from jax.experimental.pallas import tpu as pltpu

F32 = jnp.float32
BF16 = jnp.bfloat16
EPS = 1e-6
MESH = pl.DeviceIdType.MESH

SSD_CHUNK = 128
SSD_STATE = 128
SSD_HEAD_DIM = 64
HEADS_PER_GROUP = 8
GROUP_W = HEADS_PER_GROUP * SSD_HEAD_DIM
CONV_W = 5
POOL_WINDOWS = (2, 4, 8, 16)
N_CHIPS = 4

ADAM_LR = 0.001
ADAM_B1 = 0.9
ADAM_B2 = 0.999
ADAM_EPS = 1e-08
ADAM_WD = 0.01
ADAM_STEP = 10

VMEM_LIMIT = 56 * 1024 * 1024
LANE = 128


def _cp(*sem):
    return pltpu.CompilerParams(dimension_semantics=sem, vmem_limit_bytes=VMEM_LIMIT)


def _tile(dim, pref, unit=LANE):
    if dim <= pref:
        return dim
    t = (pref // unit) * unit
    while t >= unit:
        if dim % t == 0:
            return t
        t -= unit
    return dim


def _sigmoid(v):
    return 1.0 / (1.0 + jnp.exp(-v))


def _dot(a, b):
    return jnp.dot(a, b, preferred_element_type=F32)


def _dot_nt(a, b):
    return lax.dot_general(a, b, (((1,), (1,)), ((), ())), preferred_element_type=F32)


def _split3(v):
    h1 = v.astype(BF16)
    r1 = v - h1.astype(F32)
    h2 = r1.astype(BF16)
    h3 = (r1 - h2.astype(F32)).astype(BF16)
    return h1, h2, h3


def _dot_exact01(m01, v):
    mb = m01.astype(BF16)
    h1, h2, h3 = _split3(v)
    return _dot(mb, h1) + _dot(mb, h2) + _dot(mb, h3)


def _dot_nt_exact01(m01, v):
    mb = m01.astype(BF16)
    h1, h2, h3 = _split3(v)
    return _dot_nt(mb, h1) + _dot_nt(mb, h2) + _dot_nt(mb, h3)


def _mm(a, b, *, name, ta=False, tb=False, a_sel=(), b_sel=(), pair2=None, add=None, scale=1.0,
        out_dtype=F32, tm=1024, tn=1024, tk=2048):
    am, ak = a.shape[-2:][::-1] if ta else a.shape[-2:]
    bk, bn = b.shape[-2:][::-1] if tb else b.shape[-2:]
    assert ak == bk, (a.shape, b.shape, ta, tb)
    m_dim, n_dim, k_dim = am, bn, ak
    tm, tn, tk = _tile(m_dim, tm), _tile(n_dim, tn), _tile(k_dim, tk)
    nk = k_dim // tk
    grid = (n_dim // tn, m_dim // tm, nk)

    def a_spec(sel):
        lead = (None,) * len(sel)
        if ta:
            return pl.BlockSpec(lead + (tk, tm), lambda n, m, k: tuple(sel) + (k, m))
        return pl.BlockSpec(lead + (tm, tk), lambda n, m, k: tuple(sel) + (m, k))

    def b_spec(sel):
        lead = (None,) * len(sel)
        if tb:
            return pl.BlockSpec(lead + (tn, tk), lambda n, m, k: tuple(sel) + (n, k))
        return pl.BlockSpec(lead + (tk, tn), lambda n, m, k: tuple(sel) + (k, n))

    ins, specs = [a, b], [a_spec(a_sel), b_spec(b_sel)]
    if pair2 is not None:
        a2, b2, a2_sel, b2_sel = pair2
        ins += [a2, b2]
        specs += [a_spec(a2_sel), b_spec(b2_sel)]
    if add is not None:
        ins.append(add)
        specs.append(pl.BlockSpec((tm, tn), lambda n, m, k: (m, n)))
    dn = (((0 if ta else 1,), (1 if tb else 0,)), ((), ()))
    n_pairs = 2 if pair2 is not None else 1

    def body(*refs):
        pairs = [(refs[2 * i], refs[2 * i + 1]) for i in range(n_pairs)]
        pos = 2 * n_pairs
        add_ref = None
        if add is not None:
            add_ref = refs[pos]
            pos += 1
        o_ref = refs[pos]
        acc_ref = refs[pos + 1] if nk > 1 else None

        def prod():
            tot = None
            for ar, br in pairs:
                p = lax.dot_general(ar[...].astype(BF16), br[...].astype(BF16), dn, preferred_element_type=F32)
                tot = p if tot is None else tot + p
            return tot

        def finish(r):
            if scale != 1.0:
                r = r * scale
            if add_ref is not None:
                r = add_ref[...] + r
            o_ref[...] = r.astype(out_dtype)

        if nk == 1:
            finish(prod())
        else:
            k = pl.program_id(2)

            @pl.when(k == 0)
            def _():
                acc_ref[...] = jnp.zeros_like(acc_ref)

            acc_ref[...] += prod()

            @pl.when(k == nk - 1)
            def _():
                finish(acc_ref[...])

    return pl.pallas_call(
        body, name=name, grid=grid, in_specs=specs,
        out_specs=pl.BlockSpec((tm, tn), lambda n, m, k: (m, n)),
        out_shape=jax.ShapeDtypeStruct((m_dim, n_dim), out_dtype),
        scratch_shapes=[pltpu.VMEM((tm, tn), F32)] if nk > 1 else [],
        compiler_params=_cp("parallel", "parallel", "arbitrary"),
    )(*ins)


def _rmsnorm(x, g):
    t_dim, d = x.shape
    tr = _tile(t_dim, 256, 8)

    def body(x_ref, g_ref, o_ref):
        xv = x_ref[...]
        r = lax.rsqrt(jnp.mean(xv * xv, axis=-1, keepdims=True) + EPS)
        o_ref[...] = (xv * r * g_ref[...]).astype(BF16)

    return pl.pallas_call(
        body, name="rmsnorm", grid=(t_dim // tr,),
        in_specs=[pl.BlockSpec((tr, d), lambda i: (i, 0)), pl.BlockSpec((1, d), lambda i: (0, 0))],
        out_specs=pl.BlockSpec((tr, d), lambda i: (i, 0)),
        out_shape=jax.ShapeDtypeStruct((t_dim, d), BF16),
        compiler_params=_cp("parallel"),
    )(x, g)


def _rmsnorm_bwd(x, g, dh, dres):
    t_dim, d = x.shape
    tr = _tile(t_dim, 256, 8)

    def body(x_ref, g_ref, dh_ref, dres_ref, dx_ref, dg_ref):
        i = pl.program_id(0)
        xv = x_ref[...]
        r = lax.rsqrt(jnp.mean(xv * xv, axis=-1, keepdims=True) + EPS)
        n = xv * r
        dhv = dh_ref[...]
        dn = dhv * g_ref[...]

        @pl.when(i == 0)
        def _():
            dg_ref[...] = jnp.zeros_like(dg_ref)

        dg_ref[...] += jnp.sum(dhv * n, axis=0, keepdims=True)
        dx_ref[...] = dres_ref[...] + r * (dn - n * jnp.mean(dn * n, axis=-1, keepdims=True))

    row = pl.BlockSpec((tr, d), lambda i: (i, 0))
    vec = pl.BlockSpec((1, d), lambda i: (0, 0))
    return pl.pallas_call(
        body, name="rmsnorm_bwd", grid=(t_dim // tr,),
        in_specs=[row, vec, row, row], out_specs=[row, vec],
        out_shape=[jax.ShapeDtypeStruct((t_dim, d), F32), jax.ShapeDtypeStruct((1, d), F32)],
        compiler_params=_cp("arbitrary"),
    )(x, g, dh, dres)


def _loss_head(x, g, target):
    t_dim, d = x.shape
    tr = _tile(t_dim, 256, 8)

    def body(x_ref, g_ref, t_ref, loss_ref, dx_ref, dg_ref):
        i = pl.program_id(0)
        xv = x_ref[...]
        gv = g_ref[...]
        r = lax.rsqrt(jnp.mean(xv * xv, axis=-1, keepdims=True) + EPS)
        n = xv * r
        err = n * gv - t_ref[...]

        @pl.when(i == 0)
        def _():
            dg_ref[...] = jnp.zeros_like(dg_ref)
            loss_ref[...] = jnp.zeros_like(loss_ref)

        per_tok = jnp.mean(err * err, axis=-1, keepdims=True)
        loss_ref[...] += 0.5 * jnp.sum(per_tok, axis=0, keepdims=True)
        dy = err * (1.0 / d)
        dn = dy * gv
        dg_ref[...] += jnp.sum(dy * n, axis=0, keepdims=True)
        dx_ref[...] = r * (dn - n * jnp.mean(dn * n, axis=-1, keepdims=True))

    row = pl.BlockSpec((tr, d), lambda i: (i, 0))
    vec = pl.BlockSpec((1, d), lambda i: (0, 0))
    one = pl.BlockSpec((1, 1), lambda i: (0, 0))
    return pl.pallas_call(
        body, name="loss_head", grid=(t_dim // tr,),
        in_specs=[row, vec, row], out_specs=[one, row, vec],
        out_shape=[jax.ShapeDtypeStruct((1, 1), F32), jax.ShapeDtypeStruct((t_dim, d), F32),
                   jax.ShapeDtypeStruct((1, d), F32)],
        compiler_params=_cp("arbitrary"),
    )(x, g, target)


def _ffn_in(h, wg, wu, half):
    t_dim, d = h.shape
    f = wg.shape[-1]
    tm, tn = _tile(t_dim, 512), _tile(f, 1408)

    def body(h_ref, wg_ref, wu_ref, g_ref, u_ref, a_ref):
        hv = h_ref[...]
        gv = _dot(hv, wg_ref[...])
        uv = _dot(hv, wu_ref[...])
        g_ref[...] = gv.astype(BF16)
        u_ref[...] = uv.astype(BF16)
        a_ref[...] = (gv * _sigmoid(gv) * uv).astype(BF16)

    wspec = pl.BlockSpec((None, d, tn), lambda n, m: (half, 0, n))
    ospec = pl.BlockSpec((tm, tn), lambda n, m: (m, n))
    oshape = jax.ShapeDtypeStruct((t_dim, f), BF16)
    return pl.pallas_call(
        body, name="ffn_in", grid=(f // tn, t_dim // tm),
        in_specs=[pl.BlockSpec((tm, d), lambda n, m: (m, 0)), wspec, wspec],
        out_specs=[ospec, ospec, ospec], out_shape=[oshape, oshape, oshape],
        compiler_params=_cp("parallel", "parallel"),
    )(h, wg, wu)


def _ffn_bwd_act(dx, wd, g, u, half):
    t_dim, d = dx.shape
    f = wd.shape[-2]
    tm, tn = _tile(t_dim, 512), _tile(f, 1408)

    def body(dx_ref, wd_ref, g_ref, u_ref, dg_ref, du_ref):
        da = 0.5 * _dot_nt(dx_ref[...].astype(BF16), wd_ref[...])
        gv = g_ref[...].astype(F32)
        uv = u_ref[...].astype(F32)
        s = _sigmoid(gv)
        dg_ref[...] = (da * uv * (s * (1.0 + gv * (1.0 - s)))).astype(BF16)
        du_ref[...] = (da * gv * s).astype(BF16)

    tile = pl.BlockSpec((tm, tn), lambda n, m: (m, n))
    oshape = jax.ShapeDtypeStruct((t_dim, f), BF16)
    return pl.pallas_call(
        body, name="ffn_bwd_act", grid=(f // tn, t_dim // tm),
        in_specs=[pl.BlockSpec((tm, d), lambda n, m: (m, 0)),
                  pl.BlockSpec((None, tn, d), lambda n, m: (half, n, 0)), tile, tile],
        out_specs=[tile, tile], out_shape=[oshape, oshape],
        compiler_params=_cp("parallel", "parallel"),
    )(dx, wd, g, u)


def _ffn_fwd(x, norm_g, wg, wu, wd, half):
    h = _rmsnorm(x, norm_g)
    g, u, a = _ffn_in(h, wg, wu, half)
    x_new = _mm(a, wd, name="ffn_out", b_sel=(half,), add=x, scale=0.5, tk=1408)
    return x_new, (x, h, g, u, a)


def _ffn_bwd(dx, saved, norm_g, wg, wu, wd, half):
    x, h, g, u, a = saved
    dg, du = _ffn_bwd_act(dx, wd, g, u, half)
    d_wd = _mm(a, dx, name="ffn_dwd", ta=True, scale=0.5, out_dtype=BF16, tm=1408, tn=1024)
    d_wg = _mm(h, dg, name="ffn_dwgu", ta=True, out_dtype=BF16, tm=1024, tn=1408)
    d_wu = _mm(h, du, name="ffn_dwgu", ta=True, out_dtype=BF16, tm=1024, tn=1408)
    dh = _mm(dg, wg, name="ffn_dh", tb=True, b_sel=(half,), pair2=(du, wu, (), (half,)), tk=1408)
    dx_new, dnorm = _rmsnorm_bwd(x, norm_g, dh, dx)
    return dx_new, dnorm, d_wg, d_wu, d_wd


def _shifted(v, off, t_idx):
    if off == 0:
        return v
    t_dim = v.shape[0]
    sh = pltpu.roll(v, (-off) % t_dim, 0)
    valid = jnp.logical_and(t_idx + off >= 0, t_idx + off < t_dim)
    return jnp.where(valid, sh, 0.0)


def _conv_pre(u, w_ref, b_ref, t_idx):
    acc = jnp.zeros_like(u) + b_ref[...]
    shifted = []
    for k in range(CONV_W):
        sh = _shifted(u, k - CONV_W // 2, t_idx)
        shifted.append(sh)
        acc = acc + w_ref[k:k + 1, :] * sh
    return acc, shifted


def _conv_silu(proj, conv_w, conv_b, col0):
    t_dim = proj.shape[0]
    cd = conv_w.shape[-1]
    cb = _tile(cd, 256)
    assert col0 % cb == 0

    def body(u_ref, w_ref, b_ref, o_ref):
        t_idx = lax.broadcasted_iota(jnp.int32, (t_dim, cb), 0)
        pre, _ = _conv_pre(u_ref[...], w_ref, b_ref, t_idx)
        o_ref[...] = pre * _sigmoid(pre)

    return pl.pallas_call(
        body, name="conv_silu", grid=(cd // cb,),
        in_specs=[pl.BlockSpec((t_dim, cb), lambda j: (0, col0 // cb + j)),
                  pl.BlockSpec((CONV_W, cb), lambda j: (0, j)), pl.BlockSpec((1, cb), lambda j: (0, j))],
        out_specs=pl.BlockSpec((t_dim, cb), lambda j: (0, j)),
        out_shape=jax.ShapeDtypeStruct((t_dim, cd), F32),
        compiler_params=_cp("parallel"),
    )(proj, conv_w, conv_b)


def _conv_silu_bwd(proj, conv_w, conv_b, dact2, col0):
    t_dim = proj.shape[0]
    cd = conv_w.shape[-1]
    cb = _tile(cd, 256)

    def body(u_ref, w_ref, b_ref, da_ref, du_ref, dwb_ref):
        t_idx = lax.broadcasted_iota(jnp.int32, (t_dim, cb), 0)
        pre, shifted = _conv_pre(u_ref[...], w_ref, b_ref, t_idx)
        s = _sigmoid(pre)
        dpre = (da_ref[0] + da_ref[1]) * (s * (1.0 + pre * (1.0 - s)))
        du = jnp.zeros_like(dpre)
        for k in range(CONV_W):
            du = du + w_ref[k:k + 1, :] * _shifted(dpre, -(k - CONV_W // 2), t_idx)
            dwb_ref[k:k + 1, :] = jnp.sum(dpre * shifted[k], axis=0, keepdims=True)
        dwb_ref[CONV_W:CONV_W + 1, :] = jnp.sum(dpre, axis=0, keepdims=True)
        dwb_ref[CONV_W + 1:8, :] = jnp.zeros((8 - CONV_W - 1, cb), F32)
        du_ref[...] = du.astype(BF16)

    return pl.pallas_call(
        body, name="conv_silu_bwd", grid=(cd // cb,),
        in_specs=[pl.BlockSpec((t_dim, cb), lambda j: (0, col0 // cb + j)),
                  pl.BlockSpec((CONV_W, cb), lambda j: (0, j)), pl.BlockSpec((1, cb), lambda j: (0, j)),
                  pl.BlockSpec((2, t_dim, cb), lambda j: (0, 0, j))],
        out_specs=[pl.BlockSpec((t_dim, cb), lambda j: (0, j)), pl.BlockSpec((8, cb), lambda j: (0, j))],
        out_shape=[jax.ShapeDtypeStruct((t_dim, cd), BF16), jax.ShapeDtypeStruct((8, cd), F32)],
        compiler_params=_cp("parallel"),
    )(proj, conv_w, conv_b, dact2)


def _softplus_fwd(dt_raw, bias):
    def body(r_ref, b_ref, o_ref):
        v = r_ref[...] + b_ref[...]
        o_ref[...] = jnp.maximum(v, 0.0) + jnp.log(1.0 + jnp.exp(-jnp.abs(v)))

    return pl.pallas_call(body, name="softplus", out_shape=jax.ShapeDtypeStruct(dt_raw.shape, F32))(dt_raw, bias)


def _softplus_bwd(dt_raw, bias, ddt, da, a_log):
    def body(r_ref, b_ref, ddt_ref, da_ref, al_ref, dr_ref, db_ref, dal_ref):
        dv = ddt_ref[...] * _sigmoid(r_ref[...] + b_ref[...])
        dr_ref[...] = dv.astype(BF16)
        db_ref[...] = jnp.sum(dv, axis=0, keepdims=True)
        dal_ref[...] = -da_ref[...] * jnp.exp(al_ref[...])

    vec = jax.ShapeDtypeStruct(bias.shape, F32)
    return pl.pallas_call(
        body, name="softplus_bwd",
        out_shape=[jax.ShapeDtypeStruct(dt_raw.shape, BF16), vec, vec])(dt_raw, bias, ddt, da, a_log)


def _chunk_setup(d, dt_ref, al_ref):
    q = SSD_CHUNK
    ii = lax.broadcasted_iota(jnp.int32, (q, q), 0)
    jj = lax.broadcasted_iota(jnp.int32, (q, q), 1)
    sgn = 1 - 2 * d
    mask = (jj - ii) * sgn <= 0
    mask_t = (ii - jj) * sgn <= 0
    m01 = mask.astype(F32)
    m01_t = mask_t.astype(F32)
    dt = dt_ref[...]
    a = -jnp.exp(al_ref[...])
    dta = dt * a
    cs = _dot_exact01(m01, dta)
    tot = jnp.sum(dta, axis=0, keepdims=True)
    return mask, mask_t, m01, m01_t, dt, a, dta, cs, tot


def _head_lanes():
    hh = lax.broadcasted_iota(jnp.int32, (HEADS_PER_GROUP, GROUP_W), 0)
    ll = lax.broadcasted_iota(jnp.int32, (HEADS_PER_GROUP, GROUP_W), 1)
    return jnp.logical_and(ll >= hh * SSD_HEAD_DIM, ll < (hh + 1) * SSD_HEAD_DIM).astype(BF16)


def _expand(v, e16):
    h1, h2, h3 = _split3(v)
    return _dot(h1, e16) + _dot(h2, e16) + _dot(h3, e16)


def _expand_row(v, e16):
    return _expand(jnp.broadcast_to(v, (8, HEADS_PER_GROUP)), e16)[0:1]


def _per_head(v, e16):
    h1, h2, h3 = _split3(v)
    return _dot_nt(h1, e16) + _dot_nt(h2, e16) + _dot_nt(h3, e16)


def _rows_of(cs):
    n = cs.shape[1]
    eye = lax.broadcasted_iota(jnp.int32, (n, n), 0) == lax.broadcasted_iota(jnp.int32, (n, n), 1)
    return _dot_nt_exact01(eye.astype(F32), cs)


def _ssd_specs(t_dim, di, g_cnt, chunk_of):
    q, ns = SSD_CHUNK, SSD_STATE
    x_spec = pl.BlockSpec((q, GROUP_W), lambda d, g, c: (chunk_of(d, c), g))
    b_spec = pl.BlockSpec((q, ns), lambda d, g, c: (chunk_of(d, c), di // ns + g))
    c_spec = pl.BlockSpec((q, ns), lambda d, g, c: (chunk_of(d, c), di // ns + g_cnt + g))
    dt_spec = pl.BlockSpec((None, None, q, HEADS_PER_GROUP), lambda d, g, c: (d, g, chunk_of(d, c), 0))
    al_spec = pl.BlockSpec((None, None, 1, HEADS_PER_GROUP), lambda d, g, c: (d, g, 0, 0))
    return x_spec, b_spec, c_spec, dt_spec, al_spec


def _ssd_fwd(xbc, dt4, al4):
    t_dim = xbc.shape[0]
    g_cnt = dt4.shape[1]
    di = g_cnt * GROUP_W
    q, ns, p = SSD_CHUNK, SSD_STATE, SSD_HEAD_DIM
    nc = t_dim // q

    def chunk_of(d, c):
        return c + d * (nc - 1 - 2 * c)

    def body(x_ref, b_ref, c_ref, dt_ref, al_ref, y_ref, hin_ref, h_sc):
        d = pl.program_id(0)
        c = pl.program_id(2)

        @pl.when(c == 0)
        def _():
            h_sc[...] = jnp.zeros_like(h_sc)

        mask, mask_t, m01, m01_t, dt, a, dta, cs, tot = _chunk_setup(d, dt_ref, al_ref)
        e16 = _head_lanes()
        cs_rows = _rows_of(cs)
        cb16 = c_ref[...].astype(BF16)
        bt16 = b_ref[...].T.astype(BF16)
        cb = _dot(cb16, bt16)
        hin = h_sc[...]
        hin_ref[...] = hin
        xdt = x_ref[...] * _expand(dt, e16)
        y_off = _dot(cb16, hin.astype(BF16)) * _expand(jnp.exp(cs), e16)
        low = lax.broadcasted_iota(jnp.int32, (q, 2 * p), 1) < p
        for pair in range(HEADS_PER_GROUP // 2):
            ps = slice(2 * p * pair, 2 * p * (pair + 1))
            blk = xdt[:, ps]
            acc = y_off[:, ps]
            for hh in range(2):
                h = 2 * pair + hh
                lmat = jnp.exp(jnp.where(mask, cs[:, h:h + 1] - cs_rows[h:h + 1, :], -1e30))
                xm = jnp.where(low, blk, 0.0) if hh == 0 else jnp.where(low, 0.0, blk)
                acc = acc + _dot((cb * lmat).astype(BF16), xm.astype(BF16))
            y_ref[:, ps] = acc
        xd = xdt * _expand(jnp.exp(tot - cs), e16)
        st = _dot(bt16, xd.astype(BF16))
        h_sc[...] = hin * _expand_row(jnp.exp(tot), e16) + st

    x_spec, b_spec, c_spec, dt_spec, al_spec = _ssd_specs(t_dim, di, g_cnt, chunk_of)
    return pl.pallas_call(
        body, name="ssd_fwd", grid=(2, g_cnt, nc),
        in_specs=[x_spec, b_spec, c_spec, dt_spec, al_spec],
        out_specs=[pl.BlockSpec((None, q, GROUP_W), lambda d, g, c: (d, chunk_of(d, c), g)),
                   pl.BlockSpec((None, None, None, ns, GROUP_W), lambda d, g, c: (d, g, chunk_of(d, c), 0, 0))],
        out_shape=[jax.ShapeDtypeStruct((2, t_dim, di), F32),
                   jax.ShapeDtypeStruct((2, g_cnt, nc, ns, GROUP_W), F32)],
        scratch_shapes=[pltpu.VMEM((ns, GROUP_W), F32)],
        compiler_params=_cp("parallel", "parallel", "arbitrary"),
    )(xbc, xbc, xbc, dt4, al4)


def _ssd_bwd(xbc, dt4, al4, hin_all, dy, dvec):
    t_dim = xbc.shape[0]
    g_cnt = dt4.shape[1]
    di = g_cnt * GROUP_W
    q, ns, p, hg = SSD_CHUNK, SSD_STATE, SSD_HEAD_DIM, HEADS_PER_GROUP
    nc = t_dim // q

    def chunk_of(d, c):
        return (nc - 1 - c) + d * (2 * c - nc + 1)

    def body(x_ref, b_ref, c_ref, dt_ref, al_ref, hin_ref, dy_ref, dv_ref,
             dx_ref, db_ref, dc_ref, ddt_ref, da_ref, g_sc, dxdt_sc, zrow_sc):
        d = pl.program_id(0)
        c = pl.program_id(2)

        @pl.when(c == 0)
        def _():
            g_sc[...] = jnp.zeros_like(g_sc)
            da_ref[...] = jnp.zeros_like(da_ref)

        mask, mask_t, m01, m01_t, dt, a, dta, cs, tot = _chunk_setup(d, dt_ref, al_ref)
        xv = x_ref[...]
        dyv = dy_ref[...]
        bb = b_ref[...].astype(BF16)
        cb16 = c_ref[...].astype(BF16)
        bt16 = b_ref[...].T.astype(BF16)
        ct16 = c_ref[...].T.astype(BF16)
        cb = _dot(cb16, bt16)
        cbt = _dot(bb, ct16)
        hin = hin_ref[...]
        hin16 = hin.astype(BF16)
        gst = g_sc[...]
        gst16 = gst.astype(BF16)
        ch = _dot(cb16, hin16)
        wst = _dot(bb, gst16)
        skip = jnp.where(d == 0, 1.0, 0.0)
        e16 = _head_lanes()
        cs_rows = _rows_of(cs)
        dt_x = _expand(dt, e16)
        e_x = _expand(jnp.exp(cs), e16)
        dec = jnp.exp(tot - cs)
        etot = jnp.exp(tot)
        xdt = xv * dt_x
        dye = dyv * e_x
        t1 = _per_head(wst * xdt, e16) * dec
        dcs = _per_head(dye * ch, e16) - t1
        dtot = jnp.sum(t1, axis=0, keepdims=True)
        dec_x = _expand(dec, e16)
        dxdt_state = wst * dec_x
        dcb = jnp.zeros((q, q), F32)
        low = lax.broadcasted_iota(jnp.int32, (q, 2 * p), 1) < p
        col8 = lax.broadcasted_iota(jnp.int32, (q, hg), 1)
        for pair in range(hg // 2):
            ps = slice(2 * p * pair, 2 * p * (pair + 1))
            xblk, dyblk = xdt[:, ps], dyv[:, ps]
            acc = dxdt_state[:, ps]
            for hh in range(2):
                h = 2 * pair + hh
                seg = cs[:, h:h + 1] - cs_rows[h:h + 1, :]
                lmat = jnp.exp(jnp.where(mask, seg, -1e30))
                lmat_t = jnp.exp(jnp.where(mask_t, -seg, -1e30))
                if hh == 0:
                    xm, dym = jnp.where(low, xblk, 0.0).astype(BF16), jnp.where(low, dyblk, 0.0).astype(BF16)
                else:
                    xm, dym = jnp.where(low, 0.0, xblk).astype(BF16), jnp.where(low, 0.0, dyblk).astype(BF16)
                acc = acc + _dot((cbt * lmat_t).astype(BF16), dym)
                dm = _dot_nt(dym, xm)
                z = dm * (cb * lmat)
                dcb = dcb + dm * lmat
                dcs = dcs + jnp.where(col8 == h, jnp.sum(z, axis=-1, keepdims=True), 0.0)
                zrow_sc[h:h + 1, :] = jnp.sum(z, axis=0, keepdims=True)
            dxdt_sc[:, ps] = acc
        dxdt = dxdt_sc[...]
        dx_ref[...] = dxdt * dt_x + skip * dyv * dv_ref[...]
        dye16 = dye.astype(BF16)
        dcb16 = dcb.astype(BF16)
        dc_ref[...] = _dot_nt(dye16, hin16) + _dot(dcb16, bb)
        db_ref[...] = _dot_nt((xdt * dec_x).astype(BF16), gst16) + _dot(dcb.T.astype(BF16), cb16)
        g_sc[...] = _dot(ct16, dye16) + gst * _expand_row(etot, e16)
        carried = jnp.broadcast_to(jnp.sum(gst * hin, axis=0, keepdims=True), (8, GROUP_W))
        dtot = dtot + _per_head(carried, e16)[0:1] * etot
        ddta = _dot_exact01(m01_t, dcs) - _dot_nt_exact01(m01_t, zrow_sc[...]) + dtot
        ddt_ref[...] = _per_head(dxdt * xv, e16) + ddta * a
        da_ref[...] += jnp.sum(ddta * dt, axis=0, keepdims=True)

    x_spec, b_spec, c_spec, dt_spec, al_spec = _ssd_specs(t_dim, di, g_cnt, chunk_of)
    hin_spec = pl.BlockSpec((None, None, None, ns, GROUP_W), lambda d, g, c: (d, g, chunk_of(d, c), 0, 0))
    dy_spec = pl.BlockSpec((q, GROUP_W), lambda d, g, c: (chunk_of(d, c), g))
    dv_spec = pl.BlockSpec((1, GROUP_W), lambda d, g, c: (0, g))
    gn = g_cnt * ns
    return pl.pallas_call(
        body, name="ssd_bwd", grid=(2, g_cnt, nc),
        in_specs=[x_spec, b_spec, c_spec, dt_spec, al_spec, hin_spec, dy_spec, dv_spec],
        out_specs=[pl.BlockSpec((None, q, GROUP_W), lambda d, g, c: (d, chunk_of(d, c), g)),
                   pl.BlockSpec((None, q, ns), lambda d, g, c: (d, chunk_of(d, c), g)),
                   pl.BlockSpec((None, q, ns), lambda d, g, c: (d, chunk_of(d, c), g)),
                   pl.BlockSpec((None, None, q, hg), lambda d, g, c: (d, g, chunk_of(d, c), 0)),
                   pl.BlockSpec((None, None, 1, hg), lambda d, g, c: (d, g, 0, 0))],
        out_shape=[jax.ShapeDtypeStruct((2, t_dim, di), F32), jax.ShapeDtypeStruct((2, t_dim, gn), F32),
                   jax.ShapeDtypeStruct((2, t_dim, gn), F32), jax.ShapeDtypeStruct((2, g_cnt, t_dim, hg), F32),
                   jax.ShapeDtypeStruct((2, g_cnt, 1, hg), F32)],
        scratch_shapes=[pltpu.VMEM((ns, GROUP_W), F32), pltpu.VMEM((q, GROUP_W), F32), pltpu.VMEM((hg, q), F32)],
        compiler_params=_cp("parallel", "parallel", "arbitrary"),
    )(xbc, xbc, xbc, dt4, al4, hin_all, dy, dvec)


def _gate_norm(y2, xbc, proj, dvec, ng):
    t_dim, di = y2.shape[1:]
    tr = _tile(t_dim, 128, 8)

    def body(y2_ref, x_ref, z_ref, dv_ref, ng_ref, o_ref):
        y = y2_ref[0] + y2_ref[1] + x_ref[...] * dv_ref[...]
        z = z_ref[...]
        v = y * z * _sigmoid(z)
        r = lax.rsqrt(jnp.mean(v * v, axis=-1, keepdims=True) + EPS)
        o_ref[...] = (v * r * ng_ref[...]).astype(BF16)

    row = pl.BlockSpec((tr, di), lambda i: (i, 0))
    vec = pl.BlockSpec((1, di), lambda i: (0, 0))
    return pl.pallas_call(
        body, name="gate_norm", grid=(t_dim // tr,),
        in_specs=[pl.BlockSpec((2, tr, di), lambda i: (0, i, 0)), row, row, vec, vec],
        out_specs=row, out_shape=jax.ShapeDtypeStruct((t_dim, di), BF16),
        compiler_params=_cp("parallel"),
    )(y2, xbc, proj, dvec, ng)


def _gate_norm_bwd(y2, xbc, proj, dvec, ng, dyn):
    t_dim, di = y2.shape[1:]
    tr = _tile(t_dim, 128, 8)

    def body(y2_ref, x_ref, z_ref, dv_ref, ng_ref, dyn_ref, dy_ref, dz_ref, dng_ref, dd_ref):
        i = pl.program_id(0)
        xv = x_ref[...]
        y = y2_ref[0] + y2_ref[1] + xv * dv_ref[...]
        z = z_ref[...]
        s = _sigmoid(z)
        v = y * z * s
        r = lax.rsqrt(jnp.mean(v * v, axis=-1, keepdims=True) + EPS)
        n = v * r
        dynv = dyn_ref[...]
        dn = dynv * ng_ref[...]
        dv = r * (dn - n * jnp.mean(dn * n, axis=-1, keepdims=True))
        dy = dv * z * s

        @pl.when(i == 0)
        def _():
            dng_ref[...] = jnp.zeros_like(dng_ref)
            dd_ref[...] = jnp.zeros_like(dd_ref)

        dng_ref[...] += jnp.sum(dynv * n, axis=0, keepdims=True)
        dd_ref[...] += jnp.sum(dy * xv, axis=0, keepdims=True)
        dy_ref[...] = dy
        dz_ref[...] = (dv * y * (s * (1.0 + z * (1.0 - s)))).astype(BF16)

    row = pl.BlockSpec((tr, di), lambda i: (i, 0))
    vec = pl.BlockSpec((1, di), lambda i: (0, 0))
    return pl.pallas_call(
        body, name="gate_norm_bwd", grid=(t_dim // tr,),
        in_specs=[pl.BlockSpec((2, tr, di), lambda i: (0, i, 0)), row, row, vec, vec, row],
        out_specs=[row, row, vec, vec],
        out_shape=[jax.ShapeDtypeStruct((t_dim, di), F32), jax.ShapeDtypeStruct((t_dim, di), BF16),
                   jax.ShapeDtypeStruct((1, di), F32), jax.ShapeDtypeStruct((1, di), F32)],
        compiler_params=_cp("arbitrary"),
    )(y2, xbc, proj, dvec, ng, dyn)


def _dt_to_groups(dt):
    t_dim, h2 = dt.shape
    g_cnt = h2 // 2 // HEADS_PER_GROUP
    return dt.reshape(t_dim, 2, g_cnt, HEADS_PER_GROUP).transpose(1, 2, 0, 3)


def _dt_from_groups(dt4):
    _, g_cnt, t_dim, hg = dt4.shape
    return dt4.transpose(2, 0, 1, 3).reshape(t_dim, 2 * g_cnt * hg)


def _ssd_mixer_fwd(x, norm_g, w_in, conv_w, conv_b, dt_bias, a_log, d_skip, ssd_norm, w_out, j):
    heads = d_skip.shape[0]
    di = heads * SSD_HEAD_DIM
    g_cnt = heads // HEADS_PER_GROUP
    cd = conv_w.shape[-1]
    hn = _rmsnorm(x, norm_g)
    proj = _mm(hn, w_in, name="ssd_proj", tb=True, b_sel=(j,), tn=1152)
    xbc = _conv_silu(proj, conv_w, conv_b, di)
    dt_raw = proj[:, di + cd:]
    dt = _softplus_fwd(dt_raw, dt_bias)
    dt4 = _dt_to_groups(dt)
    al4 = a_log.reshape(2, g_cnt, 1, HEADS_PER_GROUP)
    y2, hin = _ssd_fwd(xbc, dt4, al4)
    dvec = jnp.repeat(d_skip, SSD_HEAD_DIM).reshape(1, di)
    yn = _gate_norm(y2, xbc, proj, dvec, ssd_norm)
    x_new = _mm(yn, w_out, name="ssd_out", b_sel=(j,), add=x)
    return x_new, (x, hn, proj, xbc, dt_raw, dt4, al4, y2, hin, dvec, yn)


def _ssd_mixer_bwd(dx, saved, norm_g, w_in, conv_w, conv_b, dt_bias, a_log, ssd_norm, w_out, j):
    x, hn, proj, xbc, dt_raw, dt4, al4, y2, hin, dvec, yn = saved
    di = dvec.shape[1]
    heads = di // SSD_HEAD_DIM
    d_wout = _mm(yn, dx, name="ssd_dwout", ta=True, out_dtype=BF16)
    dyn = _mm(dx, w_out, name="ssd_dyn", tb=True, b_sel=(j,))
    dy, dz, d_ng, dd_col = _gate_norm_bwd(y2, xbc, proj, dvec, ssd_norm, dyn)
    dx2, db2, dc2, ddt4, da4 = _ssd_bwd(xbc, dt4, al4, hin, dy, dvec)
    dact2 = jnp.concatenate([dx2, db2, dc2], axis=-1)
    dxbc, dwb = _conv_silu_bwd(proj, conv_w, conv_b, dact2, di)
    ddt_raw, d_bias, d_alog = _softplus_bwd(dt_raw, dt_bias, _dt_from_groups(ddt4), da4.reshape(1, 2 * heads), a_log)
    dproj = jnp.concatenate([dz, dxbc, ddt_raw], axis=-1)
    d_win = _mm(dproj, hn, name="ssd_dwin", ta=True, out_dtype=BF16, tm=1152)
    dhn = _mm(dproj, w_in, name="ssd_dhn", b_sel=(j,), tk=1152)
    dx_new, d_norm = _rmsnorm_bwd(x, norm_g, dhn, dx)
    small = dict(mix_norm=d_norm, conv_w=dwb[:CONV_W], conv_b=dwb[CONV_W:CONV_W + 1], dt_bias=d_bias, a_log=d_alog,
                 ssd_d=dd_col.reshape(heads, SSD_HEAD_DIM).sum(axis=1), ssd_norm=d_ng)
    return dx_new, small, d_win, d_wout


def _pool_count(t_idx, w, t_dim):
    hi = jnp.minimum(t_idx + w // 2, t_dim)
    lo = jnp.maximum(t_idx - w // 2, 0)
    return (hi - lo).astype(F32)


def _pool_mix(u, transpose):
    t_dim, d = u.shape
    gd = d // len(POOL_WINDOWS)
    cb = _tile(gd, 256)
    per = gd // cb

    def body(u_ref, o_ref):
        gi = pl.program_id(0)
        t_idx = lax.broadcasted_iota(jnp.int32, (t_dim, cb), 0)
        uv = u_ref[...]
        for widx, w in enumerate(POOL_WINDOWS):
            @pl.when(gi == widx)
            def _(w=w):
                cnt = _pool_count(t_idx, w, t_dim)
                src = uv / cnt if transpose else uv
                acc = jnp.zeros_like(uv)
                for k in range(-(w // 2), w // 2):
                    acc = acc + _shifted(src, -k if transpose else k, t_idx)
                res = acc - uv if transpose else acc / cnt - uv
                o_ref[...] = res.astype(BF16)

    spec = pl.BlockSpec((t_dim, cb), lambda gi, j: (0, gi * per + j))
    return pl.pallas_call(
        body, name="pool_mix_t" if transpose else "pool_mix", grid=(len(POOL_WINDOWS), per),
        in_specs=[spec], out_specs=spec, out_shape=jax.ShapeDtypeStruct((t_dim, d), BF16),
        compiler_params=_cp("parallel", "parallel"),
    )(u)


def _pool_group(mix, wgrp, scale, j):
    t_dim, d = mix.shape
    gd = wgrp.shape[-1]
    tm = _tile(t_dim, 512)

    def body(m_ref, w_ref, s_ref, v_ref, vs_ref):
        v = _dot(m_ref[...], w_ref[...])
        v_ref[...] = v
        vs_ref[...] = (v * s_ref[...]).astype(BF16)

    tile = pl.BlockSpec((tm, gd), lambda gi, m: (m, gi))
    return pl.pallas_call(
        body, name="pool_group", grid=(d // gd, t_dim // tm),
        in_specs=[tile, pl.BlockSpec((None, None, gd, gd), lambda gi, m: (j, gi, 0, 0)),
                  pl.BlockSpec((1, gd), lambda gi, m: (0, gi))],
        out_specs=[tile, tile],
        out_shape=[jax.ShapeDtypeStruct((t_dim, d), F32), jax.ShapeDtypeStruct((t_dim, d), BF16)],
        compiler_params=_cp("parallel", "parallel"),
    )(mix, wgrp, scale)


def _pool_group_bwd(dvs, v, mix, wgrp, scale, j):
    t_dim, d = mix.shape
    gd = wgrp.shape[-1]
    n_g = d // gd
    tm = _tile(t_dim, 512)
    nm = t_dim // tm

    def body(dvs_ref, v_ref, m_ref, w_ref, s_ref, dmix_ref, ds_ref, dw_ref, acc_ref):
        m = pl.program_id(1)
        dvsv = dvs_ref[...]

        @pl.when(m == 0)
        def _():
            ds_ref[...] = jnp.zeros_like(ds_ref)
            acc_ref[...] = jnp.zeros_like(acc_ref)

        ds_ref[...] += jnp.sum(dvsv * v_ref[...], axis=0, keepdims=True)
        dv16 = (dvsv * s_ref[...]).astype(BF16)
        dmix_ref[...] = _dot_nt(dv16, w_ref[...])
        acc_ref[...] += _dot(m_ref[...].T, dv16)

        @pl.when(m == nm - 1)
        def _():
            dw_ref[...] = acc_ref[...].astype(BF16)

    tile = pl.BlockSpec((tm, gd), lambda gi, m: (m, gi))
    vec = pl.BlockSpec((1, gd), lambda gi, m: (0, gi))
    return pl.pallas_call(
        body, name="pool_group_bwd", grid=(n_g, nm),
        in_specs=[tile, tile, tile, pl.BlockSpec((None, None, gd, gd), lambda gi, m: (j, gi, 0, 0)), vec],
        out_specs=[tile, vec, pl.BlockSpec((None, gd, gd), lambda gi, m: (gi, 0, 0))],
        out_shape=[jax.ShapeDtypeStruct((t_dim, d), F32), jax.ShapeDtypeStruct((1, d), F32),
                   jax.ShapeDtypeStruct((n_g, gd, gd), BF16)],
        scratch_shapes=[pltpu.VMEM((gd, gd), F32)],
        compiler_params=_cp("parallel", "arbitrary"),
    )(dvs, v, mix, wgrp, scale)


def _pool_mixer_fwd(x, norm_g, w_in, wgrp, scale, w_out, j):
    hn = _rmsnorm(x, norm_g)
    u = _mm(hn, w_in, name="pool_u", b_sel=(j,))
    mix = _pool_mix(u, False)
    v, vs = _pool_group(mix, wgrp, scale, j)
    x_new = _mm(vs, w_out, name="pool_out", b_sel=(j,), add=x)
    return x_new, (x, hn, mix, v, vs)


def _pool_mixer_bwd(dx, saved, norm_g, w_in, wgrp, scale, w_out, j):
    x, hn, mix, v, vs = saved
    d_wout = _mm(vs, dx, name="pool_dw", ta=True, out_dtype=BF16)
    dvs = _mm(dx, w_out, name="pool_dvs", tb=True, b_sel=(j,))
    dmix, d_scale, d_wgrp = _pool_group_bwd(dvs, v, mix, wgrp, scale, j)
    du = _pool_mix(dmix, True)
    d_win = _mm(hn, du, name="pool_dw", ta=True, out_dtype=BF16)
    dhn = _mm(du, w_in, name="pool_dhn", tb=True, b_sel=(j,))
    dx_new, d_norm = _rmsnorm_bwd(x, norm_g, dhn, dx)
    return dx_new, d_norm, d_scale, d_win, d_wgrp, d_wout


def _local_step(x, target, depth, fetch, emit, on_loss, small, tok):
    saved, wts = [], {}

    def weights(group, after):
        if group not in wts:
            wts[group] = fetch(group, after)
        return wts[group]

    for i in range(depth):
        j = i // 2
        x, s0 = _ffn_fwd(x, small["ffn_norm"][i, 0][None] + tok, *weights(("ffn", i, 0), x), 0)
        mg = small["mix_norm"][i][None]
        if i % 2 == 0:
            heads = small["ssd_d"].shape[1]
            w_in, w_out = weights(("ssd", j), x)
            x, s1 = _ssd_mixer_fwd(x, mg, w_in, small["ssd_conv_w"][j], small["ssd_conv_b"][j][None],
                                   small["ssd_dt_bias"][j].reshape(1, 2 * heads), small["ssd_a_log"][j].reshape(1, 2 * heads),
                                   small["ssd_d"][j], small["ssd_norm"][j][None], w_out, 0)
        else:
            p_in, p_grp, p_out = weights(("pool", j), x)
            x, s1 = _pool_mixer_fwd(x, mg, p_in, p_grp, small["pool_scale"][j][None], p_out, 0)
        x, s2 = _ffn_fwd(x, small["ffn_norm"][i, 1][None], *weights(("ffn", i, 1), x), 0)
        saved.append((s0, s1, s2))
    loss, dx, d_final = _loss_head(x, small["final_norm"][None], target)

    gs = {k: {} for k in ("ffn_norm", "mix_norm", "ssd_conv_w", "ssd_conv_b", "ssd_dt_bias", "ssd_a_log", "ssd_d",
                          "ssd_norm", "pool_scale")}
    tok = on_loss(loss)
    for i in reversed(range(depth)):
        j = i // 2
        s0, s1, s2 = saved[i]
        for half, sv in ((1, s2), (0, None)):
            if half == 0:
                sv = s0
                mg = small["mix_norm"][i][None] + tok
                if i % 2 == 0:
                    heads = small["ssd_d"].shape[1]
                    w_in, w_out = wts[("ssd", j)]
                    dx, sm, d_win, d_wout = _ssd_mixer_bwd(
                        dx, s1, mg, w_in, small["ssd_conv_w"][j], small["ssd_conv_b"][j][None],
                        small["ssd_dt_bias"][j].reshape(1, 2 * heads), small["ssd_a_log"][j].reshape(1, 2 * heads),
                        small["ssd_norm"][j][None], w_out, 0)
                    gs["mix_norm"][i] = sm["mix_norm"][0]
                    gs["ssd_conv_w"][j] = sm["conv_w"]
                    gs["ssd_conv_b"][j] = sm["conv_b"][0]
                    gs["ssd_dt_bias"][j] = sm["dt_bias"].reshape(2, heads)
                    gs["ssd_a_log"][j] = sm["a_log"].reshape(2, heads)
                    gs["ssd_d"][j] = sm["ssd_d"]
                    gs["ssd_norm"][j] = sm["ssd_norm"][0]
                    tok = tok + emit(("ssd", j), [d_win, d_wout])
                else:
                    p_in, p_grp, p_out = wts[("pool", j)]
                    dx, d_norm, d_scale, d_win, d_wgrp, d_wout = _pool_mixer_bwd(
                        dx, s1, mg, p_in, p_grp, small["pool_scale"][j][None], p_out, 0)
                    gs["mix_norm"][i] = d_norm[0]
                    gs["pool_scale"][j] = d_scale[0]
                    tok = tok + emit(("pool", j), [d_win, d_wgrp, d_wout])
            dx, d_norm, d_wg, d_wu, d_wd = _ffn_bwd(dx, sv, small["ffn_norm"][i, half][None] + tok,
                                                   *wts[("ffn", i, half)], 0)
            gs["ffn_norm"][(i, half)] = d_norm[0]
            tok = tok + emit(("ffn", i, half), [d_wg, d_wu, d_wd])
    n_s, n_p = (depth + 1) // 2, depth // 2
    gsmall = dict(
        ffn_norm=jnp.stack([jnp.stack([gs["ffn_norm"][(i, h)] for h in range(2)]) for i in range(depth)]),
        mix_norm=jnp.stack([gs["mix_norm"][i] for i in range(depth)]),
        ssd_conv_w=jnp.stack([gs["ssd_conv_w"][j] for j in range(n_s)]),
        ssd_conv_b=jnp.stack([gs["ssd_conv_b"][j] for j in range(n_s)]),
        ssd_dt_bias=jnp.stack([gs["ssd_dt_bias"][j] for j in range(n_s)]),
        ssd_a_log=jnp.stack([gs["ssd_a_log"][j] for j in range(n_s)]),
        ssd_d=jnp.stack([gs["ssd_d"][j] for j in range(n_s)]),
        ssd_norm=jnp.stack([gs["ssd_norm"][j] for j in range(n_s)]),
        pool_scale=jnp.stack([gs["pool_scale"][j] for j in range(n_p)]),
        final_norm=d_final[0],
    )
    return loss, dx, gsmall, tok


ANY = pl.BlockSpec(memory_space=pl.ANY)


def _mesh_pos():
    return lax.axis_index("x"), lax.axis_index("y"), lax.axis_index("c")


def _other_chips(x, y):
    return [(1 - x, y), (x, 1 - y), (1 - x, 1 - y)]


def _win(ref, windows, lead=()):
    rest = len(ref.shape) - len(lead)
    idx = tuple(lead) + tuple(pl.ds(*windows[ax]) if ax in windows else slice(None) for ax in range(rest))
    return ref.at[idx]


def _remote(src, dst, send_sems, recv_sems, k, peer):
    return pltpu.make_async_remote_copy(src_ref=src, dst_ref=dst, send_sem=send_sems.at[k], recv_sem=recv_sems.at[k],
                                        device_id=peer, device_id_type=MESH)


def _cast_place(w3, chip1, mode, l0, nl, out_dtype=BF16):
    _, r, c = w3.shape
    tr = _tile(r, 256, 16)
    nr = r // tr
    if mode == "cols":
        out_shape, blk = (nl, r, N_CHIPS * c), (None, tr, c)
        omap = lambda l, i, s: (l, i, s[0])
    elif mode == "rows":
        out_shape, blk = (nl, N_CHIPS * r, c), (None, tr, c)
        omap = lambda l, i, s: (l, s[0] * nr + i, 0)
    else:
        out_shape, blk = (nl, N_CHIPS, r, c), (None, None, tr, c)
        omap = lambda l, i, s: (l, s[0], i, 0)

    def body(s_ref, w_ref, o_ref):
        o_ref[...] = w_ref[...].astype(out_dtype)

    return pl.pallas_call(
        body, name="cast_place_" + mode, out_shape=jax.ShapeDtypeStruct(out_shape, out_dtype),
        grid_spec=pltpu.PrefetchScalarGridSpec(
            num_scalar_prefetch=1, grid=(nl, nr),
            in_specs=[pl.BlockSpec((None, tr, c), lambda l, i, s: (l0 + l, i, 0))],
            out_specs=pl.BlockSpec(blk, omap)),
        compiler_params=_cp("parallel", "parallel"),
    )(chip1, w3)


def _gather_begin(name, fulls, axes):
    n = len(fulls)
    shapes = [f.shape for f in fulls]

    def full_win(refs, t, sidx, hidx):
        sa, ha = axes[t]
        ssz, hsz = shapes[t][sa] // N_CHIPS, shapes[t][ha] // 2
        return _win(refs[t], {sa: (sidx * ssz, ssz), ha: (hidx * hsz, hsz)})

    def chips_plan(refs, x, y, c):
        mine = 2 * x + y
        return [(full_win(refs, t, mine, c), full_win(refs, t, mine, c), (px, py, c))
                for t in range(n) for px, py in _other_chips(x, y)]

    def sibling_plan(in_refs, out_refs, x, y, c):
        wins = [full_win(out_refs, t, 2 * px + py, c) for t in range(n) for px, py in _other_chips(x, y)]
        return [(w, w, (x, y, 1 - c)) for w in wins]

    sems, thru, token = _split_start(name + "_start", fulls, chips_plan, 3 * n)
    return (name, sems, thru, chips_plan, sibling_plan, 3 * n), token


def _gather_finish(pending, after):
    name, sems, thru, chips_plan, sibling_plan, n_copies = pending
    landed = _split_wait(name + "_wait", sems, thru, chips_plan, n_copies, after)
    return _exchange(name + "_sibling", landed, [jax.ShapeDtypeStruct(f.shape, f.dtype) for f in landed],
                     sibling_plan, n_copies, inplace=True)


def _exchange(name, inputs, out_shapes, plan, n_copies, inplace=False):
    n_in, n_out = len(inputs), len(out_shapes)

    def body(*refs):
        in_refs, out_refs = refs[:n_in], refs[n_in:n_in + n_out]
        send_sems, recv_sems = refs[n_in + n_out:]
        x, y, c = _mesh_pos()
        copies = plan(in_refs, out_refs, x, y, c)
        assert len(copies) == n_copies
        started = []
        for k, (src, dst, peer) in enumerate(copies):
            cp = _remote(src, dst, send_sems, recv_sems, k, peer)
            cp.start()
            started.append(cp)
        for cp in started:
            cp.wait()

    return pl.pallas_call(
        body, name=name, in_specs=[ANY] * n_in, out_specs=[ANY] * n_out, out_shape=out_shapes,
        input_output_aliases={t: t for t in range(n_in)} if inplace else {},
        scratch_shapes=[pltpu.SemaphoreType.DMA((n_copies,)), pltpu.SemaphoreType.DMA((n_copies,))],
    )(*inputs)


HBM = pl.BlockSpec(memory_space=pltpu.HBM)
SEM = pl.BlockSpec(memory_space=pltpu.SEMAPHORE)
DATAFLOW = pltpu.SideEffectType.DATAFLOW_SIDE_EFFECTING


def _split_start(name, bufs, plan, n_copies):
    n = len(bufs)

    def body(*refs):
        ins = refs[:n]
        send_sems, recv_sems = refs[n], refs[n + 1]
        token = refs[2 * n + 2]
        x, y, c = _mesh_pos()
        copies = plan(ins, x, y, c)
        assert len(copies) == n_copies
        for k, (src, dst, peer) in enumerate(copies):
            _remote(src, dst, send_sems, recv_sems, k, peer).start()
        token[...] = jnp.zeros_like(token)

    outs = pl.pallas_call(
        body, name=name,
        out_shape=(pltpu.SemaphoreType.DMA((n_copies,)), pltpu.SemaphoreType.DMA((n_copies,)),
                   *[pltpu.HBM(b.shape, b.dtype) for b in bufs], jax.ShapeDtypeStruct((8, LANE), F32)),
        in_specs=[HBM] * n, out_specs=(SEM, SEM, *[HBM] * n, pl.BlockSpec(memory_space=pltpu.VMEM)),
        input_output_aliases={t: 2 + t for t in range(n)},
        compiler_params=pltpu.CompilerParams(has_side_effects=DATAFLOW),
    )(*[pltpu.with_memory_space_constraint(b, pltpu.HBM) for b in bufs])
    return (outs[0], outs[1]), list(outs[2:2 + n]), outs[2 + n][0, 0]


def _split_wait(name, sems, bufs, plan, n_copies, after):
    n = len(bufs)

    def body(*refs):
        ins = refs[:n]
        send_sems, recv_sems = refs[n], refs[n + 1]
        x, y, c = _mesh_pos()
        copies = plan(ins, x, y, c)
        assert len(copies) == n_copies
        for k, (src, dst, peer) in enumerate(copies):
            cp = _remote(src, dst, send_sems, recv_sems, k, peer)
            cp.wait_send()
            cp.wait_recv()

    outs = pl.pallas_call(
        body, name=name, out_shape=tuple(pltpu.HBM(b.shape, b.dtype) for b in bufs),
        in_specs=[HBM] * n + [SEM, SEM, ANY], out_specs=tuple([HBM] * n),
        input_output_aliases={t: t for t in range(n)},
        compiler_params=pltpu.CompilerParams(has_side_effects=DATAFLOW),
    )(*bufs, sems[0], sems[1], after)
    return list(outs)


def _halved(shape, ha):
    out = list(shape)
    out[ha] //= 2
    return tuple(out)


def _sharded(shape, sa):
    out = list(shape)
    out[sa] //= N_CHIPS
    return tuple(out)


def _rs_cores_begin(name, grads, axes):
    n = len(grads)
    shapes = [g.shape for g in grads]
    landing = [lax.empty(_halved(g.shape, ha), g.dtype) for g, (_, ha) in zip(grads, axes)]

    def plan(refs, x, y, c):
        copies = []
        for t in range(n):
            ha = axes[t][1]
            hsz = shapes[t][ha] // 2
            copies.append((_win(refs[t], {ha: ((1 - c) * hsz, hsz)}), refs[n + t], (x, y, 1 - c)))
        return copies

    sems, thru, token = _split_start(name + "_start", list(grads) + landing, plan, n)
    return (name, sems, thru, plan, n), token


def _rs_cores_finish(pending, after):
    name, sems, thru, plan, n = pending
    done = _split_wait(name + "_wait", sems, thru, plan, n, after)
    return done[:n], done[n:]


def _small_begin(vec):
    landing = lax.empty((8,) + vec.shape, vec.dtype)

    def plan(refs, x, y, c):
        me = 4 * x + 2 * y + c
        copies = []
        for k in range(1, 8):
            kx, ky, kc = k // 4, (k // 2) % 2, k % 2
            copies.append((refs[0], refs[1].at[me], (x ^ kx, y ^ ky, c ^ kc)))
        return copies

    sems, thru, token = _split_start("small_start", [vec, landing], plan, 7)
    return (sems, thru, plan), token


def _small_finish(pending, after, me1):
    sems, thru, plan = pending
    vec, landing = _split_wait("small_wait", sems, thru, plan, 7, after)

    def body(s_ref, v_ref, l_ref, o_ref):
        tot = jnp.where(s_ref[0] == 0, v_ref[...], l_ref[0])
        for i in range(1, 8):
            tot = tot + jnp.where(s_ref[0] == i, v_ref[...], l_ref[i])
        o_ref[...] = tot

    rows = vec.shape[0]
    return pl.pallas_call(
        body, name="small_sum", out_shape=jax.ShapeDtypeStruct(vec.shape, F32),
        grid_spec=pltpu.PrefetchScalarGridSpec(
            num_scalar_prefetch=1, grid=(1,),
            in_specs=[pl.BlockSpec((rows, LANE), lambda i, s: (0, 0)), pl.BlockSpec((8, rows, LANE), lambda i, s: (0, 0, 0))],
            out_specs=pl.BlockSpec((rows, LANE), lambda i, s: (0, 0))),
    )(me1, vec, landing)


def _rs_chips_begin(name, halves, axes):
    n = len(halves)
    shapes = [h.shape for h in halves]
    landing = [lax.empty((N_CHIPS - 1,) + _sharded(h.shape, sa), h.dtype) for h, (sa, _) in zip(halves, axes)]

    def plan(refs, x, y, c):
        copies = []
        for t in range(n):
            sa = axes[t][0]
            ssz = shapes[t][sa] // N_CHIPS
            for j, (px, py) in enumerate(_other_chips(x, y)):
                copies.append((_win(refs[t], {sa: ((2 * px + py) * ssz, ssz)}), refs[n + t].at[j], (px, py, c)))
        return copies

    sems, thru, token = _split_start(name + "_start", list(halves) + landing, plan, 3 * n)
    return (name, sems, thru, plan, 3 * n), token


def _rs_chips_finish(pending, after):
    name, sems, thru, plan, n_copies = pending
    done = _split_wait(name + "_wait", sems, thru, plan, n_copies, after)
    n = len(done) // 2
    return done[:n], done[n:]


def _rs_finish(name, groups):
    flat = [(gi, t) for gi, grp in enumerate(groups) for t in range(len(grp))]
    shapes = [jax.ShapeDtypeStruct((len(grp),) + grp[0].shape, grp[0].dtype) for grp in groups]

    def plan(in_refs, out_refs, x, y, c):
        return [(in_refs[k], out_refs[gi].at[t], (x, y, 1 - c)) for k, (gi, t) in enumerate(flat)]

    return _exchange(name, [a for grp in groups for a in grp], shapes, plan, len(flat))


def _rows2d(a):
    return a.reshape(-1, a.shape[-1])


def _view3(a):
    return a.reshape((-1,) + a.shape[-2:])


def _add_pair(g, recv, ha, core1):
    g3, r3 = _view3(g), _view3(recv)
    rows_half = ha + 3 - g.ndim == 1
    n_l, r, c = r3.shape
    tr = _tile(r, 256, 16)
    nr = r // tr
    gmap = (lambda l, i, s: (l, s[0] * nr + i, 0)) if rows_half else (lambda l, i, s: (l, i, s[0]))

    def body(s_ref, g_ref, r_ref, o_ref):
        o_ref[...] = (g_ref[...].astype(F32) + r_ref[...].astype(F32)).astype(BF16)

    spec = pl.BlockSpec((None, tr, c), lambda l, i, s: (l, i, 0))
    out = pl.pallas_call(
        body, name="add_pair", out_shape=jax.ShapeDtypeStruct(r3.shape, BF16),
        grid_spec=pltpu.PrefetchScalarGridSpec(num_scalar_prefetch=1, grid=(n_l, nr),
                                               in_specs=[pl.BlockSpec((None, tr, c), gmap), spec], out_specs=spec),
        compiler_params=_cp("parallel", "parallel"))(core1, g3, r3)
    return out.reshape(recv.shape)


def _add_four(cs, recv, sa, chip1):
    c3 = _view3(cs)
    s3 = sa + 3 - cs.ndim
    lo, ro, co = (dim // N_CHIPS if ax == s3 else dim for ax, dim in enumerate(c3.shape))
    r4 = recv.reshape(N_CHIPS - 1, lo, ro, co)
    tr = _tile(ro, 256, 16)
    nr = ro // tr
    if s3 == 0:
        cmap = lambda l, i, s: (s[0] * lo + l, i, 0)
    elif s3 == 1:
        cmap = lambda l, i, s: (l, s[0] * nr + i, 0)
    else:
        cmap = lambda l, i, s: (l, i, s[0])

    def body(s_ref, c_ref, r_ref, out_ref):
        out_ref[...] = ((c_ref[...].astype(F32) + r_ref[0].astype(F32)) + r_ref[1].astype(F32)) + r_ref[2].astype(F32)

    out = pl.pallas_call(
        body, name="add_four", out_shape=jax.ShapeDtypeStruct((lo, ro, co), F32),
        grid_spec=pltpu.PrefetchScalarGridSpec(
            num_scalar_prefetch=1, grid=(lo, nr),
            in_specs=[pl.BlockSpec((None, tr, co), cmap),
                      pl.BlockSpec((N_CHIPS - 1, None, tr, co), lambda l, i, s: (0, l, i, 0))],
            out_specs=pl.BlockSpec((None, tr, co), lambda l, i, s: (l, i, 0))),
        compiler_params=_cp("parallel", "parallel"))(chip1, c3, r4)
    return out.reshape(recv.shape[1:])


def _adamw_halves(w, own, recv, m, v, rows_half, core1, l0=0, prev=None):
    w3, m3, v3, o3, r3 = (_view3(a) for a in (w, m, v, own, recv))
    _, _, c = w3.shape
    n_l, rh, ch = o3.shape
    tr = _tile(rh, 128, 8)
    nr = rh // tr
    c1 = 1.0 - ADAM_B1 ** ADAM_STEP
    c2 = 1.0 - ADAM_B2 ** ADAM_STEP

    n_prev = 0 if prev is None else 4

    def body(s_ref, w_ref, o_ref, r_ref, m_ref, v_ref, *rest):
        g_ref, d_ref, nm_ref, nv_ref = rest[n_prev:]
        gv = jnp.where(pl.program_id(1) == s_ref[0], o_ref[...], r_ref[...])
        nm = ADAM_B1 * m_ref[...] + (1.0 - ADAM_B1) * gv
        nv = ADAM_B2 * v_ref[...] + (1.0 - ADAM_B2) * (gv * gv)
        g_ref[...] = gv
        nm_ref[...] = nm
        nv_ref[...] = nv
        d_ref[...] = -ADAM_LR * ((nm / c1) / (jnp.sqrt(nv / c2) + ADAM_EPS) + ADAM_WD * w_ref[...])

    wmap = (lambda l, h, i, s: (l0 + l, h * nr + i, 0)) if rows_half else (lambda l, h, i, s: (l0 + l, i, h))
    wspec = pl.BlockSpec((None, tr, ch), wmap)
    ospec = pl.BlockSpec((None, tr, ch), lambda l, h, i, s: (jnp.where(h == s[0], l, 0), jnp.where(h == s[0], i, 0), 0))
    rspec = pl.BlockSpec((None, tr, ch), lambda l, h, i, s: (jnp.where(h == s[0], 0, l), jnp.where(h == s[0], 0, i), 0))
    osh = jax.ShapeDtypeStruct(w3.shape, F32)
    before = [] if prev is None else [_view3(p) for p in prev]
    outs = pl.pallas_call(
        body, name="adamw_halves", out_shape=[osh] * 4,
        grid_spec=pltpu.PrefetchScalarGridSpec(
            num_scalar_prefetch=1, grid=(n_l, 2, nr),
            in_specs=[wspec, ospec, rspec, wspec, wspec] + [ANY] * n_prev, out_specs=[wspec] * 4),
        input_output_aliases={6 + k: k for k in range(n_prev)},
        compiler_params=_cp("parallel", "parallel", "parallel"))(core1, w3, o3, r3, m3, v3, *before)
    return tuple(o.reshape(w.shape) for o in outs)


def _adamw(w, g, m, v):
    shape = w.shape
    w2, g2, m2, v2 = (_rows2d(a) if a.ndim > 1 else a.reshape(1, -1) for a in (w, g, m, v))
    rows, cols = w2.shape
    tr = _tile(rows, 256, 8)
    c1 = 1.0 - ADAM_B1 ** ADAM_STEP
    c2 = 1.0 - ADAM_B2 ** ADAM_STEP

    def body(w_ref, g_ref, m_ref, v_ref, d_ref, nm_ref, nv_ref):
        gv = g_ref[...]
        nm = ADAM_B1 * m_ref[...] + (1.0 - ADAM_B1) * gv
        nv = ADAM_B2 * v_ref[...] + (1.0 - ADAM_B2) * (gv * gv)
        nm_ref[...] = nm
        nv_ref[...] = nv
        d_ref[...] = -ADAM_LR * ((nm / c1) / (jnp.sqrt(nv / c2) + ADAM_EPS) + ADAM_WD * w_ref[...])

    spec = pl.BlockSpec((tr, cols), lambda i: (i, 0))
    osh = jax.ShapeDtypeStruct((rows, cols), F32)
    outs = pl.pallas_call(body, name="adamw", grid=(rows // tr,), in_specs=[spec] * 4, out_specs=[spec] * 3,
                          out_shape=[osh] * 3, compiler_params=_cp("parallel"))(w2, g2, m2, v2)
    return tuple(o.reshape(shape) for o in outs)


def _pack(arrs):
    flat = jnp.concatenate([a.reshape(-1) for a in arrs])
    n = flat.shape[0]
    rows = -(-n // (8 * LANE)) * 8
    return jnp.pad(flat, (0, rows * LANE - n)).reshape(rows, LANE)


def _unpack(packed, shapes):
    flat = packed.reshape(-1)
    out, pos = [], 0
    for sh in shapes:
        size = 1
        for dsz in sh:
            size *= dsz
        out.append(flat[pos:pos + size].reshape(sh))
        pos += size
    return out


BIG = ("ffn_w_gate", "ffn_w_up", "ffn_w_down", "ssd_w_in", "ssd_w_out", "pool_w_in", "pool_w_group", "pool_w_out")
WEIGHTS = ("ffn_norm", "ffn_w_gate", "ffn_w_up", "ffn_w_down", "mix_norm", "ssd_w_in", "ssd_conv_w", "ssd_conv_b",
           "ssd_dt_bias", "ssd_a_log", "ssd_d", "ssd_norm", "ssd_w_out", "pool_w_in", "pool_w_group", "pool_scale",
           "pool_w_out", "final_norm")
SMALL = tuple(k for k in WEIGHTS if k not in BIG)
SMALL_SHARDED = {"ffn_norm": 2, "ssd_conv_w": 2, "pool_scale": 1}


def kernel(x, ffn_norm, ffn_w_gate, ffn_w_up, ffn_w_down, mix_norm, ssd_w_in, ssd_conv_w, ssd_conv_b, ssd_dt_bias, ssd_a_log, ssd_d, ssd_norm, ssd_w_out, pool_w_in, pool_w_group, pool_scale, pool_w_out, final_norm, loss_target, m_ffn_norm, m_ffn_w_gate, m_ffn_w_up, m_ffn_w_down, m_mix_norm, m_ssd_w_in, m_ssd_conv_w, m_ssd_conv_b, m_ssd_dt_bias, m_ssd_a_log, m_ssd_d, m_ssd_norm, m_ssd_w_out, m_pool_w_in, m_pool_w_group, m_pool_scale, m_pool_w_out, m_final_norm, v_ffn_norm, v_ffn_w_gate, v_ffn_w_up, v_ffn_w_down, v_mix_norm, v_ssd_w_in, v_ssd_conv_w, v_ssd_conv_b, v_ssd_dt_bias, v_ssd_a_log, v_ssd_d, v_ssd_norm, v_ssd_w_out, v_pool_w_in, v_pool_w_group, v_pool_scale, v_pool_w_out, v_final_norm):
    w = dict(ffn_norm=ffn_norm, ffn_w_gate=ffn_w_gate, ffn_w_up=ffn_w_up, ffn_w_down=ffn_w_down, mix_norm=mix_norm,
             ssd_w_in=ssd_w_in, ssd_conv_w=ssd_conv_w, ssd_conv_b=ssd_conv_b, ssd_dt_bias=ssd_dt_bias,
             ssd_a_log=ssd_a_log, ssd_d=ssd_d, ssd_norm=ssd_norm, ssd_w_out=ssd_w_out, pool_w_in=pool_w_in,
             pool_w_group=pool_w_group, pool_scale=pool_scale, pool_w_out=pool_w_out, final_norm=final_norm)
    mom = dict(ffn_norm=m_ffn_norm, ffn_w_gate=m_ffn_w_gate, ffn_w_up=m_ffn_w_up, ffn_w_down=m_ffn_w_down,
               mix_norm=m_mix_norm, ssd_w_in=m_ssd_w_in, ssd_conv_w=m_ssd_conv_w, ssd_conv_b=m_ssd_conv_b,
               ssd_dt_bias=m_ssd_dt_bias, ssd_a_log=m_ssd_a_log, ssd_d=m_ssd_d, ssd_norm=m_ssd_norm,
               ssd_w_out=m_ssd_w_out, pool_w_in=m_pool_w_in, pool_w_group=m_pool_w_group, pool_scale=m_pool_scale,
               pool_w_out=m_pool_w_out, final_norm=m_final_norm)
    vel = dict(ffn_norm=v_ffn_norm, ffn_w_gate=v_ffn_w_gate, ffn_w_up=v_ffn_w_up, ffn_w_down=v_ffn_w_down,
               mix_norm=v_mix_norm, ssd_w_in=v_ssd_w_in, ssd_conv_w=v_ssd_conv_w, ssd_conv_b=v_ssd_conv_b,
               ssd_dt_bias=v_ssd_dt_bias, ssd_a_log=v_ssd_a_log, ssd_d=v_ssd_d, ssd_norm=v_ssd_norm,
               ssd_w_out=v_ssd_w_out, pool_w_in=v_pool_w_in, pool_w_group=v_pool_w_group, pool_scale=v_pool_scale,
               pool_w_out=v_pool_w_out, final_norm=v_final_norm)
    depth = ffn_w_gate.shape[0]
    n_s, n_p = ssd_w_in.shape[0], pool_w_in.shape[0]
    chip = 2 * lax.axis_index("x") + lax.axis_index("y")

    chip1 = jnp.reshape(chip, (1,)).astype(jnp.int32)
    core1 = jnp.reshape(lax.axis_index("c"), (1,)).astype(jnp.int32)

    gate3, up3, down3 = _view3(ffn_w_gate), _view3(ffn_w_up), _view3(ffn_w_down)
    w_in_t = ssd_w_in.transpose(0, 2, 1)
    pending = {}
    sharded_names = tuple(SMALL_SHARDED)
    tok = jnp.zeros((), F32)

    def begin(group, name, fulls, axes):
        nonlocal tok
        pending[group], t = _gather_begin(name, fulls, axes)
        tok = tok + t

    for i in range(depth):
        for h in range(2):
            fulls = [_cast_place(gate3, chip1, "cols", 2 * i + h, 1), _cast_place(up3, chip1, "cols", 2 * i + h, 1),
                     _cast_place(down3, chip1, "rows", 2 * i + h, 1)]
            if (i, h) == (0, 0):
                packed_small = _pack([w[k] for k in sharded_names])
                fulls.append(_cast_place(packed_small[None], chip1, "slot", 0, 1, F32))
                begin(("ffn", i, h), "gather_first", fulls, [(2, 1), (2, 1), (1, 2), (1, 2)])
            else:
                begin(("ffn", i, h), "gather_ffn", fulls, [(2, 1), (2, 1), (1, 2)])
            j = i // 2
            if h == 0 and i % 2 == 0:
                begin(("ssd", j), "gather_ssd", [_cast_place(w_in_t, chip1, "rows", j, 1),
                                                 _cast_place(ssd_w_out, chip1, "rows", j, 1)], [(1, 2), (1, 2)])
            if h == 0 and i % 2 == 1:
                n_g = pool_w_group.shape[1]
                grp_full = _cast_place(_view3(pool_w_group), chip1, "rows", j * n_g, n_g)
                begin(("pool", j), "gather_pool", [_cast_place(pool_w_in, chip1, "rows", j, 1), grp_full[None],
                                                   _cast_place(pool_w_out, chip1, "rows", j, 1)],
                      [(1, 2), (2, 3), (1, 2)])

    fetched = {}

    def fetch(group, after):
        if group not in fetched:
            got = _gather_finish(pending[group], after)
            fetched[group] = got[:3]
            if group == ("ffn", 0, 0):
                fetched["small"] = got[3][0]
        return fetched[group]

    fetch(("ffn", 0, 0), x)
    small = {k: w[k] for k in SMALL if k not in SMALL_SHARDED}
    per_chip = [_unpack(fetched["small"][s], [w[k].shape for k in sharded_names]) for s in range(N_CHIPS)]
    for t, k in enumerate(sharded_names):
        small[k] = jnp.concatenate([per_chip[s][t] for s in range(N_CHIPS)], axis=SMALL_SHARDED[k])

    ffn_names = ("ffn_w_gate", "ffn_w_up", "ffn_w_down")
    group_axes = dict(ffn=[(1, 0), (1, 0), (0, 1)], ssd=[(0, 1), (0, 1)], pool=[(0, 1), (1, 2), (0, 1)])
    group_names = dict(ffn=ffn_names, ssd=("ssd_w_in", "ssd_w_out"), pool=("pool_w_in", "pool_w_group", "pool_w_out"))
    travelling = {}
    between_cores = []

    def advance(after):
        group, pend = between_cores.pop()
        axes = group_axes[group[0]]
        grads, recv_a = _rs_cores_finish(pend, after)
        chip_sums = [_add_pair(g, r, ha, core1) for g, r, (_, ha) in zip(grads, recv_a, axes)]
        travelling[group], t = _rs_chips_begin("rs_chips_" + group[0], chip_sums, axes)
        return t

    def emit(group, grads):
        kind = group[0]
        t = advance(grads[0]) if between_cores else jnp.zeros((), F32)
        pend, t2 = _rs_cores_begin("rs_cores_" + kind, grads, group_axes[kind])
        between_cores.append((group, pend))
        return t + t2

    total = {}

    def on_loss(part):
        total["loss"] = lax.psum(part[0, 0], ("x", "y", "c"))
        return jnp.minimum(total["loss"], 0.0)

    _, dx, gsmall, t_emits = _local_step(x[0], loss_target[0], depth, fetch, emit, on_loss, small, tok)
    loss = total["loss"]

    t_last = advance(dx)
    small_pending, t_small = _small_begin(_pack([gsmall[k] for k in SMALL]))
    started = jnp.reshape(t_emits + t_last + t_small, (1, 1))

    rows_half = dict(ffn_w_gate=True, ffn_w_up=True, ffn_w_down=False, ssd_w_in=False, ssd_w_out=False,
                     pool_w_in=False, pool_w_group=False, pool_w_out=False)
    w_t, mom_t, vel_t = ({**src, "ssd_w_in": src["ssd_w_in"].transpose(0, 2, 1)} for src in (w, mom, vel))
    results = {}

    def finish(groups, tag, after):
        sums = {k: {} for k in BIG}
        for group in groups:
            kind = group[0]
            chip_sums, recv_b = _rs_chips_finish(travelling[group], after)
            for k, cs, r, (sa, _) in zip(group_names[kind], chip_sums, recv_b, group_axes[kind]):
                sums[k][group[1:]] = _add_four(cs, r, sa, chip1)
        names = [k for k in BIG if sums[k]]
        own = {k: [sums[k][idx] for idx in sorted(sums[k])] for k in names}
        from_sibling = dict(zip(names, _rs_finish("rs_finish_" + tag, [own[k] for k in names])))
        for k in names:
            first = min(sums[k])
            l0 = first[0] * 2 + first[1] if k in ffn_names else first[0]
            results[k] = _adamw_halves(w_t[k], jnp.stack(own[k]), from_sibling[k], mom_t[k], vel_t[k], rows_half[k],
                                       core1, l0, results.get(k))
        return jnp.stack([results[k][3][(0,) * results[k][3].ndim] for k in names]).reshape(1, -1)

    late = [g for g in travelling if g[1] == 0 and g[0] != "pool"]
    early = [g for g in travelling if g not in late]
    done_early = finish(early, "early", started) if early else started
    done_late = finish(late, "late", done_early)
    results["ssd_w_in"] = tuple(r.transpose(0, 2, 1) for r in results["ssd_w_in"])
    grad, delta, new_m, new_v = ({k: results[k][t] for k in BIG} for t in range(4))

    small_shapes = [gsmall[k].shape for k in SMALL]
    me1 = 2 * chip1 + core1
    summed = _unpack(_small_finish(small_pending, done_late, me1), small_shapes)
    for k, g in zip(SMALL, summed):
        if k in SMALL_SHARDED:
            ax = SMALL_SHARDED[k]
            size = w[k].shape[ax]
            g = lax.dynamic_slice_in_dim(g, chip * size, size, axis=ax)
        grad[k] = g

    shapes = [w[k].shape for k in SMALL]
    packed = _adamw(*(_pack([src[k] for k in SMALL]) for src in (w, grad, mom, vel)))
    for dst, p in zip((delta, new_m, new_v), packed):
        for k, a in zip(SMALL, _unpack(p, shapes)):
            dst[k] = a

    return (loss, dx[None], *[grad[k] for k in WEIGHTS], *[delta[k] for k in WEIGHTS],
            *[new_m[k] for k in WEIGHTS], *[new_v[k] for k in WEIGHTS])
```

```python
import functools

import jax
import jax.numpy as jnp
from jax import lax
from jax.experimental import pallas as pl
from jax.experimental.pallas import tpu as pltpu

F32 = jnp.float32
BF16 = jnp.bfloat16
EPS = 1e-6
MESH = pl.DeviceIdType.MESH

SSD_CHUNK = 128
SSD_STATE = 128
SSD_HEAD_DIM = 64
HEADS_PER_GROUP = 8
GROUP_W = HEADS_PER_GROUP * SSD_HEAD_DIM
CONV_W = 5
POOL_WINDOWS = (2, 4, 8, 16)
N_CHIPS = 4

ADAM_LR = 0.001
ADAM_B1 = 0.9
ADAM_B2 = 0.999
ADAM_EPS = 1e-08
ADAM_WD = 0.01
ADAM_STEP = 10

VMEM_LIMIT = 56 * 1024 * 1024
LANE = 128


def _cp(*sem):
    return pltpu.CompilerParams(dimension_semantics=sem, vmem_limit_bytes=VMEM_LIMIT)


def _tile(dim, pref, unit=LANE):
    if dim <= pref:
        return dim
    t = (pref // unit) * unit
    while t >= unit:
        if dim % t == 0:
            return t
        t -= unit
    return dim


def _sigmoid(v):
    return 1.0 / (1.0 + jnp.exp(-v))


def _dot(a, b):
    return jnp.dot(a, b, preferred_element_type=F32)


def _dot_nt(a, b):
    return lax.dot_general(a, b, (((1,), (1,)), ((), ())), preferred_element_type=F32)


def _split3(v):
    h1 = v.astype(BF16)
    r1 = v - h1.astype(F32)
    h2 = r1.astype(BF16)
    h3 = (r1 - h2.astype(F32)).astype(BF16)
    return h1, h2, h3


def _dot_exact01(m01, v):
    mb = m01.astype(BF16)
    h1, h2, h3 = _split3(v)
    return _dot(mb, h1) + _dot(mb, h2) + _dot(mb, h3)


def _dot_nt_exact01(m01, v):
    mb = m01.astype(BF16)
    h1, h2, h3 = _split3(v)
    return _dot_nt(mb, h1) + _dot_nt(mb, h2) + _dot_nt(mb, h3)


def _mm(a, b, *, name, ta=False, tb=False, a_sel=(), b_sel=(), pair2=None, add=None, scale=1.0,
        out_dtype=F32, tm=1024, tn=1024, tk=2048):
    am, ak = a.shape[-2:][::-1] if ta else a.shape[-2:]
    bk, bn = b.shape[-2:][::-1] if tb else b.shape[-2:]
    assert ak == bk, (a.shape, b.shape, ta, tb)
    m_dim, n_dim, k_dim = am, bn, ak
    tm, tn, tk = _tile(m_dim, tm), _tile(n_dim, tn), _tile(k_dim, tk)
    nk = k_dim // tk
    grid = (n_dim // tn, m_dim // tm, nk)

    def a_spec(sel):
        lead = (None,) * len(sel)
        if ta:
            return pl.BlockSpec(lead + (tk, tm), lambda n, m, k: tuple(sel) + (k, m))
        return pl.BlockSpec(lead + (tm, tk), lambda n, m, k: tuple(sel) + (m, k))

    def b_spec(sel):
        lead = (None,) * len(sel)
        if tb:
            return pl.BlockSpec(lead + (tn, tk), lambda n, m, k: tuple(sel) + (n, k))
        return pl.BlockSpec(lead + (tk, tn), lambda n, m, k: tuple(sel) + (k, n))

    ins, specs = [a, b], [a_spec(a_sel), b_spec(b_sel)]
    if pair2 is not None:
        a2, b2, a2_sel, b2_sel = pair2
        ins += [a2, b2]
        specs += [a_spec(a2_sel), b_spec(b2_sel)]
    if add is not None:
        ins.append(add)
        specs.append(pl.BlockSpec((tm, tn), lambda n, m, k: (m, n)))
    dn = (((0 if ta else 1,), (1 if tb else 0,)), ((), ()))
    n_pairs = 2 if pair2 is not None else 1

    def body(*refs):
        pairs = [(refs[2 * i], refs[2 * i + 1]) for i in range(n_pairs)]
        pos = 2 * n_pairs
        add_ref = None
        if add is not None:
            add_ref = refs[pos]
            pos += 1
        o_ref = refs[pos]
        acc_ref = refs[pos + 1] if nk > 1 else None

        def prod():
            tot = None
            for ar, br in pairs:
                p = lax.dot_general(ar[...].astype(BF16), br[...].astype(BF16), dn, preferred_element_type=F32)
                tot = p if tot is None else tot + p
            return tot

        def finish(r):
            if scale != 1.0:
                r = r * scale
            if add_ref is not None:
                r = add_ref[...] + r
            o_ref[...] = r.astype(out_dtype)

        if nk == 1:
            finish(prod())
        else:
            k = pl.program_id(2)

            @pl.when(k == 0)
            def _():
                acc_ref[...] = jnp.zeros_like(acc_ref)

            acc_ref[...] += prod()

            @pl.when(k == nk - 1)
            def _():
                finish(acc_ref[...])

    return pl.pallas_call(
        body, name=name, grid=grid, in_specs=specs,
        out_specs=pl.BlockSpec((tm, tn), lambda n, m, k: (m, n)),
        out_shape=jax.ShapeDtypeStruct((m_dim, n_dim), out_dtype),
        scratch_shapes=[pltpu.VMEM((tm, tn), F32)] if nk > 1 else [],
        compiler_params=_cp("parallel", "parallel", "arbitrary"),
    )(*ins)


def _rmsnorm(x, g):
    t_dim, d = x.shape
    tr = _tile(t_dim, 256, 8)

    def body(x_ref, g_ref, o_ref):
        xv = x_ref[...]
        r = lax.rsqrt(jnp.mean(xv * xv, axis=-1, keepdims=True) + EPS)
        o_ref[...] = (xv * r * g_ref[...]).astype(BF16)

    return pl.pallas_call(
        body, name="rmsnorm", grid=(t_dim // tr,),
        in_specs=[pl.BlockSpec((tr, d), lambda i: (i, 0)), pl.BlockSpec((1, d), lambda i: (0, 0))],
        out_specs=pl.BlockSpec((tr, d), lambda i: (i, 0)),
        out_shape=jax.ShapeDtypeStruct((t_dim, d), BF16),
        compiler_params=_cp("parallel"),
    )(x, g)


def _rmsnorm_bwd(x, g, dh, dres):
    t_dim, d = x.shape
    tr = _tile(t_dim, 256, 8)

    def body(x_ref, g_ref, dh_ref, dres_ref, dx_ref, dg_ref):
        i = pl.program_id(0)
        xv = x_ref[...]
        r = lax.rsqrt(jnp.mean(xv * xv, axis=-1, keepdims=True) + EPS)
        n = xv * r
        dhv = dh_ref[...]
        dn = dhv * g_ref[...]

        @pl.when(i == 0)
        def _():
            dg_ref[...] = jnp.zeros_like(dg_ref)

        dg_ref[...] += jnp.sum(dhv * n, axis=0, keepdims=True)
        dx_ref[...] = dres_ref[...] + r * (dn - n * jnp.mean(dn * n, axis=-1, keepdims=True))

    row = pl.BlockSpec((tr, d), lambda i: (i, 0))
    vec = pl.BlockSpec((1, d), lambda i: (0, 0))
    return pl.pallas_call(
        body, name="rmsnorm_bwd", grid=(t_dim // tr,),
        in_specs=[row, vec, row, row], out_specs=[row, vec],
        out_shape=[jax.ShapeDtypeStruct((t_dim, d), F32), jax.ShapeDtypeStruct((1, d), F32)],
        compiler_params=_cp("arbitrary"),
    )(x, g, dh, dres)


def _loss_head(x, g, target):
    t_dim, d = x.shape
    tr = _tile(t_dim, 256, 8)

    def body(x_ref, g_ref, t_ref, loss_ref, dx_ref, dg_ref):
        i = pl.program_id(0)
        xv = x_ref[...]
        gv = g_ref[...]
        r = lax.rsqrt(jnp.mean(xv * xv, axis=-1, keepdims=True) + EPS)
        n = xv * r
        err = n * gv - t_ref[...]

        @pl.when(i == 0)
        def _():
            dg_ref[...] = jnp.zeros_like(dg_ref)
            loss_ref[...] = jnp.zeros_like(loss_ref)

        per_tok = jnp.mean(err * err, axis=-1, keepdims=True)
        loss_ref[...] += 0.5 * jnp.sum(per_tok, axis=0, keepdims=True)
        dy = err * (1.0 / d)
        dn = dy * gv
        dg_ref[...] += jnp.sum(dy * n, axis=0, keepdims=True)
        dx_ref[...] = r * (dn - n * jnp.mean(dn * n, axis=-1, keepdims=True))

    row = pl.BlockSpec((tr, d), lambda i: (i, 0))
    vec = pl.BlockSpec((1, d), lambda i: (0, 0))
    one = pl.BlockSpec((1, 1), lambda i: (0, 0))
    return pl.pallas_call(
        body, name="loss_head", grid=(t_dim // tr,),
        in_specs=[row, vec, row], out_specs=[one, row, vec],
        out_shape=[jax.ShapeDtypeStruct((1, 1), F32), jax.ShapeDtypeStruct((t_dim, d), F32),
                   jax.ShapeDtypeStruct((1, d), F32)],
        compiler_params=_cp("arbitrary"),
    )(x, g, target)


def _ffn_in(h, wg, wu, half):
    t_dim, d = h.shape
    f = wg.shape[-1]
    tm, tn = _tile(t_dim, 512), _tile(f, 1408)

    def body(h_ref, wg_ref, wu_ref, g_ref, u_ref, a_ref):
        hv = h_ref[...]
        gv = _dot(hv, wg_ref[...])
        uv = _dot(hv, wu_ref[...])
        g_ref[...] = gv.astype(BF16)
        u_ref[...] = uv.astype(BF16)
        a_ref[...] = (gv * _sigmoid(gv) * uv).astype(BF16)

    wspec = pl.BlockSpec((None, d, tn), lambda n, m: (half, 0, n))
    ospec = pl.BlockSpec((tm, tn), lambda n, m: (m, n))
    oshape = jax.ShapeDtypeStruct((t_dim, f), BF16)
    return pl.pallas_call(
        body, name="ffn_in", grid=(f // tn, t_dim // tm),
        in_specs=[pl.BlockSpec((tm, d), lambda n, m: (m, 0)), wspec, wspec],
        out_specs=[ospec, ospec, ospec], out_shape=[oshape, oshape, oshape],
        compiler_params=_cp("parallel", "parallel"),
    )(h, wg, wu)


def _ffn_bwd_act(dx, wd, g, u, half):
    t_dim, d = dx.shape
    f = wd.shape[-2]
    tm, tn = _tile(t_dim, 512), _tile(f, 1408)

    def body(dx_ref, wd_ref, g_ref, u_ref, dg_ref, du_ref):
        da = 0.5 * _dot_nt(dx_ref[...].astype(BF16), wd_ref[...])
        gv = g_ref[...].astype(F32)
        uv = u_ref[...].astype(F32)
        s = _sigmoid(gv)
        dg_ref[...] = (da * uv * (s * (1.0 + gv * (1.0 - s)))).astype(BF16)
        du_ref[...] = (da * gv * s).astype(BF16)

    tile = pl.BlockSpec((tm, tn), lambda n, m: (m, n))
    oshape = jax.ShapeDtypeStruct((t_dim, f), BF16)
    return pl.pallas_call(
        body, name="ffn_bwd_act", grid=(f // tn, t_dim // tm),
        in_specs=[pl.BlockSpec((tm, d), lambda n, m: (m, 0)),
                  pl.BlockSpec((None, tn, d), lambda n, m: (half, n, 0)), tile, tile],
        out_specs=[tile, tile], out_shape=[oshape, oshape],
        compiler_params=_cp("parallel", "parallel"),
    )(dx, wd, g, u)


def _ffn_fwd(x, norm_g, wg, wu, wd, half):
    h = _rmsnorm(x, norm_g)
    g, u, a = _ffn_in(h, wg, wu, half)
    x_new = _mm(a, wd, name="ffn_out", b_sel=(half,), add=x, scale=0.5, tk=1408)
    return x_new, (x, h, g, u, a)


def _ffn_bwd(dx, saved, norm_g, wg, wu, wd, half):
    x, h, g, u, a = saved
    dg, du = _ffn_bwd_act(dx, wd, g, u, half)
    d_wd = _mm(a, dx, name="ffn_dwd", ta=True, scale=0.5, out_dtype=BF16, tm=1408, tn=1024)
    d_wg = _mm(h, dg, name="ffn_dwgu", ta=True, out_dtype=BF16, tm=1024, tn=1408)
    d_wu = _mm(h, du, name="ffn_dwgu", ta=True, out_dtype=BF16, tm=1024, tn=1408)
    dh = _mm(dg, wg, name="ffn_dh", tb=True, b_sel=(half,), pair2=(du, wu, (), (half,)), tk=1408)
    dx_new, dnorm = _rmsnorm_bwd(x, norm_g, dh, dx)
    return dx_new, dnorm, d_wg, d_wu, d_wd


def _shifted(v, off, t_idx):
    if off == 0:
        return v
    t_dim = v.shape[0]
    sh = pltpu.roll(v, (-off) % t_dim, 0)
    valid = jnp.logical_and(t_idx + off >= 0, t_idx + off < t_dim)
    return jnp.where(valid, sh, 0.0)


def _conv_pre(u, w_ref, b_ref, t_idx):
    acc = jnp.zeros_like(u) + b_ref[...]
    shifted = []
    for k in range(CONV_W):
        sh = _shifted(u, k - CONV_W // 2, t_idx)
        shifted.append(sh)
        acc = acc + w_ref[k:k + 1, :] * sh
    return acc, shifted


def _conv_silu(proj, conv_w, conv_b, col0):
    t_dim = proj.shape[0]
    cd = conv_w.shape[-1]
    cb = _tile(cd, 256)
    assert col0 % cb == 0

    def body(u_ref, w_ref, b_ref, o_ref):
        t_idx = lax.broadcasted_iota(jnp.int32, (t_dim, cb), 0)
        pre, _ = _conv_pre(u_ref[...], w_ref, b_ref, t_idx)
        o_ref[...] = pre * _sigmoid(pre)

    return pl.pallas_call(
        body, name="conv_silu", grid=(cd // cb,),
        in_specs=[pl.BlockSpec((t_dim, cb), lambda j: (0, col0 // cb + j)),
                  pl.BlockSpec((CONV_W, cb), lambda j: (0, j)), pl.BlockSpec((1, cb), lambda j: (0, j))],
        out_specs=pl.BlockSpec((t_dim, cb), lambda j: (0, j)),
        out_shape=jax.ShapeDtypeStruct((t_dim, cd), F32),
        compiler_params=_cp("parallel"),
    )(proj, conv_w, conv_b)


def _conv_silu_bwd(proj, conv_w, conv_b, dact2, col0, ch0):
    t_dim = proj.shape[0]
    cd = dact2.shape[-1]
    cb = _tile(cd, 256)
    assert (col0 + ch0) % cb == 0 and ch0 % cb == 0
    p0, c0 = (col0 + ch0) // cb, ch0 // cb

    def body(u_ref, w_ref, b_ref, da_ref, du_ref, dwb_ref):
        t_idx = lax.broadcasted_iota(jnp.int32, (t_dim, cb), 0)
        pre, shifted = _conv_pre(u_ref[...], w_ref, b_ref, t_idx)
        s = _sigmoid(pre)
        dpre = (da_ref[0] + da_ref[1]) * (s * (1.0 + pre * (1.0 - s)))
        du = jnp.zeros_like(dpre)
        for k in range(CONV_W):
            du = du + w_ref[k:k + 1, :] * _shifted(dpre, -(k - CONV_W // 2), t_idx)
            dwb_ref[k:k + 1, :] = jnp.sum(dpre * shifted[k], axis=0, keepdims=True)
        dwb_ref[CONV_W:CONV_W + 1, :] = jnp.sum(dpre, axis=0, keepdims=True)
        dwb_ref[CONV_W + 1:8, :] = jnp.zeros((8 - CONV_W - 1, cb), F32)
        du_ref[...] = du.astype(BF16)

    return pl.pallas_call(
        body, name="conv_silu_bwd", grid=(cd // cb,),
        in_specs=[pl.BlockSpec((t_dim, cb), lambda j: (0, p0 + j)),
                  pl.BlockSpec((CONV_W, cb), lambda j: (0, c0 + j)), pl.BlockSpec((1, cb), lambda j: (0, c0 + j)),
                  pl.BlockSpec((2, t_dim, cb), lambda j: (0, 0, j))],
        out_specs=[pl.BlockSpec((t_dim, cb), lambda j: (0, j)), pl.BlockSpec((8, cb), lambda j: (0, j))],
        out_shape=[jax.ShapeDtypeStruct((t_dim, cd), BF16), jax.ShapeDtypeStruct((8, cd), F32)],
        compiler_params=_cp("parallel"),
    )(proj, conv_w, conv_b, dact2)


def _softplus_fwd(dt_raw, bias):
    def body(r_ref, b_ref, o_ref):
        v = r_ref[...] + b_ref[...]
        o_ref[...] = jnp.maximum(v, 0.0) + jnp.log(1.0 + jnp.exp(-jnp.abs(v)))

    return pl.pallas_call(body, name="softplus", out_shape=jax.ShapeDtypeStruct(dt_raw.shape, F32))(dt_raw, bias)


def _softplus_bwd(dt_raw, bias, ddt, da, a_log):
    def body(r_ref, b_ref, ddt_ref, da_ref, al_ref, dr_ref, db_ref, dal_ref):
        dv = ddt_ref[...] * _sigmoid(r_ref[...] + b_ref[...])
        dr_ref[...] = dv.astype(BF16)
        db_ref[...] = jnp.sum(dv, axis=0, keepdims=True)
        dal_ref[...] = -da_ref[...] * jnp.exp(al_ref[...])

    vec = jax.ShapeDtypeStruct(bias.shape, F32)
    return pl.pallas_call(
        body, name="softplus_bwd",
        out_shape=[jax.ShapeDtypeStruct(dt_raw.shape, BF16), vec, vec])(dt_raw, bias, ddt, da, a_log)


def _chunk_setup(d, dt_ref, al_ref):
    q = SSD_CHUNK
    ii = lax.broadcasted_iota(jnp.int32, (q, q), 0)
    jj = lax.broadcasted_iota(jnp.int32, (q, q), 1)
    sgn = 1 - 2 * d
    mask = (jj - ii) * sgn <= 0
    mask_t = (ii - jj) * sgn <= 0
    m01 = mask.astype(F32)
    m01_t = mask_t.astype(F32)
    dt = dt_ref[...]
    a = -jnp.exp(al_ref[...])
    dta = dt * a
    cs = _dot_exact01(m01, dta)
    tot = jnp.sum(dta, axis=0, keepdims=True)
    return mask, mask_t, m01, m01_t, dt, a, dta, cs, tot


def _head_lanes():
    hh = lax.broadcasted_iota(jnp.int32, (HEADS_PER_GROUP, GROUP_W), 0)
    ll = lax.broadcasted_iota(jnp.int32, (HEADS_PER_GROUP, GROUP_W), 1)
    return jnp.logical_and(ll >= hh * SSD_HEAD_DIM, ll < (hh + 1) * SSD_HEAD_DIM).astype(BF16)


def _expand(v, e16):
    h1, h2, h3 = _split3(v)
    return _dot(h1, e16) + _dot(h2, e16) + _dot(h3, e16)


def _expand_row(v, e16):
    return _expand(jnp.broadcast_to(v, (8, HEADS_PER_GROUP)), e16)[0:1]


def _per_head(v, e16):
    h1, h2, h3 = _split3(v)
    return _dot_nt(h1, e16) + _dot_nt(h2, e16) + _dot_nt(h3, e16)


def _rows_of(cs):
    n = cs.shape[1]
    eye = lax.broadcasted_iota(jnp.int32, (n, n), 0) == lax.broadcasted_iota(jnp.int32, (n, n), 1)
    return _dot_nt_exact01(eye.astype(F32), cs)


def _ssd_specs(t_dim, di, g_cnt, chunk_of):
    q, ns = SSD_CHUNK, SSD_STATE
    x_spec = pl.BlockSpec((q, GROUP_W), lambda d, g, c: (chunk_of(d, c), g))
    b_spec = pl.BlockSpec((q, ns), lambda d, g, c: (chunk_of(d, c), di // ns + g))
    c_spec = pl.BlockSpec((q, ns), lambda d, g, c: (chunk_of(d, c), di // ns + g_cnt + g))
    dt_spec = pl.BlockSpec((None, None, q, HEADS_PER_GROUP), lambda d, g, c: (d, g, chunk_of(d, c), 0))
    al_spec = pl.BlockSpec((None, None, 1, HEADS_PER_GROUP), lambda d, g, c: (d, g, 0, 0))
    return x_spec, b_spec, c_spec, dt_spec, al_spec


def _ssd_fwd(xbc, dt4, al4):
    t_dim = xbc.shape[0]
    g_cnt = dt4.shape[1]
    di = g_cnt * GROUP_W
    q, ns, p = SSD_CHUNK, SSD_STATE, SSD_HEAD_DIM
    nc = t_dim // q

    def chunk_of(d, c):
        return c + d * (nc - 1 - 2 * c)

    def body(x_ref, b_ref, c_ref, dt_ref, al_ref, y_ref, hin_ref, h_sc):
        d = pl.program_id(0)
        c = pl.program_id(2)

        @pl.when(c == 0)
        def _():
            h_sc[...] = jnp.zeros_like(h_sc)

        mask, mask_t, m01, m01_t, dt, a, dta, cs, tot = _chunk_setup(d, dt_ref, al_ref)
        e16 = _head_lanes()
        cs_rows = _rows_of(cs)
        cb16 = c_ref[...].astype(BF16)
        bt16 = b_ref[...].T.astype(BF16)
        cb = _dot(cb16, bt16)
        hin = h_sc[...]
        hin_ref[...] = hin
        xdt = x_ref[...] * _expand(dt, e16)
        y_off = _dot(cb16, hin.astype(BF16)) * _expand(jnp.exp(cs), e16)
        low = lax.broadcasted_iota(jnp.int32, (q, 2 * p), 1) < p
        for pair in range(HEADS_PER_GROUP // 2):
            ps = slice(2 * p * pair, 2 * p * (pair + 1))
            blk = xdt[:, ps]
            acc = y_off[:, ps]
            for hh in range(2):
                h = 2 * pair + hh
                lmat = jnp.exp(jnp.where(mask, cs[:, h:h + 1] - cs_rows[h:h + 1, :], -1e30))
                xm = jnp.where(low, blk, 0.0) if hh == 0 else jnp.where(low, 0.0, blk)
                acc = acc + _dot((cb * lmat).astype(BF16), xm.astype(BF16))
            y_ref[:, ps] = acc
        xd = xdt * _expand(jnp.exp(tot - cs), e16)
        st = _dot(bt16, xd.astype(BF16))
        h_sc[...] = hin * _expand_row(jnp.exp(tot), e16) + st

    x_spec, b_spec, c_spec, dt_spec, al_spec = _ssd_specs(t_dim, di, g_cnt, chunk_of)
    return pl.pallas_call(
        body, name="ssd_fwd", grid=(2, g_cnt, nc),
        in_specs=[x_spec, b_spec, c_spec, dt_spec, al_spec],
        out_specs=[pl.BlockSpec((None, q, GROUP_W), lambda d, g, c: (d, chunk_of(d, c), g)),
                   pl.BlockSpec((None, None, None, ns, GROUP_W), lambda d, g, c: (d, g, chunk_of(d, c), 0, 0))],
        out_shape=[jax.ShapeDtypeStruct((2, t_dim, di), F32),
                   jax.ShapeDtypeStruct((2, g_cnt, nc, ns, GROUP_W), F32)],
        scratch_shapes=[pltpu.VMEM((ns, GROUP_W), F32)],
        compiler_params=_cp("parallel", "parallel", "arbitrary"),
    )(xbc, xbc, xbc, dt4, al4)


def _ssd_bwd(xbc, dt4, al4, hin_all, dy, dvec):
    t_dim = xbc.shape[0]
    g_cnt = dt4.shape[1]
    di = g_cnt * GROUP_W
    q, ns, p, hg = SSD_CHUNK, SSD_STATE, SSD_HEAD_DIM, HEADS_PER_GROUP
    nc = t_dim // q

    def chunk_of(d, c):
        return (nc - 1 - c) + d * (2 * c - nc + 1)

    def body(x_ref, b_ref, c_ref, dt_ref, al_ref, hin_ref, dy_ref, dv_ref,
             dx_ref, db_ref, dc_ref, ddt_ref, da_ref, g_sc, dxdt_sc, zrow_sc):
        d = pl.program_id(0)
        c = pl.program_id(2)

        @pl.when(c == 0)
        def _():
            g_sc[...] = jnp.zeros_like(g_sc)
            da_ref[...] = jnp.zeros_like(da_ref)

        mask, mask_t, m01, m01_t, dt, a, dta, cs, tot = _chunk_setup(d, dt_ref, al_ref)
        xv = x_ref[...]
        dyv = dy_ref[...]
        bb = b_ref[...].astype(BF16)
        cb16 = c_ref[...].astype(BF16)
        bt16 = b_ref[...].T.astype(BF16)
        ct16 = c_ref[...].T.astype(BF16)
        cb = _dot(cb16, bt16)
        cbt = _dot(bb, ct16)
        hin = hin_ref[...]
        hin16 = hin.astype(BF16)
        gst = g_sc[...]
        gst16 = gst.astype(BF16)
        ch = _dot(cb16, hin16)
        wst = _dot(bb, gst16)
        skip = jnp.where(d == 0, 1.0, 0.0)
        e16 = _head_lanes()
        cs_rows = _rows_of(cs)
        dt_x = _expand(dt, e16)
        e_x = _expand(jnp.exp(cs), e16)
        dec = jnp.exp(tot - cs)
        etot = jnp.exp(tot)
        xdt = xv * dt_x
        dye = dyv * e_x
        t1 = _per_head(wst * xdt, e16) * dec
        dcs = _per_head(dye * ch, e16) - t1
        dtot = jnp.sum(t1, axis=0, keepdims=True)
        dec_x = _expand(dec, e16)
        dxdt_state = wst * dec_x
        dcb = jnp.zeros((q, q), F32)
        low = lax.broadcasted_iota(jnp.int32, (q, 2 * p), 1) < p
        col8 = lax.broadcasted_iota(jnp.int32, (q, hg), 1)
        for pair in range(hg // 2):
            ps = slice(2 * p * pair, 2 * p * (pair + 1))
            xblk, dyblk = xdt[:, ps], dyv[:, ps]
            acc = dxdt_state[:, ps]
            for hh in range(2):
                h = 2 * pair + hh
                seg = cs[:, h:h + 1] - cs_rows[h:h + 1, :]
                lmat = jnp.exp(jnp.where(mask, seg, -1e30))
                lmat_t = jnp.exp(jnp.where(mask_t, -seg, -1e30))
                if hh == 0:
                    xm, dym = jnp.where(low, xblk, 0.0).astype(BF16), jnp.where(low, dyblk, 0.0).astype(BF16)
                else:
                    xm, dym = jnp.where(low, 0.0, xblk).astype(BF16), jnp.where(low, 0.0, dyblk).astype(BF16)
                acc = acc + _dot((cbt * lmat_t).astype(BF16), dym)
                dm = _dot_nt(dym, xm)
                z = dm * (cb * lmat)
                dcb = dcb + dm * lmat
                dcs = dcs + jnp.where(col8 == h, jnp.sum(z, axis=-1, keepdims=True), 0.0)
                zrow_sc[h:h + 1, :] = jnp.sum(z, axis=0, keepdims=True)
            dxdt_sc[:, ps] = acc
        dxdt = dxdt_sc[...]
        dx_ref[...] = dxdt * dt_x + skip * dyv * dv_ref[...]
        dye16 = dye.astype(BF16)
        dcb16 = dcb.astype(BF16)
        dc_ref[...] = _dot_nt(dye16, hin16) + _dot(dcb16, bb)
        db_ref[...] = _dot_nt((xdt * dec_x).astype(BF16), gst16) + _dot(dcb.T.astype(BF16), cb16)
        g_sc[...] = _dot(ct16, dye16) + gst * _expand_row(etot, e16)
        carried = jnp.broadcast_to(jnp.sum(gst * hin, axis=0, keepdims=True), (8, GROUP_W))
        dtot = dtot + _per_head(carried, e16)[0:1] * etot
        ddta = _dot_exact01(m01_t, dcs) - _dot_nt_exact01(m01_t, zrow_sc[...]) + dtot
        ddt_ref[...] = _per_head(dxdt * xv, e16) + ddta * a
        da_ref[...] += jnp.sum(ddta * dt, axis=0, keepdims=True)

    x_spec, b_spec, c_spec, dt_spec, al_spec = _ssd_specs(t_dim, di, g_cnt, chunk_of)
    hin_spec = pl.BlockSpec((None, None, None, ns, GROUP_W), lambda d, g, c: (d, g, chunk_of(d, c), 0, 0))
    dy_spec = pl.BlockSpec((q, GROUP_W), lambda d, g, c: (chunk_of(d, c), g))
    dv_spec = pl.BlockSpec((1, GROUP_W), lambda d, g, c: (0, g))
    gn = g_cnt * ns
    return pl.pallas_call(
        body, name="ssd_bwd", grid=(2, g_cnt, nc),
        in_specs=[x_spec, b_spec, c_spec, dt_spec, al_spec, hin_spec, dy_spec, dv_spec],
        out_specs=[pl.BlockSpec((None, q, GROUP_W), lambda d, g, c: (d, chunk_of(d, c), g)),
                   pl.BlockSpec((None, q, ns), lambda d, g, c: (d, chunk_of(d, c), g)),
                   pl.BlockSpec((None, q, ns), lambda d, g, c: (d, chunk_of(d, c), g)),
                   pl.BlockSpec((None, None, q, hg), lambda d, g, c: (d, g, chunk_of(d, c), 0)),
                   pl.BlockSpec((None, None, 1, hg), lambda d, g, c: (d, g, 0, 0))],
        out_shape=[jax.ShapeDtypeStruct((2, t_dim, di), F32), jax.ShapeDtypeStruct((2, t_dim, gn), F32),
                   jax.ShapeDtypeStruct((2, t_dim, gn), F32), jax.ShapeDtypeStruct((2, g_cnt, t_dim, hg), F32),
                   jax.ShapeDtypeStruct((2, g_cnt, 1, hg), F32)],
        scratch_shapes=[pltpu.VMEM((ns, GROUP_W), F32), pltpu.VMEM((q, GROUP_W), F32), pltpu.VMEM((hg, q), F32)],
        compiler_params=_cp("parallel", "parallel", "arbitrary"),
    )(xbc, xbc, xbc, dt4, al4, hin_all, dy, dvec)


def _gate_norm(y2, xbc, proj, dvec, ng):
    t_dim, di = y2.shape[1:]
    tr = _tile(t_dim, 128, 8)

    def body(y2_ref, x_ref, z_ref, dv_ref, ng_ref, o_ref):
        y = y2_ref[0] + y2_ref[1] + x_ref[...] * dv_ref[...]
        z = z_ref[...]
        v = y * z * _sigmoid(z)
        r = lax.rsqrt(jnp.mean(v * v, axis=-1, keepdims=True) + EPS)
        o_ref[...] = (v * r * ng_ref[...]).astype(BF16)

    row = pl.BlockSpec((tr, di), lambda i: (i, 0))
    vec = pl.BlockSpec((1, di), lambda i: (0, 0))
    return pl.pallas_call(
        body, name="gate_norm", grid=(t_dim // tr,),
        in_specs=[pl.BlockSpec((2, tr, di), lambda i: (0, i, 0)), row, row, vec, vec],
        out_specs=row, out_shape=jax.ShapeDtypeStruct((t_dim, di), BF16),
        compiler_params=_cp("parallel"),
    )(y2, xbc, proj, dvec, ng)


def _gate_norm_bwd(y2, xbc, proj, dvec, ng, dyn):
    t_dim, di = y2.shape[1:]
    tr = _tile(t_dim, 128, 8)

    def body(y2_ref, x_ref, z_ref, dv_ref, ng_ref, dyn_ref, dy_ref, dz_ref, dng_ref, dd_ref):
        i = pl.program_id(0)
        xv = x_ref[...]
        y = y2_ref[0] + y2_ref[1] + xv * dv_ref[...]
        z = z_ref[...]
        s = _sigmoid(z)
        v = y * z * s
        r = lax.rsqrt(jnp.mean(v * v, axis=-1, keepdims=True) + EPS)
        n = v * r
        dynv = dyn_ref[...]
        dn = dynv * ng_ref[...]
        dv = r * (dn - n * jnp.mean(dn * n, axis=-1, keepdims=True))
        dy = dv * z * s

        @pl.when(i == 0)
        def _():
            dng_ref[...] = jnp.zeros_like(dng_ref)
            dd_ref[...] = jnp.zeros_like(dd_ref)

        dng_ref[...] += jnp.sum(dynv * n, axis=0, keepdims=True)
        dd_ref[...] += jnp.sum(dy * xv, axis=0, keepdims=True)
        dy_ref[...] = dy
        dz_ref[...] = (dv * y * (s * (1.0 + z * (1.0 - s)))).astype(BF16)

    row = pl.BlockSpec((tr, di), lambda i: (i, 0))
    vec = pl.BlockSpec((1, di), lambda i: (0, 0))
    return pl.pallas_call(
        body, name="gate_norm_bwd", grid=(t_dim // tr,),
        in_specs=[pl.BlockSpec((2, tr, di), lambda i: (0, i, 0)), row, row, vec, vec, row],
        out_specs=[row, row, vec, vec],
        out_shape=[jax.ShapeDtypeStruct((t_dim, di), F32), jax.ShapeDtypeStruct((t_dim, di), BF16),
                   jax.ShapeDtypeStruct((1, di), F32), jax.ShapeDtypeStruct((1, di), F32)],
        compiler_params=_cp("arbitrary"),
    )(y2, xbc, proj, dvec, ng, dyn)


def _dt_to_groups(dt):
    t_dim, h2 = dt.shape
    g_cnt = h2 // 2 // HEADS_PER_GROUP
    return dt.reshape(t_dim, 2, g_cnt, HEADS_PER_GROUP).transpose(1, 2, 0, 3)


def _dt_from_groups(dt4):
    _, g_cnt, t_dim, hg = dt4.shape
    return dt4.transpose(2, 0, 1, 3).reshape(t_dim, 2 * g_cnt * hg)


def _ssd_mixer_fwd(x, norm_g, w_in, conv_w, conv_b, dt_bias, a_log, d_skip, ssd_norm, w_out, j):
    heads = d_skip.shape[0]
    di = heads * SSD_HEAD_DIM
    g_cnt = heads // HEADS_PER_GROUP
    cd = conv_w.shape[-1]
    hn = _rmsnorm(x, norm_g)
    proj = _mm(hn, w_in, name="ssd_proj", tb=True, b_sel=(j,), tn=1152)
    xbc = _conv_silu(proj, conv_w, conv_b, di)
    dt_raw = proj[:, di + cd:]
    dt = _softplus_fwd(dt_raw, dt_bias)
    dt4 = _dt_to_groups(dt)
    al4 = a_log.reshape(2, g_cnt, 1, HEADS_PER_GROUP)
    y2, hin = _ssd_fwd(xbc, dt4, al4)
    dvec = jnp.repeat(d_skip, SSD_HEAD_DIM).reshape(1, di)
    yn = _gate_norm(y2, xbc, proj, dvec, ssd_norm)
    x_new = _mm(yn, w_out, name="ssd_out", b_sel=(j,), add=x)
    return x_new, (x, hn, proj, xbc, dt_raw, dt4, al4, y2, hin, dvec, yn)


def _ssd_mixer_bwd(dx, saved, norm_g, w_in, conv_w, conv_b, dt_bias, a_log, ssd_norm, w_out, j):
    x, hn, proj, xbc, dt_raw, dt4, al4, y2, hin, dvec, yn = saved
    di = dvec.shape[1]
    heads = di // SSD_HEAD_DIM
    d_wout = _mm(yn, dx, name="ssd_dwout", ta=True, out_dtype=BF16)
    dyn = _mm(dx, w_out, name="ssd_dyn", tb=True, b_sel=(j,))
    dy, dz, d_ng, dd_col = _gate_norm_bwd(y2, xbc, proj, dvec, ssd_norm, dyn)
    dx2, db2, dc2, ddt4, da4 = _ssd_bwd(xbc, dt4, al4, hin, dy, dvec)
    gn = db2.shape[-1]
    conv_parts = [_conv_silu_bwd(proj, conv_w, conv_b, part, di, ch0)
                  for part, ch0 in ((dx2, 0), (db2, di), (dc2, di + gn))]
    dwb = jnp.concatenate([p[1] for p in conv_parts], axis=-1)
    ddt_raw, d_bias, d_alog = _softplus_bwd(dt_raw, dt_bias, _dt_from_groups(ddt4), da4.reshape(1, 2 * heads), a_log)
    dproj = jnp.concatenate([dz] + [p[0] for p in conv_parts] + [ddt_raw], axis=-1)
    d_win = _mm(dproj, hn, name="ssd_dwin", ta=True, out_dtype=BF16, tm=1152)
    dhn = _mm(dproj, w_in, name="ssd_dhn", b_sel=(j,), tk=1152)
    dx_new, d_norm = _rmsnorm_bwd(x, norm_g, dhn, dx)
    small = dict(mix_norm=d_norm, conv_w=dwb[:CONV_W], conv_b=dwb[CONV_W:CONV_W + 1], dt_bias=d_bias, a_log=d_alog,
                 ssd_d=dd_col.reshape(heads, SSD_HEAD_DIM).sum(axis=1), ssd_norm=d_ng)
    return dx_new, small, d_win, d_wout


def _pool_count(t_idx, w, t_dim):
    hi = jnp.minimum(t_idx + w // 2, t_dim)
    lo = jnp.maximum(t_idx - w // 2, 0)
    return (hi - lo).astype(F32)


def _pool_mix(u, transpose):
    t_dim, d = u.shape
    gd = d // len(POOL_WINDOWS)
    cb = _tile(gd, 256)
    per = gd // cb

    def body(u_ref, o_ref):
        gi = pl.program_id(0)
        t_idx = lax.broadcasted_iota(jnp.int32, (t_dim, cb), 0)
        uv = u_ref[...]
        for widx, w in enumerate(POOL_WINDOWS):
            @pl.when(gi == widx)
            def _(w=w):
                cnt = _pool_count(t_idx, w, t_dim)
                src = uv / cnt if transpose else uv
                acc = jnp.zeros_like(uv)
                for k in range(-(w // 2), w // 2):
                    acc = acc + _shifted(src, -k if transpose else k, t_idx)
                res = acc - uv if transpose else acc / cnt - uv
                o_ref[...] = res.astype(BF16)

    spec = pl.BlockSpec((t_dim, cb), lambda gi, j: (0, gi * per + j))
    return pl.pallas_call(
        body, name="pool_mix_t" if transpose else "pool_mix", grid=(len(POOL_WINDOWS), per),
        in_specs=[spec], out_specs=spec, out_shape=jax.ShapeDtypeStruct((t_dim, d), BF16),
        compiler_params=_cp("parallel", "parallel"),
    )(u)


def _pool_group(mix, wgrp, scale, j):
    t_dim, d = mix.shape
    gd = wgrp.shape[-1]
    tm = _tile(t_dim, 512)

    def body(m_ref, w_ref, s_ref, v_ref, vs_ref):
        v = _dot(m_ref[...], w_ref[...])
        v_ref[...] = v
        vs_ref[...] = (v * s_ref[...]).astype(BF16)

    tile = pl.BlockSpec((tm, gd), lambda gi, m: (m, gi))
    return pl.pallas_call(
        body, name="pool_group", grid=(d // gd, t_dim // tm),
        in_specs=[tile, pl.BlockSpec((None, None, gd, gd), lambda gi, m: (j, gi, 0, 0)),
                  pl.BlockSpec((1, gd), lambda gi, m: (0, gi))],
        out_specs=[tile, tile],
        out_shape=[jax.ShapeDtypeStruct((t_dim, d), F32), jax.ShapeDtypeStruct((t_dim, d), BF16)],
        compiler_params=_cp("parallel", "parallel"),
    )(mix, wgrp, scale)


def _pool_group_bwd(dvs, v, mix, wgrp, scale, j):
    t_dim, d = mix.shape
    gd = wgrp.shape[-1]
    n_g = d // gd
    tm = _tile(t_dim, 512)
    nm = t_dim // tm

    def body(dvs_ref, v_ref, m_ref, w_ref, s_ref, dmix_ref, ds_ref, dw_ref, acc_ref):
        m = pl.program_id(1)
        dvsv = dvs_ref[...]

        @pl.when(m == 0)
        def _():
            ds_ref[...] = jnp.zeros_like(ds_ref)
            acc_ref[...] = jnp.zeros_like(acc_ref)

        ds_ref[...] += jnp.sum(dvsv * v_ref[...], axis=0, keepdims=True)
        dv16 = (dvsv * s_ref[...]).astype(BF16)
        dmix_ref[...] = _dot_nt(dv16, w_ref[...])
        acc_ref[...] += _dot(m_ref[...].T, dv16)

        @pl.when(m == nm - 1)
        def _():
            dw_ref[...] = acc_ref[...].astype(BF16)

    tile = pl.BlockSpec((tm, gd), lambda gi, m: (m, gi))
    vec = pl.BlockSpec((1, gd), lambda gi, m: (0, gi))
    return pl.pallas_call(
        body, name="pool_group_bwd", grid=(n_g, nm),
        in_specs=[tile, tile, tile, pl.BlockSpec((None, None, gd, gd), lambda gi, m: (j, gi, 0, 0)), vec],
        out_specs=[tile, vec, pl.BlockSpec((None, gd, gd), lambda gi, m: (gi, 0, 0))],
        out_shape=[jax.ShapeDtypeStruct((t_dim, d), F32), jax.ShapeDtypeStruct((1, d), F32),
                   jax.ShapeDtypeStruct((n_g, gd, gd), BF16)],
        scratch_shapes=[pltpu.VMEM((gd, gd), F32)],
        compiler_params=_cp("parallel", "arbitrary"),
    )(dvs, v, mix, wgrp, scale)


def _pool_mixer_fwd(x, norm_g, w_in, wgrp, scale, w_out, j):
    hn = _rmsnorm(x, norm_g)
    u = _mm(hn, w_in, name="pool_u", b_sel=(j,))
    mix = _pool_mix(u, False)
    v, vs = _pool_group(mix, wgrp, scale, j)
    x_new = _mm(vs, w_out, name="pool_out", b_sel=(j,), add=x)
    return x_new, (x, hn, mix, v, vs)


def _pool_mixer_bwd(dx, saved, norm_g, w_in, wgrp, scale, w_out, j):
    x, hn, mix, v, vs = saved
    d_wout = _mm(vs, dx, name="pool_dw", ta=True, out_dtype=BF16)
    dvs = _mm(dx, w_out, name="pool_dvs", tb=True, b_sel=(j,))
    dmix, d_scale, d_wgrp = _pool_group_bwd(dvs, v, mix, wgrp, scale, j)
    du = _pool_mix(dmix, True)
    d_win = _mm(hn, du, name="pool_dw", ta=True, out_dtype=BF16)
    dhn = _mm(du, w_in, name="pool_dhn", tb=True, b_sel=(j,))
    dx_new, d_norm = _rmsnorm_bwd(x, norm_g, dhn, dx)
    return dx_new, d_norm, d_scale, d_win, d_wgrp, d_wout


def _local_step(x, target, depth, fetch, emit, on_loss, small, tok):
    saved, wts = [], {}

    def weights(group, after):
        if group not in wts:
            wts[group] = fetch(group, after)
        return wts[group]

    for i in range(depth):
        j = i // 2
        x, s0 = _ffn_fwd(x, small["ffn_norm"][i, 0][None] + tok, *weights(("ffn", i, 0), x), 0)
        mg = small["mix_norm"][i][None]
        if i % 2 == 0:
            heads = small["ssd_d"].shape[1]
            w_in, w_out = weights(("ssd", j), x)
            x, s1 = _ssd_mixer_fwd(x, mg, w_in, small["ssd_conv_w"][j], small["ssd_conv_b"][j][None],
                                   small["ssd_dt_bias"][j].reshape(1, 2 * heads), small["ssd_a_log"][j].reshape(1, 2 * heads),
                                   small["ssd_d"][j], small["ssd_norm"][j][None], w_out, 0)
        else:
            p_in, p_grp, p_out = weights(("pool", j), x)
            x, s1 = _pool_mixer_fwd(x, mg, p_in, p_grp, small["pool_scale"][j][None], p_out, 0)
        x, s2 = _ffn_fwd(x, small["ffn_norm"][i, 1][None], *weights(("ffn", i, 1), x), 0)
        saved.append((s0, s1, s2))
    loss, dx, d_final = _loss_head(x, small["final_norm"][None], target)

    gs = {k: {} for k in ("ffn_norm", "mix_norm", "ssd_conv_w", "ssd_conv_b", "ssd_dt_bias", "ssd_a_log", "ssd_d",
                          "ssd_norm", "pool_scale")}
    tok = on_loss(loss)
    for i in reversed(range(depth)):
        j = i // 2
        s0, s1, s2 = saved[i]
        for half, sv in ((1, s2), (0, None)):
            if half == 0:
                sv = s0
                mg = small["mix_norm"][i][None] + tok
                if i % 2 == 0:
                    heads = small["ssd_d"].shape[1]
                    w_in, w_out = wts[("ssd", j)]
                    dx, sm, d_win, d_wout = _ssd_mixer_bwd(
                        dx, s1, mg, w_in, small["ssd_conv_w"][j], small["ssd_conv_b"][j][None],
                        small["ssd_dt_bias"][j].reshape(1, 2 * heads), small["ssd_a_log"][j].reshape(1, 2 * heads),
                        small["ssd_norm"][j][None], w_out, 0)
                    gs["mix_norm"][i] = sm["mix_norm"][0]
                    gs["ssd_conv_w"][j] = sm["conv_w"]
                    gs["ssd_conv_b"][j] = sm["conv_b"][0]
                    gs["ssd_dt_bias"][j] = sm["dt_bias"].reshape(2, heads)
                    gs["ssd_a_log"][j] = sm["a_log"].reshape(2, heads)
                    gs["ssd_d"][j] = sm["ssd_d"]
                    gs["ssd_norm"][j] = sm["ssd_norm"][0]
                    tok = tok + emit(("ssd", j), [d_win, d_wout])
                else:
                    p_in, p_grp, p_out = wts[("pool", j)]
                    dx, d_norm, d_scale, d_win, d_wgrp, d_wout = _pool_mixer_bwd(
                        dx, s1, mg, p_in, p_grp, small["pool_scale"][j][None], p_out, 0)
                    gs["mix_norm"][i] = d_norm[0]
                    gs["pool_scale"][j] = d_scale[0]
                    tok = tok + emit(("pool", j), [d_win, d_wgrp, d_wout])
            dx, d_norm, d_wg, d_wu, d_wd = _ffn_bwd(dx, sv, small["ffn_norm"][i, half][None] + tok,
                                                   *wts[("ffn", i, half)], 0)
            gs["ffn_norm"][(i, half)] = d_norm[0]
            tok = tok + emit(("ffn", i, half), [d_wg, d_wu, d_wd])
    n_s, n_p = (depth + 1) // 2, depth // 2
    gsmall = dict(
        ffn_norm=jnp.stack([jnp.stack([gs["ffn_norm"][(i, h)] for h in range(2)]) for i in range(depth)]),
        mix_norm=jnp.stack([gs["mix_norm"][i] for i in range(depth)]),
        ssd_conv_w=jnp.stack([gs["ssd_conv_w"][j] for j in range(n_s)]),
        ssd_conv_b=jnp.stack([gs["ssd_conv_b"][j] for j in range(n_s)]),
        ssd_dt_bias=jnp.stack([gs["ssd_dt_bias"][j] for j in range(n_s)]),
        ssd_a_log=jnp.stack([gs["ssd_a_log"][j] for j in range(n_s)]),
        ssd_d=jnp.stack([gs["ssd_d"][j] for j in range(n_s)]),
        ssd_norm=jnp.stack([gs["ssd_norm"][j] for j in range(n_s)]),
        pool_scale=jnp.stack([gs["pool_scale"][j] for j in range(n_p)]),
        final_norm=d_final[0],
    )
    return loss, dx, gsmall, tok


ANY = pl.BlockSpec(memory_space=pl.ANY)


def _mesh_pos():
    return lax.axis_index("x"), lax.axis_index("y"), lax.axis_index("c")


def _other_chips(x, y):
    return [(1 - x, y), (x, 1 - y), (1 - x, 1 - y)]


def _win(ref, windows, lead=()):
    rest = len(ref.shape) - len(lead)
    idx = tuple(lead) + tuple(pl.ds(*windows[ax]) if ax in windows else slice(None) for ax in range(rest))
    return ref.at[idx]


def _remote(src, dst, send_sems, recv_sems, k, peer):
    return pltpu.make_async_remote_copy(src_ref=src, dst_ref=dst, send_sem=send_sems.at[k], recv_sem=recv_sems.at[k],
                                        device_id=peer, device_id_type=MESH)


def _cast_place(w3, chip1, mode, l0, nl, out_dtype=BF16):
    _, r, c = w3.shape
    tr = _tile(r, 256, 16)
    nr = r // tr
    if mode == "cols":
        out_shape, blk = (nl, r, N_CHIPS * c), (None, tr, c)
        omap = lambda l, i, s: (l, i, s[0])
    elif mode == "rows":
        out_shape, blk = (nl, N_CHIPS * r, c), (None, tr, c)
        omap = lambda l, i, s: (l, s[0] * nr + i, 0)
    else:
        out_shape, blk = (nl, N_CHIPS, r, c), (None, None, tr, c)
        omap = lambda l, i, s: (l, s[0], i, 0)

    def body(s_ref, w_ref, o_ref):
        o_ref[...] = w_ref[...].astype(out_dtype)

    return pl.pallas_call(
        body, name="cast_place_" + mode, out_shape=jax.ShapeDtypeStruct(out_shape, out_dtype),
        grid_spec=pltpu.PrefetchScalarGridSpec(
            num_scalar_prefetch=1, grid=(nl, nr),
            in_specs=[pl.BlockSpec((None, tr, c), lambda l, i, s: (l0 + l, i, 0))],
            out_specs=pl.BlockSpec(blk, omap)),
        compiler_params=_cp("parallel", "parallel"),
    )(chip1, w3)


def _gather_begin(name, fulls, axes):
    n = len(fulls)
    shapes = [f.shape for f in fulls]

    def full_win(refs, t, sidx, hidx):
        sa, ha = axes[t]
        ssz, hsz = shapes[t][sa] // N_CHIPS, shapes[t][ha] // 2
        return _win(refs[t], {sa: (sidx * ssz, ssz), ha: (hidx * hsz, hsz)})

    def chips_plan(refs, x, y, c):
        mine = 2 * x + y
        return [(full_win(refs, t, mine, c), full_win(refs, t, mine, c), (px, py, c))
                for t in range(n) for px, py in _other_chips(x, y)]

    def sibling_plan(in_refs, out_refs, x, y, c):
        wins = [full_win(out_refs, t, 2 * px + py, c) for t in range(n) for px, py in _other_chips(x, y)]
        return [(w, w, (x, y, 1 - c)) for w in wins]

    sems, thru, token = _split_start(name + "_start", fulls, chips_plan, 3 * n)
    return (name, sems, thru, chips_plan, sibling_plan, 3 * n), token


def _gather_finish(pending, after):
    name, sems, thru, chips_plan, sibling_plan, n_copies = pending
    landed = _split_wait(name + "_wait", sems, thru, chips_plan, n_copies, after)
    return _exchange(name + "_sibling", landed, [jax.ShapeDtypeStruct(f.shape, f.dtype) for f in landed],
                     sibling_plan, n_copies, inplace=True)


def _exchange(name, inputs, out_shapes, plan, n_copies, inplace=False):
    n_in, n_out = len(inputs), len(out_shapes)

    def body(*refs):
        in_refs, out_refs = refs[:n_in], refs[n_in:n_in + n_out]
        send_sems, recv_sems = refs[n_in + n_out:]
        x, y, c = _mesh_pos()
        copies = plan(in_refs, out_refs, x, y, c)
        assert len(copies) == n_copies
        started = []
        for k, (src, dst, peer) in enumerate(copies):
            cp = _remote(src, dst, send_sems, recv_sems, k, peer)
            cp.start()
            started.append(cp)
        for cp in started:
            cp.wait()

    return pl.pallas_call(
        body, name=name, in_specs=[ANY] * n_in, out_specs=[ANY] * n_out, out_shape=out_shapes,
        input_output_aliases={t: t for t in range(n_in)} if inplace else {},
        scratch_shapes=[pltpu.SemaphoreType.DMA((n_copies,)), pltpu.SemaphoreType.DMA((n_copies,))],
    )(*inputs)


HBM = pl.BlockSpec(memory_space=pltpu.HBM)
SEM = pl.BlockSpec(memory_space=pltpu.SEMAPHORE)
DATAFLOW = pltpu.SideEffectType.DATAFLOW_SIDE_EFFECTING


def _split_start(name, bufs, plan, n_copies):
    n = len(bufs)

    def body(*refs):
        ins = refs[:n]
        send_sems, recv_sems = refs[n], refs[n + 1]
        token = refs[2 * n + 2]
        x, y, c = _mesh_pos()
        copies = plan(ins, x, y, c)
        assert len(copies) == n_copies
        for k, (src, dst, peer) in enumerate(copies):
            _remote(src, dst, send_sems, recv_sems, k, peer).start()
        token[...] = jnp.zeros_like(token)

    outs = pl.pallas_call(
        body, name=name,
        out_shape=(pltpu.SemaphoreType.DMA((n_copies,)), pltpu.SemaphoreType.DMA((n_copies,)),
                   *[pltpu.HBM(b.shape, b.dtype) for b in bufs], jax.ShapeDtypeStruct((8, LANE), F32)),
        in_specs=[HBM] * n, out_specs=(SEM, SEM, *[HBM] * n, pl.BlockSpec(memory_space=pltpu.VMEM)),
        input_output_aliases={t: 2 + t for t in range(n)},
        compiler_params=pltpu.CompilerParams(has_side_effects=DATAFLOW),
    )(*[pltpu.with_memory_space_constraint(b, pltpu.HBM) for b in bufs])
    return (outs[0], outs[1]), list(outs[2:2 + n]), outs[2 + n][0, 0]


def _split_wait(name, sems, bufs, plan, n_copies, after):
    n = len(bufs)

    def body(*refs):
        ins = refs[:n]
        send_sems, recv_sems = refs[n], refs[n + 1]
        x, y, c = _mesh_pos()
        copies = plan(ins, x, y, c)
        assert len(copies) == n_copies
        for k, (src, dst, peer) in enumerate(copies):
            cp = _remote(src, dst, send_sems, recv_sems, k, peer)
            cp.wait_send()
            cp.wait_recv()

    outs = pl.pallas_call(
        body, name=name, out_shape=tuple(pltpu.HBM(b.shape, b.dtype) for b in bufs),
        in_specs=[HBM] * n + [SEM, SEM, ANY], out_specs=tuple([HBM] * n),
        input_output_aliases={t: t for t in range(n)},
        compiler_params=pltpu.CompilerParams(has_side_effects=DATAFLOW),
    )(*bufs, sems[0], sems[1], after)
    return list(outs)


def _halved(shape, ha):
    out = list(shape)
    out[ha] //= 2
    return tuple(out)


def _sharded(shape, sa):
    out = list(shape)
    out[sa] //= N_CHIPS
    return tuple(out)


def _rs_cores_begin(name, grads, axes):
    n = len(grads)
    shapes = [g.shape for g in grads]
    landing = [lax.empty(_halved(g.shape, ha), g.dtype) for g, (_, ha) in zip(grads, axes)]

    def plan(refs, x, y, c):
        copies = []
        for t in range(n):
            ha = axes[t][1]
            hsz = shapes[t][ha] // 2
            copies.append((_win(refs[t], {ha: ((1 - c) * hsz, hsz)}), refs[n + t], (x, y, 1 - c)))
        return copies

    sems, thru, token = _split_start(name + "_start", list(grads) + landing, plan, n)
    return (name, sems, thru, plan, n), token


def _rs_cores_finish(pending, after):
    name, sems, thru, plan, n = pending
    done = _split_wait(name + "_wait", sems, thru, plan, n, after)
    return done[:n], done[n:]


def _small_begin(vec):
    landing = lax.empty((8,) + vec.shape, vec.dtype)

    def plan(refs, x, y, c):
        me = 4 * x + 2 * y + c
        copies = []
        for k in range(1, 8):
            kx, ky, kc = k // 4, (k // 2) % 2, k % 2
            copies.append((refs[0], refs[1].at[me], (x ^ kx, y ^ ky, c ^ kc)))
        return copies

    sems, thru, token = _split_start("small_start", [vec, landing], plan, 7)
    return (sems, thru, plan), token


def _small_finish(pending, after, me1):
    sems, thru, plan = pending
    vec, landing = _split_wait("small_wait", sems, thru, plan, 7, after)

    def body(s_ref, v_ref, l_ref, o_ref):
        tot = jnp.where(s_ref[0] == 0, v_ref[...], l_ref[0])
        for i in range(1, 8):
            tot = tot + jnp.where(s_ref[0] == i, v_ref[...], l_ref[i])
        o_ref[...] = tot

    rows = vec.shape[0]
    return pl.pallas_call(
        body, name="small_sum", out_shape=jax.ShapeDtypeStruct(vec.shape, F32),
        grid_spec=pltpu.PrefetchScalarGridSpec(
            num_scalar_prefetch=1, grid=(1,),
            in_specs=[pl.BlockSpec((rows, LANE), lambda i, s: (0, 0)), pl.BlockSpec((8, rows, LANE), lambda i, s: (0, 0, 0))],
            out_specs=pl.BlockSpec((rows, LANE), lambda i, s: (0, 0))),
    )(me1, vec, landing)


def _rs_chips_begin(name, halves, axes):
    n = len(halves)
    shapes = [h.shape for h in halves]
    landing = [lax.empty((N_CHIPS - 1,) + _sharded(h.shape, sa), h.dtype) for h, (sa, _) in zip(halves, axes)]

    def plan(refs, x, y, c):
        copies = []
        for t in range(n):
            sa = axes[t][0]
            ssz = shapes[t][sa] // N_CHIPS
            for j, (px, py) in enumerate(_other_chips(x, y)):
                copies.append((_win(refs[t], {sa: ((2 * px + py) * ssz, ssz)}), refs[n + t].at[j], (px, py, c)))
        return copies

    sems, thru, token = _split_start(name + "_start", list(halves) + landing, plan, 3 * n)
    return (name, sems, thru, plan, 3 * n), token


def _rs_chips_finish(pending, after):
    name, sems, thru, plan, n_copies = pending
    done = _split_wait(name + "_wait", sems, thru, plan, n_copies, after)
    n = len(done) // 2
    return done[:n], done[n:]


def _rs_finish_begin(name, sums):
    n = len(sums)
    landing = lax.empty((n,) + sums[0].shape, sums[0].dtype)

    def plan(refs, x, y, c):
        return [(refs[t], refs[n].at[t], (x, y, 1 - c)) for t in range(n)]

    sems, thru, token = _split_start(name + "_start", list(sums) + [landing], plan, n)
    return (name, sems, thru, plan, n), token


def _rs_finish_end(pending, after):
    name, sems, thru, plan, n = pending
    done = _split_wait(name + "_wait", sems, thru, plan, n, after)
    return done[:n], done[n]


def _rows2d(a):
    return a.reshape(-1, a.shape[-1])


def _view3(a):
    return a.reshape((-1,) + a.shape[-2:])


def _add_pair(g, recv, ha, core1):
    g3, r3 = _view3(g), _view3(recv)
    rows_half = ha + 3 - g.ndim == 1
    n_l, r, c = r3.shape
    tr = _tile(r, 256, 16)
    nr = r // tr
    gmap =(lambda l, i, s: (l, s[0] * nr + i, 0)) if rows_half else (lambda l, i, s: (l, i, s[0]))

    def body(s_ref, g_ref, r_ref, o_ref):
        o_ref[...] = (g_ref[...].astype(F32) + r_ref[...].astype(F32)).astype(BF16)

    spec = pl.BlockSpec((None, tr, c), lambda l, i, s: (l, i, 0))
    out = pl.pallas_call(
        body, name="add_pair", out_shape=jax.ShapeDtypeStruct(r3.shape, BF16),
        grid_spec=pltpu.PrefetchScalarGridSpec(num_scalar_prefetch=1, grid=(n_l, nr),
                                               in_specs=[pl.BlockSpec((None, tr, c), gmap), spec], out_specs=spec),
        compiler_params=_cp("parallel", "parallel"))(core1, g3, r3)
    return out.reshape(recv.shape)


def _add_four(cs, recv, sa, chip1):
    c3 = _view3(cs)
    s3 = sa + 3 - cs.ndim
    lo, ro, co = (dim // N_CHIPS if ax == s3 else dim for ax, dim in enumerate(c3.shape))
    r4 = recv.reshape(N_CHIPS - 1, lo, ro, co)
    tr = _tile(ro, 256, 16)
    nr = ro // tr
    if s3 == 0:
        cmap = lambda l, i, s: (s[0] * lo + l, i, 0)
    elif s3 == 1:
        cmap = lambda l, i, s: (l, s[0] * nr + i, 0)
    else:
        cmap = lambda l, i, s: (l, i, s[0])

    def body(s_ref, c_ref, r_ref, out_ref):
        out_ref[...] = ((c_ref[...].astype(F32) + r_ref[0].astype(F32)) + r_ref[1].astype(F32)) + r_ref[2].astype(F32)

    out = pl.pallas_call(
        body, name="add_four", out_shape=jax.ShapeDtypeStruct((lo, ro, co), F32),
        grid_spec=pltpu.PrefetchScalarGridSpec(
            num_scalar_prefetch=1, grid=(lo, nr),
            in_specs=[pl.BlockSpec((None, tr, co), cmap),
                      pl.BlockSpec((N_CHIPS - 1, None, tr, co), lambda l, i, s: (0, l, i, 0))],
            out_specs=pl.BlockSpec((None, tr, co), lambda l, i, s: (l, i, 0))),
        compiler_params=_cp("parallel", "parallel"))(chip1, c3, r4)
    return out.reshape(recv.shape[1:])


def _adamw_halves(w, own, recv, m, v, rows_half, core1, l0=0, prev=None):
    w3, m3, v3, o3, r3 = (_view3(a) for a in (w, m, v, own, recv))
    _, _, c = w3.shape
    n_l, rh, ch = o3.shape
    tr = _tile(rh, 128, 8)
    nr = rh // tr
    c1 = 1.0 - ADAM_B1 ** ADAM_STEP
    c2 = 1.0 - ADAM_B2 ** ADAM_STEP

    n_prev = 0 if prev is None else 4

    def body(s_ref, w_ref, o_ref, r_ref, m_ref, v_ref, *rest):
        g_ref, d_ref, nm_ref, nv_ref = rest[n_prev:]
        gv = jnp.where(pl.program_id(1) == s_ref[0], o_ref[...], r_ref[...])
        nm = ADAM_B1 * m_ref[...] + (1.0 - ADAM_B1) * gv
        nv = ADAM_B2 * v_ref[...] + (1.0 - ADAM_B2) * (gv * gv)
        g_ref[...] = gv
        nm_ref[...] = nm
        nv_ref[...] = nv
        d_ref[...] = -ADAM_LR * ((nm / c1) / (jnp.sqrt(nv / c2) + ADAM_EPS) + ADAM_WD * w_ref[...])

    wmap = (lambda l, h, i, s: (l0 + l, h * nr + i, 0)) if rows_half else (lambda l, h, i, s: (l0 + l, i, h))
    wspec = pl.BlockSpec((None, tr, ch), wmap)
    ospec = pl.BlockSpec((None, tr, ch), lambda l, h, i, s: (jnp.where(h == s[0], l, 0), jnp.where(h == s[0], i, 0), 0))
    rspec = pl.BlockSpec((None, tr, ch), lambda l, h, i, s: (jnp.where(h == s[0], 0, l), jnp.where(h == s[0], 0, i), 0))
    osh = jax.ShapeDtypeStruct(w3.shape, F32)
    before = [] if prev is None else [_view3(p) for p in prev]
    outs = pl.pallas_call(
        body, name="adamw_halves", out_shape=[osh] * 4,
        grid_spec=pltpu.PrefetchScalarGridSpec(
            num_scalar_prefetch=1, grid=(n_l, 2, nr),
            in_specs=[wspec, ospec, rspec, wspec, wspec] + [ANY] * n_prev, out_specs=[wspec] * 4),
        input_output_aliases={6 + k: k for k in range(n_prev)},
        compiler_params=_cp("parallel", "parallel", "parallel"))(core1, w3, o3, r3, m3, v3, *before)
    return tuple(o.reshape(w.shape) for o in outs)


def _adamw(w, g, m, v):
    shape = w.shape
    w2, g2, m2, v2 = (_rows2d(a) if a.ndim > 1 else a.reshape(1, -1) for a in (w, g, m, v))
    rows, cols = w2.shape
    tr = _tile(rows, 256, 8)
    c1 = 1.0 - ADAM_B1 ** ADAM_STEP
    c2 = 1.0 - ADAM_B2 ** ADAM_STEP

    def body(w_ref, g_ref, m_ref, v_ref, d_ref, nm_ref, nv_ref):
        gv = g_ref[...]
        nm = ADAM_B1 * m_ref[...] + (1.0 - ADAM_B1) * gv
        nv = ADAM_B2 * v_ref[...] + (1.0 - ADAM_B2) * (gv * gv)
        nm_ref[...] = nm
        nv_ref[...] = nv
        d_ref[...] = -ADAM_LR * ((nm / c1) / (jnp.sqrt(nv / c2) + ADAM_EPS) + ADAM_WD * w_ref[...])

    spec = pl.BlockSpec((tr, cols), lambda i: (i, 0))
    osh = jax.ShapeDtypeStruct((rows, cols), F32)
    outs = pl.pallas_call(body, name="adamw", grid=(rows // tr,), in_specs=[spec] * 4, out_specs=[spec] * 3,
                          out_shape=[osh] * 3, compiler_params=_cp("parallel"))(w2, g2, m2, v2)
    return tuple(o.reshape(shape) for o in outs)


def _pack(arrs):
    flat = jnp.concatenate([a.reshape(-1) for a in arrs])
    n = flat.shape[0]
    rows = -(-n // (8 * LANE)) * 8
    return jnp.pad(flat, (0, rows * LANE - n)).reshape(rows, LANE)


def _unpack(packed, shapes):
    flat = packed.reshape(-1)
    out, pos = [], 0
    for sh in shapes:
        size = 1
        for dsz in sh:
            size *= dsz
        out.append(flat[pos:pos + size].reshape(sh))
        pos += size
    return out


BIG = ("ffn_w_gate", "ffn_w_up", "ffn_w_down", "ssd_w_in", "ssd_w_out", "pool_w_in", "pool_w_group", "pool_w_out")
WEIGHTS = ("ffn_norm", "ffn_w_gate", "ffn_w_up", "ffn_w_down", "mix_norm", "ssd_w_in", "ssd_conv_w", "ssd_conv_b",
           "ssd_dt_bias", "ssd_a_log", "ssd_d", "ssd_norm", "ssd_w_out", "pool_w_in", "pool_w_group", "pool_scale",
           "pool_w_out", "final_norm")
SMALL = tuple(k for k in WEIGHTS if k not in BIG)
SMALL_SHARDED = {"ffn_norm": 2, "ssd_conv_w": 2, "pool_scale": 1}


def kernel(x, ffn_norm, ffn_w_gate, ffn_w_up, ffn_w_down, mix_norm, ssd_w_in, ssd_conv_w, ssd_conv_b, ssd_dt_bias, ssd_a_log, ssd_d, ssd_norm, ssd_w_out, pool_w_in, pool_w_group, pool_scale, pool_w_out, final_norm, loss_target, m_ffn_norm, m_ffn_w_gate, m_ffn_w_up, m_ffn_w_down, m_mix_norm, m_ssd_w_in, m_ssd_conv_w, m_ssd_conv_b, m_ssd_dt_bias, m_ssd_a_log, m_ssd_d, m_ssd_norm, m_ssd_w_out, m_pool_w_in, m_pool_w_group, m_pool_scale, m_pool_w_out, m_final_norm, v_ffn_norm, v_ffn_w_gate, v_ffn_w_up, v_ffn_w_down, v_mix_norm, v_ssd_w_in, v_ssd_conv_w, v_ssd_conv_b, v_ssd_dt_bias, v_ssd_a_log, v_ssd_d, v_ssd_norm, v_ssd_w_out, v_pool_w_in, v_pool_w_group, v_pool_scale, v_pool_w_out, v_final_norm):
    w = dict(ffn_norm=ffn_norm, ffn_w_gate=ffn_w_gate, ffn_w_up=ffn_w_up, ffn_w_down=ffn_w_down, mix_norm=mix_norm,
             ssd_w_in=ssd_w_in, ssd_conv_w=ssd_conv_w, ssd_conv_b=ssd_conv_b, ssd_dt_bias=ssd_dt_bias,
             ssd_a_log=ssd_a_log, ssd_d=ssd_d, ssd_norm=ssd_norm, ssd_w_out=ssd_w_out, pool_w_in=pool_w_in,
             pool_w_group=pool_w_group, pool_scale=pool_scale, pool_w_out=pool_w_out, final_norm=final_norm)
    mom = dict(ffn_norm=m_ffn_norm, ffn_w_gate=m_ffn_w_gate, ffn_w_up=m_ffn_w_up, ffn_w_down=m_ffn_w_down,
               mix_norm=m_mix_norm, ssd_w_in=m_ssd_w_in, ssd_conv_w=m_ssd_conv_w, ssd_conv_b=m_ssd_conv_b,
               ssd_dt_bias=m_ssd_dt_bias, ssd_a_log=m_ssd_a_log, ssd_d=m_ssd_d, ssd_norm=m_ssd_norm,
               ssd_w_out=m_ssd_w_out, pool_w_in=m_pool_w_in, pool_w_group=m_pool_w_group, pool_scale=m_pool_scale,
               pool_w_out=m_pool_w_out, final_norm=m_final_norm)
    vel = dict(ffn_norm=v_ffn_norm, ffn_w_gate=v_ffn_w_gate, ffn_w_up=v_ffn_w_up, ffn_w_down=v_ffn_w_down,
               mix_norm=v_mix_norm, ssd_w_in=v_ssd_w_in, ssd_conv_w=v_ssd_conv_w, ssd_conv_b=v_ssd_conv_b,
               ssd_dt_bias=v_ssd_dt_bias, ssd_a_log=v_ssd_a_log, ssd_d=v_ssd_d, ssd_norm=v_ssd_norm,
               ssd_w_out=v_ssd_w_out, pool_w_in=v_pool_w_in, pool_w_group=v_pool_w_group, pool_scale=v_pool_scale,
               pool_w_out=v_pool_w_out, final_norm=v_final_norm)
    depth = ffn_w_gate.shape[0]
    n_s, n_p = ssd_w_in.shape[0], pool_w_in.shape[0]
    chip = 2 * lax.axis_index("x") + lax.axis_index("y")

    chip1 = jnp.reshape(chip, (1,)).astype(jnp.int32)
    core1 = jnp.reshape(lax.axis_index("c"), (1,)).astype(jnp.int32)

    gate3, up3, down3 = _view3(ffn_w_gate), _view3(ffn_w_up), _view3(ffn_w_down)
    w_in_t = ssd_w_in.transpose(0, 2, 1)
    pending = {}
    sharded_names = tuple(SMALL_SHARDED)
    tok = jnp.zeros((), F32)

    def begin(group, name, fulls, axes):
        nonlocal tok
        pending[group], t = _gather_begin(name, fulls, axes)
        tok = tok + t

    for i in range(depth):
        for h in range(2):
            fulls = [_cast_place(gate3, chip1, "cols", 2 * i + h, 1), _cast_place(up3, chip1, "cols", 2 * i + h, 1),
                     _cast_place(down3, chip1, "rows", 2 * i + h, 1)]
            if (i, h) == (0, 0):
                packed_small = _pack([w[k] for k in sharded_names])
                fulls.append(_cast_place(packed_small[None], chip1, "slot", 0, 1, F32))
                begin(("ffn", i, h), "gather_first", fulls, [(2, 1), (2, 1), (1, 2), (1, 2)])
            else:
                begin(("ffn", i, h), "gather_ffn", fulls, [(2, 1), (2, 1), (1, 2)])
            j = i // 2
            if h == 0 and i % 2 == 0:
                begin(("ssd", j), "gather_ssd", [_cast_place(w_in_t, chip1, "rows", j, 1),
                                                 _cast_place(ssd_w_out, chip1, "rows", j, 1)], [(1, 2), (1, 2)])
            if h == 0 and i % 2 == 1:
                n_g = pool_w_group.shape[1]
                grp_full = _cast_place(_view3(pool_w_group), chip1, "rows", j * n_g, n_g)
                begin(("pool", j), "gather_pool", [_cast_place(pool_w_in, chip1, "rows", j, 1), grp_full[None],
                                                   _cast_place(pool_w_out, chip1, "rows", j, 1)],
                      [(1, 2), (2, 3), (1, 2)])

    fetched = {}

    def fetch(group, after):
        if group not in fetched:
            got = _gather_finish(pending[group], after)
            fetched[group] = got[:3]
            if group == ("ffn", 0, 0):
                fetched["small"] = got[3][0]
        return fetched[group]

    fetch(("ffn", 0, 0), x)
    small = {k: w[k] for k in SMALL if k not in SMALL_SHARDED}
    per_chip = [_unpack(fetched["small"][s], [w[k].shape for k in sharded_names]) for s in range(N_CHIPS)]
    for t, k in enumerate(sharded_names):
        small[k] = jnp.concatenate([per_chip[s][t] for s in range(N_CHIPS)], axis=SMALL_SHARDED[k])

    ffn_names = ("ffn_w_gate", "ffn_w_up", "ffn_w_down")
    group_axes = dict(ffn=[(1, 0), (1, 0), (0, 1)], ssd=[(0, 1), (0, 1)], pool=[(0, 1), (1, 2), (0, 1)])
    group_names = dict(ffn=ffn_names, ssd=("ssd_w_in", "ssd_w_out"), pool=("pool_w_in", "pool_w_group", "pool_w_out"))
    travelling = {}
    between_cores = []

    def advance(after):
        group, pend = between_cores.pop()
        axes = group_axes[group[0]]
        grads, recv_a = _rs_cores_finish(pend, after)
        chip_sums = [_add_pair(g, r, ha, core1) for g, r, (_, ha) in zip(grads, recv_a, axes)]
        travelling[group], t = _rs_chips_begin("rs_chips_" + group[0], chip_sums, axes)
        return t

    def emit(group, grads):
        kind = group[0]
        t = advance(grads[0]) if between_cores else jnp.zeros((), F32)
        pend, t2 = _rs_cores_begin("rs_cores_" + kind, grads, group_axes[kind])
        between_cores.append((group, pend))
        return t + t2

    total = {}

    def on_loss(part):
        total["loss"] = lax.psum(part[0, 0], ("x", "y", "c"))
        return jnp.minimum(total["loss"], 0.0)

    _, dx, gsmall, t_emits = _local_step(x[0], loss_target[0], depth, fetch, emit, on_loss, small, tok)
    loss = total["loss"]

    t_last = advance(dx)
    small_pending, t_small = _small_begin(_pack([gsmall[k] for k in SMALL]))
    started = jnp.reshape(t_emits + t_last + t_small, (1, 1))

    rows_half = dict(ffn_w_gate=True, ffn_w_up=True, ffn_w_down=False, ssd_w_in=False, ssd_w_out=False,
                     pool_w_in=False, pool_w_group=False, pool_w_out=False)
    w_t, mom_t, vel_t = ({**src, "ssd_w_in": src["ssd_w_in"].transpose(0, 2, 1)} for src in (w, mom, vel))
    results = {}

    def finish(groups, after):
        sums = {k: {} for k in BIG}
        for group in groups:
            kind = group[0]
            chip_sums, recv_b = _rs_chips_finish(travelling[group], after)
            for k, cs, r, (sa, _) in zip(group_names[kind], chip_sums, recv_b, group_axes[kind]):
                sums[k][group[1:]] = _add_four(cs, r, sa, chip1)
        names = [k for k in BIG if sums[k]]
        sending, t_sent = {}, jnp.zeros((), F32)
        for k in names:
            sending[k], t = _rs_finish_begin("rs_finish_" + k, [sums[k][idx] for idx in sorted(sums[k])])
            t_sent = t_sent + t
        before = t_sent
        for k in names:
            own, from_sibling = _rs_finish_end(sending[k], jnp.stack([t_sent, before]).reshape(1, 2))
            first = min(sums[k])
            l0 = first[0] * 2 + first[1] if k in ffn_names else first[0]
            results[k] = _adamw_halves(w_t[k], jnp.stack(own), from_sibling, mom_t[k], vel_t[k], rows_half[k],
                                       core1, l0, results.get(k))
            before = results[k][3][(0,) * results[k][3].ndim]
        return jnp.stack([results[k][3][(0,) * results[k][3].ndim] for k in names]).reshape(1, -1)

    late = [g for g in travelling if g[1] == 0 and g[0] != "pool"]
    early = [g for g in travelling if g not in late]
    done_early = finish(early, started) if early else started
    done_late = finish(late, done_early)
    results["ssd_w_in"] = tuple(r.transpose(0, 2, 1) for r in results["ssd_w_in"])
    grad, delta, new_m, new_v = ({k: results[k][t] for k in BIG} for t in range(4))

    small_shapes = [gsmall[k].shape for k in SMALL]
    me1 = 2 * chip1 + core1
    summed = _unpack(_small_finish(small_pending, done_late, me1), small_shapes)
    for k, g in zip(SMALL, summed):
        if k in SMALL_SHARDED:
            ax = SMALL_SHARDED[k]
            size = w[k].shape[ax]
            g = lax.dynamic_slice_in_dim(g, chip * size, size, axis=ax)
        grad[k] = g

    shapes = [w[k].shape for k in SMALL]
    packed = _adamw(*(_pack([src[k] for k in SMALL]) for src in (w, grad, mom, vel)))
    for dst, p in zip((delta, new_m, new_v), packed):
        for k, a in zip(SMALL, _unpack(p, shapes)):
            dst[k] = a

    return (loss, dx[None], *[grad[k] for k in WEIGHTS], *[delta[k] for k in WEIGHTS],
            *[new_m[k] for k in WEIGHTS], *[new_v[k] for k in WEIGHTS])
```

```python
import functools

import jax
import jax.numpy as jnp
from jax import lax
from jax.experimental import pallas as pl
from jax.experimental.pallas import tpu as pltpu

F32 = jnp.float32
BF16 = jnp.bfloat16
EPS = 1e-6
MESH = pl.DeviceIdType.MESH

SSD_CHUNK = 128
SSD_STATE = 128
SSD_HEAD_DIM = 64
HEADS_PER_GROUP = 8
GROUP_W = HEADS_PER_GROUP * SSD_HEAD_DIM
CONV_W = 5
POOL_WINDOWS = (2, 4, 8, 16)
N_CHIPS = 4

ADAM_LR = 0.001
ADAM_B1 = 0.9
ADAM_B2 = 0.999
ADAM_EPS = 1e-08
ADAM_WD = 0.01
ADAM_STEP = 10

VMEM_LIMIT = 56 * 1024 * 1024
LANE = 128


def _cp(*sem):
    return pltpu.CompilerParams(dimension_semantics=sem, vmem_limit_bytes=VMEM_LIMIT)


def _tile(dim, pref, unit=LANE):
    if dim <= pref:
        return dim
    t = (pref // unit) * unit
    while t >= unit:
        if dim % t == 0:
            return t
        t -= unit
    return dim


def _sigmoid(v):
    return 1.0 / (1.0 + jnp.exp(-v))


def _dot(a, b):
    return jnp.dot(a, b, preferred_element_type=F32)


def _dot_nt(a, b):
    return lax.dot_general(a, b, (((1,), (1,)), ((), ())), preferred_element_type=F32)


def _split3(v):
    h1 = v.astype(BF16)
    r1 = v - h1.astype(F32)
    h2 = r1.astype(BF16)
    h3 = (r1 - h2.astype(F32)).astype(BF16)
    return h1, h2, h3


def _dot_exact01(m01, v):
    mb = m01.astype(BF16)
    h1, h2, h3 = _split3(v)
    return _dot(mb, h1) + _dot(mb, h2) + _dot(mb, h3)


def _dot_nt_exact01(m01, v):
    mb = m01.astype(BF16)
    h1, h2, h3 = _split3(v)
    return _dot_nt(mb, h1) + _dot_nt(mb, h2) + _dot_nt(mb, h3)


def _mm(a, b, *, name, ta=False, tb=False, a_sel=(), b_sel=(), pair2=None, add=None, scale=1.0,
        out_dtype=F32, tm=1024, tn=1024, tk=2048):
    am, ak = a.shape[-2:][::-1] if ta else a.shape[-2:]
    bk, bn = b.shape[-2:][::-1] if tb else b.shape[-2:]
    assert ak == bk, (a.shape, b.shape, ta, tb)
    m_dim, n_dim, k_dim = am, bn, ak
    tm, tn, tk = _tile(m_dim, tm), _tile(n_dim, tn), _tile(k_dim, tk)
    nk = k_dim // tk
    grid = (n_dim // tn, m_dim // tm, nk)

    def a_spec(sel):
        lead = (None,) * len(sel)
        if ta:
            return pl.BlockSpec(lead + (tk, tm), lambda n, m, k: tuple(sel) + (k, m))
        return pl.BlockSpec(lead + (tm, tk), lambda n, m, k: tuple(sel) + (m, k))

    def b_spec(sel):
        lead = (None,) * len(sel)
        if tb:
            return pl.BlockSpec(lead + (tn, tk), lambda n, m, k: tuple(sel) + (n, k))
        return pl.BlockSpec(lead + (tk, tn), lambda n, m, k: tuple(sel) + (k, n))

    ins, specs = [a, b], [a_spec(a_sel), b_spec(b_sel)]
    if pair2 is not None:
        a2, b2, a2_sel, b2_sel = pair2
        ins += [a2, b2]
        specs += [a_spec(a2_sel), b_spec(b2_sel)]
    if add is not None:
        ins.append(add)
        specs.append(pl.BlockSpec((tm, tn), lambda n, m, k: (m, n)))
    dn = (((0 if ta else 1,), (1 if tb else 0,)), ((), ()))
    n_pairs = 2 if pair2 is not None else 1

    def body(*refs):
        pairs = [(refs[2 * i], refs[2 * i + 1]) for i in range(n_pairs)]
        pos = 2 * n_pairs
        add_ref = None
        if add is not None:
            add_ref = refs[pos]
            pos += 1
        o_ref = refs[pos]
        acc_ref = refs[pos + 1] if nk > 1 else None

        def prod():
            tot = None
            for ar, br in pairs:
                p = lax.dot_general(ar[...].astype(BF16), br[...].astype(BF16), dn, preferred_element_type=F32)
                tot = p if tot is None else tot + p
            return tot

        def finish(r):
            if scale != 1.0:
                r = r * scale
            if add_ref is not None:
                r = add_ref[...] + r
            o_ref[...] = r.astype(out_dtype)

        if nk == 1:
            finish(prod())
        else:
            k = pl.program_id(2)

            @pl.when(k == 0)
            def _():
                acc_ref[...] = jnp.zeros_like(acc_ref)

            acc_ref[...] += prod()

            @pl.when(k == nk - 1)
            def _():
                finish(acc_ref[...])

    return pl.pallas_call(
        body, name=name, grid=grid, in_specs=specs,
        out_specs=pl.BlockSpec((tm, tn), lambda n, m, k: (m, n)),
        out_shape=jax.ShapeDtypeStruct((m_dim, n_dim), out_dtype),
        scratch_shapes=[pltpu.VMEM((tm, tn), F32)] if nk > 1 else [],
        compiler_params=_cp("parallel", "parallel", "arbitrary"),
    )(*ins)


def _rmsnorm(x, g):
    t_dim, d = x.shape
    tr = _tile(t_dim, 256, 8)

    def body(x_ref, g_ref, o_ref):
        xv = x_ref[...]
        r = lax.rsqrt(jnp.mean(xv * xv, axis=-1, keepdims=True) + EPS)
        o_ref[...] = (xv * r * g_ref[...]).astype(BF16)

    return pl.pallas_call(
        body, name="rmsnorm", grid=(t_dim // tr,),
        in_specs=[pl.BlockSpec((tr, d), lambda i: (i, 0)), pl.BlockSpec((1, d), lambda i: (0, 0))],
        out_specs=pl.BlockSpec((tr, d), lambda i: (i, 0)),
        out_shape=jax.ShapeDtypeStruct((t_dim, d), BF16),
        compiler_params=_cp("parallel"),
    )(x, g)


def _rmsnorm_bwd(x, g, dh, dres):
    t_dim, d = x.shape
    tr = _tile(t_dim, 256, 8)

    def body(x_ref, g_ref, dh_ref, dres_ref, dx_ref, dg_ref):
        i = pl.program_id(0)
        xv = x_ref[...]
        r = lax.rsqrt(jnp.mean(xv * xv, axis=-1, keepdims=True) + EPS)
        n = xv * r
        dhv = dh_ref[...]
        dn = dhv * g_ref[...]

        @pl.when(i == 0)
        def _():
            dg_ref[...] = jnp.zeros_like(dg_ref)

        dg_ref[...] += jnp.sum(dhv * n, axis=0, keepdims=True)
        dx_ref[...] = dres_ref[...] + r * (dn - n * jnp.mean(dn * n, axis=-1, keepdims=True))

    row = pl.BlockSpec((tr, d), lambda i: (i, 0))
    vec = pl.BlockSpec((1, d), lambda i: (0, 0))
    return pl.pallas_call(
        body, name="rmsnorm_bwd", grid=(t_dim // tr,),
        in_specs=[row, vec, row, row], out_specs=[row, vec],
        out_shape=[jax.ShapeDtypeStruct((t_dim, d), F32), jax.ShapeDtypeStruct((1, d), F32)],
        compiler_params=_cp("arbitrary"),
    )(x, g, dh, dres)


def _loss_head(x, g, target):
    t_dim, d = x.shape
    tr = _tile(t_dim, 256, 8)

    def body(x_ref, g_ref, t_ref, loss_ref, dx_ref, dg_ref):
        i = pl.program_id(0)
        xv = x_ref[...]
        gv = g_ref[...]
        r = lax.rsqrt(jnp.mean(xv * xv, axis=-1, keepdims=True) + EPS)
        n = xv * r
        err = n * gv - t_ref[...]

        @pl.when(i == 0)
        def _():
            dg_ref[...] = jnp.zeros_like(dg_ref)
            loss_ref[...] = jnp.zeros_like(loss_ref)

        per_tok = jnp.mean(err * err, axis=-1, keepdims=True)
        loss_ref[...] += 0.5 * jnp.sum(per_tok, axis=0, keepdims=True)
        dy = err * (1.0 / d)
        dn = dy * gv
        dg_ref[...] += jnp.sum(dy * n, axis=0, keepdims=True)
        dx_ref[...] = r * (dn - n * jnp.mean(dn * n, axis=-1, keepdims=True))

    row = pl.BlockSpec((tr, d), lambda i: (i, 0))
    vec = pl.BlockSpec((1, d), lambda i: (0, 0))
    one = pl.BlockSpec((1, 1), lambda i: (0, 0))
    return pl.pallas_call(
        body, name="loss_head", grid=(t_dim // tr,),
        in_specs=[row, vec, row], out_specs=[one, row, vec],
        out_shape=[jax.ShapeDtypeStruct((1, 1), F32), jax.ShapeDtypeStruct((t_dim, d), F32),
                   jax.ShapeDtypeStruct((1, d), F32)],
        compiler_params=_cp("arbitrary"),
    )(x, g, target)


def _ffn_in(h, wg, wu, half):
    t_dim, d = h.shape
    f = wg.shape[-1]
    tm, tn = _tile(t_dim, 512), _tile(f, 1408)

    def body(h_ref, wg_ref, wu_ref, g_ref, u_ref, a_ref):
        hv = h_ref[...]
        gv = _dot(hv, wg_ref[...])
        uv = _dot(hv, wu_ref[...])
        g_ref[...] = gv.astype(BF16)
        u_ref[...] = uv.astype(BF16)
        a_ref[...] = (gv * _sigmoid(gv) * uv).astype(BF16)

    wspec = pl.BlockSpec((None, d, tn), lambda n, m: (half, 0, n))
    ospec = pl.BlockSpec((tm, tn), lambda n, m: (m, n))
    oshape = jax.ShapeDtypeStruct((t_dim, f), BF16)
    return pl.pallas_call(
        body, name="ffn_in", grid=(f // tn, t_dim // tm),
        in_specs=[pl.BlockSpec((tm, d), lambda n, m: (m, 0)), wspec, wspec],
        out_specs=[ospec, ospec, ospec], out_shape=[oshape, oshape, oshape],
        compiler_params=_cp("parallel", "parallel"),
    )(h, wg, wu)


def _ffn_bwd_act(dx, wd, g, u, half):
    t_dim, d = dx.shape
    f = wd.shape[-2]
    tm, tn = _tile(t_dim, 512), _tile(f, 1408)

    def body(dx_ref, wd_ref, g_ref, u_ref, dg_ref, du_ref):
        da = 0.5 * _dot_nt(dx_ref[...].astype(BF16), wd_ref[...])
        gv = g_ref[...].astype(F32)
        uv = u_ref[...].astype(F32)
        s = _sigmoid(gv)
        dg_ref[...] = (da * uv * (s * (1.0 + gv * (1.0 - s)))).astype(BF16)
        du_ref[...] = (da * gv * s).astype(BF16)

    tile = pl.BlockSpec((tm, tn), lambda n, m: (m, n))
    oshape = jax.ShapeDtypeStruct((t_dim, f), BF16)
    return pl.pallas_call(
        body, name="ffn_bwd_act", grid=(f // tn, t_dim // tm),
        in_specs=[pl.BlockSpec((tm, d), lambda n, m: (m, 0)),
                  pl.BlockSpec((None, tn, d), lambda n, m: (half, n, 0)), tile, tile],
        out_specs=[tile, tile], out_shape=[oshape, oshape],
        compiler_params=_cp("parallel", "parallel"),
    )(dx, wd, g, u)


def _ffn_fwd(x, norm_g, wg, wu, wd, half):
    h = _rmsnorm(x, norm_g)
    g, u, a = _ffn_in(h, wg, wu, half)
    x_new = _mm(a, wd, name="ffn_out", b_sel=(half,), add=x, scale=0.5, tk=1408)
    return x_new, (x, h, g, u, a)


def _ffn_bwd(dx, saved, norm_g, wg, wu, wd, half):
    x, h, g, u, a = saved
    dg, du = _ffn_bwd_act(dx, wd, g, u, half)
    d_wd = _mm(a, dx, name="ffn_dwd", ta=True, scale=0.5, out_dtype=BF16, tm=1408, tn=1024)
    d_wg = _mm(h, dg, name="ffn_dwgu", ta=True, out_dtype=BF16, tm=1024, tn=1408)
    d_wu = _mm(h, du, name="ffn_dwgu", ta=True, out_dtype=BF16, tm=1024, tn=1408)
    dh = _mm(dg, wg, name="ffn_dh", tb=True, b_sel=(half,), pair2=(du, wu, (), (half,)), tk=1408)
    dx_new, dnorm = _rmsnorm_bwd(x, norm_g, dh, dx)
    return dx_new, dnorm, d_wg, d_wu, d_wd


def _shifted(v, off, t_idx):
    if off == 0:
        return v
    t_dim = v.shape[0]
    sh = pltpu.roll(v, (-off) % t_dim, 0)
    valid = jnp.logical_and(t_idx + off >= 0, t_idx + off < t_dim)
    return jnp.where(valid, sh, 0.0)


def _conv_pre(u, w_ref, b_ref, t_idx):
    acc = jnp.zeros_like(u) + b_ref[...]
    shifted = []
    for k in range(CONV_W):
        sh = _shifted(u, k - CONV_W // 2, t_idx)
        shifted.append(sh)
        acc = acc + w_ref[k:k + 1, :] * sh
    return acc, shifted


def _conv_silu(proj, conv_w, conv_b, col0):
    t_dim = proj.shape[0]
    cd = conv_w.shape[-1]
    cb = _tile(cd, 256)
    assert col0 % cb == 0

    def body(u_ref, w_ref, b_ref, o_ref):
        t_idx = lax.broadcasted_iota(jnp.int32, (t_dim, cb), 0)
        pre, _ = _conv_pre(u_ref[...], w_ref, b_ref, t_idx)
        o_ref[...] = pre * _sigmoid(pre)

    return pl.pallas_call(
        body, name="conv_silu", grid=(cd // cb,),
        in_specs=[pl.BlockSpec((t_dim, cb), lambda j: (0, col0 // cb + j)),
                  pl.BlockSpec((CONV_W, cb), lambda j: (0, j)), pl.BlockSpec((1, cb), lambda j: (0, j))],
        out_specs=pl.BlockSpec((t_dim, cb), lambda j: (0, j)),
        out_shape=jax.ShapeDtypeStruct((t_dim, cd), F32),
        compiler_params=_cp("parallel"),
    )(proj, conv_w, conv_b)


def _conv_silu_bwd(proj, conv_w, conv_b, dact2, col0, ch0):
    t_dim = proj.shape[0]
    cd = dact2.shape[-1]
    cb = _tile(cd, 256)
    assert (col0 + ch0) % cb == 0 and ch0 % cb == 0
    p0, c0 = (col0 + ch0) // cb, ch0 // cb

    def body(u_ref, w_ref, b_ref, da_ref, du_ref, dwb_ref):
        t_idx = lax.broadcasted_iota(jnp.int32, (t_dim, cb), 0)
        pre, shifted = _conv_pre(u_ref[...], w_ref, b_ref, t_idx)
        s = _sigmoid(pre)
        dpre = (da_ref[0] + da_ref[1]) * (s * (1.0 + pre * (1.0 - s)))
        du = jnp.zeros_like(dpre)
        for k in range(CONV_W):
            du = du + w_ref[k:k + 1, :] * _shifted(dpre, -(k - CONV_W // 2), t_idx)
            dwb_ref[k:k + 1, :] = jnp.sum(dpre * shifted[k], axis=0, keepdims=True)
        dwb_ref[CONV_W:CONV_W + 1, :] = jnp.sum(dpre, axis=0, keepdims=True)
        dwb_ref[CONV_W + 1:8, :] = jnp.zeros((8 - CONV_W - 1, cb), F32)
        du_ref[...] = du.astype(BF16)

    return pl.pallas_call(
        body, name="conv_silu_bwd", grid=(cd // cb,),
        in_specs=[pl.BlockSpec((t_dim, cb), lambda j: (0, p0 + j)),
                  pl.BlockSpec((CONV_W, cb), lambda j: (0, c0 + j)), pl.BlockSpec((1, cb), lambda j: (0, c0 + j)),
                  pl.BlockSpec((2, t_dim, cb), lambda j: (0, 0, j))],
        out_specs=[pl.BlockSpec((t_dim, cb), lambda j: (0, j)), pl.BlockSpec((8, cb), lambda j: (0, j))],
        out_shape=[jax.ShapeDtypeStruct((t_dim, cd), BF16), jax.ShapeDtypeStruct((8, cd), F32)],
        compiler_params=_cp("parallel"),
    )(proj, conv_w, conv_b, dact2)


def _softplus_fwd(dt_raw, bias):
    def body(r_ref, b_ref, o_ref):
        v = r_ref[...] + b_ref[...]
        o_ref[...] = jnp.maximum(v, 0.0) + jnp.log(1.0 + jnp.exp(-jnp.abs(v)))

    return pl.pallas_call(body, name="softplus", out_shape=jax.ShapeDtypeStruct(dt_raw.shape, F32))(dt_raw, bias)


def _softplus_bwd(dt_raw, bias, ddt, da, a_log):
    def body(r_ref, b_ref, ddt_ref, da_ref, al_ref, dr_ref, db_ref, dal_ref):
        dv = ddt_ref[...] * _sigmoid(r_ref[...] + b_ref[...])
        dr_ref[...] = dv.astype(BF16)
        db_ref[...] = jnp.sum(dv, axis=0, keepdims=True)
        dal_ref[...] = -da_ref[...] * jnp.exp(al_ref[...])

    vec = jax.ShapeDtypeStruct(bias.shape, F32)
    return pl.pallas_call(
        body, name="softplus_bwd",
        out_shape=[jax.ShapeDtypeStruct(dt_raw.shape, BF16), vec, vec])(dt_raw, bias, ddt, da, a_log)


def _chunk_setup(d, dt_ref, al_ref):
    q = SSD_CHUNK
    ii = lax.broadcasted_iota(jnp.int32, (q, q), 0)
    jj = lax.broadcasted_iota(jnp.int32, (q, q), 1)
    sgn = 1 - 2 * d
    mask = (jj - ii) * sgn <= 0
    mask_t = (ii - jj) * sgn <= 0
    m01 = mask.astype(F32)
    m01_t = mask_t.astype(F32)
    dt = dt_ref[...]
    a = -jnp.exp(al_ref[...])
    dta = dt * a
    cs = _dot_exact01(m01, dta)
    tot = jnp.sum(dta, axis=0, keepdims=True)
    return mask, mask_t, m01, m01_t, dt, a, dta, cs, tot


def _head_lanes():
    hh = lax.broadcasted_iota(jnp.int32, (HEADS_PER_GROUP, GROUP_W), 0)
    ll = lax.broadcasted_iota(jnp.int32, (HEADS_PER_GROUP, GROUP_W), 1)
    return jnp.logical_and(ll >= hh * SSD_HEAD_DIM, ll < (hh + 1) * SSD_HEAD_DIM).astype(BF16)


def _expand(v, e16):
    h1, h2, h3 = _split3(v)
    return _dot(h1, e16) + _dot(h2, e16) + _dot(h3, e16)


def _expand_row(v, e16):
    return _expand(jnp.broadcast_to(v, (8, HEADS_PER_GROUP)), e16)[0:1]


def _per_head(v, e16):
    h1, h2, h3 = _split3(v)
    return _dot_nt(h1, e16) + _dot_nt(h2, e16) + _dot_nt(h3, e16)


def _rows_of(cs):
    n = cs.shape[1]
    eye = lax.broadcasted_iota(jnp.int32, (n, n), 0) == lax.broadcasted_iota(jnp.int32, (n, n), 1)
    return _dot_nt_exact01(eye.astype(F32), cs)


def _ssd_specs(t_dim, di, g_cnt, chunk_of):
    q, ns = SSD_CHUNK, SSD_STATE
    x_spec = pl.BlockSpec((q, GROUP_W), lambda d, g, c: (chunk_of(d, c), g))
    b_spec = pl.BlockSpec((q, ns), lambda d, g, c: (chunk_of(d, c), di // ns + g))
    c_spec = pl.BlockSpec((q, ns), lambda d, g, c: (chunk_of(d, c), di // ns + g_cnt + g))
    dt_spec = pl.BlockSpec((None, None, q, HEADS_PER_GROUP), lambda d, g, c: (d, g, chunk_of(d, c), 0))
    al_spec = pl.BlockSpec((None, None, 1, HEADS_PER_GROUP), lambda d, g, c: (d, g, 0, 0))
    return x_spec, b_spec, c_spec, dt_spec, al_spec


def _ssd_fwd(xbc, dt4, al4):
    t_dim = xbc.shape[0]
    g_cnt = dt4.shape[1]
    di = g_cnt * GROUP_W
    q, ns, p = SSD_CHUNK, SSD_STATE, SSD_HEAD_DIM
    nc = t_dim // q

    def chunk_of(d, c):
        return c + d * (nc - 1 - 2 * c)

    def body(x_ref, b_ref, c_ref, dt_ref, al_ref, y_ref, hin_ref, h_sc):
        d = pl.program_id(0)
        c = pl.program_id(2)

        @pl.when(c == 0)
        def _():
            h_sc[...] = jnp.zeros_like(h_sc)

        mask, mask_t, m01, m01_t, dt, a, dta, cs, tot = _chunk_setup(d, dt_ref, al_ref)
        e16 = _head_lanes()
        cs_rows = _rows_of(cs)
        cb16 = c_ref[...].astype(BF16)
        bt16 = b_ref[...].T.astype(BF16)
        cb = _dot(cb16, bt16)
        hin = h_sc[...]
        hin_ref[...] = hin
        xdt = x_ref[...] * _expand(dt, e16)
        y_off = _dot(cb16, hin.astype(BF16)) * _expand(jnp.exp(cs), e16)
        low = lax.broadcasted_iota(jnp.int32, (q, 2 * p), 1) < p
        for pair in range(HEADS_PER_GROUP // 2):
            ps = slice(2 * p * pair, 2 * p * (pair + 1))
            blk = xdt[:, ps]
            acc = y_off[:, ps]
            for hh in range(2):
                h = 2 * pair + hh
                lmat = jnp.exp(jnp.where(mask, cs[:, h:h + 1] - cs_rows[h:h + 1, :], -1e30))
                xm = jnp.where(low, blk, 0.0) if hh == 0 else jnp.where(low, 0.0, blk)
                acc = acc + _dot((cb * lmat).astype(BF16), xm.astype(BF16))
            y_ref[:, ps] = acc
        xd = xdt * _expand(jnp.exp(tot - cs), e16)
        st = _dot(bt16, xd.astype(BF16))
        h_sc[...] = hin * _expand_row(jnp.exp(tot), e16) + st

    x_spec, b_spec, c_spec, dt_spec, al_spec = _ssd_specs(t_dim, di, g_cnt, chunk_of)
    return pl.pallas_call(
        body, name="ssd_fwd", grid=(2, g_cnt, nc),
        in_specs=[x_spec, b_spec, c_spec, dt_spec, al_spec],
        out_specs=[pl.BlockSpec((None, q, GROUP_W), lambda d, g, c: (d, chunk_of(d, c), g)),
                   pl.BlockSpec((None, None, None, ns, GROUP_W), lambda d, g, c: (d, g, chunk_of(d, c), 0, 0))],
        out_shape=[jax.ShapeDtypeStruct((2, t_dim, di), F32),
                   jax.ShapeDtypeStruct((2, g_cnt, nc, ns, GROUP_W), F32)],
        scratch_shapes=[pltpu.VMEM((ns, GROUP_W), F32)],
        compiler_params=_cp("parallel", "parallel", "arbitrary"),
    )(xbc, xbc, xbc, dt4, al4)


def _ssd_bwd(xbc, dt4, al4, hin_all, dy, dvec):
    t_dim = xbc.shape[0]
    g_cnt = dt4.shape[1]
    di = g_cnt * GROUP_W
    q, ns, p, hg = SSD_CHUNK, SSD_STATE, SSD_HEAD_DIM, HEADS_PER_GROUP
    nc = t_dim // q

    def chunk_of(d, c):
        return (nc - 1 - c) + d * (2 * c - nc + 1)

    def body(x_ref, b_ref, c_ref, dt_ref, al_ref, hin_ref, dy_ref, dv_ref,
             dx_ref, db_ref, dc_ref, ddt_ref, da_ref, g_sc, dxdt_sc, zrow_sc):
        d = pl.program_id(0)
        c = pl.program_id(2)

        @pl.when(c == 0)
        def _():
            g_sc[...] = jnp.zeros_like(g_sc)
            da_ref[...] = jnp.zeros_like(da_ref)

        mask, mask_t, m01, m01_t, dt, a, dta, cs, tot = _chunk_setup(d, dt_ref, al_ref)
        xv = x_ref[...]
        dyv = dy_ref[...]
        bb = b_ref[...].astype(BF16)
        cb16 = c_ref[...].astype(BF16)
        bt16 = b_ref[...].T.astype(BF16)
        ct16 = c_ref[...].T.astype(BF16)
        cb = _dot(cb16, bt16)
        cbt = _dot(bb, ct16)
        hin = hin_ref[...]
        hin16 = hin.astype(BF16)
        gst = g_sc[...]
        gst16 = gst.astype(BF16)
        ch = _dot(cb16, hin16)
        wst = _dot(bb, gst16)
        skip = jnp.where(d == 0, 1.0, 0.0)
        e16 = _head_lanes()
        cs_rows = _rows_of(cs)
        dt_x = _expand(dt, e16)
        e_x = _expand(jnp.exp(cs), e16)
        dec = jnp.exp(tot - cs)
        etot = jnp.exp(tot)
        xdt = xv * dt_x
        dye = dyv * e_x
        t1 = _per_head(wst * xdt, e16) * dec
        dcs = _per_head(dye * ch, e16) - t1
        dtot = jnp.sum(t1, axis=0, keepdims=True)
        dec_x = _expand(dec, e16)
        dxdt_state = wst * dec_x
        dcb = jnp.zeros((q, q), F32)
        low = lax.broadcasted_iota(jnp.int32, (q, 2 * p), 1) < p
        col8 = lax.broadcasted_iota(jnp.int32, (q, hg), 1)
        for pair in range(hg // 2):
            ps = slice(2 * p * pair, 2 * p * (pair + 1))
            xblk, dyblk = xdt[:, ps], dyv[:, ps]
            acc = dxdt_state[:, ps]
            for hh in range(2):
                h = 2 * pair + hh
                seg = cs[:, h:h + 1] - cs_rows[h:h + 1, :]
                lmat = jnp.exp(jnp.where(mask, seg, -1e30))
                lmat_t = jnp.exp(jnp.where(mask_t, -seg, -1e30))
                if hh == 0:
                    xm, dym = jnp.where(low, xblk, 0.0).astype(BF16), jnp.where(low, dyblk, 0.0).astype(BF16)
                else:
                    xm, dym = jnp.where(low, 0.0, xblk).astype(BF16), jnp.where(low, 0.0, dyblk).astype(BF16)
                acc = acc + _dot((cbt * lmat_t).astype(BF16), dym)
                dm = _dot_nt(dym, xm)
                z = dm * (cb * lmat)
                dcb = dcb + dm * lmat
                dcs = dcs + jnp.where(col8 == h, jnp.sum(z, axis=-1, keepdims=True), 0.0)
                zrow_sc[h:h + 1, :] = jnp.sum(z, axis=0, keepdims=True)
            dxdt_sc[:, ps] = acc
        dxdt = dxdt_sc[...]
        dx_ref[...] = dxdt * dt_x + skip * dyv * dv_ref[...]
        dye16 = dye.astype(BF16)
        dcb16 = dcb.astype(BF16)
        dc_ref[...] = _dot_nt(dye16, hin16) + _dot(dcb16, bb)
        db_ref[...] = _dot_nt((xdt * dec_x).astype(BF16), gst16) + _dot(dcb.T.astype(BF16), cb16)
        g_sc[...] = _dot(ct16, dye16) + gst * _expand_row(etot, e16)
        carried = jnp.broadcast_to(jnp.sum(gst * hin, axis=0, keepdims=True), (8, GROUP_W))
        dtot = dtot + _per_head(carried, e16)[0:1] * etot
        ddta = _dot_exact01(m01_t, dcs) - _dot_nt_exact01(m01_t, zrow_sc[...]) + dtot
        ddt_ref[...] = _per_head(dxdt * xv, e16) + ddta * a
        da_ref[...] += jnp.sum(ddta * dt, axis=0, keepdims=True)

    x_spec, b_spec, c_spec, dt_spec, al_spec = _ssd_specs(t_dim, di, g_cnt, chunk_of)
    hin_spec = pl.BlockSpec((None, None, None, ns, GROUP_W), lambda d, g, c: (d, g, chunk_of(d, c), 0, 0))
    dy_spec = pl.BlockSpec((q, GROUP_W), lambda d, g, c: (chunk_of(d, c), g))
    dv_spec = pl.BlockSpec((1, GROUP_W), lambda d, g, c: (0, g))
    gn = g_cnt * ns
    return pl.pallas_call(
        body, name="ssd_bwd", grid=(2, g_cnt, nc),
        in_specs=[x_spec, b_spec, c_spec, dt_spec, al_spec, hin_spec, dy_spec, dv_spec],
        out_specs=[pl.BlockSpec((None, q, GROUP_W), lambda d, g, c: (d, chunk_of(d, c), g)),
                   pl.BlockSpec((None, q, ns), lambda d, g, c: (d, chunk_of(d, c), g)),
                   pl.BlockSpec((None, q, ns), lambda d, g, c: (d, chunk_of(d, c), g)),
                   pl.BlockSpec((None, None, q, hg), lambda d, g, c: (d, g, chunk_of(d, c), 0)),
                   pl.BlockSpec((None, None, 1, hg), lambda d, g, c: (d, g, 0, 0))],
        out_shape=[jax.ShapeDtypeStruct((2, t_dim, di), F32), jax.ShapeDtypeStruct((2, t_dim, gn), F32),
                   jax.ShapeDtypeStruct((2, t_dim, gn), F32), jax.ShapeDtypeStruct((2, g_cnt, t_dim, hg), F32),
                   jax.ShapeDtypeStruct((2, g_cnt, 1, hg), F32)],
        scratch_shapes=[pltpu.VMEM((ns, GROUP_W), F32), pltpu.VMEM((q, GROUP_W), F32), pltpu.VMEM((hg, q), F32)],
        compiler_params=_cp("parallel", "parallel", "arbitrary"),
    )(xbc, xbc, xbc, dt4, al4, hin_all, dy, dvec)


def _gate_norm(y2, xbc, proj, dvec, ng):
    t_dim, di = y2.shape[1:]
    tr = _tile(t_dim, 128, 8)

    def body(y2_ref, x_ref, z_ref, dv_ref, ng_ref, o_ref):
        y = y2_ref[0] + y2_ref[1] + x_ref[...] * dv_ref[...]
        z = z_ref[...]
        v = y * z * _sigmoid(z)
        r = lax.rsqrt(jnp.mean(v * v, axis=-1, keepdims=True) + EPS)
        o_ref[...] = (v * r * ng_ref[...]).astype(BF16)

    row = pl.BlockSpec((tr, di), lambda i: (i, 0))
    vec = pl.BlockSpec((1, di), lambda i: (0, 0))
    return pl.pallas_call(
        body, name="gate_norm", grid=(t_dim // tr,),
        in_specs=[pl.BlockSpec((2, tr, di), lambda i: (0, i, 0)), row, row, vec, vec],
        out_specs=row, out_shape=jax.ShapeDtypeStruct((t_dim, di), BF16),
        compiler_params=_cp("parallel"),
    )(y2, xbc, proj, dvec, ng)


def _gate_norm_bwd(y2, xbc, proj, dvec, ng, dyn):
    t_dim, di = y2.shape[1:]
    tr = _tile(t_dim, 128, 8)

    def body(y2_ref, x_ref, z_ref, dv_ref, ng_ref, dyn_ref, dy_ref, dz_ref, dng_ref, dd_ref):
        i = pl.program_id(0)
        xv = x_ref[...]
        y = y2_ref[0] + y2_ref[1] + xv * dv_ref[...]
        z = z_ref[...]
        s = _sigmoid(z)
        v = y * z * s
        r = lax.rsqrt(jnp.mean(v * v, axis=-1, keepdims=True) + EPS)
        n = v * r
        dynv = dyn_ref[...]
        dn = dynv * ng_ref[...]
        dv = r * (dn - n * jnp.mean(dn * n, axis=-1, keepdims=True))
        dy = dv * z * s

        @pl.when(i == 0)
        def _():
            dng_ref[...] = jnp.zeros_like(dng_ref)
            dd_ref[...] = jnp.zeros_like(dd_ref)

        dng_ref[...] += jnp.sum(dynv * n, axis=0, keepdims=True)
        dd_ref[...] += jnp.sum(dy * xv, axis=0, keepdims=True)
        dy_ref[...] = dy
        dz_ref[...] = (dv * y * (s * (1.0 + z * (1.0 - s)))).astype(BF16)

    row = pl.BlockSpec((tr, di), lambda i: (i, 0))
    vec = pl.BlockSpec((1, di), lambda i: (0, 0))
    return pl.pallas_call(
        body, name="gate_norm_bwd", grid=(t_dim // tr,),
        in_specs=[pl.BlockSpec((2, tr, di), lambda i: (0, i, 0)), row, row, vec, vec, row],
        out_specs=[row, row, vec, vec],
        out_shape=[jax.ShapeDtypeStruct((t_dim, di), F32), jax.ShapeDtypeStruct((t_dim, di), BF16),
                   jax.ShapeDtypeStruct((1, di), F32), jax.ShapeDtypeStruct((1, di), F32)],
        compiler_params=_cp("arbitrary"),
    )(y2, xbc, proj, dvec, ng, dyn)


def _dt_to_groups(dt):
    t_dim, h2 = dt.shape
    g_cnt = h2 // 2 // HEADS_PER_GROUP
    return dt.reshape(t_dim, 2, g_cnt, HEADS_PER_GROUP).transpose(1, 2, 0, 3)


def _dt_from_groups(dt4):
    _, g_cnt, t_dim, hg = dt4.shape
    return dt4.transpose(2, 0, 1, 3).reshape(t_dim, 2 * g_cnt * hg)


def _ssd_mixer_fwd(x, norm_g, w_in, conv_w, conv_b, dt_bias, a_log, d_skip, ssd_norm, w_out, j):
    heads = d_skip.shape[0]
    di = heads * SSD_HEAD_DIM
    g_cnt = heads // HEADS_PER_GROUP
    cd = conv_w.shape[-1]
    hn = _rmsnorm(x, norm_g)
    proj = _mm(hn, w_in, name="ssd_proj", tb=True, b_sel=(j,), tn=1152)
    xbc = _conv_silu(proj, conv_w, conv_b, di)
    dt_raw = proj[:, di + cd:]
    dt = _softplus_fwd(dt_raw, dt_bias)
    dt4 = _dt_to_groups(dt)
    al4 = a_log.reshape(2, g_cnt, 1, HEADS_PER_GROUP)
    y2, hin = _ssd_fwd(xbc, dt4, al4)
    dvec = jnp.repeat(d_skip, SSD_HEAD_DIM).reshape(1, di)
    yn = _gate_norm(y2, xbc, proj, dvec, ssd_norm)
    x_new = _mm(yn, w_out, name="ssd_out", b_sel=(j,), add=x)
    return x_new, (x, hn, proj, xbc, dt_raw, dt4, al4, y2, hin, dvec, yn)


def _ssd_mixer_bwd(dx, saved, norm_g, w_in, conv_w, conv_b, dt_bias, a_log, ssd_norm, w_out, j):
    x, hn, proj, xbc, dt_raw, dt4, al4, y2, hin, dvec, yn = saved
    di = dvec.shape[1]
    heads = di // SSD_HEAD_DIM
    d_wout = _mm(yn, dx, name="ssd_dwout", ta=True, out_dtype=BF16)
    dyn = _mm(dx, w_out, name="ssd_dyn", tb=True, b_sel=(j,))
    dy, dz, d_ng, dd_col = _gate_norm_bwd(y2, xbc, proj, dvec, ssd_norm, dyn)
    dx2, db2, dc2, ddt4, da4 = _ssd_bwd(xbc, dt4, al4, hin, dy, dvec)
    gn = db2.shape[-1]
    conv_parts = [_conv_silu_bwd(proj, conv_w, conv_b, part, di, ch0)
                  for part, ch0 in ((dx2, 0), (db2, di), (dc2, di + gn))]
    dwb = jnp.concatenate([p[1] for p in conv_parts], axis=-1)
    ddt_raw, d_bias, d_alog = _softplus_bwd(dt_raw, dt_bias, _dt_from_groups(ddt4), da4.reshape(1, 2 * heads), a_log)
    dproj = jnp.concatenate([dz] + [p[0] for p in conv_parts] + [ddt_raw], axis=-1)
    d_win = _mm(dproj, hn, name="ssd_dwin", ta=True, out_dtype=BF16, tm=1152)
    dhn = _mm(dproj, w_in, name="ssd_dhn", b_sel=(j,), tk=1152)
    dx_new, d_norm = _rmsnorm_bwd(x, norm_g, dhn, dx)
    small = dict(mix_norm=d_norm, conv_w=dwb[:CONV_W], conv_b=dwb[CONV_W:CONV_W + 1], dt_bias=d_bias, a_log=d_alog,
                 ssd_d=dd_col.reshape(heads, SSD_HEAD_DIM).sum(axis=1), ssd_norm=d_ng)
    return dx_new, small, d_win, d_wout


def _pool_count(t_idx, w, t_dim):
    hi = jnp.minimum(t_idx + w // 2, t_dim)
    lo = jnp.maximum(t_idx - w // 2, 0)
    return (hi - lo).astype(F32)


def _pool_mix(u, transpose):
    t_dim, d = u.shape
    gd = d // len(POOL_WINDOWS)
    cb = _tile(gd, 256)
    per = gd // cb

    def body(u_ref, o_ref):
        gi = pl.program_id(0)
        t_idx = lax.broadcasted_iota(jnp.int32, (t_dim, cb), 0)
        uv = u_ref[...]
        for widx, w in enumerate(POOL_WINDOWS):
            @pl.when(gi == widx)
            def _(w=w):
                cnt = _pool_count(t_idx, w, t_dim)
                src = uv / cnt if transpose else uv
                acc = jnp.zeros_like(uv)
                for k in range(-(w // 2), w // 2):
                    acc = acc + _shifted(src, -k if transpose else k, t_idx)
                res = acc - uv if transpose else acc / cnt - uv
                o_ref[...] = res.astype(BF16)

    spec = pl.BlockSpec((t_dim, cb), lambda gi, j: (0, gi * per + j))
    return pl.pallas_call(
        body, name="pool_mix_t" if transpose else "pool_mix", grid=(len(POOL_WINDOWS), per),
        in_specs=[spec], out_specs=spec, out_shape=jax.ShapeDtypeStruct((t_dim, d), BF16),
        compiler_params=_cp("parallel", "parallel"),
    )(u)


def _pool_group(mix, wgrp, scale, j):
    t_dim, d = mix.shape
    gd = wgrp.shape[-1]
    tm = _tile(t_dim, 512)

    def body(m_ref, w_ref, s_ref, v_ref, vs_ref):
        v = _dot(m_ref[...], w_ref[...])
        v_ref[...] = v
        vs_ref[...] = (v * s_ref[...]).astype(BF16)

    tile = pl.BlockSpec((tm, gd), lambda gi, m: (m, gi))
    return pl.pallas_call(
        body, name="pool_group", grid=(d // gd, t_dim // tm),
        in_specs=[tile, pl.BlockSpec((None, None, gd, gd), lambda gi, m: (j, gi, 0, 0)),
                  pl.BlockSpec((1, gd), lambda gi, m: (0, gi))],
        out_specs=[tile, tile],
        out_shape=[jax.ShapeDtypeStruct((t_dim, d), F32), jax.ShapeDtypeStruct((t_dim, d), BF16)],
        compiler_params=_cp("parallel", "parallel"),
    )(mix, wgrp, scale)


def _pool_group_bwd(dvs, v, mix, wgrp, scale, j):
    t_dim, d = mix.shape
    gd = wgrp.shape[-1]
    n_g = d // gd
    tm = _tile(t_dim, 512)
    nm = t_dim // tm

    def body(dvs_ref, v_ref, m_ref, w_ref, s_ref, dmix_ref, ds_ref, dw_ref, acc_ref):
        m = pl.program_id(1)
        dvsv = dvs_ref[...]

        @pl.when(m == 0)
        def _():
            ds_ref[...] = jnp.zeros_like(ds_ref)
            acc_ref[...] = jnp.zeros_like(acc_ref)

        ds_ref[...] += jnp.sum(dvsv * v_ref[...], axis=0, keepdims=True)
        dv16 = (dvsv * s_ref[...]).astype(BF16)
        dmix_ref[...] = _dot_nt(dv16, w_ref[...])
        acc_ref[...] += _dot(m_ref[...].T, dv16)

        @pl.when(m == nm - 1)
        def _():
            dw_ref[...] = acc_ref[...].astype(BF16)

    tile = pl.BlockSpec((tm, gd), lambda gi, m: (m, gi))
    vec = pl.BlockSpec((1, gd), lambda gi, m: (0, gi))
    return pl.pallas_call(
        body, name="pool_group_bwd", grid=(n_g, nm),
        in_specs=[tile, tile, tile, pl.BlockSpec((None, None, gd, gd), lambda gi, m: (j, gi, 0, 0)), vec],
        out_specs=[tile, vec, pl.BlockSpec((None, gd, gd), lambda gi, m: (gi, 0, 0))],
        out_shape=[jax.ShapeDtypeStruct((t_dim, d), F32), jax.ShapeDtypeStruct((1, d), F32),
                   jax.ShapeDtypeStruct((n_g, gd, gd), BF16)],
        scratch_shapes=[pltpu.VMEM((gd, gd), F32)],
        compiler_params=_cp("parallel", "arbitrary"),
    )(dvs, v, mix, wgrp, scale)


def _pool_mixer_fwd(x, norm_g, w_in, wgrp, scale, w_out, j):
    hn = _rmsnorm(x, norm_g)
    u = _mm(hn, w_in, name="pool_u", b_sel=(j,))
    mix = _pool_mix(u, False)
    v, vs = _pool_group(mix, wgrp, scale, j)
    x_new = _mm(vs, w_out, name="pool_out", b_sel=(j,), add=x)
    return x_new, (x, hn, mix, v, vs)


def _pool_mixer_bwd(dx, saved, norm_g, w_in, wgrp, scale, w_out, j):
    x, hn, mix, v, vs = saved
    d_wout = _mm(vs, dx, name="pool_dw", ta=True, out_dtype=BF16)
    dvs = _mm(dx, w_out, name="pool_dvs", tb=True, b_sel=(j,))
    dmix, d_scale, d_wgrp = _pool_group_bwd(dvs, v, mix, wgrp, scale, j)
    du = _pool_mix(dmix, True)
    d_win = _mm(hn, du, name="pool_dw", ta=True, out_dtype=BF16)
    dhn = _mm(du, w_in, name="pool_dhn", tb=True, b_sel=(j,))
    dx_new, d_norm = _rmsnorm_bwd(x, norm_g, dhn, dx)
    return dx_new, d_norm, d_scale, d_win, d_wgrp, d_wout


def _local_step(x, target, depth, fetch, emit, on_loss, small, tok):
    saved, wts = [], {}

    def weights(group, after):
        if group not in wts:
            wts[group] = fetch(group, after)
        return wts[group]

    for i in range(depth):
        j = i // 2
        x, s0 = _ffn_fwd(x, small["ffn_norm"][i, 0][None] + tok, *weights(("ffn", i, 0), x), 0)
        mg = small["mix_norm"][i][None]
        if i % 2 == 0:
            heads = small["ssd_d"].shape[1]
            w_in, w_out = weights(("ssd", j), x)
            x, s1 = _ssd_mixer_fwd(x, mg, w_in, small["ssd_conv_w"][j], small["ssd_conv_b"][j][None],
                                   small["ssd_dt_bias"][j].reshape(1, 2 * heads), small["ssd_a_log"][j].reshape(1, 2 * heads),
                                   small["ssd_d"][j], small["ssd_norm"][j][None], w_out, 0)
        else:
            p_in, p_grp, p_out = weights(("pool", j), x)
            x, s1 = _pool_mixer_fwd(x, mg, p_in, p_grp, small["pool_scale"][j][None], p_out, 0)
        x, s2 = _ffn_fwd(x, small["ffn_norm"][i, 1][None], *weights(("ffn", i, 1), x), 0)
        saved.append((s0, s1, s2))
    loss, dx, d_final = _loss_head(x, small["final_norm"][None], target)

    gs = {k: {} for k in ("ffn_norm", "mix_norm", "ssd_conv_w", "ssd_conv_b", "ssd_dt_bias", "ssd_a_log", "ssd_d",
                          "ssd_norm", "pool_scale")}
    tok = on_loss(loss)
    for i in reversed(range(depth)):
        j = i // 2
        s0, s1, s2 = saved[i]
        for half, sv in ((1, s2), (0, None)):
            if half == 0:
                sv = s0
                mg = small["mix_norm"][i][None] + tok
                if i % 2 == 0:
                    heads = small["ssd_d"].shape[1]
                    w_in, w_out = wts[("ssd", j)]
                    dx, sm, d_win, d_wout = _ssd_mixer_bwd(
                        dx, s1, mg, w_in, small["ssd_conv_w"][j], small["ssd_conv_b"][j][None],
                        small["ssd_dt_bias"][j].reshape(1, 2 * heads), small["ssd_a_log"][j].reshape(1, 2 * heads),
                        small["ssd_norm"][j][None], w_out, 0)
                    gs["mix_norm"][i] = sm["mix_norm"][0]
                    gs["ssd_conv_w"][j] = sm["conv_w"]
                    gs["ssd_conv_b"][j] = sm["conv_b"][0]
                    gs["ssd_dt_bias"][j] = sm["dt_bias"].reshape(2, heads)
                    gs["ssd_a_log"][j] = sm["a_log"].reshape(2, heads)
                    gs["ssd_d"][j] = sm["ssd_d"]
                    gs["ssd_norm"][j] = sm["ssd_norm"][0]
                    tok = tok + emit(("ssd", j), [d_win, d_wout])
                else:
                    p_in, p_grp, p_out = wts[("pool", j)]
                    dx, d_norm, d_scale, d_win, d_wgrp, d_wout = _pool_mixer_bwd(
                        dx, s1, mg, p_in, p_grp, small["pool_scale"][j][None], p_out, 0)
                    gs["mix_norm"][i] = d_norm[0]
                    gs["pool_scale"][j] = d_scale[0]
                    tok = tok + emit(("pool", j), [d_win, d_wgrp, d_wout])
            dx, d_norm, d_wg, d_wu, d_wd = _ffn_bwd(dx, sv, small["ffn_norm"][i, half][None] + tok,
                                                   *wts[("ffn", i, half)], 0)
            gs["ffn_norm"][(i, half)] = d_norm[0]
            tok = tok + emit(("ffn", i, half), [d_wg, d_wu, d_wd])
    n_s, n_p = (depth + 1) // 2, depth // 2
    gsmall = dict(
        ffn_norm=jnp.stack([jnp.stack([gs["ffn_norm"][(i, h)] for h in range(2)]) for i in range(depth)]),
        mix_norm=jnp.stack([gs["mix_norm"][i] for i in range(depth)]),
        ssd_conv_w=jnp.stack([gs["ssd_conv_w"][j] for j in range(n_s)]),
        ssd_conv_b=jnp.stack([gs["ssd_conv_b"][j] for j in range(n_s)]),
        ssd_dt_bias=jnp.stack([gs["ssd_dt_bias"][j] for j in range(n_s)]),
        ssd_a_log=jnp.stack([gs["ssd_a_log"][j] for j in range(n_s)]),
        ssd_d=jnp.stack([gs["ssd_d"][j] for j in range(n_s)]),
        ssd_norm=jnp.stack([gs["ssd_norm"][j] for j in range(n_s)]),
        pool_scale=jnp.stack([gs["pool_scale"][j] for j in range(n_p)]),
        final_norm=d_final[0],
    )
    return loss, dx, gsmall, tok


ANY = pl.BlockSpec(memory_space=pl.ANY)


def _mesh_pos():
    return lax.axis_index("x"), lax.axis_index("y"), lax.axis_index("c")


def _other_chips(x, y):
    return [(1 - x, y), (x, 1 - y), (1 - x, 1 - y)]


def _win(ref, windows, lead=()):
    rest = len(ref.shape) - len(lead)
    idx = tuple(lead) + tuple(pl.ds(*windows[ax]) if ax in windows else slice(None) for ax in range(rest))
    return ref.at[idx]


def _remote(src, dst, send_sems, recv_sems, k, peer):
    return pltpu.make_async_remote_copy(src_ref=src, dst_ref=dst, send_sem=send_sems.at[k], recv_sem=recv_sems.at[k],
                                        device_id=peer, device_id_type=MESH)


def _cast_place(w3, chip1, mode, l0, nl, out_dtype=BF16):
    _, r, c = w3.shape
    tr = _tile(r, 256, 16)
    nr = r // tr
    if mode == "cols":
        out_shape, blk = (nl, r, N_CHIPS * c), (None, tr, c)
        omap = lambda l, i, s: (l, i, s[0])
    elif mode == "rows":
        out_shape, blk = (nl, N_CHIPS * r, c), (None, tr, c)
        omap = lambda l, i, s: (l, s[0] * nr + i, 0)
    else:
        out_shape, blk = (nl, N_CHIPS, r, c), (None, None, tr, c)
        omap = lambda l, i, s: (l, s[0], i, 0)

    def body(s_ref, w_ref, o_ref):
        o_ref[...] = w_ref[...].astype(out_dtype)

    return pl.pallas_call(
        body, name="cast_place_" + mode, out_shape=jax.ShapeDtypeStruct(out_shape, out_dtype),
        grid_spec=pltpu.PrefetchScalarGridSpec(
            num_scalar_prefetch=1, grid=(nl, nr),
            in_specs=[pl.BlockSpec((None, tr, c), lambda l, i, s: (l0 + l, i, 0))],
            out_specs=pl.BlockSpec(blk, omap)),
        compiler_params=_cp("parallel", "parallel"),
    )(chip1, w3)


def _gather_begin(name, fulls, axes):
    n = len(fulls)
    shapes = [f.shape for f in fulls]

    def full_win(refs, t, sidx, hidx):
        sa, ha = axes[t]
        ssz, hsz = shapes[t][sa] // N_CHIPS, shapes[t][ha] // 2
        return _win(refs[t], {sa: (sidx * ssz, ssz), ha: (hidx * hsz, hsz)})

    def chips_plan(refs, x, y, c):
        mine = 2 * x + y
        return [(full_win(refs, t, mine, c), full_win(refs, t, mine, c), (px, py, c))
                for t in range(n) for px, py in _other_chips(x, y)]

    def sibling_plan(in_refs, out_refs, x, y, c):
        wins = [full_win(out_refs, t, 2 * px + py, c) for t in range(n) for px, py in _other_chips(x, y)]
        return [(w, w, (x, y, 1 - c)) for w in wins]

    sems, thru, token = _split_start(name + "_start", fulls, chips_plan, 3 * n)
    return (name, sems, thru, chips_plan, sibling_plan, 3 * n), token


def _gather_finish(pending, after):
    name, sems, thru, chips_plan, sibling_plan, n_copies = pending
    landed = _split_wait(name + "_wait", sems, thru, chips_plan, n_copies, after)
    return _exchange(name + "_sibling", landed, [jax.ShapeDtypeStruct(f.shape, f.dtype) for f in landed],
                     sibling_plan, n_copies, inplace=True)


def _exchange(name, inputs, out_shapes, plan, n_copies, inplace=False):
    n_in, n_out = len(inputs), len(out_shapes)

    def body(*refs):
        in_refs, out_refs = refs[:n_in], refs[n_in:n_in + n_out]
        send_sems, recv_sems = refs[n_in + n_out:]
        x, y, c = _mesh_pos()
        copies = plan(in_refs, out_refs, x, y, c)
        assert len(copies) == n_copies
        started = []
        for k, (src, dst, peer) in enumerate(copies):
            cp = _remote(src, dst, send_sems, recv_sems, k, peer)
            cp.start()
            started.append(cp)
        for cp in started:
            cp.wait()

    return pl.pallas_call(
        body, name=name, in_specs=[ANY] * n_in, out_specs=[ANY] * n_out, out_shape=out_shapes,
        input_output_aliases={t: t for t in range(n_in)} if inplace else {},
        scratch_shapes=[pltpu.SemaphoreType.DMA((n_copies,)), pltpu.SemaphoreType.DMA((n_copies,))],
    )(*inputs)


HBM = pl.BlockSpec(memory_space=pltpu.HBM)
SEM = pl.BlockSpec(memory_space=pltpu.SEMAPHORE)
DATAFLOW = pltpu.SideEffectType.DATAFLOW_SIDE_EFFECTING


def _split_start(name, bufs, plan, n_copies):
    n = len(bufs)

    def body(*refs):
        ins = refs[:n]
        send_sems, recv_sems = refs[n], refs[n + 1]
        token = refs[2 * n + 2]
        x, y, c = _mesh_pos()
        copies = plan(ins, x, y, c)
        assert len(copies) == n_copies
        for k, (src, dst, peer) in enumerate(copies):
            _remote(src, dst, send_sems, recv_sems, k, peer).start()
        token[...] = jnp.zeros_like(token)

    outs = pl.pallas_call(
        body, name=name,
        out_shape=(pltpu.SemaphoreType.DMA((n_copies,)), pltpu.SemaphoreType.DMA((n_copies,)),
                   *[pltpu.HBM(b.shape, b.dtype) for b in bufs], jax.ShapeDtypeStruct((8, LANE), F32)),
        in_specs=[HBM] * n, out_specs=(SEM, SEM, *[HBM] * n, pl.BlockSpec(memory_space=pltpu.VMEM)),
        input_output_aliases={t: 2 + t for t in range(n)},
        compiler_params=pltpu.CompilerParams(has_side_effects=DATAFLOW),
    )(*[pltpu.with_memory_space_constraint(b, pltpu.HBM) for b in bufs])
    return (outs[0], outs[1]), list(outs[2:2 + n]), outs[2 + n][0, 0]


def _split_wait(name, sems, bufs, plan, n_copies, after):
    n = len(bufs)

    def body(*refs):
        ins = refs[:n]
        send_sems, recv_sems = refs[n], refs[n + 1]
        x, y, c = _mesh_pos()
        copies = plan(ins, x, y, c)
        assert len(copies) == n_copies
        for k, (src, dst, peer) in enumerate(copies):
            cp = _remote(src, dst, send_sems, recv_sems, k, peer)
            cp.wait_send()
            cp.wait_recv()

    outs = pl.pallas_call(
        body, name=name, out_shape=tuple(pltpu.HBM(b.shape, b.dtype) for b in bufs),
        in_specs=[HBM] * n + [SEM, SEM, ANY], out_specs=tuple([HBM] * n),
        input_output_aliases={t: t for t in range(n)},
        compiler_params=pltpu.CompilerParams(has_side_effects=DATAFLOW),
    )(*bufs, sems[0], sems[1], after)
    return list(outs)


def _halved(shape, ha):
    out = list(shape)
    out[ha] //= 2
    return tuple(out)


def _sharded(shape, sa):
    out = list(shape)
    out[sa] //= N_CHIPS
    return tuple(out)


def _rs_cores_begin(name, grads, axes):
    n = len(grads)
    shapes = [g.shape for g in grads]
    landing = [lax.empty(_halved(g.shape, ha), g.dtype) for g, (_, ha) in zip(grads, axes)]

    def plan(refs, x, y, c):
        copies = []
        for t in range(n):
            ha = axes[t][1]
            hsz = shapes[t][ha] // 2
            copies.append((_win(refs[t], {ha: ((1 - c) * hsz, hsz)}), refs[n + t], (x, y, 1 - c)))
        return copies

    sems, thru, token = _split_start(name + "_start", list(grads) + landing, plan, n)
    return (name, sems, thru, plan, n), token


def _rs_cores_finish(pending, after):
    name, sems, thru, plan, n = pending
    done = _split_wait(name + "_wait", sems, thru, plan, n, after)
    return done[:n], done[n:]


def _small_begin(vec):
    landing = lax.empty((8,) + vec.shape, vec.dtype)

    def plan(refs, x, y, c):
        me = 4 * x + 2 * y + c
        copies = []
        for k in range(1, 8):
            kx, ky, kc = k // 4, (k // 2) % 2, k % 2
            copies.append((refs[0], refs[1].at[me], (x ^ kx, y ^ ky, c ^ kc)))
        return copies

    sems, thru, token = _split_start("small_start", [vec, landing], plan, 7)
    return (sems, thru, plan), token


def _small_finish(pending, after, me1):
    sems, thru, plan = pending
    vec, landing = _split_wait("small_wait", sems, thru, plan, 7, after)

    def body(s_ref, v_ref, l_ref, o_ref):
        tot = jnp.where(s_ref[0] == 0, v_ref[...], l_ref[0])
        for i in range(1, 8):
            tot = tot + jnp.where(s_ref[0] == i, v_ref[...], l_ref[i])
        o_ref[...] = tot

    rows = vec.shape[0]
    return pl.pallas_call(
        body, name="small_sum", out_shape=jax.ShapeDtypeStruct(vec.shape, F32),
        grid_spec=pltpu.PrefetchScalarGridSpec(
            num_scalar_prefetch=1, grid=(1,),
            in_specs=[pl.BlockSpec((rows, LANE), lambda i, s: (0, 0)), pl.BlockSpec((8, rows, LANE), lambda i, s: (0, 0, 0))],
            out_specs=pl.BlockSpec((rows, LANE), lambda i, s: (0, 0))),
    )(me1, vec, landing)


def _rs_chips_begin(name, halves, axes):
    n = len(halves)
    shapes = [h.shape for h in halves]
    landing = [lax.empty((N_CHIPS - 1,) + _sharded(h.shape, sa), h.dtype) for h, (sa, _) in zip(halves, axes)]

    def plan(refs, x, y, c):
        copies = []
        for t in range(n):
            sa = axes[t][0]
            ssz = shapes[t][sa] // N_CHIPS
            for j, (px, py) in enumerate(_other_chips(x, y)):
                copies.append((_win(refs[t], {sa: ((2 * px + py) * ssz, ssz)}), refs[n + t].at[j], (px, py, c)))
        return copies

    sems, thru, token = _split_start(name + "_start", list(halves) + landing, plan, 3 * n)
    return (name, sems, thru, plan, 3 * n), token


def _rs_chips_finish(pending, after):
    name, sems, thru, plan, n_copies = pending
    done = _split_wait(name + "_wait", sems, thru, plan, n_copies, after)
    n = len(done) // 2
    return done[:n], done[n:]


def _rs_finish_begin(name, sums):
    n = len(sums)
    landing = lax.empty((n,) + sums[0].shape, sums[0].dtype)

    def plan(refs, x, y, c):
        return [(refs[t], refs[n].at[t], (x, y, 1 - c)) for t in range(n)]

    sems, thru, token = _split_start(name + "_start", list(sums) + [landing], plan, n)
    return (name, sems, thru, plan, n), token


def _rs_finish_end(pending, after):
    name, sems, thru, plan, n = pending
    done = _split_wait(name + "_wait", sems, thru, plan, n, after)
    return done[:n], done[n]


def _rows2d(a):
    return a.reshape(-1, a.shape[-1])


def _view3(a):
    return a.reshape((-1,) + a.shape[-2:])


def _add_pair(g, recv, ha, core1):
    g3, r3 = _view3(g), _view3(recv)
    rows_half = ha + 3 - g.ndim == 1
    n_l, r, c = r3.shape
    tr = _tile(r, 512, 16)
    nr = r // tr
    gmap = (lambda l, i, s: (l, s[0] * nr + i, 0)) if rows_half else (lambda l, i, s: (l, i, s[0]))

    def body(s_ref, g_ref, r_ref, o_ref):
        o_ref[...] = (g_ref[...].astype(F32) + r_ref[...].astype(F32)).astype(BF16)

    spec = pl.BlockSpec((None, tr, c), lambda l, i, s: (l, i, 0))
    out = pl.pallas_call(
        body, name="add_pair", out_shape=jax.ShapeDtypeStruct(r3.shape, BF16),
        grid_spec=pltpu.PrefetchScalarGridSpec(num_scalar_prefetch=1, grid=(n_l, nr),
                                               in_specs=[pl.BlockSpec((None, tr, c), gmap), spec], out_specs=spec),
        compiler_params=_cp("parallel", "parallel"))(core1, g3, r3)
    return out.reshape(recv.shape)


def _add_four(cs, recv, sa, chip1):
    c3 = _view3(cs)
    s3 = sa + 3 - cs.ndim
    lo, ro, co = (dim // N_CHIPS if ax == s3 else dim for ax, dim in enumerate(c3.shape))
    r4 = recv.reshape(N_CHIPS - 1, lo, ro, co)
    tr = _tile(ro, 256, 16)
    nr = ro // tr
    if s3 == 0:
        cmap = lambda l, i, s: (s[0] * lo + l, i, 0)
    elif s3 == 1:
        cmap = lambda l, i, s: (l, s[0] * nr + i, 0)
    else:
        cmap = lambda l, i, s: (l, i, s[0])

    def body(s_ref, c_ref, r_ref, out_ref):
        out_ref[...] = ((c_ref[...].astype(F32) + r_ref[0].astype(F32)) + r_ref[1].astype(F32)) + r_ref[2].astype(F32)

    out = pl.pallas_call(
        body, name="add_four", out_shape=jax.ShapeDtypeStruct((lo, ro, co), F32),
        grid_spec=pltpu.PrefetchScalarGridSpec(
            num_scalar_prefetch=1, grid=(lo, nr),
            in_specs=[pl.BlockSpec((None, tr, co), cmap),
                      pl.BlockSpec((N_CHIPS - 1, None, tr, co), lambda l, i, s: (0, l, i, 0))],
            out_specs=pl.BlockSpec((None, tr, co), lambda l, i, s: (l, i, 0))),
        compiler_params=_cp("parallel", "parallel"))(chip1, c3, r4)
    return out.reshape(recv.shape[1:])


def _adamw_halves(w, own, recv, m, v, rows_half, core1, l0=0, prev=None):
    w3, m3, v3, o3, r3 = (_view3(a) for a in (w, m, v, own, recv))
    _, _, c = w3.shape
    n_l, rh, ch = o3.shape
    tr = _tile(rh, 128, 8)
    nr = rh // tr
    c1 = 1.0 - ADAM_B1 ** ADAM_STEP
    c2 = 1.0 - ADAM_B2 ** ADAM_STEP

    n_prev = 0 if prev is None else 4

    def body(s_ref, w_ref, o_ref, r_ref, m_ref, v_ref, *rest):
        g_ref, d_ref, nm_ref, nv_ref = rest[n_prev:]
        gv = jnp.where(pl.program_id(1) == s_ref[0], o_ref[...], r_ref[...])
        nm = ADAM_B1 * m_ref[...] + (1.0 - ADAM_B1) * gv
        nv = ADAM_B2 * v_ref[...] + (1.0 - ADAM_B2) * (gv * gv)
        g_ref[...] = gv
        nm_ref[...] = nm
        nv_ref[...] = nv
        d_ref[...] = -ADAM_LR * ((nm / c1) / (jnp.sqrt(nv / c2) + ADAM_EPS) + ADAM_WD * w_ref[...])

    wmap = (lambda l, h, i, s: (l0 + l, h * nr + i, 0)) if rows_half else (lambda l, h, i, s: (l0 + l, i, h))
    wspec = pl.BlockSpec((None, tr, ch), wmap)
    ospec = pl.BlockSpec((None, tr, ch), lambda l, h, i, s: (jnp.where(h == s[0], l, 0), jnp.where(h == s[0], i, 0), 0))
    rspec = pl.BlockSpec((None, tr, ch), lambda l, h, i, s: (jnp.where(h == s[0], 0, l), jnp.where(h == s[0], 0, i), 0))
    osh = jax.ShapeDtypeStruct(w3.shape, F32)
    before = [] if prev is None else [_view3(p) for p in prev]
    outs = pl.pallas_call(
        body, name="adamw_halves", out_shape=[osh] * 4,
        grid_spec=pltpu.PrefetchScalarGridSpec(
            num_scalar_prefetch=1, grid=(n_l, 2, nr),
            in_specs=[wspec, ospec, rspec, wspec, wspec] + [ANY] * n_prev, out_specs=[wspec] * 4),
        input_output_aliases={6 + k: k for k in range(n_prev)},
        compiler_params=_cp("parallel", "parallel", "parallel"))(core1, w3, o3, r3, m3, v3, *before)
    return tuple(o.reshape(w.shape) for o in outs)


def _adamw(w, g, m, v):
    shape = w.shape
    w2, g2, m2, v2 = (_rows2d(a) if a.ndim > 1 else a.reshape(1, -1) for a in (w, g, m, v))
    rows, cols = w2.shape
    tr = _tile(rows, 256, 8)
    c1 = 1.0 - ADAM_B1 ** ADAM_STEP
    c2 = 1.0 - ADAM_B2 ** ADAM_STEP

    def body(w_ref, g_ref, m_ref, v_ref, d_ref, nm_ref, nv_ref):
        gv = g_ref[...]
        nm = ADAM_B1 * m_ref[...] + (1.0 - ADAM_B1) * gv
        nv = ADAM_B2 * v_ref[...] + (1.0 - ADAM_B2) * (gv * gv)
        nm_ref[...] = nm
        nv_ref[...] = nv
        d_ref[...] = -ADAM_LR * ((nm / c1) / (jnp.sqrt(nv / c2) + ADAM_EPS) + ADAM_WD * w_ref[...])

    spec = pl.BlockSpec((tr, cols), lambda i: (i, 0))
    osh = jax.ShapeDtypeStruct((rows, cols), F32)
    outs = pl.pallas_call(body, name="adamw", grid=(rows // tr,), in_specs=[spec] * 4, out_specs=[spec] * 3,
                          out_shape=[osh] * 3, compiler_params=_cp("parallel"))(w2, g2, m2, v2)
    return tuple(o.reshape(shape) for o in outs)


def _pack(arrs):
    flat = jnp.concatenate([a.reshape(-1) for a in arrs])
    n = flat.shape[0]
    rows = -(-n // (8 * LANE)) * 8
    return jnp.pad(flat, (0, rows * LANE - n)).reshape(rows, LANE)


def _unpack(packed, shapes):
    flat = packed.reshape(-1)
    out, pos = [], 0
    for sh in shapes:
        size = 1
        for dsz in sh:
            size *= dsz
        out.append(flat[pos:pos + size].reshape(sh))
        pos += size
    return out


BIG = ("ffn_w_gate", "ffn_w_up", "ffn_w_down", "ssd_w_in", "ssd_w_out", "pool_w_in", "pool_w_group", "pool_w_out")
WEIGHTS = ("ffn_norm", "ffn_w_gate", "ffn_w_up", "ffn_w_down", "mix_norm", "ssd_w_in", "ssd_conv_w", "ssd_conv_b",
           "ssd_dt_bias", "ssd_a_log", "ssd_d", "ssd_norm", "ssd_w_out", "pool_w_in", "pool_w_group", "pool_scale",
           "pool_w_out", "final_norm")
SMALL = tuple(k for k in WEIGHTS if k not in BIG)
SMALL_SHARDED = {"ffn_norm": 2, "ssd_conv_w": 2, "pool_scale": 1}


def kernel(x, ffn_norm, ffn_w_gate, ffn_w_up, ffn_w_down, mix_norm, ssd_w_in, ssd_conv_w, ssd_conv_b, ssd_dt_bias, ssd_a_log, ssd_d, ssd_norm, ssd_w_out, pool_w_in, pool_w_group, pool_scale, pool_w_out, final_norm, loss_target, m_ffn_norm, m_ffn_w_gate, m_ffn_w_up, m_ffn_w_down, m_mix_norm, m_ssd_w_in, m_ssd_conv_w, m_ssd_conv_b, m_ssd_dt_bias, m_ssd_a_log, m_ssd_d, m_ssd_norm, m_ssd_w_out, m_pool_w_in, m_pool_w_group, m_pool_scale, m_pool_w_out, m_final_norm, v_ffn_norm, v_ffn_w_gate, v_ffn_w_up, v_ffn_w_down, v_mix_norm, v_ssd_w_in, v_ssd_conv_w, v_ssd_conv_b, v_ssd_dt_bias, v_ssd_a_log, v_ssd_d, v_ssd_norm, v_ssd_w_out, v_pool_w_in, v_pool_w_group, v_pool_scale, v_pool_w_out, v_final_norm):
    w = dict(ffn_norm=ffn_norm, ffn_w_gate=ffn_w_gate, ffn_w_up=ffn_w_up, ffn_w_down=ffn_w_down, mix_norm=mix_norm,
             ssd_w_in=ssd_w_in, ssd_conv_w=ssd_conv_w, ssd_conv_b=ssd_conv_b, ssd_dt_bias=ssd_dt_bias,
             ssd_a_log=ssd_a_log, ssd_d=ssd_d, ssd_norm=ssd_norm, ssd_w_out=ssd_w_out, pool_w_in=pool_w_in,
             pool_w_group=pool_w_group, pool_scale=pool_scale, pool_w_out=pool_w_out, final_norm=final_norm)
    mom = dict(ffn_norm=m_ffn_norm, ffn_w_gate=m_ffn_w_gate, ffn_w_up=m_ffn_w_up, ffn_w_down=m_ffn_w_down,
               mix_norm=m_mix_norm, ssd_w_in=m_ssd_w_in, ssd_conv_w=m_ssd_conv_w, ssd_conv_b=m_ssd_conv_b,
               ssd_dt_bias=m_ssd_dt_bias, ssd_a_log=m_ssd_a_log, ssd_d=m_ssd_d, ssd_norm=m_ssd_norm,
               ssd_w_out=m_ssd_w_out, pool_w_in=m_pool_w_in, pool_w_group=m_pool_w_group, pool_scale=m_pool_scale,
               pool_w_out=m_pool_w_out, final_norm=m_final_norm)
    vel = dict(ffn_norm=v_ffn_norm, ffn_w_gate=v_ffn_w_gate, ffn_w_up=v_ffn_w_up, ffn_w_down=v_ffn_w_down,
               mix_norm=v_mix_norm, ssd_w_in=v_ssd_w_in, ssd_conv_w=v_ssd_conv_w, ssd_conv_b=v_ssd_conv_b,
               ssd_dt_bias=v_ssd_dt_bias, ssd_a_log=v_ssd_a_log, ssd_d=v_ssd_d, ssd_norm=v_ssd_norm,
               ssd_w_out=v_ssd_w_out, pool_w_in=v_pool_w_in, pool_w_group=v_pool_w_group, pool_scale=v_pool_scale,
               pool_w_out=v_pool_w_out, final_norm=v_final_norm)
    depth = ffn_w_gate.shape[0]
    n_s, n_p = ssd_w_in.shape[0], pool_w_in.shape[0]
    chip = 2 * lax.axis_index("x") + lax.axis_index("y")

    chip1 = jnp.reshape(chip, (1,)).astype(jnp.int32)
    core1 = jnp.reshape(lax.axis_index("c"), (1,)).astype(jnp.int32)

    gate3, up3, down3 = _view3(ffn_w_gate), _view3(ffn_w_up), _view3(ffn_w_down)
    w_in_t = ssd_w_in.transpose(0, 2, 1)
    pending = {}
    sharded_names = tuple(SMALL_SHARDED)
    tok = jnp.zeros((), F32)

    def begin(group, name, fulls, axes):
        nonlocal tok
        pending[group], t = _gather_begin(name, fulls, axes)
        tok = tok + t

    for i in range(depth):
        for h in range(2):
            fulls = [_cast_place(gate3, chip1, "cols", 2 * i + h, 1), _cast_place(up3, chip1, "cols", 2 * i + h, 1),
                     _cast_place(down3, chip1, "rows", 2 * i + h, 1)]
            if (i, h) == (0, 0):
                packed_small = _pack([w[k] for k in sharded_names])
                fulls.append(_cast_place(packed_small[None], chip1, "slot", 0, 1, F32))
                begin(("ffn", i, h), "gather_first", fulls, [(2, 1), (2, 1), (1, 2), (1, 2)])
            else:
                begin(("ffn", i, h), "gather_ffn", fulls, [(2, 1), (2, 1), (1, 2)])
            j = i // 2
            if h == 0 and i % 2 == 0:
                begin(("ssd", j), "gather_ssd", [_cast_place(w_in_t, chip1, "rows", j, 1),
                                                 _cast_place(ssd_w_out, chip1, "rows", j, 1)], [(1, 2), (1, 2)])
            if h == 0 and i % 2 == 1:
                n_g = pool_w_group.shape[1]
                grp_full = _cast_place(_view3(pool_w_group), chip1, "rows", j * n_g, n_g)
                begin(("pool", j), "gather_pool", [_cast_place(pool_w_in, chip1, "rows", j, 1), grp_full[None],
                                                   _cast_place(pool_w_out, chip1, "rows", j, 1)],
                      [(1, 2), (2, 3), (1, 2)])

    fetched = {}

    def fetch(group, after):
        if group not in fetched:
            got = _gather_finish(pending[group], after)
            fetched[group] = got[:3]
            if group == ("ffn", 0, 0):
                fetched["small"] = got[3][0]
        return fetched[group]

    fetch(("ffn", 0, 0), x)
    small = {k: w[k] for k in SMALL if k not in SMALL_SHARDED}
    per_chip = [_unpack(fetched["small"][s], [w[k].shape for k in sharded_names]) for s in range(N_CHIPS)]
    for t, k in enumerate(sharded_names):
        small[k] = jnp.concatenate([per_chip[s][t] for s in range(N_CHIPS)], axis=SMALL_SHARDED[k])

    ffn_names = ("ffn_w_gate", "ffn_w_up", "ffn_w_down")
    group_axes = dict(ffn=[(1, 0), (1, 0), (0, 1)], ssd=[(0, 1), (0, 1)], pool=[(0, 1), (1, 2), (0, 1)])
    group_names = dict(ffn=ffn_names, ssd=("ssd_w_in", "ssd_w_out"), pool=("pool_w_in", "pool_w_group", "pool_w_out"))
    travelling = {}
    between_cores = []

    def advance(after):
        group, pend = between_cores.pop()
        axes = group_axes[group[0]]
        grads, recv_a = _rs_cores_finish(pend, after)
        chip_sums = [_add_pair(g, r, ha, core1) for g, r, (_, ha) in zip(grads, recv_a, axes)]
        travelling[group], t = _rs_chips_begin("rs_chips_" + group[0], chip_sums, axes)
        return t

    def emit(group, grads):
        kind = group[0]
        t = advance(grads[0]) if between_cores else jnp.zeros((), F32)
        pend, t2 = _rs_cores_begin("rs_cores_" + kind, grads, group_axes[kind])
        between_cores.append((group, pend))
        return t + t2

    total = {}

    def on_loss(part):
        total["loss"] = lax.psum(part[0, 0], ("x", "y", "c"))
        return jnp.minimum(total["loss"], 0.0)

    _, dx, gsmall, t_emits = _local_step(x[0], loss_target[0], depth, fetch, emit, on_loss, small, tok)
    loss = total["loss"]

    t_last = advance(dx)
    small_pending, t_small = _small_begin(_pack([gsmall[k] for k in SMALL]))
    started = jnp.reshape(t_emits + t_last + t_small, (1, 1))

    rows_half = dict(ffn_w_gate=True, ffn_w_up=True, ffn_w_down=False, ssd_w_in=False, ssd_w_out=False,
                     pool_w_in=False, pool_w_group=False, pool_w_out=False)
    w_t, mom_t, vel_t = ({**src, "ssd_w_in": src["ssd_w_in"].transpose(0, 2, 1)} for src in (w, mom, vel))
    results = {}

    def finish(groups, after):
        sums = {k: {} for k in BIG}
        for group in groups:
            kind = group[0]
            chip_sums, recv_b = _rs_chips_finish(travelling[group], after)
            for k, cs, r, (sa, _) in zip(group_names[kind], chip_sums, recv_b, group_axes[kind]):
                sums[k][group[1:]] = _add_four(cs, r, sa, chip1)
        names = [k for k in BIG if sums[k]]
        sending, t_sent = {}, jnp.zeros((), F32)
        for k in names:
            sending[k], t = _rs_finish_begin("rs_finish_" + k, [sums[k][idx] for idx in sorted(sums[k])])
            t_sent = t_sent + t
        before = t_sent
        for k in names:
            own, from_sibling = _rs_finish_end(sending[k], jnp.stack([t_sent, before]).reshape(1, 2))
            first = min(sums[k])
            l0 = first[0] * 2 + first[1] if k in ffn_names else first[0]
            results[k] = _adamw_halves(w_t[k], jnp.stack(own), from_sibling, mom_t[k], vel_t[k], rows_half[k],
                                       core1, l0, results.get(k))
            before = results[k][3][(0,) * results[k][3].ndim]
        return jnp.stack([results[k][3][(0,) * results[k][3].ndim] for k in names]).reshape(1, -1)

    late = [g for g in travelling if g[1] == 0 and g[0] != "pool"]
    early = [g for g in travelling if g not in late]
    done_early = finish(early, started) if early else started
    done_late = finish(late, done_early)
    results["ssd_w_in"] = tuple(r.transpose(0, 2, 1) for r in results["ssd_w_in"])
    grad, delta, new_m, new_v = ({k: results[k][t] for k in BIG} for t in range(4))

    small_shapes = [gsmall[k].shape for k in SMALL]
    me1 = 2 * chip1 + core1
    summed = _unpack(_small_finish(small_pending, done_late, me1), small_shapes)
    for k, g in zip(SMALL, summed):
        if k in SMALL_SHARDED:
            ax = SMALL_SHARDED[k]
            size = w[k].shape[ax]
            g = lax.dynamic_slice_in_dim(g, chip * size, size, axis=ax)
        grad[k] = g

    shapes = [w[k].shape for k in SMALL]
    packed = _adamw(*(_pack([src[k] for k in SMALL]) for src in (w, grad, mom, vel)))
    for dst, p in zip((delta, new_m, new_v), packed):
        for k, a in zip(SMALL, _unpack(p, shapes)):
            dst[k] = a

    return (loss, dx[None], *[grad[k] for k in WEIGHTS], *[delta[k] for k in WEIGHTS],
            *[new_m[k] for k in WEIGHTS], *[new_v[k] for k in WEIGHTS])
```

```python
import functools

import jax
import jax.numpy as jnp
from jax import lax
from jax.experimental import pallas as pl
from jax.experimental.pallas import tpu as pltpu

F32 = jnp.float32
BF16 = jnp.bfloat16
EPS = 1e-6
MESH = pl.DeviceIdType.MESH

SSD_CHUNK = 128
SSD_STATE = 128
SSD_HEAD_DIM = 64
HEADS_PER_GROUP = 8
GROUP_W = HEADS_PER_GROUP * SSD_HEAD_DIM
CONV_W = 5
POOL_WINDOWS = (2, 4, 8, 16)
N_CHIPS = 4

ADAM_LR = 0.001
ADAM_B1 = 0.9
ADAM_B2 = 0.999
ADAM_EPS = 1e-08
ADAM_WD = 0.01
ADAM_STEP = 10

VMEM_LIMIT = 56 * 1024 * 1024
LANE = 128


def _cp(*sem):
    return pltpu.CompilerParams(dimension_semantics=sem, vmem_limit_bytes=VMEM_LIMIT)


def _tile(dim, pref, unit=LANE):
    if dim <= pref:
        return dim
    t = (pref // unit) * unit
    while t >= unit:
        if dim % t == 0:
            return t
        t -= unit
    return dim


def _sigmoid(v):
    return 1.0 / (1.0 + jnp.exp(-v))


def _dot(a, b):
    return jnp.dot(a, b, preferred_element_type=F32)


def _dot_nt(a, b):
    return lax.dot_general(a, b, (((1,), (1,)), ((), ())), preferred_element_type=F32)


def _split3(v):
    h1 = v.astype(BF16)
    r1 = v - h1.astype(F32)
    h2 = r1.astype(BF16)
    h3 = (r1 - h2.astype(F32)).astype(BF16)
    return h1, h2, h3


def _dot_exact01(m01, v):
    mb = m01.astype(BF16)
    h1, h2, h3 = _split3(v)
    return _dot(mb, h1) + _dot(mb, h2) + _dot(mb, h3)


def _dot_nt_exact01(m01, v):
    mb = m01.astype(BF16)
    h1, h2, h3 = _split3(v)
    return _dot_nt(mb, h1) + _dot_nt(mb, h2) + _dot_nt(mb, h3)


def _mm(a, b, *, name, ta=False, tb=False, a_sel=(), b_sel=(), pair2=None, add=None, scale=1.0,
        out_dtype=F32, tm=1024, tn=1024, tk=2048):
    am, ak = a.shape[-2:][::-1] if ta else a.shape[-2:]
    bk, bn = b.shape[-2:][::-1] if tb else b.shape[-2:]
    assert ak == bk, (a.shape, b.shape, ta, tb)
    m_dim, n_dim, k_dim = am, bn, ak
    tm, tn, tk = _tile(m_dim, tm), _tile(n_dim, tn), _tile(k_dim, tk)
    nk = k_dim // tk
    grid = (n_dim // tn, m_dim // tm, nk)

    def a_spec(sel):
        lead = (None,) * len(sel)
        if ta:
            return pl.BlockSpec(lead + (tk, tm), lambda n, m, k: tuple(sel) + (k, m))
        return pl.BlockSpec(lead + (tm, tk), lambda n, m, k: tuple(sel) + (m, k))

    def b_spec(sel):
        lead = (None,) * len(sel)
        if tb:
            return pl.BlockSpec(lead + (tn, tk), lambda n, m, k: tuple(sel) + (n, k))
        return pl.BlockSpec(lead + (tk, tn), lambda n, m, k: tuple(sel) + (k, n))

    ins, specs = [a, b], [a_spec(a_sel), b_spec(b_sel)]
    if pair2 is not None:
        a2, b2, a2_sel, b2_sel = pair2
        ins += [a2, b2]
        specs += [a_spec(a2_sel), b_spec(b2_sel)]
    if add is not None:
        ins.append(add)
        specs.append(pl.BlockSpec((tm, tn), lambda n, m, k: (m, n)))
    dn = (((0 if ta else 1,), (1 if tb else 0,)), ((), ()))
    n_pairs = 2 if pair2 is not None else 1

    def body(*refs):
        pairs = [(refs[2 * i], refs[2 * i + 1]) for i in range(n_pairs)]
        pos = 2 * n_pairs
        add_ref = None
        if add is not None:
            add_ref = refs[pos]
            pos += 1
        o_ref = refs[pos]
        acc_ref = refs[pos + 1] if nk > 1 else None

        def prod():
            tot = None
            for ar, br in pairs:
                p = lax.dot_general(ar[...].astype(BF16), br[...].astype(BF16), dn, preferred_element_type=F32)
                tot = p if tot is None else tot + p
            return tot

        def finish(r):
            if scale != 1.0:
                r = r * scale
            if add_ref is not None:
                r = add_ref[...] + r
            o_ref[...] = r.astype(out_dtype)

        if nk == 1:
            finish(prod())
        else:
            k = pl.program_id(2)

            @pl.when(k == 0)
            def _():
                acc_ref[...] = jnp.zeros_like(acc_ref)

            acc_ref[...] += prod()

            @pl.when(k == nk - 1)
            def _():
                finish(acc_ref[...])

    return pl.pallas_call(
        body, name=name, grid=grid, in_specs=specs,
        out_specs=pl.BlockSpec((tm, tn), lambda n, m, k: (m, n)),
        out_shape=jax.ShapeDtypeStruct((m_dim, n_dim), out_dtype),
        scratch_shapes=[pltpu.VMEM((tm, tn), F32)] if nk > 1 else [],
        compiler_params=_cp("parallel", "parallel", "arbitrary"),
    )(*ins)


def _rmsnorm(x, g):
    t_dim, d = x.shape
    tr = _tile(t_dim, 256, 8)

    def body(x_ref, g_ref, o_ref):
        xv = x_ref[...]
        r = lax.rsqrt(jnp.mean(xv * xv, axis=-1, keepdims=True) + EPS)
        o_ref[...] = (xv * r * g_ref[...]).astype(BF16)

    return pl.pallas_call(
        body, name="rmsnorm", grid=(t_dim // tr,),
        in_specs=[pl.BlockSpec((tr, d), lambda i: (i, 0)), pl.BlockSpec((1, d), lambda i: (0, 0))],
        out_specs=pl.BlockSpec((tr, d), lambda i: (i, 0)),
        out_shape=jax.ShapeDtypeStruct((t_dim, d), BF16),
        compiler_params=_cp("parallel"),
    )(x, g)


def _rmsnorm_bwd(x, g, dh, dres):
    t_dim, d = x.shape
    tr = _tile(t_dim, 256, 8)

    def body(x_ref, g_ref, dh_ref, dres_ref, dx_ref, dg_ref):
        i = pl.program_id(0)
        xv = x_ref[...]
        r = lax.rsqrt(jnp.mean(xv * xv, axis=-1, keepdims=True) + EPS)
        n = xv * r
        dhv = dh_ref[...]
        dn = dhv * g_ref[...]

        @pl.when(i == 0)
        def _():
            dg_ref[...] = jnp.zeros_like(dg_ref)

        dg_ref[...] += jnp.sum(dhv * n, axis=0, keepdims=True)
        dx_ref[...] = dres_ref[...] + r * (dn - n * jnp.mean(dn * n, axis=-1, keepdims=True))

    row = pl.BlockSpec((tr, d), lambda i: (i, 0))
    vec = pl.BlockSpec((1, d), lambda i: (0, 0))
    return pl.pallas_call(
        body, name="rmsnorm_bwd", grid=(t_dim // tr,),
        in_specs=[row, vec, row, row], out_specs=[row, vec],
        out_shape=[jax.ShapeDtypeStruct((t_dim, d), F32), jax.ShapeDtypeStruct((1, d), F32)],
        compiler_params=_cp("arbitrary"),
    )(x, g, dh, dres)


def _loss_head(x, g, target):
    t_dim, d = x.shape
    tr = _tile(t_dim, 256, 8)

    def body(x_ref, g_ref, t_ref, loss_ref, dx_ref, dg_ref):
        i = pl.program_id(0)
        xv = x_ref[...]
        gv = g_ref[...]
        r = lax.rsqrt(jnp.mean(xv * xv, axis=-1, keepdims=True) + EPS)
        n = xv * r
        err = n * gv - t_ref[...]

        @pl.when(i == 0)
        def _():
            dg_ref[...] = jnp.zeros_like(dg_ref)
            loss_ref[...] = jnp.zeros_like(loss_ref)

        per_tok = jnp.mean(err * err, axis=-1, keepdims=True)
        loss_ref[...] += 0.5 * jnp.sum(per_tok, axis=0, keepdims=True)
        dy = err * (1.0 / d)
        dn = dy * gv
        dg_ref[...] += jnp.sum(dy * n, axis=0, keepdims=True)
        dx_ref[...] = r * (dn - n * jnp.mean(dn * n, axis=-1, keepdims=True))

    row = pl.BlockSpec((tr, d), lambda i: (i, 0))
    vec = pl.BlockSpec((1, d), lambda i: (0, 0))
    one = pl.BlockSpec((1, 1), lambda i: (0, 0))
    return pl.pallas_call(
        body, name="loss_head", grid=(t_dim // tr,),
        in_specs=[row, vec, row], out_specs=[one, row, vec],
        out_shape=[jax.ShapeDtypeStruct((1, 1), F32), jax.ShapeDtypeStruct((t_dim, d), F32),
                   jax.ShapeDtypeStruct((1, d), F32)],
        compiler_params=_cp("arbitrary"),
    )(x, g, target)


def _ffn_in(h, wg, wu, half):
    t_dim, d = h.shape
    f = wg.shape[-1]
    tm, tn = _tile(t_dim, 512), _tile(f, 1408)

    def body(h_ref, wg_ref, wu_ref, g_ref, u_ref, a_ref):
        hv = h_ref[...]
        gv = _dot(hv, wg_ref[...])
        uv = _dot(hv, wu_ref[...])
        g_ref[...] = gv.astype(BF16)
        u_ref[...] = uv.astype(BF16)
        a_ref[...] = (gv * _sigmoid(gv) * uv).astype(BF16)

    wspec = pl.BlockSpec((None, d, tn), lambda n, m: (half, 0, n))
    ospec = pl.BlockSpec((tm, tn), lambda n, m: (m, n))
    oshape = jax.ShapeDtypeStruct((t_dim, f), BF16)
    return pl.pallas_call(
        body, name="ffn_in", grid=(f // tn, t_dim // tm),
        in_specs=[pl.BlockSpec((tm, d), lambda n, m: (m, 0)), wspec, wspec],
        out_specs=[ospec, ospec, ospec], out_shape=[oshape, oshape, oshape],
        compiler_params=_cp("parallel", "parallel"),
    )(h, wg, wu)


def _ffn_bwd_act(dx, wd, g, u, half):
    t_dim, d = dx.shape
    f = wd.shape[-2]
    tm, tn = _tile(t_dim, 512), _tile(f, 1408)

    def body(dx_ref, wd_ref, g_ref, u_ref, dg_ref, du_ref):
        da = 0.5 * _dot_nt(dx_ref[...].astype(BF16), wd_ref[...])
        gv = g_ref[...].astype(F32)
        uv = u_ref[...].astype(F32)
        s = _sigmoid(gv)
        dg_ref[...] = (da * uv * (s * (1.0 + gv * (1.0 - s)))).astype(BF16)
        du_ref[...] = (da * gv * s).astype(BF16)

    tile = pl.BlockSpec((tm, tn), lambda n, m: (m, n))
    oshape = jax.ShapeDtypeStruct((t_dim, f), BF16)
    return pl.pallas_call(
        body, name="ffn_bwd_act", grid=(f // tn, t_dim // tm),
        in_specs=[pl.BlockSpec((tm, d), lambda n, m: (m, 0)),
                  pl.BlockSpec((None, tn, d), lambda n, m: (half, n, 0)), tile, tile],
        out_specs=[tile, tile], out_shape=[oshape, oshape],
        compiler_params=_cp("parallel", "parallel"),
    )(dx, wd, g, u)


def _ffn_fwd(x, norm_g, wg, wu, wd, half):
    h = _rmsnorm(x, norm_g)
    g, u, a = _ffn_in(h, wg, wu, half)
    x_new = _mm(a, wd, name="ffn_out", b_sel=(half,), add=x, scale=0.5, tk=1408)
    return x_new, (x, h, g, u, a)


def _ffn_bwd(dx, saved, norm_g, wg, wu, wd, half):
    x, h, g, u, a = saved
    dg, du = _ffn_bwd_act(dx, wd, g, u, half)
    d_wd = _mm(a, dx, name="ffn_dwd", ta=True, scale=0.5, out_dtype=BF16, tm=1408, tn=1024)
    d_wg = _mm(h, dg, name="ffn_dwgu", ta=True, out_dtype=BF16, tm=1024, tn=1408)
    d_wu = _mm(h, du, name="ffn_dwgu", ta=True, out_dtype=BF16, tm=1024, tn=1408)
    dh = _mm(dg, wg, name="ffn_dh", tb=True, b_sel=(half,), pair2=(du, wu, (), (half,)), tk=1408)
    dx_new, dnorm = _rmsnorm_bwd(x, norm_g, dh, dx)
    return dx_new, dnorm, d_wg, d_wu, d_wd


def _shifted(v, off, t_idx):
    if off == 0:
        return v
    t_dim = v.shape[0]
    sh = pltpu.roll(v, (-off) % t_dim, 0)
    valid = jnp.logical_and(t_idx + off >= 0, t_idx + off < t_dim)
    return jnp.where(valid, sh, 0.0)


def _conv_pre(u, w_ref, b_ref, t_idx):
    acc = jnp.zeros_like(u) + b_ref[...]
    shifted = []
    for k in range(CONV_W):
        sh = _shifted(u, k - CONV_W // 2, t_idx)
        shifted.append(sh)
        acc = acc + w_ref[k:k + 1, :] * sh
    return acc, shifted


def _conv_silu(proj, conv_w, conv_b, col0):
    t_dim = proj.shape[0]
    cd = conv_w.shape[-1]
    cb = _tile(cd, 256)
    assert col0 % cb == 0

    def body(u_ref, w_ref, b_ref, o_ref):
        t_idx = lax.broadcasted_iota(jnp.int32, (t_dim, cb), 0)
        pre, _ = _conv_pre(u_ref[...], w_ref, b_ref, t_idx)
        o_ref[...] = pre * _sigmoid(pre)

    return pl.pallas_call(
        body, name="conv_silu", grid=(cd // cb,),
        in_specs=[pl.BlockSpec((t_dim, cb), lambda j: (0, col0 // cb + j)),
                  pl.BlockSpec((CONV_W, cb), lambda j: (0, j)), pl.BlockSpec((1, cb), lambda j: (0, j))],
        out_specs=pl.BlockSpec((t_dim, cb), lambda j: (0, j)),
        out_shape=jax.ShapeDtypeStruct((t_dim, cd), F32),
        compiler_params=_cp("parallel"),
    )(proj, conv_w, conv_b)


def _conv_silu_bwd(proj, conv_w, conv_b, dact2, col0, ch0):
    t_dim = proj.shape[0]
    cd = dact2.shape[-1]
    cb = _tile(cd, 256)
    assert (col0 + ch0) % cb == 0 and ch0 % cb == 0
    p0, c0 = (col0 + ch0) // cb, ch0 // cb

    def body(u_ref, w_ref, b_ref, da_ref, du_ref, dwb_ref):
        t_idx = lax.broadcasted_iota(jnp.int32, (t_dim, cb), 0)
        pre, shifted = _conv_pre(u_ref[...], w_ref, b_ref, t_idx)
        s = _sigmoid(pre)
        dpre = (da_ref[0] + da_ref[1]) * (s * (1.0 + pre * (1.0 - s)))
        du = jnp.zeros_like(dpre)
        for k in range(CONV_W):
            du = du + w_ref[k:k + 1, :] * _shifted(dpre, -(k - CONV_W // 2), t_idx)
            dwb_ref[k:k + 1, :] = jnp.sum(dpre * shifted[k], axis=0, keepdims=True)
        dwb_ref[CONV_W:CONV_W + 1, :] = jnp.sum(dpre, axis=0, keepdims=True)
        dwb_ref[CONV_W + 1:8, :] = jnp.zeros((8 - CONV_W - 1, cb), F32)
        du_ref[...] = du.astype(BF16)

    return pl.pallas_call(
        body, name="conv_silu_bwd", grid=(cd // cb,),
        in_specs=[pl.BlockSpec((t_dim, cb), lambda j: (0, p0 + j)),
                  pl.BlockSpec((CONV_W, cb), lambda j: (0, c0 + j)), pl.BlockSpec((1, cb), lambda j: (0, c0 + j)),
                  pl.BlockSpec((2, t_dim, cb), lambda j: (0, 0, j))],
        out_specs=[pl.BlockSpec((t_dim, cb), lambda j: (0, j)), pl.BlockSpec((8, cb), lambda j: (0, j))],
        out_shape=[jax.ShapeDtypeStruct((t_dim, cd), BF16), jax.ShapeDtypeStruct((8, cd), F32)],
        compiler_params=_cp("parallel"),
    )(proj, conv_w, conv_b, dact2)


def _softplus_fwd(dt_raw, bias):
    def body(r_ref, b_ref, o_ref):
        v = r_ref[...] + b_ref[...]
        o_ref[...] = jnp.maximum(v, 0.0) + jnp.log(1.0 + jnp.exp(-jnp.abs(v)))

    return pl.pallas_call(body, name="softplus", out_shape=jax.ShapeDtypeStruct(dt_raw.shape, F32))(dt_raw, bias)


def _softplus_bwd(dt_raw, bias, ddt, da, a_log):
    def body(r_ref, b_ref, ddt_ref, da_ref, al_ref, dr_ref, db_ref, dal_ref):
        dv = ddt_ref[...] * _sigmoid(r_ref[...] + b_ref[...])
        dr_ref[...] = dv.astype(BF16)
        db_ref[...] = jnp.sum(dv, axis=0, keepdims=True)
        dal_ref[...] = -da_ref[...] * jnp.exp(al_ref[...])

    vec = jax.ShapeDtypeStruct(bias.shape, F32)
    return pl.pallas_call(
        body, name="softplus_bwd",
        out_shape=[jax.ShapeDtypeStruct(dt_raw.shape, BF16), vec, vec])(dt_raw, bias, ddt, da, a_log)


def _chunk_setup(d, dt_ref, al_ref):
    q = SSD_CHUNK
    ii = lax.broadcasted_iota(jnp.int32, (q, q), 0)
    jj = lax.broadcasted_iota(jnp.int32, (q, q), 1)
    sgn = 1 - 2 * d
    mask = (jj - ii) * sgn <= 0
    mask_t = (ii - jj) * sgn <= 0
    m01 = mask.astype(F32)
    m01_t = mask_t.astype(F32)
    dt = dt_ref[...]
    a = -jnp.exp(al_ref[...])
    dta = dt * a
    cs = _dot_exact01(m01, dta)
    tot = jnp.sum(dta, axis=0, keepdims=True)
    return mask, mask_t, m01, m01_t, dt, a, dta, cs, tot


def _head_lanes():
    hh = lax.broadcasted_iota(jnp.int32, (HEADS_PER_GROUP, GROUP_W), 0)
    ll = lax.broadcasted_iota(jnp.int32, (HEADS_PER_GROUP, GROUP_W), 1)
    return jnp.logical_and(ll >= hh * SSD_HEAD_DIM, ll < (hh + 1) * SSD_HEAD_DIM).astype(BF16)


def _expand(v, e16):
    h1, h2, h3 = _split3(v)
    return _dot(h1, e16) + _dot(h2, e16) + _dot(h3, e16)


def _expand_row(v, e16):
    return _expand(jnp.broadcast_to(v, (8, HEADS_PER_GROUP)), e16)[0:1]


def _per_head(v, e16):
    h1, h2, h3 = _split3(v)
    return _dot_nt(h1, e16) + _dot_nt(h2, e16) + _dot_nt(h3, e16)


def _rows_of(cs):
    n = cs.shape[1]
    eye = lax.broadcasted_iota(jnp.int32, (n, n), 0) == lax.broadcasted_iota(jnp.int32, (n, n), 1)
    return _dot_nt_exact01(eye.astype(F32), cs)


def _ssd_specs(t_dim, di, g_cnt, chunk_of):
    q, ns = SSD_CHUNK, SSD_STATE
    x_spec = pl.BlockSpec((q, GROUP_W), lambda d, g, c: (chunk_of(d, c), g))
    b_spec = pl.BlockSpec((q, ns), lambda d, g, c: (chunk_of(d, c), di // ns + g))
    c_spec = pl.BlockSpec((q, ns), lambda d, g, c: (chunk_of(d, c), di // ns + g_cnt + g))
    dt_spec = pl.BlockSpec((None, None, q, HEADS_PER_GROUP), lambda d, g, c: (d, g, chunk_of(d, c), 0))
    al_spec = pl.BlockSpec((None, None, 1, HEADS_PER_GROUP), lambda d, g, c: (d, g, 0, 0))
    return x_spec, b_spec, c_spec, dt_spec, al_spec


def _ssd_fwd(xbc, dt4, al4):
    t_dim = xbc.shape[0]
    g_cnt = dt4.shape[1]
    di = g_cnt * GROUP_W
    q, ns, p = SSD_CHUNK, SSD_STATE, SSD_HEAD_DIM
    nc = t_dim // q

    def chunk_of(d, c):
        return c + d * (nc - 1 - 2 * c)

    def body(x_ref, b_ref, c_ref, dt_ref, al_ref, y_ref, hin_ref, h_sc):
        d = pl.program_id(0)
        c = pl.program_id(2)

        @pl.when(c == 0)
        def _():
            h_sc[...] = jnp.zeros_like(h_sc)

        mask, mask_t, m01, m01_t, dt, a, dta, cs, tot = _chunk_setup(d, dt_ref, al_ref)
        e16 = _head_lanes()
        cs_rows = _rows_of(cs)
        cb16 = c_ref[...].astype(BF16)
        bt16 = b_ref[...].T.astype(BF16)
        cb = _dot(cb16, bt16)
        hin = h_sc[...]
        hin_ref[...] = hin
        xdt = x_ref[...] * _expand(dt, e16)
        y_off = _dot(cb16, hin.astype(BF16)) * _expand(jnp.exp(cs), e16)
        low = lax.broadcasted_iota(jnp.int32, (q, 2 * p), 1) < p
        for pair in range(HEADS_PER_GROUP // 2):
            ps = slice(2 * p * pair, 2 * p * (pair + 1))
            blk = xdt[:, ps]
            acc = y_off[:, ps]
            for hh in range(2):
                h = 2 * pair + hh
                lmat = jnp.exp(jnp.where(mask, cs[:, h:h + 1] - cs_rows[h:h + 1, :], -1e30))
                xm = jnp.where(low, blk, 0.0) if hh == 0 else jnp.where(low, 0.0, blk)
                acc = acc + _dot((cb * lmat).astype(BF16), xm.astype(BF16))
            y_ref[:, ps] = acc
        xd = xdt * _expand(jnp.exp(tot - cs), e16)
        st = _dot(bt16, xd.astype(BF16))
        h_sc[...] = hin * _expand_row(jnp.exp(tot), e16) + st

    x_spec, b_spec, c_spec, dt_spec, al_spec = _ssd_specs(t_dim, di, g_cnt, chunk_of)
    return pl.pallas_call(
        body, name="ssd_fwd", grid=(2, g_cnt, nc),
        in_specs=[x_spec, b_spec, c_spec, dt_spec, al_spec],
        out_specs=[pl.BlockSpec((None, q, GROUP_W), lambda d, g, c: (d, chunk_of(d, c), g)),
                   pl.BlockSpec((None, None, None, ns, GROUP_W), lambda d, g, c: (d, g, chunk_of(d, c), 0, 0))],
        out_shape=[jax.ShapeDtypeStruct((2, t_dim, di), F32),
                   jax.ShapeDtypeStruct((2, g_cnt, nc, ns, GROUP_W), F32)],
        scratch_shapes=[pltpu.VMEM((ns, GROUP_W), F32)],
        compiler_params=_cp("parallel", "parallel", "arbitrary"),
    )(xbc, xbc, xbc, dt4, al4)


def _ssd_bwd(xbc, dt4, al4, hin_all, dy, dvec):
    t_dim = xbc.shape[0]
    g_cnt = dt4.shape[1]
    di = g_cnt * GROUP_W
    q, ns, p, hg = SSD_CHUNK, SSD_STATE, SSD_HEAD_DIM, HEADS_PER_GROUP
    nc = t_dim // q

    def chunk_of(d, c):
        return (nc - 1 - c) + d * (2 * c - nc + 1)

    def body(x_ref, b_ref, c_ref, dt_ref, al_ref, hin_ref, dy_ref, dv_ref,
             dx_ref, db_ref, dc_ref, ddt_ref, da_ref, g_sc, dxdt_sc, zrow_sc):
        d = pl.program_id(0)
        c = pl.program_id(2)

        @pl.when(c == 0)
        def _():
            g_sc[...] = jnp.zeros_like(g_sc)
            da_ref[...] = jnp.zeros_like(da_ref)

        mask, mask_t, m01, m01_t, dt, a, dta, cs, tot = _chunk_setup(d, dt_ref, al_ref)
        xv = x_ref[...]
        dyv = dy_ref[...]
        bb = b_ref[...].astype(BF16)
        cb16 = c_ref[...].astype(BF16)
        bt16 = b_ref[...].T.astype(BF16)
        ct16 = c_ref[...].T.astype(BF16)
        cb = _dot(cb16, bt16)
        cbt = _dot(bb, ct16)
        hin = hin_ref[...]
        hin16 = hin.astype(BF16)
        gst = g_sc[...]
        gst16 = gst.astype(BF16)
        ch = _dot(cb16, hin16)
        wst = _dot(bb, gst16)
        skip = jnp.where(d == 0, 1.0, 0.0)
        e16 = _head_lanes()
        cs_rows = _rows_of(cs)
        dt_x = _expand(dt, e16)
        e_x = _expand(jnp.exp(cs), e16)
        dec = jnp.exp(tot - cs)
        etot = jnp.exp(tot)
        xdt = xv * dt_x
        dye = dyv * e_x
        t1 = _per_head(wst * xdt, e16) * dec
        dcs = _per_head(dye * ch, e16) - t1
        dtot = jnp.sum(t1, axis=0, keepdims=True)
        dec_x = _expand(dec, e16)
        dxdt_state = wst * dec_x
        dcb = jnp.zeros((q, q), F32)
        low = lax.broadcasted_iota(jnp.int32, (q, 2 * p), 1) < p
        col8 = lax.broadcasted_iota(jnp.int32, (q, hg), 1)
        for pair in range(hg // 2):
            ps = slice(2 * p * pair, 2 * p * (pair + 1))
            xblk, dyblk = xdt[:, ps], dyv[:, ps]
            acc = dxdt_state[:, ps]
            for hh in range(2):
                h = 2 * pair + hh
                seg = cs[:, h:h + 1] - cs_rows[h:h + 1, :]
                lmat = jnp.exp(jnp.where(mask, seg, -1e30))
                lmat_t = jnp.exp(jnp.where(mask_t, -seg, -1e30))
                if hh == 0:
                    xm, dym = jnp.where(low, xblk, 0.0).astype(BF16), jnp.where(low, dyblk, 0.0).astype(BF16)
                else:
                    xm, dym = jnp.where(low, 0.0, xblk).astype(BF16), jnp.where(low, 0.0, dyblk).astype(BF16)
                acc = acc + _dot((cbt * lmat_t).astype(BF16), dym)
                dm = _dot_nt(dym, xm)
                z = dm * (cb * lmat)
                dcb = dcb + dm * lmat
                dcs = dcs + jnp.where(col8 == h, jnp.sum(z, axis=-1, keepdims=True), 0.0)
                zrow_sc[h:h + 1, :] = jnp.sum(z, axis=0, keepdims=True)
            dxdt_sc[:, ps] = acc
        dxdt = dxdt_sc[...]
        dx_ref[...] = dxdt * dt_x + skip * dyv * dv_ref[...]
        dye16 = dye.astype(BF16)
        dcb16 = dcb.astype(BF16)
        dc_ref[...] = _dot_nt(dye16, hin16) + _dot(dcb16, bb)
        db_ref[...] = _dot_nt((xdt * dec_x).astype(BF16), gst16) + _dot(dcb.T.astype(BF16), cb16)
        g_sc[...] = _dot(ct16, dye16) + gst * _expand_row(etot, e16)
        carried = jnp.broadcast_to(jnp.sum(gst * hin, axis=0, keepdims=True), (8, GROUP_W))
        dtot = dtot + _per_head(carried, e16)[0:1] * etot
        ddta = _dot_exact01(m01_t, dcs) - _dot_nt_exact01(m01_t, zrow_sc[...]) + dtot
        ddt_ref[...] = _per_head(dxdt * xv, e16) + ddta * a
        da_ref[...] += jnp.sum(ddta * dt, axis=0, keepdims=True)

    x_spec, b_spec, c_spec, dt_spec, al_spec = _ssd_specs(t_dim, di, g_cnt, chunk_of)
    hin_spec = pl.BlockSpec((None, None, None, ns, GROUP_W), lambda d, g, c: (d, g, chunk_of(d, c), 0, 0))
    dy_spec = pl.BlockSpec((q, GROUP_W), lambda d, g, c: (chunk_of(d, c), g))
    dv_spec = pl.BlockSpec((1, GROUP_W), lambda d, g, c: (0, g))
    gn = g_cnt * ns
    return pl.pallas_call(
        body, name="ssd_bwd", grid=(2, g_cnt, nc),
        in_specs=[x_spec, b_spec, c_spec, dt_spec, al_spec, hin_spec, dy_spec, dv_spec],
        out_specs=[pl.BlockSpec((None, q, GROUP_W), lambda d, g, c: (d, chunk_of(d, c), g)),
                   pl.BlockSpec((None, q, ns), lambda d, g, c: (d, chunk_of(d, c), g)),
                   pl.BlockSpec((None, q, ns), lambda d, g, c: (d, chunk_of(d, c), g)),
                   pl.BlockSpec((None, None, q, hg), lambda d, g, c: (d, g, chunk_of(d, c), 0)),
                   pl.BlockSpec((None, None, 1, hg), lambda d, g, c: (d, g, 0, 0))],
        out_shape=[jax.ShapeDtypeStruct((2, t_dim, di), F32), jax.ShapeDtypeStruct((2, t_dim, gn), F32),
                   jax.ShapeDtypeStruct((2, t_dim, gn), F32), jax.ShapeDtypeStruct((2, g_cnt, t_dim, hg), F32),
                   jax.ShapeDtypeStruct((2, g_cnt, 1, hg), F32)],
        scratch_shapes=[pltpu.VMEM((ns, GROUP_W), F32), pltpu.VMEM((q, GROUP_W), F32), pltpu.VMEM((hg, q), F32)],
        compiler_params=_cp("parallel", "parallel", "arbitrary"),
    )(xbc, xbc, xbc, dt4, al4, hin_all, dy, dvec)


def _gate_norm(y2, xbc, proj, dvec, ng):
    t_dim, di = y2.shape[1:]
    tr = _tile(t_dim, 128, 8)

    def body(y2_ref, x_ref, z_ref, dv_ref, ng_ref, o_ref):
        y = y2_ref[0] + y2_ref[1] + x_ref[...] * dv_ref[...]
        z = z_ref[...]
        v = y * z * _sigmoid(z)
        r = lax.rsqrt(jnp.mean(v * v, axis=-1, keepdims=True) + EPS)
        o_ref[...] = (v * r * ng_ref[...]).astype(BF16)

    row = pl.BlockSpec((tr, di), lambda i: (i, 0))
    vec = pl.BlockSpec((1, di), lambda i: (0, 0))
    return pl.pallas_call(
        body, name="gate_norm", grid=(t_dim // tr,),
        in_specs=[pl.BlockSpec((2, tr, di), lambda i: (0, i, 0)), row, row, vec, vec],
        out_specs=row, out_shape=jax.ShapeDtypeStruct((t_dim, di), BF16),
        compiler_params=_cp("parallel"),
    )(y2, xbc, proj, dvec, ng)


def _gate_norm_bwd(y2, xbc, proj, dvec, ng, dyn):
    t_dim, di = y2.shape[1:]
    tr = _tile(t_dim, 128, 8)

    def body(y2_ref, x_ref, z_ref, dv_ref, ng_ref, dyn_ref, dy_ref, dz_ref, dng_ref, dd_ref):
        i = pl.program_id(0)
        xv = x_ref[...]
        y = y2_ref[0] + y2_ref[1] + xv * dv_ref[...]
        z = z_ref[...]
        s = _sigmoid(z)
        v = y * z * s
        r = lax.rsqrt(jnp.mean(v * v, axis=-1, keepdims=True) + EPS)
        n = v * r
        dynv = dyn_ref[...]
        dn = dynv * ng_ref[...]
        dv = r * (dn - n * jnp.mean(dn * n, axis=-1, keepdims=True))
        dy = dv * z * s

        @pl.when(i == 0)
        def _():
            dng_ref[...] = jnp.zeros_like(dng_ref)
            dd_ref[...] = jnp.zeros_like(dd_ref)

        dng_ref[...] += jnp.sum(dynv * n, axis=0, keepdims=True)
        dd_ref[...] += jnp.sum(dy * xv, axis=0, keepdims=True)
        dy_ref[...] = dy
        dz_ref[...] = (dv * y * (s * (1.0 + z * (1.0 - s)))).astype(BF16)

    row = pl.BlockSpec((tr, di), lambda i: (i, 0))
    vec = pl.BlockSpec((1, di), lambda i: (0, 0))
    return pl.pallas_call(
        body, name="gate_norm_bwd", grid=(t_dim // tr,),
        in_specs=[pl.BlockSpec((2, tr, di), lambda i: (0, i, 0)), row, row, vec, vec, row],
        out_specs=[row, row, vec, vec],
        out_shape=[jax.ShapeDtypeStruct((t_dim, di), F32), jax.ShapeDtypeStruct((t_dim, di), BF16),
                   jax.ShapeDtypeStruct((1, di), F32), jax.ShapeDtypeStruct((1, di), F32)],
        compiler_params=_cp("arbitrary"),
    )(y2, xbc, proj, dvec, ng, dyn)


def _dt_to_groups(dt):
    t_dim, h2 = dt.shape
    g_cnt = h2 // 2 // HEADS_PER_GROUP
    return dt.reshape(t_dim, 2, g_cnt, HEADS_PER_GROUP).transpose(1, 2, 0, 3)


def _dt_from_groups(dt4):
    _, g_cnt, t_dim, hg = dt4.shape
    return dt4.transpose(2, 0, 1, 3).reshape(t_dim, 2 * g_cnt * hg)


def _ssd_mixer_fwd(x, norm_g, w_in, conv_w, conv_b, dt_bias, a_log, d_skip, ssd_norm, w_out, j):
    heads = d_skip.shape[0]
    di = heads * SSD_HEAD_DIM
    g_cnt = heads // HEADS_PER_GROUP
    cd = conv_w.shape[-1]
    hn = _rmsnorm(x, norm_g)
    proj = _mm(hn, w_in, name="ssd_proj", tb=True, b_sel=(j,), tn=1152)
    xbc = _conv_silu(proj, conv_w, conv_b, di)
    dt_raw = proj[:, di + cd:]
    dt = _softplus_fwd(dt_raw, dt_bias)
    dt4 = _dt_to_groups(dt)
    al4 = a_log.reshape(2, g_cnt, 1, HEADS_PER_GROUP)
    y2, hin = _ssd_fwd(xbc, dt4, al4)
    dvec = jnp.repeat(d_skip, SSD_HEAD_DIM).reshape(1, di)
    yn = _gate_norm(y2, xbc, proj, dvec, ssd_norm)
    x_new = _mm(yn, w_out, name="ssd_out", b_sel=(j,), add=x)
    return x_new, (x, hn, proj, xbc, dt_raw, dt4, al4, y2, hin, dvec, yn)


def _ssd_mixer_bwd(dx, saved, norm_g, w_in, conv_w, conv_b, dt_bias, a_log, ssd_norm, w_out, j):
    x, hn, proj, xbc, dt_raw, dt4, al4, y2, hin, dvec, yn = saved
    di = dvec.shape[1]
    heads = di // SSD_HEAD_DIM
    d_wout = _mm(yn, dx, name="ssd_dwout", ta=True, out_dtype=BF16)
    dyn = _mm(dx, w_out, name="ssd_dyn", tb=True, b_sel=(j,))
    dy, dz, d_ng, dd_col = _gate_norm_bwd(y2, xbc, proj, dvec, ssd_norm, dyn)
    dx2, db2, dc2, ddt4, da4 = _ssd_bwd(xbc, dt4, al4, hin, dy, dvec)
    gn = db2.shape[-1]
    conv_parts = [_conv_silu_bwd(proj, conv_w, conv_b, part, di, ch0)
                  for part, ch0 in ((dx2, 0), (db2, di), (dc2, di + gn))]
    dwb = jnp.concatenate([p[1] for p in conv_parts], axis=-1)
    ddt_raw, d_bias, d_alog = _softplus_bwd(dt_raw, dt_bias, _dt_from_groups(ddt4), da4.reshape(1, 2 * heads), a_log)
    dproj = jnp.concatenate([dz] + [p[0] for p in conv_parts] + [ddt_raw], axis=-1)
    d_win = _mm(dproj, hn, name="ssd_dwin", ta=True, out_dtype=BF16, tm=1152)
    dhn = _mm(dproj, w_in, name="ssd_dhn", b_sel=(j,), tk=1152)
    dx_new, d_norm = _rmsnorm_bwd(x, norm_g, dhn, dx)
    small = dict(mix_norm=d_norm, conv_w=dwb[:CONV_W], conv_b=dwb[CONV_W:CONV_W + 1], dt_bias=d_bias, a_log=d_alog,
                 ssd_d=dd_col.reshape(heads, SSD_HEAD_DIM).sum(axis=1), ssd_norm=d_ng)
    return dx_new, small, d_win, d_wout


def _pool_count(t_idx, w, t_dim):
    hi = jnp.minimum(t_idx + w // 2, t_dim)
    lo = jnp.maximum(t_idx - w // 2, 0)
    return (hi - lo).astype(F32)


def _pool_mix(u, transpose):
    t_dim, d = u.shape
    gd = d // len(POOL_WINDOWS)
    cb = _tile(gd, 256)
    per = gd // cb

    def body(u_ref, o_ref):
        gi = pl.program_id(0)
        t_idx = lax.broadcasted_iota(jnp.int32, (t_dim, cb), 0)
        uv = u_ref[...]
        for widx, w in enumerate(POOL_WINDOWS):
            @pl.when(gi == widx)
            def _(w=w):
                cnt = _pool_count(t_idx, w, t_dim)
                src = uv / cnt if transpose else uv
                acc = jnp.zeros_like(uv)
                for k in range(-(w // 2), w // 2):
                    acc = acc + _shifted(src, -k if transpose else k, t_idx)
                res = acc - uv if transpose else acc / cnt - uv
                o_ref[...] = res.astype(BF16)

    spec = pl.BlockSpec((t_dim, cb), lambda gi, j: (0, gi * per + j))
    return pl.pallas_call(
        body, name="pool_mix_t" if transpose else "pool_mix", grid=(len(POOL_WINDOWS), per),
        in_specs=[spec], out_specs=spec, out_shape=jax.ShapeDtypeStruct((t_dim, d), BF16),
        compiler_params=_cp("parallel", "parallel"),
    )(u)


def _pool_group(mix, wgrp, scale, j):
    t_dim, d = mix.shape
    gd = wgrp.shape[-1]
    tm = _tile(t_dim, 512)

    def body(m_ref, w_ref, s_ref, v_ref, vs_ref):
        v = _dot(m_ref[...], w_ref[...])
        v_ref[...] = v
        vs_ref[...] = (v * s_ref[...]).astype(BF16)

    tile = pl.BlockSpec((tm, gd), lambda gi, m: (m, gi))
    return pl.pallas_call(
        body, name="pool_group", grid=(d // gd, t_dim // tm),
        in_specs=[tile, pl.BlockSpec((None, None, gd, gd), lambda gi, m: (j, gi, 0, 0)),
                  pl.BlockSpec((1, gd), lambda gi, m: (0, gi))],
        out_specs=[tile, tile],
        out_shape=[jax.ShapeDtypeStruct((t_dim, d), F32), jax.ShapeDtypeStruct((t_dim, d), BF16)],
        compiler_params=_cp("parallel", "parallel"),
    )(mix, wgrp, scale)


def _pool_group_bwd(dvs, v, mix, wgrp, scale, j):
    t_dim, d = mix.shape
    gd = wgrp.shape[-1]
    n_g = d // gd
    tm = _tile(t_dim, 512)
    nm = t_dim // tm

    def body(dvs_ref, v_ref, m_ref, w_ref, s_ref, dmix_ref, ds_ref, dw_ref, acc_ref):
        m = pl.program_id(1)
        dvsv = dvs_ref[...]

        @pl.when(m == 0)
        def _():
            ds_ref[...] = jnp.zeros_like(ds_ref)
            acc_ref[...] = jnp.zeros_like(acc_ref)

        ds_ref[...] += jnp.sum(dvsv * v_ref[...], axis=0, keepdims=True)
        dv16 = (dvsv * s_ref[...]).astype(BF16)
        dmix_ref[...] = _dot_nt(dv16, w_ref[...])
        acc_ref[...] += _dot(m_ref[...].T, dv16)

        @pl.when(m == nm - 1)
        def _():
            dw_ref[...] = acc_ref[...].astype(BF16)

    tile = pl.BlockSpec((tm, gd), lambda gi, m: (m, gi))
    vec = pl.BlockSpec((1, gd), lambda gi, m: (0, gi))
    return pl.pallas_call(
        body, name="pool_group_bwd", grid=(n_g, nm),
        in_specs=[tile, tile, tile, pl.BlockSpec((None, None, gd, gd), lambda gi, m: (j, gi, 0, 0)), vec],
        out_specs=[tile, vec, pl.BlockSpec((None, gd, gd), lambda gi, m: (gi, 0, 0))],
        out_shape=[jax.ShapeDtypeStruct((t_dim, d), F32), jax.ShapeDtypeStruct((1, d), F32),
                   jax.ShapeDtypeStruct((n_g, gd, gd), BF16)],
        scratch_shapes=[pltpu.VMEM((gd, gd), F32)],
        compiler_params=_cp("parallel", "arbitrary"),
    )(dvs, v, mix, wgrp, scale)


def _pool_mixer_fwd(x, norm_g, w_in, wgrp, scale, w_out, j):
    hn = _rmsnorm(x, norm_g)
    u = _mm(hn, w_in, name="pool_u", b_sel=(j,))
    mix = _pool_mix(u, False)
    v, vs = _pool_group(mix, wgrp, scale, j)
    x_new = _mm(vs, w_out, name="pool_out", b_sel=(j,), add=x)
    return x_new, (x, hn, mix, v, vs)


def _pool_mixer_bwd(dx, saved, norm_g, w_in, wgrp, scale, w_out, j):
    x, hn, mix, v, vs = saved
    d_wout = _mm(vs, dx, name="pool_dw", ta=True, out_dtype=BF16)
    dvs = _mm(dx, w_out, name="pool_dvs", tb=True, b_sel=(j,))
    dmix, d_scale, d_wgrp = _pool_group_bwd(dvs, v, mix, wgrp, scale, j)
    du = _pool_mix(dmix, True)
    d_win = _mm(hn, du, name="pool_dw", ta=True, out_dtype=BF16)
    dhn = _mm(du, w_in, name="pool_dhn", tb=True, b_sel=(j,))
    dx_new, d_norm = _rmsnorm_bwd(x, norm_g, dhn, dx)
    return dx_new, d_norm, d_scale, d_win, d_wgrp, d_wout


def _local_step(x, target, depth, fetch, emit, on_loss, small, tok):
    saved, wts = [], {}

    def weights(group, after):
        if group not in wts:
            wts[group] = fetch(group, after)
        return wts[group]

    for i in range(depth):
        j = i // 2
        x, s0 = _ffn_fwd(x, small["ffn_norm"][i, 0][None] + tok, *weights(("ffn", i, 0), x), 0)
        mg = small["mix_norm"][i][None]
        if i % 2 == 0:
            heads = small["ssd_d"].shape[1]
            w_in, w_out = weights(("ssd", j), x)
            x, s1 = _ssd_mixer_fwd(x, mg, w_in, small["ssd_conv_w"][j], small["ssd_conv_b"][j][None],
                                   small["ssd_dt_bias"][j].reshape(1, 2 * heads), small["ssd_a_log"][j].reshape(1, 2 * heads),
                                   small["ssd_d"][j], small["ssd_norm"][j][None], w_out, 0)
        else:
            p_in, p_grp, p_out = weights(("pool", j), x)
            x, s1 = _pool_mixer_fwd(x, mg, p_in, p_grp, small["pool_scale"][j][None], p_out, 0)
        x, s2 = _ffn_fwd(x, small["ffn_norm"][i, 1][None], *weights(("ffn", i, 1), x), 0)
        saved.append((s0, s1, s2))
    loss, dx, d_final = _loss_head(x, small["final_norm"][None], target)

    gs = {k: {} for k in ("ffn_norm", "mix_norm", "ssd_conv_w", "ssd_conv_b", "ssd_dt_bias", "ssd_a_log", "ssd_d",
                          "ssd_norm", "pool_scale")}
    tok = on_loss(loss)
    for i in reversed(range(depth)):
        j = i // 2
        s0, s1, s2 = saved[i]
        for half, sv in ((1, s2), (0, None)):
            if half == 0:
                sv = s0
                mg = small["mix_norm"][i][None] + tok
                if i % 2 == 0:
                    heads = small["ssd_d"].shape[1]
                    w_in, w_out = wts[("ssd", j)]
                    dx, sm, d_win, d_wout = _ssd_mixer_bwd(
                        dx, s1, mg, w_in, small["ssd_conv_w"][j], small["ssd_conv_b"][j][None],
                        small["ssd_dt_bias"][j].reshape(1, 2 * heads), small["ssd_a_log"][j].reshape(1, 2 * heads),
                        small["ssd_norm"][j][None], w_out, 0)
                    gs["mix_norm"][i] = sm["mix_norm"][0]
                    gs["ssd_conv_w"][j] = sm["conv_w"]
                    gs["ssd_conv_b"][j] = sm["conv_b"][0]
                    gs["ssd_dt_bias"][j] = sm["dt_bias"].reshape(2, heads)
                    gs["ssd_a_log"][j] = sm["a_log"].reshape(2, heads)
                    gs["ssd_d"][j] = sm["ssd_d"]
                    gs["ssd_norm"][j] = sm["ssd_norm"][0]
                    tok = tok + emit(("ssd", j), [d_win, d_wout])
                else:
                    p_in, p_grp, p_out = wts[("pool", j)]
                    dx, d_norm, d_scale, d_win, d_wgrp, d_wout = _pool_mixer_bwd(
                        dx, s1, mg, p_in, p_grp, small["pool_scale"][j][None], p_out, 0)
                    gs["mix_norm"][i] = d_norm[0]
                    gs["pool_scale"][j] = d_scale[0]
                    tok = tok + emit(("pool", j), [d_win, d_wgrp, d_wout])
            dx, d_norm, d_wg, d_wu, d_wd = _ffn_bwd(dx, sv, small["ffn_norm"][i, half][None] + tok,
                                                   *wts[("ffn", i, half)], 0)
            gs["ffn_norm"][(i, half)] = d_norm[0]
            tok = tok + emit(("ffn", i, half), [d_wg, d_wu, d_wd])
    n_s, n_p = (depth + 1) // 2, depth // 2
    gsmall = dict(
        ffn_norm=jnp.stack([jnp.stack([gs["ffn_norm"][(i, h)] for h in range(2)]) for i in range(depth)]),
        mix_norm=jnp.stack([gs["mix_norm"][i] for i in range(depth)]),
        ssd_conv_w=jnp.stack([gs["ssd_conv_w"][j] for j in range(n_s)]),
        ssd_conv_b=jnp.stack([gs["ssd_conv_b"][j] for j in range(n_s)]),
        ssd_dt_bias=jnp.stack([gs["ssd_dt_bias"][j] for j in range(n_s)]),
        ssd_a_log=jnp.stack([gs["ssd_a_log"][j] for j in range(n_s)]),
        ssd_d=jnp.stack([gs["ssd_d"][j] for j in range(n_s)]),
        ssd_norm=jnp.stack([gs["ssd_norm"][j] for j in range(n_s)]),
        pool_scale=jnp.stack([gs["pool_scale"][j] for j in range(n_p)]),
        final_norm=d_final[0],
    )
    return loss, dx, gsmall, tok


ANY = pl.BlockSpec(memory_space=pl.ANY)


def _mesh_pos():
    return lax.axis_index("x"), lax.axis_index("y"), lax.axis_index("c")


def _other_chips(x, y):
    return [(1 - x, y), (x, 1 - y), (1 - x, 1 - y)]


def _win(ref, windows, lead=()):
    rest = len(ref.shape) - len(lead)
    idx = tuple(lead) + tuple(pl.ds(*windows[ax]) if ax in windows else slice(None) for ax in range(rest))
    return ref.at[idx]


def _remote(src, dst, send_sems, recv_sems, k, peer):
    return pltpu.make_async_remote_copy(src_ref=src, dst_ref=dst, send_sem=send_sems.at[k], recv_sem=recv_sems.at[k],
                                        device_id=peer, device_id_type=MESH)


def _cast_place(w3, chip1, mode, l0, nl, out_dtype=BF16):
    _, r, c = w3.shape
    tr = _tile(r, 512, 16)
    nr = r // tr
    if mode == "cols":
        out_shape, blk = (nl, r, N_CHIPS * c), (None, tr, c)
        omap = lambda l, i, s: (l, i, s[0])
    elif mode == "rows":
        out_shape, blk = (nl, N_CHIPS * r, c), (None, tr, c)
        omap = lambda l, i, s: (l, s[0] * nr + i, 0)
    else:
        out_shape, blk = (nl, N_CHIPS, r, c), (None, None, tr, c)
        omap = lambda l, i, s: (l, s[0], i, 0)

    def body(s_ref, w_ref, o_ref):
        o_ref[...] = w_ref[...].astype(out_dtype)

    return pl.pallas_call(
        body, name="cast_place_" + mode, out_shape=jax.ShapeDtypeStruct(out_shape, out_dtype),
        grid_spec=pltpu.PrefetchScalarGridSpec(
            num_scalar_prefetch=1, grid=(nl, nr),
            in_specs=[pl.BlockSpec((None, tr, c), lambda l, i, s: (l0 + l, i, 0))],
            out_specs=pl.BlockSpec(blk, omap)),
        compiler_params=_cp("parallel", "parallel"),
    )(chip1, w3)


def _gather_begin(name, fulls, axes):
    n = len(fulls)
    shapes = [f.shape for f in fulls]

    def full_win(refs, t, sidx, hidx):
        sa, ha = axes[t]
        ssz, hsz = shapes[t][sa] // N_CHIPS, shapes[t][ha] // 2
        return _win(refs[t], {sa: (sidx * ssz, ssz), ha: (hidx * hsz, hsz)})

    def chips_plan(refs, x, y, c):
        mine = 2 * x + y
        return [(full_win(refs, t, mine, c), full_win(refs, t, mine, c), (px, py, c))
                for t in range(n) for px, py in _other_chips(x, y)]

    def sibling_plan(in_refs, out_refs, x, y, c):
        wins = [full_win(out_refs, t, 2 * px + py, c) for t in range(n) for px, py in _other_chips(x, y)]
        return [(w, w, (x, y, 1 - c)) for w in wins]

    sems, thru, token = _split_start(name + "_start", fulls, chips_plan, 3 * n)
    return (name, sems, thru, chips_plan, sibling_plan, 3 * n), token


def _gather_finish(pending, after):
    name, sems, thru, chips_plan, sibling_plan, n_copies = pending
    landed = _split_wait(name + "_wait", sems, thru, chips_plan, n_copies, after)
    return _exchange(name + "_sibling", landed, [jax.ShapeDtypeStruct(f.shape, f.dtype) for f in landed],
                     sibling_plan, n_copies, inplace=True)


def _exchange(name, inputs, out_shapes, plan, n_copies, inplace=False):
    n_in, n_out = len(inputs), len(out_shapes)

    def body(*refs):
        in_refs, out_refs = refs[:n_in], refs[n_in:n_in + n_out]
        send_sems, recv_sems = refs[n_in + n_out:]
        x, y, c = _mesh_pos()
        copies = plan(in_refs, out_refs, x, y, c)
        assert len(copies) == n_copies
        started = []
        for k, (src, dst, peer) in enumerate(copies):
            cp = _remote(src, dst, send_sems, recv_sems, k, peer)
            cp.start()
            started.append(cp)
        for cp in started:
            cp.wait()

    return pl.pallas_call(
        body, name=name, in_specs=[ANY] * n_in, out_specs=[ANY] * n_out, out_shape=out_shapes,
        input_output_aliases={t: t for t in range(n_in)} if inplace else {},
        scratch_shapes=[pltpu.SemaphoreType.DMA((n_copies,)), pltpu.SemaphoreType.DMA((n_copies,))],
    )(*inputs)


HBM = pl.BlockSpec(memory_space=pltpu.HBM)
SEM = pl.BlockSpec(memory_space=pltpu.SEMAPHORE)
DATAFLOW = pltpu.SideEffectType.DATAFLOW_SIDE_EFFECTING


def _split_start(name, bufs, plan, n_copies):
    n = len(bufs)

    def body(*refs):
        ins = refs[:n]
        send_sems, recv_sems = refs[n], refs[n + 1]
        token = refs[2 * n + 2]
        x, y, c = _mesh_pos()
        copies = plan(ins, x, y, c)
        assert len(copies) == n_copies
        for k, (src, dst, peer) in enumerate(copies):
            _remote(src, dst, send_sems, recv_sems, k, peer).start()
        token[...] = jnp.zeros_like(token)

    outs = pl.pallas_call(
        body, name=name,
        out_shape=(pltpu.SemaphoreType.DMA((n_copies,)), pltpu.SemaphoreType.DMA((n_copies,)),
                   *[pltpu.HBM(b.shape, b.dtype) for b in bufs], jax.ShapeDtypeStruct((8, LANE), F32)),
        in_specs=[HBM] * n, out_specs=(SEM, SEM, *[HBM] * n, pl.BlockSpec(memory_space=pltpu.VMEM)),
        input_output_aliases={t: 2 + t for t in range(n)},
        compiler_params=pltpu.CompilerParams(has_side_effects=DATAFLOW),
    )(*[pltpu.with_memory_space_constraint(b, pltpu.HBM) for b in bufs])
    return (outs[0], outs[1]), list(outs[2:2 + n]), outs[2 + n][0, 0]


def _split_wait(name, sems, bufs, plan, n_copies, after):
    n = len(bufs)

    def body(*refs):
        ins = refs[:n]
        send_sems, recv_sems = refs[n], refs[n + 1]
        x, y, c = _mesh_pos()
        copies = plan(ins, x, y, c)
        assert len(copies) == n_copies
        for k, (src, dst, peer) in enumerate(copies):
            cp = _remote(src, dst, send_sems, recv_sems, k, peer)
            cp.wait_send()
            cp.wait_recv()

    outs = pl.pallas_call(
        body, name=name, out_shape=tuple(pltpu.HBM(b.shape, b.dtype) for b in bufs),
        in_specs=[HBM] * n + [SEM, SEM, ANY], out_specs=tuple([HBM] * n),
        input_output_aliases={t: t for t in range(n)},
        compiler_params=pltpu.CompilerParams(has_side_effects=DATAFLOW),
    )(*bufs, sems[0], sems[1], after)
    return list(outs)


def _halved(shape, ha):
    out = list(shape)
    out[ha] //= 2
    return tuple(out)


def _sharded(shape, sa):
    out = list(shape)
    out[sa] //= N_CHIPS
    return tuple(out)


def _rs_cores_begin(name, grads, axes):
    n = len(grads)
    shapes = [g.shape for g in grads]
    landing = [lax.empty(_halved(g.shape, ha), g.dtype) for g, (_, ha) in zip(grads, axes)]

    def plan(refs, x, y, c):
        copies = []
        for t in range(n):
            ha = axes[t][1]
            hsz = shapes[t][ha] // 2
            copies.append((_win(refs[t], {ha: ((1 - c) * hsz, hsz)}), refs[n + t], (x, y, 1 - c)))
        return copies

    sems, thru, token = _split_start(name + "_start", list(grads) + landing, plan, n)
    return (name, sems, thru, plan, n), token


def _rs_cores_finish(pending, after):
    name, sems, thru, plan, n = pending
    done = _split_wait(name + "_wait", sems, thru, plan, n, after)
    return done[:n], done[n:]


def _small_begin(vec):
    landing = lax.empty((8,) + vec.shape, vec.dtype)

    def plan(refs, x, y, c):
        me = 4 * x + 2 * y + c
        copies = []
        for k in range(1, 8):
            kx, ky, kc = k // 4, (k // 2) % 2, k % 2
            copies.append((refs[0], refs[1].at[me], (x ^ kx, y ^ ky, c ^ kc)))
        return copies

    sems, thru, token = _split_start("small_start", [vec, landing], plan, 7)
    return (sems, thru, plan), token


def _small_finish(pending, after, me1):
    sems, thru, plan = pending
    vec, landing = _split_wait("small_wait", sems, thru, plan, 7, after)

    def body(s_ref, v_ref, l_ref, o_ref):
        tot = jnp.where(s_ref[0] == 0, v_ref[...], l_ref[0])
        for i in range(1, 8):
            tot = tot + jnp.where(s_ref[0] == i, v_ref[...], l_ref[i])
        o_ref[...] = tot

    rows = vec.shape[0]
    return pl.pallas_call(
        body, name="small_sum", out_shape=jax.ShapeDtypeStruct(vec.shape, F32),
        grid_spec=pltpu.PrefetchScalarGridSpec(
            num_scalar_prefetch=1, grid=(1,),
            in_specs=[pl.BlockSpec((rows, LANE), lambda i, s: (0, 0)), pl.BlockSpec((8, rows, LANE), lambda i, s: (0, 0, 0))],
            out_specs=pl.BlockSpec((rows, LANE), lambda i, s: (0, 0))),
    )(me1, vec, landing)


def _rs_chips_begin(name, halves, axes):
    n = len(halves)
    shapes = [h.shape for h in halves]
    landing = [lax.empty((N_CHIPS - 1,) + _sharded(h.shape, sa), h.dtype) for h, (sa, _) in zip(halves, axes)]

    def plan(refs, x, y, c):
        copies = []
        for t in range(n):
            sa = axes[t][0]
            ssz = shapes[t][sa] // N_CHIPS
            for j, (px, py) in enumerate(_other_chips(x, y)):
                copies.append((_win(refs[t], {sa: ((2 * px + py) * ssz, ssz)}), refs[n + t].at[j], (px, py, c)))
        return copies

    sems, thru, token = _split_start(name + "_start", list(halves) + landing, plan, 3 * n)
    return (name, sems, thru, plan, 3 * n), token


def _rs_chips_finish(pending, after):
    name, sems, thru, plan, n_copies = pending
    done = _split_wait(name + "_wait", sems, thru, plan, n_copies, after)
    n = len(done) // 2
    return done[:n], done[n:]


def _rs_finish_begin(name, sums):
    n = len(sums)
    landing = lax.empty((n,) + sums[0].shape, sums[0].dtype)

    def plan(refs, x, y, c):
        return [(refs[t], refs[n].at[t], (x, y, 1 - c)) for t in range(n)]

    sems, thru, token = _split_start(name + "_start", list(sums) + [landing], plan, n)
    return (name, sems, thru, plan, n), token


def _rs_finish_end(pending, after):
    name, sems, thru, plan, n = pending
    done = _split_wait(name + "_wait", sems, thru, plan, n, after)
    return done[:n], done[n]


def _rows2d(a):
    return a.reshape(-1, a.shape[-1])


def _view3(a):
    return a.reshape((-1,) + a.shape[-2:])


def _add_pair(g, recv, ha, core1):
    g3, r3 = _view3(g), _view3(recv)
    rows_half = ha + 3 - g.ndim == 1
    n_l, r, c = r3.shape
    tr = _tile(r, 512, 16)
    nr = r // tr
    gmap = (lambda l, i, s: (l, s[0] * nr + i, 0)) if rows_half else (lambda l, i, s: (l, i, s[0]))

    def body(s_ref, g_ref, r_ref, o_ref):
        o_ref[...] = (g_ref[...].astype(F32) + r_ref[...].astype(F32)).astype(BF16)

    spec = pl.BlockSpec((None, tr, c), lambda l, i, s: (l, i, 0))
    out = pl.pallas_call(
        body, name="add_pair", out_shape=jax.ShapeDtypeStruct(r3.shape, BF16),
        grid_spec=pltpu.PrefetchScalarGridSpec(num_scalar_prefetch=1, grid=(n_l, nr),
                                               in_specs=[pl.BlockSpec((None, tr, c), gmap), spec], out_specs=spec),
        compiler_params=_cp("parallel", "parallel"))(core1, g3, r3)
    return out.reshape(recv.shape)


def _add_four(cs, recv, sa, chip1):
    c3 = _view3(cs)
    s3 = sa + 3 - cs.ndim
    lo, ro, co = (dim // N_CHIPS if ax == s3 else dim for ax, dim in enumerate(c3.shape))
    r4 = recv.reshape(N_CHIPS - 1, lo, ro, co)
    tr = _tile(ro, 512, 16)
    nr = ro // tr
    if s3 == 0:
        cmap = lambda l, i, s: (s[0] * lo + l, i, 0)
    elif s3 == 1:
        cmap = lambda l, i, s: (l, s[0] * nr + i, 0)
    else:
        cmap = lambda l, i, s: (l, i, s[0])

    def body(s_ref, c_ref, r_ref, out_ref):
        out_ref[...] = ((c_ref[...].astype(F32) + r_ref[0].astype(F32)) + r_ref[1].astype(F32)) + r_ref[2].astype(F32)

    out = pl.pallas_call(
        body, name="add_four", out_shape=jax.ShapeDtypeStruct((lo, ro, co), F32),
        grid_spec=pltpu.PrefetchScalarGridSpec(
            num_scalar_prefetch=1, grid=(lo, nr),
            in_specs=[pl.BlockSpec((None, tr, co), cmap),
                      pl.BlockSpec((N_CHIPS - 1, None, tr, co), lambda l, i, s: (0, l, i, 0))],
            out_specs=pl.BlockSpec((None, tr, co), lambda l, i, s: (l, i, 0))),
        compiler_params=_cp("parallel", "parallel"))(chip1, c3, r4)
    return out.reshape(recv.shape[1:])


def _adamw_halves(w, own, recv, m, v, rows_half, core1, l0=0, prev=None):
    w3, m3, v3, o3, r3 = (_view3(a) for a in (w, m, v, own, recv))
    _, _, c = w3.shape
    n_l, rh, ch = o3.shape
    tr = _tile(rh, 128, 8)
    nr = rh // tr
    c1 = 1.0 - ADAM_B1 ** ADAM_STEP
    c2 = 1.0 - ADAM_B2 ** ADAM_STEP

    n_prev = 0 if prev is None else 4

    def body(s_ref, w_ref, o_ref, r_ref, m_ref, v_ref, *rest):
        g_ref, d_ref, nm_ref, nv_ref = rest[n_prev:]
        gv = jnp.where(pl.program_id(1) == s_ref[0], o_ref[...], r_ref[...])
        nm = ADAM_B1 * m_ref[...] + (1.0 - ADAM_B1) * gv
        nv = ADAM_B2 * v_ref[...] + (1.0 - ADAM_B2) * (gv * gv)
        g_ref[...] = gv
        nm_ref[...] = nm
        nv_ref[...] = nv
        d_ref[...] = -ADAM_LR * ((nm / c1) / (jnp.sqrt(nv / c2) + ADAM_EPS) + ADAM_WD * w_ref[...])

    wmap = (lambda l, h, i, s: (l0 + l, h * nr + i, 0)) if rows_half else (lambda l, h, i, s: (l0 + l, i, h))
    wspec = pl.BlockSpec((None, tr, ch), wmap)
    ospec = pl.BlockSpec((None, tr, ch), lambda l, h, i, s: (jnp.where(h == s[0], l, 0), jnp.where(h == s[0], i, 0), 0))
    rspec = pl.BlockSpec((None, tr, ch), lambda l, h, i, s: (jnp.where(h == s[0], 0, l), jnp.where(h == s[0], 0, i), 0))
    osh = jax.ShapeDtypeStruct(w3.shape, F32)
    before = [] if prev is None else [_view3(p) for p in prev]
    outs = pl.pallas_call(
        body, name="adamw_halves", out_shape=[osh] * 4,
        grid_spec=pltpu.PrefetchScalarGridSpec(
            num_scalar_prefetch=1, grid=(n_l, 2, nr),
            in_specs=[wspec, ospec, rspec, wspec, wspec] + [ANY] * n_prev, out_specs=[wspec] * 4),
        input_output_aliases={6 + k: k for k in range(n_prev)},
        compiler_params=_cp("parallel", "parallel", "parallel"))(core1, w3, o3, r3, m3, v3, *before)
    return tuple(o.reshape(w.shape) for o in outs)


def _adamw(w, g, m, v):
    shape = w.shape
    w2, g2, m2, v2 = (_rows2d(a) if a.ndim > 1 else a.reshape(1, -1) for a in (w, g, m, v))
    rows, cols = w2.shape
    tr = _tile(rows, 256, 8)
    c1 = 1.0 - ADAM_B1 ** ADAM_STEP
    c2 = 1.0 - ADAM_B2 ** ADAM_STEP

    def body(w_ref, g_ref, m_ref, v_ref, d_ref, nm_ref, nv_ref):
        gv = g_ref[...]
        nm = ADAM_B1 * m_ref[...] + (1.0 - ADAM_B1) * gv
        nv = ADAM_B2 * v_ref[...] + (1.0 - ADAM_B2) * (gv * gv)
        nm_ref[...] = nm
        nv_ref[...] = nv
        d_ref[...] = -ADAM_LR * ((nm / c1) / (jnp.sqrt(nv / c2) + ADAM_EPS) + ADAM_WD * w_ref[...])

    spec = pl.BlockSpec((tr, cols), lambda i: (i, 0))
    osh = jax.ShapeDtypeStruct((rows, cols), F32)
    outs = pl.pallas_call(body, name="adamw", grid=(rows // tr,), in_specs=[spec] * 4, out_specs=[spec] * 3,
                          out_shape=[osh] * 3, compiler_params=_cp("parallel"))(w2, g2, m2, v2)
    return tuple(o.reshape(shape) for o in outs)


def _pack(arrs):
    flat = jnp.concatenate([a.reshape(-1) for a in arrs])
    n = flat.shape[0]
    rows = -(-n // (8 * LANE)) * 8
    return jnp.pad(flat, (0, rows * LANE - n)).reshape(rows, LANE)


def _unpack(packed, shapes):
    flat = packed.reshape(-1)
    out, pos = [], 0
    for sh in shapes:
        size = 1
        for dsz in sh:
            size *= dsz
        out.append(flat[pos:pos + size].reshape(sh))
        pos += size
    return out


BIG = ("ffn_w_gate", "ffn_w_up", "ffn_w_down", "ssd_w_in", "ssd_w_out", "pool_w_in", "pool_w_group", "pool_w_out")
WEIGHTS = ("ffn_norm", "ffn_w_gate", "ffn_w_up", "ffn_w_down", "mix_norm", "ssd_w_in", "ssd_conv_w", "ssd_conv_b",
           "ssd_dt_bias", "ssd_a_log", "ssd_d", "ssd_norm", "ssd_w_out", "pool_w_in", "pool_w_group", "pool_scale",
           "pool_w_out", "final_norm")
SMALL = tuple(k for k in WEIGHTS if k not in BIG)
SMALL_SHARDED = {"ffn_norm": 2, "ssd_conv_w": 2, "pool_scale": 1}


def kernel(x, ffn_norm, ffn_w_gate, ffn_w_up, ffn_w_down, mix_norm, ssd_w_in, ssd_conv_w, ssd_conv_b, ssd_dt_bias, ssd_a_log, ssd_d, ssd_norm, ssd_w_out, pool_w_in, pool_w_group, pool_scale, pool_w_out, final_norm, loss_target, m_ffn_norm, m_ffn_w_gate, m_ffn_w_up, m_ffn_w_down, m_mix_norm, m_ssd_w_in, m_ssd_conv_w, m_ssd_conv_b, m_ssd_dt_bias, m_ssd_a_log, m_ssd_d, m_ssd_norm, m_ssd_w_out, m_pool_w_in, m_pool_w_group, m_pool_scale, m_pool_w_out, m_final_norm, v_ffn_norm, v_ffn_w_gate, v_ffn_w_up, v_ffn_w_down, v_mix_norm, v_ssd_w_in, v_ssd_conv_w, v_ssd_conv_b, v_ssd_dt_bias, v_ssd_a_log, v_ssd_d, v_ssd_norm, v_ssd_w_out, v_pool_w_in, v_pool_w_group, v_pool_scale, v_pool_w_out, v_final_norm):
    w = dict(ffn_norm=ffn_norm, ffn_w_gate=ffn_w_gate, ffn_w_up=ffn_w_up, ffn_w_down=ffn_w_down, mix_norm=mix_norm,
             ssd_w_in=ssd_w_in, ssd_conv_w=ssd_conv_w, ssd_conv_b=ssd_conv_b, ssd_dt_bias=ssd_dt_bias,
             ssd_a_log=ssd_a_log, ssd_d=ssd_d, ssd_norm=ssd_norm, ssd_w_out=ssd_w_out, pool_w_in=pool_w_in,
             pool_w_group=pool_w_group, pool_scale=pool_scale, pool_w_out=pool_w_out, final_norm=final_norm)
    mom = dict(ffn_norm=m_ffn_norm, ffn_w_gate=m_ffn_w_gate, ffn_w_up=m_ffn_w_up, ffn_w_down=m_ffn_w_down,
               mix_norm=m_mix_norm, ssd_w_in=m_ssd_w_in, ssd_conv_w=m_ssd_conv_w, ssd_conv_b=m_ssd_conv_b,
               ssd_dt_bias=m_ssd_dt_bias, ssd_a_log=m_ssd_a_log, ssd_d=m_ssd_d, ssd_norm=m_ssd_norm,
               ssd_w_out=m_ssd_w_out, pool_w_in=m_pool_w_in, pool_w_group=m_pool_w_group, pool_scale=m_pool_scale,
               pool_w_out=m_pool_w_out, final_norm=m_final_norm)
    vel = dict(ffn_norm=v_ffn_norm, ffn_w_gate=v_ffn_w_gate, ffn_w_up=v_ffn_w_up, ffn_w_down=v_ffn_w_down,
               mix_norm=v_mix_norm, ssd_w_in=v_ssd_w_in, ssd_conv_w=v_ssd_conv_w, ssd_conv_b=v_ssd_conv_b,
               ssd_dt_bias=v_ssd_dt_bias, ssd_a_log=v_ssd_a_log, ssd_d=v_ssd_d, ssd_norm=v_ssd_norm,
               ssd_w_out=v_ssd_w_out, pool_w_in=v_pool_w_in, pool_w_group=v_pool_w_group, pool_scale=v_pool_scale,
               pool_w_out=v_pool_w_out, final_norm=v_final_norm)
    depth = ffn_w_gate.shape[0]
    n_s, n_p = ssd_w_in.shape[0], pool_w_in.shape[0]
    chip = 2 * lax.axis_index("x") + lax.axis_index("y")

    chip1 = jnp.reshape(chip, (1,)).astype(jnp.int32)
    core1 = jnp.reshape(lax.axis_index("c"), (1,)).astype(jnp.int32)

    gate3, up3, down3 = _view3(ffn_w_gate), _view3(ffn_w_up), _view3(ffn_w_down)
    w_in_t = ssd_w_in.transpose(0, 2, 1)
    pending = {}
    sharded_names = tuple(SMALL_SHARDED)
    tok = jnp.zeros((), F32)

    def begin(group, name, fulls, axes):
        nonlocal tok
        pending[group], t = _gather_begin(name, fulls, axes)
        tok = tok + t

    for i in range(depth):
        for h in range(2):
            fulls = [_cast_place(gate3, chip1, "cols", 2 * i + h, 1), _cast_place(up3, chip1, "cols", 2 * i + h, 1),
                     _cast_place(down3, chip1, "rows", 2 * i + h, 1)]
            if (i, h) == (0, 0):
                packed_small = _pack([w[k] for k in sharded_names])
                fulls.append(_cast_place(packed_small[None], chip1, "slot", 0, 1, F32))
                begin(("ffn", i, h), "gather_first", fulls, [(2, 1), (2, 1), (1, 2), (1, 2)])
            else:
                begin(("ffn", i, h), "gather_ffn", fulls, [(2, 1), (2, 1), (1, 2)])
            j = i // 2
            if h == 0 and i % 2 == 0:
                begin(("ssd", j), "gather_ssd", [_cast_place(w_in_t, chip1, "rows", j, 1),
                                                 _cast_place(ssd_w_out, chip1, "rows", j, 1)], [(1, 2), (1, 2)])
            if h == 0 and i % 2 == 1:
                n_g = pool_w_group.shape[1]
                grp_full = _cast_place(_view3(pool_w_group), chip1, "rows", j * n_g, n_g)
                begin(("pool", j), "gather_pool", [_cast_place(pool_w_in, chip1, "rows", j, 1), grp_full[None],
                                                   _cast_place(pool_w_out, chip1, "rows", j, 1)],
                      [(1, 2), (2, 3), (1, 2)])

    fetched = {}

    def fetch(group, after):
        if group not in fetched:
            got = _gather_finish(pending[group], after)
            fetched[group] = got[:3]
            if group == ("ffn", 0, 0):
                fetched["small"] = got[3][0]
        return fetched[group]

    fetch(("ffn", 0, 0), x)
    small = {k: w[k] for k in SMALL if k not in SMALL_SHARDED}
    per_chip = [_unpack(fetched["small"][s], [w[k].shape for k in sharded_names]) for s in range(N_CHIPS)]
    for t, k in enumerate(sharded_names):
        small[k] = jnp.concatenate([per_chip[s][t] for s in range(N_CHIPS)], axis=SMALL_SHARDED[k])

    ffn_names = ("ffn_w_gate", "ffn_w_up", "ffn_w_down")
    group_axes = dict(ffn=[(1, 0), (1, 0), (0, 1)], ssd=[(0, 1), (0, 1)], pool=[(0, 1), (1, 2), (0, 1)])
    group_names = dict(ffn=ffn_names, ssd=("ssd_w_in", "ssd_w_out"), pool=("pool_w_in", "pool_w_group", "pool_w_out"))
    travelling = {}
    between_cores = []

    def advance(after):
        group, pend = between_cores.pop()
        axes = group_axes[group[0]]
        grads, recv_a = _rs_cores_finish(pend, after)
        chip_sums = [_add_pair(g, r, ha, core1) for g, r, (_, ha) in zip(grads, recv_a, axes)]
        travelling[group], t = _rs_chips_begin("rs_chips_" + group[0], chip_sums, axes)
        return t

    def emit(group, grads):
        kind = group[0]
        t = advance(grads[0]) if between_cores else jnp.zeros((), F32)
        pend, t2 = _rs_cores_begin("rs_cores_" + kind, grads, group_axes[kind])
        between_cores.append((group, pend))
        return t + t2

    total = {}

    def on_loss(part):
        total["loss"] = lax.psum(part[0, 0], ("x", "y", "c"))
        return jnp.minimum(total["loss"], 0.0)

    _, dx, gsmall, t_emits = _local_step(x[0], loss_target[0], depth, fetch, emit, on_loss, small, tok)
    loss = total["loss"]

    t_last = advance(dx)
    small_pending, t_small = _small_begin(_pack([gsmall[k] for k in SMALL]))
    started = jnp.reshape(t_emits + t_last + t_small, (1, 1))

    rows_half = dict(ffn_w_gate=True, ffn_w_up=True, ffn_w_down=False, ssd_w_in=False, ssd_w_out=False,
                     pool_w_in=False, pool_w_group=False, pool_w_out=False)
    w_t, mom_t, vel_t = ({**src, "ssd_w_in": src["ssd_w_in"].transpose(0, 2, 1)} for src in (w, mom, vel))
    results = {}

    def finish(groups, after):
        sums = {k: {} for k in BIG}
        for group in groups:
            kind = group[0]
            chip_sums, recv_b = _rs_chips_finish(travelling[group], after)
            for k, cs, r, (sa, _) in zip(group_names[kind], chip_sums, recv_b, group_axes[kind]):
                sums[k][group[1:]] = _add_four(cs, r, sa, chip1)
        names = [k for k in BIG if sums[k]]
        sending, t_sent = {}, jnp.zeros((), F32)
        for k in names:
            sending[k], t = _rs_finish_begin("rs_finish_" + k, [sums[k][idx] for idx in sorted(sums[k])])
            t_sent = t_sent + t
        before = t_sent
        for k in names:
            own, from_sibling = _rs_finish_end(sending[k], jnp.stack([t_sent, before]).reshape(1, 2))
            first = min(sums[k])
            l0 = first[0] * 2 + first[1] if k in ffn_names else first[0]
            results[k] = _adamw_halves(w_t[k], jnp.stack(own), from_sibling, mom_t[k], vel_t[k], rows_half[k],
                                       core1, l0, results.get(k))
            before = results[k][3][(0,) * results[k][3].ndim]
        return jnp.stack([results[k][3][(0,) * results[k][3].ndim] for k in names]).reshape(1, -1)

    late = [g for g in travelling if g[1] == 0 and g[0] != "pool"]
    early = [g for g in travelling if g not in late]
    done_early = finish(early, started) if early else started
    done_late = finish(late, done_early)
    results["ssd_w_in"] = tuple(r.transpose(0, 2, 1) for r in results["ssd_w_in"])
    grad, delta, new_m, new_v = ({k: results[k][t] for k in BIG} for t in range(4))

    small_shapes = [gsmall[k].shape for k in SMALL]
    me1 = 2 * chip1 + core1
    summed = _unpack(_small_finish(small_pending, done_late, me1), small_shapes)
    for k, g in zip(SMALL, summed):
        if k in SMALL_SHARDED:
            ax = SMALL_SHARDED[k]
            size = w[k].shape[ax]
            g = lax.dynamic_slice_in_dim(g, chip * size, size, axis=ax)
        grad[k] = g

    shapes = [w[k].shape for k in SMALL]
    packed = _adamw(*(_pack([src[k] for k in SMALL]) for src in (w, grad, mom, vel)))
    for dst, p in zip((delta, new_m, new_v), packed):
        for k, a in zip(SMALL, _unpack(p, shapes)):
            dst[k] = a

    return (loss, dx[None], *[grad[k] for k in WEIGHTS], *[delta[k] for k in WEIGHTS],
            *[new_m[k] for k in WEIGHTS], *[new_v[k] for k in WEIGHTS])
```
